```python
import math
import jax, jax.numpy as jnp
from jax import lax
import numpy as np

D_MODEL = 1024
BATCH = 8
SEQ = 4096
DEPTH = 2

CHUNK = 64
Q_BLOCK = 128
HEAD_DIM = 64
RMS_EPS = 1e-6
A_HEADS = 4
A_V_DIM = 2 * HEAD_DIM
B_HEADS = 8
IDX_HEADS = 4
IDX_DIM = HEAD_DIM
TOPK_MAX = 256
C_HEADS = D_MODEL // HEAD_DIM
C_LEFT_CHUNKS = 8
C_BAND = (C_LEFT_CHUNKS + 1) * CHUNK
C_REL_CLIP = 128
T5_BUCKETS = 32
T5_MAX_DIST = 1024
T5_HEADS = A_HEADS + B_HEADS
N_GROUPS = 4
EXPERTS_PER_GROUP = 8
N_EXPERTS = N_GROUPS * EXPERTS_PER_GROUP
EXPERT_TOPK = 2
D_EXPERT = D_MODEL // 2
N_EVEN = (DEPTH + 1) // 2
N_ODD = DEPTH // 2
AB_WIDTHS = (A_HEADS * 2 * HEAD_DIM, A_HEADS * 2 * HEAD_DIM, A_HEADS * A_V_DIM,
             B_HEADS * HEAD_DIM, HEAD_DIM, HEAD_DIM, IDX_HEADS * IDX_DIM, IDX_DIM, IDX_HEADS)
AB_SPLITS = tuple(sum(AB_WIDTHS[:i + 1]) for i in range(len(AB_WIDTHS) - 1))
AB_IN = sum(AB_WIDTHS)
AB_OUT = A_HEADS * A_V_DIM + B_HEADS * HEAD_DIM
NEG = -1e30

kernel_name = "hybrid_chunk_causal_diff_dsa_band_hmoe"


def rms_norm(x, g):
    xf = x.astype(jnp.float32)
    y = xf * lax.rsqrt(jnp.mean(xf * xf, axis=-1, keepdims=True) + RMS_EPS)
    return (y * g.astype(jnp.float32)).astype(x.dtype)


def masked_softmax(s, mask):
    return jax.nn.softmax(jnp.where(mask, s.astype(jnp.float32), NEG), axis=-1)


def t5_bucket(rel):
    half = T5_BUCKETS // 2
    max_exact = half // 2
    ret = jnp.where(rel > 0, half, 0)
    n = jnp.abs(rel)
    nf = jnp.maximum(n, 1).astype(jnp.float32)
    large = max_exact + (jnp.log(nf / max_exact) / math.log(T5_MAX_DIST / max_exact)
                         * (half - max_exact)).astype(jnp.int32)
    large = jnp.minimum(large, half - 1)
    return (ret + jnp.where(n < max_exact, n, large)).astype(jnp.int32)


def diff_sparse_mixer(xn, w_in, w_out, lam, subln, t5_bias, layer_idx):
    bsz, s_len, _ = xn.shape
    proj = xn @ w_in
    q_a, k_a, v_a, q_b, k_b, v_b, q_i, k_i, w_i = jnp.split(proj, AB_SPLITS, axis=-1)
    q_a = q_a.reshape(bsz, s_len, A_HEADS, 2, HEAD_DIM)
    k_a = k_a.reshape(bsz, s_len, A_HEADS, 2, HEAD_DIM)
    v_a = v_a.reshape(bsz, s_len, A_HEADS, A_V_DIM)
    q_b = q_b.reshape(bsz, s_len, B_HEADS, HEAD_DIM)
    q_i = q_i.reshape(bsz, s_len, IDX_HEADS, IDX_DIM)
    w_i = w_i.astype(jnp.float32) * (IDX_HEADS ** -0.5) * (IDX_DIM ** -0.5)

    lam_init = 0.8 - 0.6 * math.exp(-0.3 * layer_idx)
    lamf = lam.astype(jnp.float32)
    lam_full = jnp.exp(jnp.sum(lamf[0] * lamf[1])) - jnp.exp(jnp.sum(lamf[2] * lamf[3])) + lam_init

    top_k = min(TOPK_MAX, s_len // 4)
    scale = HEAD_DIM ** -0.5
    key_pos = jnp.arange(s_len, dtype=jnp.int32)
    key_chunk = key_pos // CHUNK
    t5_a = t5_bias[:, :A_HEADS]
    t5_b = t5_bias[:, A_HEADS:]
    gather_rows = jax.vmap(lambda arr, idx: arr[idx])

    def block(i):
        start = i * Q_BLOCK
        qpos = start + jnp.arange(Q_BLOCK, dtype=jnp.int32)
        qchunk = qpos // CHUNK
        mask = key_chunk[None, :] <= qchunk[:, None]
        qa = lax.dynamic_slice_in_dim(q_a, start, Q_BLOCK, axis=1)
        bias_a = t5_a[t5_bucket(key_pos[None, :] - qpos[:, None])].transpose(2, 0, 1)
        s_a = jnp.einsum('bqhmd,bshmd->bhmqs', qa, k_a) * scale + bias_a[None, :, None]
        p_a = masked_softmax(s_a, mask[None, None, None])
        attn_a = (p_a[:, :, 0] - lam_full * p_a[:, :, 1]).astype(v_a.dtype)
        out_a = jnp.einsum('bhqs,bshe->bqhe', attn_a, v_a)
        out_a = rms_norm(out_a, subln) * (1.0 - lam_init)
        out_a = out_a.reshape(bsz, Q_BLOCK, A_HEADS * A_V_DIM)
        qi = lax.dynamic_slice_in_dim(q_i, start, Q_BLOCK, axis=1)
        wi = lax.dynamic_slice_in_dim(w_i, start, Q_BLOCK, axis=1)
        isc = jax.nn.relu(jnp.einsum('bqhd,bsd->bqhs', qi, k_i).astype(jnp.float32))
        isc = jnp.einsum('bqhs,bqh->bqs', isc, wi)
        isc = jnp.where(mask[None], isc, -jnp.inf)
        _, sel = lax.top_k(isc, top_k)
        sel_valid = (sel // CHUNK) <= qchunk[None, :, None]
        k_sel = gather_rows(k_b, sel)
        v_sel = gather_rows(v_b, sel)
        qb = lax.dynamic_slice_in_dim(q_b, start, Q_BLOCK, axis=1)
        bias_b = t5_b[t5_bucket(sel - qpos[None, :, None])].transpose(0, 3, 1, 2)
        s_b = jnp.einsum('bqhd,bqkd->bhqk', qb, k_sel) * scale + bias_b
        p_b = masked_softmax(s_b, sel_valid[:, None]).astype(v_sel.dtype)
        out_b = jnp.einsum('bhqk,bqkd->bqhd', p_b, v_sel).reshape(bsz, Q_BLOCK, B_HEADS * HEAD_DIM)
        return jnp.concatenate([out_a, out_b], axis=-1)

    outs = lax.map(block, jnp.arange(s_len // Q_BLOCK))
    y = outs.transpose(1, 0, 2, 3).reshape(bsz, s_len, AB_OUT)
    return y @ w_out


def chunk_band_mixer(xn, w_in, w_out, rel_table):
    bsz, s_len, _ = xn.shape
    q, k, v = jnp.split(xn @ w_in, 3, axis=-1)
    q = q.reshape(bsz, s_len, C_HEADS, HEAD_DIM)
    k = k.reshape(bsz, s_len, C_HEADS, HEAD_DIM)
    v = v.reshape(bsz, s_len, C_HEADS, HEAD_DIM)
    pad = C_LEFT_CHUNKS * CHUNK
    k_pad = jnp.pad(k, ((0, 0), (pad, 0), (0, 0), (0, 0)))
    v_pad = jnp.pad(v, ((0, 0), (pad, 0), (0, 0), (0, 0)))
    qi = jnp.arange(CHUNK, dtype=jnp.int32)
    kj = jnp.arange(C_BAND, dtype=jnp.int32)
    rel = qi[:, None] - (kj[None, :] - pad)
    rel_idx = jnp.clip(rel, -C_REL_CLIP, C_REL_CLIP) + C_REL_CLIP
    bias = rel_table[rel_idx].transpose(2, 0, 1)
    scale = HEAD_DIM ** -0.5

    def chunk(c):
        start = c * CHUNK
        qc = lax.dynamic_slice_in_dim(q, start, CHUNK, axis=1)
        kc = lax.dynamic_slice_in_dim(k_pad, start, C_BAND, axis=1)
        vc = lax.dynamic_slice_in_dim(v_pad, start, C_BAND, axis=1)
        valid = (start - pad + kj) >= 0
        s = jnp.einsum('bqhd,bkhd->bhqk', qc, kc) * scale + bias[None]
        p = masked_softmax(s, valid[None, None, None, :]).astype(vc.dtype)
        return jnp.einsum('bhqk,bkhd->bqhd', p, vc)

    outs = lax.map(chunk, jnp.arange(s_len // CHUNK))
    y = outs.transpose(1, 0, 2, 3, 4).reshape(bsz, s_len, C_HEADS * HEAD_DIM)
    return y @ w_out


def hier_moe(xn, w_rg, b_rg, w_re, b_re, w_gate, w_up, w_down):
    bsz, s_len, d = xn.shape
    xt = xn.reshape(-1, d)
    n_tok = xt.shape[0]
    g_logits = (xt @ w_rg + b_rg).astype(jnp.float32)
    p_g = jax.nn.softmax(g_logits, axis=-1)
    g_sel = jnp.argmax(g_logits, axis=-1)
    p_gsel = jnp.take_along_axis(p_g, g_sel[:, None], axis=1)
    e_logits = (xt @ w_re + b_re).astype(jnp.float32).reshape(n_tok, N_GROUPS, EXPERTS_PER_GROUP)
    e_logits = jnp.take_along_axis(e_logits, g_sel[:, None, None], axis=1)[:, 0]
    p_e = jax.nn.softmax(e_logits, axis=-1)
    top_w, top_i = lax.top_k(p_e, EXPERT_TOPK)
    top_w = top_w / jnp.sum(top_w, axis=-1, keepdims=True)
    w_within = jnp.sum(jax.nn.one_hot(top_i, EXPERTS_PER_GROUP, dtype=jnp.float32) * top_w[..., None], axis=1)
    combine = (jax.nn.one_hot(g_sel, N_GROUPS, dtype=jnp.float32)[:, :, None]
               * w_within[:, None, :]).reshape(n_tok, N_EXPERTS) * p_gsel
    combine = combine.astype(xt.dtype)
    y = jnp.zeros_like(xt)
    for e in range(N_EXPERTS):
        h = jax.nn.silu(xt @ w_gate[e]) * (xt @ w_up[e])
        y = y + combine[:, e:e + 1] * (h @ w_down[e])
    return y.reshape(bsz, s_len, d)


def setup_inputs(seed: int = 0) -> dict:
    key = jax.random.key(seed)
    ks = jax.random.split(key, 20)
    f32 = jnp.float32

    def nrm(k, shape, fan_in):
        return jax.random.normal(k, shape, f32) * (fan_in ** -0.5)

    def gain(k, shape):
        return 1.0 + 0.02 * jax.random.normal(k, shape, f32)

    return {
        "x": jax.random.normal(ks[0], (BATCH, SEQ, D_MODEL), f32),
        "t5_bias": 0.2 * jax.random.normal(ks[1], (T5_BUCKETS, T5_HEADS), f32),
        "ln_mix": gain(ks[2], (DEPTH, D_MODEL)),
        "ln_ffn": gain(ks[3], (DEPTH, D_MODEL)),
        "ln_final": gain(ks[4], (D_MODEL,)),
        "ab_w_in": nrm(ks[5], (N_EVEN, D_MODEL, AB_IN), D_MODEL),
        "ab_w_out": nrm(ks[6], (N_EVEN, AB_OUT, D_MODEL), AB_OUT),
        "diff_lambda": 0.1 * jax.random.normal(ks[7], (N_EVEN, 4, HEAD_DIM), f32),
        "diff_subln": gain(ks[8], (N_EVEN, A_V_DIM)),
        "c_w_in": nrm(ks[9], (N_ODD, D_MODEL, 3 * C_HEADS * HEAD_DIM), D_MODEL),
        "c_w_out": nrm(ks[10], (N_ODD, C_HEADS * HEAD_DIM, D_MODEL), C_HEADS * HEAD_DIM),
        "c_rel_bias": 0.2 * jax.random.normal(ks[11], (N_ODD, 2 * C_REL_CLIP + 1, C_HEADS), f32),
        "moe_w_rg": nrm(ks[12], (DEPTH, D_MODEL, N_GROUPS), D_MODEL),
        "moe_b_rg": 0.01 * jax.random.normal(ks[13], (DEPTH, N_GROUPS), f32),
        "moe_w_re": nrm(ks[14], (DEPTH, D_MODEL, N_EXPERTS), D_MODEL),
        "moe_b_re": 0.01 * jax.random.normal(ks[15], (DEPTH, N_EXPERTS), f32),
        "moe_w_gate": nrm(ks[16], (DEPTH, N_EXPERTS, D_MODEL, D_EXPERT), D_MODEL),
        "moe_w_up": nrm(ks[17], (DEPTH, N_EXPERTS, D_MODEL, D_EXPERT), D_MODEL),
        "moe_w_down": nrm(ks[18], (DEPTH, N_EXPERTS, D_EXPERT, D_MODEL), D_EXPERT),
    }


def reference(x, t5_bias, ln_mix, ln_ffn, ln_final, ab_w_in, ab_w_out, diff_lambda, diff_subln,
              c_w_in, c_w_out, c_rel_bias, moe_w_rg, moe_b_rg, moe_w_re, moe_b_re,
              moe_w_gate, moe_w_up, moe_w_down):
    h = x
    for l in range(DEPTH):
        xn = rms_norm(h, ln_mix[l])
        if l % 2 == 0:
            e = l // 2
            h = h + diff_sparse_mixer(xn, ab_w_in[e], ab_w_out[e], diff_lambda[e], diff_subln[e], t5_bias, l)
        else:
            o = l // 2
            h = h + chunk_band_mixer(xn, c_w_in[o], c_w_out[o], c_rel_bias[o])
        xn = rms_norm(h, ln_ffn[l])
        h = h + hier_moe(xn, moe_w_rg[l], moe_b_rg[l], moe_w_re[l], moe_b_re[l],
                         moe_w_gate[l], moe_w_up[l], moe_w_down[l])
    return rms_norm(h, ln_final)
```

```python
import functools
import math

import jax
import jax.numpy as jnp
from jax import lax
from jax.experimental import pallas as pl
from jax.experimental.pallas import tpu as pltpu

F32 = jnp.float32
BF16 = jnp.bfloat16
I32 = jnp.int32

D_MODEL = 1024
CHUNK = 64
HEAD_DIM = 64
RMS_EPS = 1e-6
A_HEADS = 4
A_V_DIM = 2 * HEAD_DIM
B_HEADS = 8
IDX_HEADS = 4
TOPK_MAX = 256
C_HEADS = D_MODEL // HEAD_DIM
C_LEFT_CHUNKS = 8
C_REL_CLIP = 128
T5_BUCKETS = 32
T5_MAX_DIST = 1024
N_GROUPS = 4
EXPERTS_PER_GROUP = 8
N_EXPERTS = N_GROUPS * EXPERTS_PER_GROUP
D_EXPERT = D_MODEL // 2
NEG = -1e30

LANE = 128
VMEM_LIMIT = 56 * 1024 * 1024

TQ = 256
NEAR_TILES = 4
TQ_C = 2 * CHUNK
WIN_C = (C_LEFT_CHUNKS + 2) * CHUNK
TM_MOE = 256

COL_QA, COL_KA, COL_VA, COL_QB, COL_KB, COL_VB, COL_QI, COL_KI, COL_WI = (
    0, 512, 1024, 1536, 2048, 2176, 2304, 2560, 2688)
AB_COLS = 2688


def _cparams(sem, vmem=VMEM_LIMIT):
    return pltpu.CompilerParams(dimension_semantics=sem, vmem_limit_bytes=vmem)


def _dot_nt(a, b):
    return lax.dot_general(a, b, (((1,), (1,)), ((), ())), preferred_element_type=F32)


def _rms(x, g):
    return (x * lax.rsqrt(jnp.mean(x * x, axis=-1, keepdims=True) + RMS_EPS)) * g


def _norm_proj_kernel(x_ref, g_ref, w_ref, o_ref, *tail_ref, n_main):
    xn = _rms(x_ref[...], g_ref[...]).astype(BF16)
    acc = jnp.dot(xn, w_ref[...], preferred_element_type=F32)
    o_ref[...] = acc[:, :n_main].astype(o_ref.dtype)
    if tail_ref:
        tail_ref[0][...] = acc[:, n_main:]


def _norm_proj(x, g, w, n_main, tm=256):
    n, d = x.shape
    n_all = w.shape[1]
    out_shape = [jax.ShapeDtypeStruct((n, n_main), BF16)]
    out_specs = [pl.BlockSpec((tm, n_main), lambda i: (i, 0))]
    if n_all > n_main:
        out_shape.append(jax.ShapeDtypeStruct((n, n_all - n_main), F32))
        out_specs.append(pl.BlockSpec((tm, n_all - n_main), lambda i: (i, 0)))
    return pl.pallas_call(
        functools.partial(_norm_proj_kernel, n_main=n_main),
        grid=(n // tm,),
        in_specs=[pl.BlockSpec((tm, d), lambda i: (i, 0)),
                  pl.BlockSpec((1, d), lambda i: (0, 0)),
                  pl.BlockSpec((d, n_all), lambda i: (0, 0))],
        out_specs=out_specs, out_shape=out_shape,
        compiler_params=_cparams(("parallel",)),
    )(x, g.reshape(1, d), w)


def _out_proj_kernel(h_ref, y0_ref, y1_ref, w0_ref, w1_ref, o_ref):
    acc = jnp.dot(y0_ref[...], w0_ref[...], preferred_element_type=F32)
    acc = acc + jnp.dot(y1_ref[...], w1_ref[...], preferred_element_type=F32)
    o_ref[...] = h_ref[...] + acc


def _out_proj(h, y0, c0, y1, c1, w, tm=512):
    n, d = h.shape
    half = d // 2
    return pl.pallas_call(
        _out_proj_kernel,
        grid=(n // tm,),
        in_specs=[pl.BlockSpec((tm, d), lambda i: (i, 0)),
                  pl.BlockSpec((tm, half), lambda i: (i, c0)),
                  pl.BlockSpec((tm, half), lambda i: (i, c1)),
                  pl.BlockSpec((half, d), lambda i: (0, 0)),
                  pl.BlockSpec((half, d), lambda i: (1, 0))],
        out_specs=pl.BlockSpec((tm, d), lambda i: (i, 0)),
        out_shape=jax.ShapeDtypeStruct((n, d), F32),
        compiler_params=_cparams(("parallel",)),
    )(h, y0, y1, w, w)


def _t5_bucket(rel):
    half = T5_BUCKETS // 2
    max_exact = half // 2
    ret = jnp.where(rel > 0, half, 0)
    n = jnp.abs(rel)
    nf = jnp.maximum(n, 1).astype(F32)
    large = max_exact + (jnp.log(nf / max_exact) / math.log(T5_MAX_DIST / max_exact)
                         * (half - max_exact)).astype(I32)
    large = jnp.minimum(large, half - 1)
    return (ret + jnp.where(n < max_exact, n, large)).astype(I32)


def _t5_tables(t5_bias):
    span = NEAR_TILES * TQ
    r = jnp.arange(TQ, dtype=I32)[:, None]
    kr = jnp.arange(span, dtype=I32)[None, :] - (span - TQ)
    bias = t5_bias[_t5_bucket(kr - r)]
    ok = jnp.floor_divide(kr, CHUNK) <= (r // CHUNK)
    bias = jnp.where(ok[..., None], bias, NEG)
    near = bias.transpose(2, 0, 1).reshape(-1, TQ, NEAR_TILES, TQ).transpose(0, 2, 1, 3)
    far = t5_bias[_t5_bucket(jnp.asarray(-(span - TQ) - 1, I32))]
    return near, far


def _attn_a_kernel(far_ref, q_ref, k_ref, v_ref, tab_ref, lam_ref, sub_ref, o_ref,
                   m_ref, l_ref, acc_ref, *, lam_init):
    h = pl.program_id(1)
    i = pl.program_id(2)
    lane = lax.broadcasted_iota(I32, (1, LANE), 1)
    q = q_ref[...]
    zq = jnp.zeros_like(q)
    qs = jnp.concatenate([jnp.where(lane < HEAD_DIM, q, zq), jnp.where(lane >= HEAD_DIM, q, zq)], axis=0)
    m_ref[...] = jnp.full(m_ref.shape, NEG, F32)
    l_ref[...] = jnp.zeros(l_ref.shape, F32)
    acc_ref[...] = jnp.zeros(acc_ref.shape, F32)

    def tile(kt, bias):
        off = pl.multiple_of(kt * TQ, TQ)
        s = _dot_nt(qs, k_ref[pl.ds(off, TQ), :])
        s = (s.reshape(2, TQ, TQ) + bias).reshape(2 * TQ, TQ)
        m_prev = m_ref[...]
        m_new = jnp.maximum(m_prev, jnp.max(s, axis=-1, keepdims=True))
        alpha = jnp.exp(m_prev - m_new)
        p = jnp.exp(s - m_new)
        l_ref[...] = alpha * l_ref[...] + jnp.sum(p, axis=-1, keepdims=True)
        acc_ref[...] = alpha * acc_ref[...] + jnp.dot(p.astype(BF16), v_ref[pl.ds(off, TQ), :],
                                                      preferred_element_type=F32)
        m_ref[...] = m_new

    n_far = jnp.maximum(i - (NEAR_TILES - 1), 0)
    far_bias = far_ref[h]

    def far_body(kt, c):
        tile(kt, far_bias)
        return c

    lax.fori_loop(0, n_far, far_body, 0)

    def near_body(kt, c):
        tile(kt, tab_ref[kt - (i - (NEAR_TILES - 1))][None])
        return c

    lax.fori_loop(n_far, i + 1, near_body, 0)

    o = acc_ref[...] / l_ref[...]
    lam = lam_ref[...]
    lam_full = (jnp.exp(jnp.sum(lam[0:1] * lam[1:2], axis=-1, keepdims=True))
                - jnp.exp(jnp.sum(lam[2:3] * lam[3:4], axis=-1, keepdims=True)) + lam_init)
    d = o[:TQ] - lam_full * o[TQ:]
    o_ref[...] = (_rms(d, sub_ref[...]) * (1.0 - lam_init)).astype(o_ref.dtype)


def _attn_a(proj, near, far, lam, subln, lam_init):
    b, s, _ = proj.shape
    return pl.pallas_call(
        functools.partial(_attn_a_kernel, lam_init=lam_init),
        grid_spec=pltpu.PrefetchScalarGridSpec(
            num_scalar_prefetch=1,
            grid=(b, A_HEADS, s // TQ),
            in_specs=[
                pl.BlockSpec((None, TQ, LANE), lambda b_, h, i, f: (b_, i, COL_QA // LANE + h)),
                pl.BlockSpec((None, s, LANE), lambda b_, h, i, f: (b_, 0, COL_KA // LANE + h)),
                pl.BlockSpec((None, s, LANE), lambda b_, h, i, f: (b_, 0, COL_VA // LANE + h)),
                pl.BlockSpec((None, NEAR_TILES, TQ, TQ), lambda b_, h, i, f: (h, 0, 0, 0)),
                pl.BlockSpec((4, HEAD_DIM), lambda b_, h, i, f: (0, 0)),
                pl.BlockSpec((1, A_V_DIM), lambda b_, h, i, f: (0, 0)),
            ],
            out_specs=pl.BlockSpec((None, TQ, LANE), lambda b_, h, i, f: (b_, i, h)),
            scratch_shapes=[pltpu.VMEM((2 * TQ, 1), F32), pltpu.VMEM((2 * TQ, 1), F32),
                            pltpu.VMEM((2 * TQ, A_V_DIM), F32)],
        ),
        out_shape=jax.ShapeDtypeStruct((b, s, A_HEADS * A_V_DIM), BF16),
        compiler_params=_cparams(("parallel", "parallel", "arbitrary")),
    )(far, proj, proj, proj, near, lam, subln.reshape(1, A_V_DIM))


def _attn_b_kernel(far_ref, qb_ref, kb_ref, vb_ref, qi_ref, ki_ref, wi_ref, tab_ref, o_ref,
                   keys_ref, qs_ref, m_ref, l_ref, acc_ref, *, top_k, idx_bits):
    i = pl.program_id(1)
    n_kt = i + 1
    lane = lax.broadcasted_iota(I32, (1, LANE), 1)
    lo = lane < HEAD_DIM
    int_min = jnp.int32(-2 ** 31)

    def split_heads(x):
        z = jnp.zeros_like(x)
        return jnp.concatenate([jnp.where(lo, x, z), jnp.where(lo, z, x)], axis=0)

    row = lax.broadcasted_iota(I32, (TQ, TQ), 0)
    col = lax.broadcasted_iota(I32, (TQ, TQ), 1)
    chunk_gap = (col // CHUNK) - (row // CHUNK)

    def causal(kt):
        return chunk_gap <= jnp.where(kt < i, jnp.int32(TQ), jnp.int32(0))

    qi = qi_ref[...]
    qis = jnp.concatenate([split_heads(qi[:, :LANE]), split_heads(qi[:, LANE:])], axis=0)
    wi = wi_ref[...] * ((IDX_HEADS ** -0.5) * (HEAD_DIM ** -0.5))

    def score_body(kt, c):
        off = pl.multiple_of(kt * TQ, TQ)
        r = jnp.maximum(_dot_nt(qis, ki_ref[pl.ds(off, TQ), :]), 0.0)
        isc = r[0:TQ] * wi[:, 0:1]
        for hh in range(1, IDX_HEADS):
            isc = isc + r[hh * TQ:(hh + 1) * TQ] * wi[:, hh:hh + 1]
        isc = jnp.where(isc == 0.0, 0.0, isc)
        isc = jnp.where(causal(kt), isc, -jnp.inf)
        bits = lax.bitcast_convert_type(isc, I32)
        keys_ref[kt] = bits ^ ((bits >> 31) & jnp.int32(0x7FFFFFFF))
        return c

    lax.fori_loop(0, n_kt, score_body, 0)

    def count(pred_fn):
        def body(kt, acc):
            c = jnp.where(pred_fn(kt, keys_ref[kt]), 1.0, 0.0)
            return acc + c[:, :LANE] + c[:, LANE:]
        acc = lax.fori_loop(0, n_kt, body, jnp.zeros((TQ, LANE), F32))
        return jnp.sum(acc, axis=-1, keepdims=True)

    def thr_body(bi, tu):
        cand_u = tu | (jnp.int32(1) << (31 - bi))
        cand_s = cand_u ^ int_min
        cnt = count(lambda kt, kk: kk >= cand_s)
        return jnp.where(cnt >= top_k, cand_u, tu)

    tu = lax.fori_loop(0, 32, thr_body, jnp.zeros((TQ, 1), I32))
    thr = tu ^ int_min
    need = top_k - count(lambda kt, kk: kk > thr)

    def tie_body(bi, jmax):
        cand = jmax | (jnp.int32(1) << (idx_bits - 1 - bi))
        cnt = count(lambda kt, kk: (kk == thr) & ((col + kt * TQ) <= cand))
        return jnp.where(cnt <= need, cand, jmax)

    jmax = lax.fori_loop(0, idx_bits, tie_body, jnp.zeros((TQ, 1), I32))

    qb = qb_ref[...]
    for j in range(B_HEADS // 2):
        qs_ref[2 * j * TQ:(2 * j + 2) * TQ, :] = split_heads(qb[:, j * LANE:(j + 1) * LANE])
    m_ref[...] = jnp.full(m_ref.shape, NEG, F32)
    l_ref[...] = jnp.zeros(l_ref.shape, F32)
    acc_ref[...] = jnp.zeros(acc_ref.shape, F32)

    def tile(kt, bias_fn):
        off = pl.multiple_of(kt * TQ, TQ)
        kk = keys_ref[kt]
        sel = (kk > thr) | ((kk == thr) & ((col + kt * TQ) <= jmax))
        sel = sel & causal(kt)
        kb = kb_ref[pl.ds(off, TQ), :]
        vb = vb_ref[pl.ds(off, TQ), :]
        for hh in range(B_HEADS):
            rows = slice(hh * TQ, (hh + 1) * TQ)
            s = _dot_nt(qs_ref[rows, :], kb) + bias_fn(hh)
            s = jnp.where(sel, s, NEG)
            m_prev = m_ref[rows, :]
            m_new = jnp.maximum(m_prev, jnp.max(s, axis=-1, keepdims=True))
            alpha = jnp.exp(m_prev - m_new)
            p = jnp.exp(s - m_new)
            l_ref[rows, :] = alpha * l_ref[rows, :] + jnp.sum(p, axis=-1, keepdims=True)
            acc_ref[rows, :] = alpha * acc_ref[rows, :] + jnp.dot(p.astype(BF16), vb,
                                                                  preferred_element_type=F32)
            m_ref[rows, :] = m_new

    n_far = jnp.maximum(i - (NEAR_TILES - 1), 0)

    def far_body(kt, c):
        tile(kt, lambda hh: far_ref[A_HEADS + hh])
        return c

    lax.fori_loop(0, n_far, far_body, 0)

    def near_body(kt, c):
        jt = kt - (i - (NEAR_TILES - 1))
        tile(kt, lambda hh: tab_ref[hh, jt])
        return c

    lax.fori_loop(n_far, n_kt, near_body, 0)

    for j in range(B_HEADS // 2):
        r0 = slice(2 * j * TQ, (2 * j + 1) * TQ)
        r1 = slice((2 * j + 1) * TQ, (2 * j + 2) * TQ)
        o = jnp.where(lo, acc_ref[r0, :] / l_ref[r0, :], acc_ref[r1, :] / l_ref[r1, :])
        o_ref[:, j * LANE:(j + 1) * LANE] = o.astype(o_ref.dtype)


def _attn_b(proj, wi, near_b, far):
    b, s, _ = proj.shape
    top_k = min(TOPK_MAX, s // 4)
    assert top_k <= TQ and s % TQ == 0 and (s & (s - 1)) == 0
    n_kt = s // TQ
    qb_w = B_HEADS * HEAD_DIM
    qi_w = IDX_HEADS * HEAD_DIM
    return pl.pallas_call(
        functools.partial(_attn_b_kernel, top_k=float(top_k), idx_bits=int(math.log2(s))),
        grid_spec=pltpu.PrefetchScalarGridSpec(
            num_scalar_prefetch=1,
            grid=(b, n_kt),
            in_specs=[
                pl.BlockSpec((None, TQ, qb_w), lambda b_, i, f: (b_, i, COL_QB // qb_w)),
                pl.BlockSpec((None, s, LANE), lambda b_, i, f: (b_, 0, COL_KB // LANE)),
                pl.BlockSpec((None, s, LANE), lambda b_, i, f: (b_, 0, COL_VB // LANE)),
                pl.BlockSpec((None, TQ, qi_w), lambda b_, i, f: (b_, i, COL_QI // qi_w)),
                pl.BlockSpec((None, s, LANE), lambda b_, i, f: (b_, 0, COL_KI // LANE)),
                pl.BlockSpec((None, TQ, LANE), lambda b_, i, f: (b_, i, 0)),
                pl.BlockSpec((B_HEADS, NEAR_TILES, TQ, TQ), lambda b_, i, f: (0, 0, 0, 0)),
            ],
            out_specs=pl.BlockSpec((None, TQ, qb_w), lambda b_, i, f: (b_, i, 0)),
            scratch_shapes=[pltpu.VMEM((n_kt, TQ, TQ), I32),
                            pltpu.VMEM((B_HEADS * TQ, LANE), BF16),
                            pltpu.VMEM((B_HEADS * TQ, 1), F32), pltpu.VMEM((B_HEADS * TQ, 1), F32),
                            pltpu.VMEM((B_HEADS * TQ, LANE), F32)],
        ),
        out_shape=jax.ShapeDtypeStruct((b, s, qb_w), BF16),
        compiler_params=_cparams(("parallel", "arbitrary")),
    )(far, proj, proj, proj, proj, proj, wi, near_b)


def _c_tables(rel_table):
    r = jnp.arange(TQ_C, dtype=I32)[None, :, None]
    c = jnp.arange(WIN_C, dtype=I32)[None, None, :]
    off = (jnp.arange(WIN_C // TQ_C, dtype=I32) * TQ_C)[:, None, None]
    key_rel = c - off
    rel_idx = jnp.clip(r - key_rel, -C_REL_CLIP, C_REL_CLIP) + C_REL_CLIP
    kc = jnp.floor_divide(key_rel, CHUNK)
    ok = (kc <= r // CHUNK) & (kc >= r // CHUNK - C_LEFT_CHUNKS)
    bias = jnp.where(ok[..., None], rel_table[rel_idx], NEG)
    return bias.transpose(3, 0, 1, 2)


def _attn_c_kernel(q_ref, k_ref, v_ref, tab_ref, o_ref):
    i = pl.program_id(2)
    n_var = WIN_C // TQ_C
    lane = lax.broadcasted_iota(I32, (1, LANE), 1)
    lo = lane < HEAD_DIM
    q = q_ref[...]
    zq = jnp.zeros_like(q)
    qs = jnp.concatenate([jnp.where(lo, q, zq), jnp.where(lo, zq, q)], axis=0)
    var = jnp.minimum(i, n_var - 1)
    ks = pl.multiple_of(jnp.maximum(i - (n_var - 1), 0) * TQ_C, TQ_C)
    s = _dot_nt(qs, k_ref[pl.ds(ks, WIN_C), :])
    s = s + jnp.concatenate([tab_ref[0, var], tab_ref[1, var]], axis=0)
    m = jnp.max(s, axis=-1, keepdims=True)
    p = jnp.exp(s - m)
    l = jnp.sum(p, axis=-1, keepdims=True)
    pv = jnp.dot(p.astype(BF16), v_ref[pl.ds(ks, WIN_C), :], preferred_element_type=F32) / l
    o_ref[...] = jnp.where(lo, pv[:TQ_C], pv[TQ_C:]).astype(o_ref.dtype)


def _attn_c(proj, tab):
    b, s, _ = proj.shape
    assert s >= WIN_C and s % TQ_C == 0
    n_pair = C_HEADS // 2
    n_var = WIN_C // TQ_C
    tab = tab.reshape(n_pair, 2, n_var, TQ_C, WIN_C)
    return pl.pallas_call(
        _attn_c_kernel,
        grid=(b, n_pair, s // TQ_C),
        in_specs=[
            pl.BlockSpec((None, TQ_C, LANE), lambda b_, h, i: (b_, i, h)),
            pl.BlockSpec((None, s, LANE), lambda b_, h, i: (b_, 0, n_pair + h)),
            pl.BlockSpec((None, s, LANE), lambda b_, h, i: (b_, 0, 2 * n_pair + h)),
            pl.BlockSpec((None, 2, n_var, TQ_C, WIN_C), lambda b_, h, i: (h, 0, 0, 0, 0)),
        ],
        out_specs=pl.BlockSpec((None, TQ_C, LANE), lambda b_, h, i: (b_, i, h)),
        out_shape=jax.ShapeDtypeStruct((b, s, D_MODEL), BF16),
        compiler_params=_cparams(("parallel", "parallel", "arbitrary")),
    )(proj, proj, proj, tab)


def _router_kernel(h_ref, g_ref, wr_ref, br_ref, xn_ref, ei_ref, cw_ref):
    xn = _rms(h_ref[...], g_ref[...])
    xn_ref[...] = xn
    logits = jnp.dot(xn, wr_ref[...], preferred_element_type=F32,
                     precision=lax.Precision.HIGHEST) + br_ref[...]
    lane = lax.broadcasted_iota(I32, (1, LANE), 1)
    lane_f = lane.astype(F32)
    big = float(LANE)
    is_g = lane < N_GROUPS
    lg = jnp.where(is_g, logits, -jnp.inf)
    gmax = jnp.max(lg, axis=-1, keepdims=True)
    g_sel = jnp.min(jnp.where(lg == gmax, lane_f, big), axis=-1, keepdims=True)
    p_gsel = 1.0 / jnp.sum(jnp.where(is_g, jnp.exp(logits - gmax), 0.0), axis=-1, keepdims=True)
    e_grp = ((lane - N_GROUPS) // EXPERTS_PER_GROUP).astype(F32)
    in_sel = (lane >= N_GROUPS) & (lane < N_GROUPS + N_EXPERTS) & (e_grp == g_sel)
    le = jnp.where(in_sel, logits, -jnp.inf)
    m1 = jnp.max(le, axis=-1, keepdims=True)
    i1 = jnp.min(jnp.where(le == m1, lane_f, big), axis=-1, keepdims=True)
    le2 = jnp.where(lane_f == i1, -jnp.inf, le)
    m2 = jnp.max(le2, axis=-1, keepdims=True)
    i2 = jnp.min(jnp.where(le2 == m2, lane_f, big), axis=-1, keepdims=True)
    t = jnp.exp(m2 - m1)
    c1 = p_gsel / (1.0 + t)
    c2 = p_gsel * t / (1.0 + t)
    ei_ref[...] = jnp.where(lane == 0, i1, i2).astype(I32) - N_GROUPS
    cw_ref[...] = jnp.where(lane == 0, c1, c2)


def _router(h, g, w_rg, b_rg, w_re, b_re, tm=256):
    n, d = h.shape
    pad = LANE - N_GROUPS - N_EXPERTS
    wr = jnp.concatenate([w_rg, w_re, jnp.zeros((d, pad), F32)], axis=1)
    br = jnp.concatenate([b_rg, b_re, jnp.zeros((pad,), F32)]).reshape(1, LANE)
    return pl.pallas_call(
        _router_kernel,
        grid=(n // tm,),
        in_specs=[pl.BlockSpec((tm, d), lambda i: (i, 0)),
                  pl.BlockSpec((1, d), lambda i: (0, 0)),
                  pl.BlockSpec((d, LANE), lambda i: (0, 0)),
                  pl.BlockSpec((1, LANE), lambda i: (0, 0))],
        out_specs=[pl.BlockSpec((tm, d), lambda i: (i, 0)),
                   pl.BlockSpec((tm, LANE), lambda i: (i, 0)),
                   pl.BlockSpec((tm, LANE), lambda i: (i, 0))],
        out_shape=[jax.ShapeDtypeStruct((n, d), F32),
                   jax.ShapeDtypeStruct((n, LANE), I32),
                   jax.ShapeDtypeStruct((n, LANE), F32)],
        compiler_params=_cparams(("parallel",)),
    )(h, g.reshape(1, d), wr, br)


def _moe_schedule(ei, n_tok, tm):
    ids = jnp.concatenate([ei[:, 0], ei[:, 1]])
    n_asg = 2 * n_tok
    order = jnp.argsort(ids, stable=True).astype(I32)
    counts = jnp.sum((ids[:, None] == jnp.arange(N_EXPERTS, dtype=I32)[None, :]).astype(I32), axis=0)
    tiles_per = (counts + tm - 1) // tm
    tile_end = jnp.cumsum(tiles_per)
    tile_start = tile_end - tiles_per
    row_start = jnp.cumsum(counts) - counts
    n_tiles = n_asg // tm + N_EXPERTS
    t = jnp.arange(n_tiles, dtype=I32)
    te = jnp.searchsorted(tile_end, t, side="right").astype(I32)
    live = te < N_EXPERTS
    te = jnp.minimum(te, N_EXPERTS - 1)
    rank0 = (t - tile_start[te]) * tm
    nv = jnp.where(live, jnp.clip(counts[te] - rank0, 0, tm), 0).astype(I32)
    r = jnp.arange(tm, dtype=I32)[None, :]
    pos = jnp.clip(row_start[te][:, None] + rank0[:, None] + r, 0, n_asg - 1)
    asg = order[pos]
    valid = r < nv[:, None]
    src = jnp.where(valid, asg % n_tok, 0).astype(I32)
    dst = jnp.where(valid, asg, 0).astype(I32)
    return te, nv, src.reshape(n_tiles, 1, tm), dst.reshape(n_tiles, 1, tm)


def _moe_kernel(te_ref, nv_ref, src_ref, dst_ref, x_hbm, wg_ref, wu_ref, wd_ref, o_hbm,
                xbuf, ybuf, gsem, ssem, *, tm):
    nv = nv_ref[pl.program_id(0)]

    def gather(r):
        return pltpu.make_async_copy(x_hbm.at[pl.ds(src_ref[0, r], 1)], xbuf.at[pl.ds(r, 1)], gsem)

    def scatter(r):
        return pltpu.make_async_copy(ybuf.at[pl.ds(r, 1)], o_hbm.at[pl.ds(dst_ref[0, r], 1)], ssem)

    def each(n, fn):
        def body(r, c):
            fn(r)
            return c
        lax.fori_loop(0, n, body, 0)

    @pl.when(nv > 0)
    def _():
        each(tm, lambda r: gather(r).start())
        each(tm, lambda r: gather(r).wait())
        x = xbuf[...].astype(BF16)
        g = jnp.dot(x, wg_ref[...], preferred_element_type=F32)
        u = jnp.dot(x, wu_ref[...], preferred_element_type=F32)
        act = (g * jax.nn.sigmoid(g)) * u
        ybuf[...] = jnp.dot(act.astype(BF16), wd_ref[...], preferred_element_type=F32)
        each(nv, lambda r: scatter(r).start())
        each(nv, lambda r: scatter(r).wait())


def _moe_experts(xn, sched, wg, wu, wd, tm=TM_MOE):
    n, d = xn.shape
    te, nv, src, dst = sched
    n_tiles = te.shape[0]
    return pl.pallas_call(
        functools.partial(_moe_kernel, tm=tm),
        grid_spec=pltpu.PrefetchScalarGridSpec(
            num_scalar_prefetch=2,
            grid=(n_tiles,),
            in_specs=[
                pl.BlockSpec((None, 1, tm), lambda t, te_, nv_: (t, 0, 0), memory_space=pltpu.SMEM),
                pl.BlockSpec((None, 1, tm), lambda t, te_, nv_: (t, 0, 0), memory_space=pltpu.SMEM),
                pl.BlockSpec(memory_space=pl.ANY),
                pl.BlockSpec((None, d, D_EXPERT), lambda t, te_, nv_: (te_[t], 0, 0)),
                pl.BlockSpec((None, d, D_EXPERT), lambda t, te_, nv_: (te_[t], 0, 0)),
                pl.BlockSpec((None, D_EXPERT, d), lambda t, te_, nv_: (te_[t], 0, 0)),
            ],
            out_specs=pl.BlockSpec(memory_space=pl.ANY),
            scratch_shapes=[pltpu.VMEM((tm, d), F32), pltpu.VMEM((tm, d), F32),
                            pltpu.SemaphoreType.DMA(()), pltpu.SemaphoreType.DMA(())],
        ),
        out_shape=jax.ShapeDtypeStruct((2 * n, d), F32),
        compiler_params=_cparams(("arbitrary",)),
    )(te, nv, src, dst, xn, wg, wu, wd)


def _combine_kernel(h_ref, y0_ref, y1_ref, cw_ref, g_ref, o_ref, *, final):
    cw = cw_ref[...]
    out = h_ref[...] + cw[:, 0:1] * y0_ref[...] + cw[:, 1:2] * y1_ref[...]
    if final:
        out = _rms(out, g_ref[...])
    o_ref[...] = out


def _combine(h, y, cw, g, final, tm=512):
    n, d = h.shape
    y = y.reshape(2, n, d)
    return pl.pallas_call(
        functools.partial(_combine_kernel, final=final),
        grid=(n // tm,),
        in_specs=[pl.BlockSpec((tm, d), lambda i: (i, 0)),
                  pl.BlockSpec((None, tm, d), lambda i: (0, i, 0)),
                  pl.BlockSpec((None, tm, d), lambda i: (1, i, 0)),
                  pl.BlockSpec((tm, LANE), lambda i: (i, 0)),
                  pl.BlockSpec((1, d), lambda i: (0, 0))],
        out_specs=pl.BlockSpec((tm, d), lambda i: (i, 0)),
        out_shape=jax.ShapeDtypeStruct((n, d), F32),
        compiler_params=_cparams(("parallel",)),
    )(h, y, y, cw, g.reshape(1, d))


def _moe_layer(h, ln_g, w_rg, b_rg, w_re, b_re, w_gate, w_up, w_down, final_g, final):
    n = h.shape[0]
    xn, ei, cw = _router(h, ln_g, w_rg, b_rg, w_re, b_re)
    sched = _moe_schedule(ei, n, TM_MOE)
    y = _moe_experts(xn, sched, w_gate.astype(BF16), w_up.astype(BF16), w_down.astype(BF16))
    return _combine(h, y, cw, final_g, final)


def _ab_weights(w_in):
    scale = HEAD_DIM ** -0.5
    widths = (512, 512, 512, 512, 64, 64, 256, 64, 4)
    offs = [sum(widths[:j]) for j in range(len(widths) + 1)]
    q_a, k_a, v_a, q_b, k_b, v_b, q_i, k_i, w_i = [w_in[:, offs[j]:offs[j + 1]] for j in range(len(widths))]
    pad = jnp.zeros((w_in.shape[0], LANE - IDX_HEADS), w_in.dtype)
    main = jnp.concatenate([q_a * scale, k_a, v_a, q_b * scale, k_b, k_b, v_b, v_b, q_i, k_i, k_i], axis=1)
    return jnp.concatenate([main, w_i, pad], axis=1).astype(BF16)


def kernel(x, t5_bias, ln_mix, ln_ffn, ln_final, ab_w_in, ab_w_out, diff_lambda, diff_subln,
           c_w_in, c_w_out, c_rel_bias, moe_w_rg, moe_b_rg, moe_w_re, moe_b_re,
           moe_w_gate, moe_w_up, moe_w_down):
    b, s, d = x.shape
    n = b * s
    depth = ln_mix.shape[0]
    h = x.reshape(n, d)
    near, far = _t5_tables(t5_bias)
    for l in range(depth):
        if l % 2 == 0:
            e = l // 2
            lam_init = 0.8 - 0.6 * math.exp(-0.3 * l)
            proj, wi = _norm_proj(h, ln_mix[l], _ab_weights(ab_w_in[e]), AB_COLS)
            proj = proj.reshape(b, s, AB_COLS)
            y_a = _attn_a(proj, near[:A_HEADS], far, diff_lambda[e], diff_subln[e], lam_init)
            y_b = _attn_b(proj, wi.reshape(b, s, LANE), near[A_HEADS:], far)
            h = _out_proj(h, y_a.reshape(n, -1), 0, y_b.reshape(n, -1), 0, ab_w_out[e].astype(BF16))
        else:
            o = l // 2
            scale = HEAD_DIM ** -0.5
            w_in = jnp.concatenate([c_w_in[o][:, :d] * scale, c_w_in[o][:, d:]], axis=1).astype(BF16)
            (proj,) = _norm_proj(h, ln_mix[l], w_in, 3 * d)
            y = _attn_c(proj.reshape(b, s, 3 * d), _c_tables(c_rel_bias[o])).reshape(n, d)
            h = _out_proj(h, y, 0, y, 1, c_w_out[o].astype(BF16))
        h = _moe_layer(h, ln_ffn[l], moe_w_rg[l], moe_b_rg[l], moe_w_re[l], moe_b_re[l],
                       moe_w_gate[l], moe_w_up[l], moe_w_down[l], ln_final, l == depth - 1)
    return h.reshape(b, s, d)
```

```python
import functools
import math

import jax
import jax.numpy as jnp
from jax import lax
from jax.experimental import pallas as pl
from jax.experimental.pallas import tpu as pltpu

F32 = jnp.float32
BF16 = jnp.bfloat16
I32 = jnp.int32

D_MODEL = 1024
CHUNK = 64
HEAD_DIM = 64
RMS_EPS = 1e-6
A_HEADS = 4
A_V_DIM = 2 * HEAD_DIM
B_HEADS = 8
IDX_HEADS = 4
TOPK_MAX = 256
C_HEADS = D_MODEL // HEAD_DIM
C_LEFT_CHUNKS = 8
C_REL_CLIP = 128
T5_BUCKETS = 32
T5_MAX_DIST = 1024
N_GROUPS = 4
EXPERTS_PER_GROUP = 8
N_EXPERTS = N_GROUPS * EXPERTS_PER_GROUP
D_EXPERT = D_MODEL // 2
NEG = -1e30

LANE = 128
VMEM_LIMIT = 56 * 1024 * 1024

TQ = 256
NEAR_TILES = 4
TQ_C = 2 * CHUNK
WIN_C = (C_LEFT_CHUNKS + 2) * CHUNK
TM_MOE = 256

COL_QA, COL_KA, COL_VA, COL_QB, COL_KB, COL_VB, COL_QI, COL_KI, COL_WI = (
    0, 512, 1024, 1536, 2048, 2176, 2304, 2560, 2688)
AB_COLS = 2688


def _cparams(sem, vmem=VMEM_LIMIT):
    return pltpu.CompilerParams(dimension_semantics=sem, vmem_limit_bytes=vmem)


def _dot_nt(a, b):
    return lax.dot_general(a, b, (((1,), (1,)), ((), ())), preferred_element_type=F32)


def _rms(x, g):
    return (x * lax.rsqrt(jnp.mean(x * x, axis=-1, keepdims=True) + RMS_EPS)) * g


def _norm_proj_kernel(x_ref, g_ref, w_ref, o_ref, *tail_ref, n_main):
    xn = _rms(x_ref[...], g_ref[...]).astype(BF16)
    acc = jnp.dot(xn, w_ref[...], preferred_element_type=F32)
    o_ref[...] = acc[:, :n_main].astype(o_ref.dtype)
    if tail_ref:
        tail_ref[0][...] = acc[:, n_main:]


def _norm_proj(x, g, w, n_main, tm=256):
    n, d = x.shape
    n_all = w.shape[1]
    out_shape = [jax.ShapeDtypeStruct((n, n_main), BF16)]
    out_specs = [pl.BlockSpec((tm, n_main), lambda i: (i, 0))]
    if n_all > n_main:
        out_shape.append(jax.ShapeDtypeStruct((n, n_all - n_main), F32))
        out_specs.append(pl.BlockSpec((tm, n_all - n_main), lambda i: (i, 0)))
    return pl.pallas_call(
        functools.partial(_norm_proj_kernel, n_main=n_main),
        grid=(n // tm,),
        in_specs=[pl.BlockSpec((tm, d), lambda i: (i, 0)),
                  pl.BlockSpec((1, d), lambda i: (0, 0)),
                  pl.BlockSpec((d, n_all), lambda i: (0, 0))],
        out_specs=out_specs, out_shape=out_shape,
        compiler_params=_cparams(("parallel",)),
    )(x, g.reshape(1, d), w)


def _out_proj_kernel(h_ref, y0_ref, y1_ref, w0_ref, w1_ref, o_ref):
    acc = jnp.dot(y0_ref[...], w0_ref[...], preferred_element_type=F32)
    acc = acc + jnp.dot(y1_ref[...], w1_ref[...], preferred_element_type=F32)
    o_ref[...] = h_ref[...] + acc


def _out_proj(h, y0, c0, y1, c1, w, tm=512):
    n, d = h.shape
    half = d // 2
    return pl.pallas_call(
        _out_proj_kernel,
        grid=(n // tm,),
        in_specs=[pl.BlockSpec((tm, d), lambda i: (i, 0)),
                  pl.BlockSpec((tm, half), lambda i: (i, c0)),
                  pl.BlockSpec((tm, half), lambda i: (i, c1)),
                  pl.BlockSpec((half, d), lambda i: (0, 0)),
                  pl.BlockSpec((half, d), lambda i: (1, 0))],
        out_specs=pl.BlockSpec((tm, d), lambda i: (i, 0)),
        out_shape=jax.ShapeDtypeStruct((n, d), F32),
        compiler_params=_cparams(("parallel",)),
    )(h, y0, y1, w, w)


def _t5_bucket(rel):
    half = T5_BUCKETS // 2
    max_exact = half // 2
    ret = jnp.where(rel > 0, half, 0)
    n = jnp.abs(rel)
    nf = jnp.maximum(n, 1).astype(F32)
    large = max_exact + (jnp.log(nf / max_exact) / math.log(T5_MAX_DIST / max_exact)
                         * (half - max_exact)).astype(I32)
    large = jnp.minimum(large, half - 1)
    return (ret + jnp.where(n < max_exact, n, large)).astype(I32)


def _toeplitz(g, rows, width):
    length = g.shape[-1]
    assert width < length and rows <= length
    flat = jnp.tile(g, (1,) * (g.ndim - 1) + (rows,))[..., :rows * (length - 1)]
    return flat.reshape(g.shape[:-1] + (rows, length - 1))[..., :width]


def _t5_tables(t5_bias):
    span = NEAR_TILES * TQ
    length = span + TQ
    k = jnp.arange(length, dtype=I32)
    delta = jnp.where(k < span, k, k - length) - (span - TQ)
    g = t5_bias[_t5_bucket(delta)].T
    bias = _toeplitz(g, TQ, span)
    r = jnp.arange(TQ, dtype=I32)[:, None]
    kr = jnp.arange(span, dtype=I32)[None, :] - (span - TQ)
    ok = jnp.floor_divide(kr, CHUNK) <= (r // CHUNK)
    bias = jnp.where(ok[None], bias, NEG)
    near = bias.reshape(-1, TQ, NEAR_TILES, TQ).transpose(0, 2, 1, 3)
    far = t5_bias[_t5_bucket(jnp.asarray(-(span - TQ) - 1, I32))]
    return near, far


def _pass1(score_fn, i, s_ref, mf_ref, mn_ref, near_bias_fn):
    n_far = jnp.maximum(i - (NEAR_TILES - 1), 0)
    mf_ref[...] = jnp.full(mf_ref.shape, NEG, F32)
    mn_ref[...] = jnp.full(mn_ref.shape, NEG, F32)

    def fold(s):
        return jnp.maximum(s[:, :LANE], s[:, LANE:])

    def far_body(kt, c):
        s = score_fn(kt)
        s_ref[kt] = s
        mf_ref[...] = jnp.maximum(mf_ref[...], fold(s))
        return c

    lax.fori_loop(0, n_far, far_body, 0)

    def near_body(kt, c):
        s = score_fn(kt)
        s = (s.reshape(2, TQ, TQ) + near_bias_fn(kt - (i - (NEAR_TILES - 1)))).reshape(2 * TQ, TQ)
        s_ref[kt] = s
        mn_ref[...] = jnp.maximum(mn_ref[...], fold(s))
        return c

    lax.fori_loop(n_far, i + 1, near_body, 0)
    return n_far


def _pass2(i, n_far, far_col, s_ref, v_ref, mf_ref, mn_ref, l_ref, acc_ref):
    m = jnp.maximum(jnp.max(mf_ref[...], axis=-1, keepdims=True) + far_col,
                    jnp.max(mn_ref[...], axis=-1, keepdims=True))
    m_near = jnp.broadcast_to(m, mn_ref.shape)
    mn_ref[...] = m_near
    mf_ref[...] = m_near - far_col
    l_ref[...] = jnp.zeros(l_ref.shape, F32)
    acc_ref[...] = jnp.zeros(acc_ref.shape, F32)

    def tile(kt, shift_ref):
        s = s_ref[kt]
        sh = shift_ref[...]
        p0 = jnp.exp(s[:, :LANE] - sh)
        p1 = jnp.exp(s[:, LANE:] - sh)
        l_ref[...] += p0 + p1
        p = jnp.concatenate([p0, p1], axis=1).astype(BF16)
        off = pl.multiple_of(kt * TQ, TQ)
        acc_ref[...] += jnp.dot(p, v_ref[pl.ds(off, TQ), :], preferred_element_type=F32)

    def far_body(kt, c):
        tile(kt, mf_ref)
        return c

    lax.fori_loop(0, n_far, far_body, 0)

    def near_body(kt, c):
        tile(kt, mn_ref)
        return c

    lax.fori_loop(n_far, i + 1, near_body, 0)
    return acc_ref[...] / jnp.sum(l_ref[...], axis=-1, keepdims=True)


def _softmax_scratch(n_kt):
    return [pltpu.VMEM((n_kt, 2 * TQ, TQ), F32),
            pltpu.VMEM((2 * TQ, LANE), F32), pltpu.VMEM((2 * TQ, LANE), F32),
            pltpu.VMEM((2 * TQ, LANE), F32), pltpu.VMEM((2 * TQ, LANE), F32)]


def _attn_a_kernel(far_ref, q_ref, k_ref, v_ref, tab_ref, lam_ref, sub_ref, o_ref,
                   s_ref, mf_ref, mn_ref, l_ref, acc_ref, *, lam_init):
    h = pl.program_id(1)
    i = pl.program_id(2)
    lane = lax.broadcasted_iota(I32, (1, LANE), 1)
    q = q_ref[...]
    zq = jnp.zeros_like(q)
    qs = jnp.concatenate([jnp.where(lane < HEAD_DIM, q, zq), jnp.where(lane >= HEAD_DIM, q, zq)], axis=0)

    def score(kt):
        off = pl.multiple_of(kt * TQ, TQ)
        return _dot_nt(qs, k_ref[pl.ds(off, TQ), :])

    n_far = _pass1(score, i, s_ref, mf_ref, mn_ref, lambda jt: tab_ref[jt][None])
    far_col = jnp.full((2 * TQ, 1), far_ref[h], F32)
    o = _pass2(i, n_far, far_col, s_ref, v_ref, mf_ref, mn_ref, l_ref, acc_ref)
    lam = lam_ref[...]
    lam_full = (jnp.exp(jnp.sum(lam[0:1] * lam[1:2], axis=-1, keepdims=True))
                - jnp.exp(jnp.sum(lam[2:3] * lam[3:4], axis=-1, keepdims=True)) + lam_init)
    d = o[:TQ] - lam_full * o[TQ:]
    o_ref[...] = (_rms(d, sub_ref[...]) * (1.0 - lam_init)).astype(o_ref.dtype)


def _attn_a(proj, near, far, lam, subln, lam_init):
    b, s, _ = proj.shape
    return pl.pallas_call(
        functools.partial(_attn_a_kernel, lam_init=lam_init),
        grid_spec=pltpu.PrefetchScalarGridSpec(
            num_scalar_prefetch=1,
            grid=(b, A_HEADS, s // TQ),
            in_specs=[
                pl.BlockSpec((None, TQ, LANE), lambda b_, h, i, f: (b_, i, COL_QA // LANE + h)),
                pl.BlockSpec((None, s, LANE), lambda b_, h, i, f: (b_, 0, COL_KA // LANE + h)),
                pl.BlockSpec((None, s, LANE), lambda b_, h, i, f: (b_, 0, COL_VA // LANE + h)),
                pl.BlockSpec((None, NEAR_TILES, TQ, TQ), lambda b_, h, i, f: (h, 0, 0, 0)),
                pl.BlockSpec((4, HEAD_DIM), lambda b_, h, i, f: (0, 0)),
                pl.BlockSpec((1, A_V_DIM), lambda b_, h, i, f: (0, 0)),
            ],
            out_specs=pl.BlockSpec((None, TQ, LANE), lambda b_, h, i, f: (b_, i, h)),
            scratch_shapes=_softmax_scratch(s // TQ),
        ),
        out_shape=jax.ShapeDtypeStruct((b, s, A_HEADS * A_V_DIM), BF16),
        compiler_params=_cparams(("parallel", "parallel", "arbitrary")),
    )(far, proj, proj, proj, near, lam, subln.reshape(1, A_V_DIM))


def _attn_b_kernel(far_ref, qb_ref, kb_ref, vb_ref, qi_ref, ki_ref, wi_ref, tab_ref, o_ref,
                   keys_ref, selm_ref, s_ref, mf_ref, mn_ref, l_ref, acc_ref, *, top_k, idx_bits):
    i = pl.program_id(1)
    n_kt = i + 1
    lane = lax.broadcasted_iota(I32, (1, LANE), 1)
    lo = lane < HEAD_DIM
    int_min = jnp.int32(-2 ** 31)

    def split_heads(x):
        z = jnp.zeros_like(x)
        return jnp.concatenate([jnp.where(lo, x, z), jnp.where(lo, z, x)], axis=0)

    row = lax.broadcasted_iota(I32, (TQ, TQ), 0)
    col = lax.broadcasted_iota(I32, (TQ, TQ), 1)
    chunk_gap = (col // CHUNK) - (row // CHUNK)

    def causal(kt):
        return chunk_gap <= jnp.where(kt < i, jnp.int32(TQ), jnp.int32(0))

    qi = qi_ref[...]
    qis = jnp.concatenate([split_heads(qi[:, :LANE]), split_heads(qi[:, LANE:])], axis=0)
    wi = wi_ref[...] * ((IDX_HEADS ** -0.5) * (HEAD_DIM ** -0.5))

    def score_body(kt, c):
        off = pl.multiple_of(kt * TQ, TQ)
        r = jnp.maximum(_dot_nt(qis, ki_ref[pl.ds(off, TQ), :]), 0.0)
        isc = r[0:TQ] * wi[:, 0:1]
        for hh in range(1, IDX_HEADS):
            isc = isc + r[hh * TQ:(hh + 1) * TQ] * wi[:, hh:hh + 1]
        isc = jnp.where(isc == 0.0, 0.0, isc)
        isc = jnp.where(causal(kt), isc, -jnp.inf)
        bits = lax.bitcast_convert_type(isc, I32)
        keys_ref[kt] = bits ^ ((bits >> 31) & jnp.int32(0x7FFFFFFF))
        return c

    lax.fori_loop(0, n_kt, score_body, 0)

    def count(pred_fn):
        def body(kt, acc):
            c = jnp.where(pred_fn(kt, keys_ref[kt]), 1.0, 0.0)
            return acc + c[:, :LANE] + c[:, LANE:]
        acc = lax.fori_loop(0, n_kt, body, jnp.zeros((TQ, LANE), F32))
        return jnp.sum(acc, axis=-1, keepdims=True)

    def thr_body(bi, tu):
        cand_u = tu | (jnp.int32(1) << (31 - bi))
        cand_s = cand_u ^ int_min
        cnt = count(lambda kt, kk: kk >= cand_s)
        return jnp.where(cnt >= top_k, cand_u, tu)

    tu = lax.fori_loop(0, 32, thr_body, jnp.zeros((TQ, 1), I32))
    thr = tu ^ int_min
    need = top_k - count(lambda kt, kk: kk > thr)

    def tie_body(bi, jmax):
        cand = jmax | (jnp.int32(1) << (idx_bits - 1 - bi))
        cnt = count(lambda kt, kk: (kk == thr) & ((col + kt * TQ) <= cand))
        return jnp.where(cnt <= need, cand, jmax)

    jmax = lax.fori_loop(0, idx_bits, tie_body, jnp.zeros((TQ, 1), I32))

    def mask_body(kt, c):
        kk = keys_ref[kt]
        sel = ((kk > thr) | ((kk == thr) & ((col + kt * TQ) <= jmax))) & causal(kt)
        selm_ref[kt] = jnp.where(sel, 0.0, NEG)
        return c

    lax.fori_loop(0, n_kt, mask_body, 0)

    qb = qb_ref[...]
    top_rows = lax.broadcasted_iota(I32, (2 * TQ, 1), 0) < TQ
    for g in range(B_HEADS // 2):
        qs = split_heads(qb[:, g * LANE:(g + 1) * LANE])

        def score(kt, qs=qs):
            off = pl.multiple_of(kt * TQ, TQ)
            s = _dot_nt(qs, kb_ref[pl.ds(off, TQ), :])
            return (s.reshape(2, TQ, TQ) + selm_ref[kt][None]).reshape(2 * TQ, TQ)

        n_far = _pass1(score, i, s_ref, mf_ref, mn_ref, lambda jt, g=g: tab_ref[2 * g:2 * g + 2, jt])
        far_col = jnp.where(top_rows, far_ref[A_HEADS + 2 * g], far_ref[A_HEADS + 2 * g + 1])
        o = _pass2(i, n_far, far_col, s_ref, vb_ref, mf_ref, mn_ref, l_ref, acc_ref)
        o_ref[:, g * LANE:(g + 1) * LANE] = jnp.where(lo, o[:TQ], o[TQ:]).astype(o_ref.dtype)


def _attn_b(proj, wi, near_b, far):
    b, s, _ = proj.shape
    top_k = min(TOPK_MAX, s // 4)
    assert top_k <= TQ and s % TQ == 0 and (s & (s - 1)) == 0
    n_kt = s // TQ
    qb_w = B_HEADS * HEAD_DIM
    qi_w = IDX_HEADS * HEAD_DIM
    return pl.pallas_call(
        functools.partial(_attn_b_kernel, top_k=float(top_k), idx_bits=int(math.log2(s))),
        grid_spec=pltpu.PrefetchScalarGridSpec(
            num_scalar_prefetch=1,
            grid=(b, n_kt),
            in_specs=[
                pl.BlockSpec((None, TQ, qb_w), lambda b_, i, f: (b_, i, COL_QB // qb_w)),
                pl.BlockSpec((None, s, LANE), lambda b_, i, f: (b_, 0, COL_KB // LANE)),
                pl.BlockSpec((None, s, LANE), lambda b_, i, f: (b_, 0, COL_VB // LANE)),
                pl.BlockSpec((None, TQ, qi_w), lambda b_, i, f: (b_, i, COL_QI // qi_w)),
                pl.BlockSpec((None, s, LANE), lambda b_, i, f: (b_, 0, COL_KI // LANE)),
                pl.BlockSpec((None, TQ, LANE), lambda b_, i, f: (b_, i, 0)),
                pl.BlockSpec((B_HEADS, NEAR_TILES, TQ, TQ), lambda b_, i, f: (0, 0, 0, 0)),
            ],
            out_specs=pl.BlockSpec((None, TQ, qb_w), lambda b_, i, f: (b_, i, 0)),
            scratch_shapes=[pltpu.VMEM((n_kt, TQ, TQ), I32), pltpu.VMEM((n_kt, TQ, TQ), F32)]
            + _softmax_scratch(n_kt),
        ),
        out_shape=jax.ShapeDtypeStruct((b, s, qb_w), BF16),
        compiler_params=_cparams(("parallel", "arbitrary")),
    )(far, proj, proj, proj, proj, proj, wi, near_b)


PAIRS_C = 2


def _c_tables(rel_table):
    n_var = WIN_C // TQ_C
    length = WIN_C + TQ_C
    k = jnp.arange(length, dtype=I32)
    cr = jnp.where(k < WIN_C, k, k - length)
    off = (jnp.arange(n_var, dtype=I32) * TQ_C)[:, None]
    rel_idx = jnp.clip(off - cr[None, :], -C_REL_CLIP, C_REL_CLIP) + C_REL_CLIP
    g = rel_table[rel_idx].transpose(2, 0, 1)
    bias = _toeplitz(g, TQ_C, WIN_C)
    r = jnp.arange(TQ_C, dtype=I32)[None, :, None]
    c = jnp.arange(WIN_C, dtype=I32)[None, None, :]
    kc = jnp.floor_divide(c - off[:, :, None], CHUNK)
    ok = (kc <= r // CHUNK) & (kc >= r // CHUNK - C_LEFT_CHUNKS)
    return jnp.where(ok[None], bias, NEG)


def _attn_c_kernel(q_ref, k_ref, v_ref, tab_ref, o_ref):
    i = pl.program_id(2)
    n_var = WIN_C // TQ_C
    lane = lax.broadcasted_iota(I32, (1, LANE), 1)
    lo = lane < HEAD_DIM
    var = jnp.minimum(i, n_var - 1)
    ks = pl.multiple_of(jnp.maximum(i - (n_var - 1), 0) * TQ_C, TQ_C)
    for pp in range(PAIRS_C):
        cols = slice(pp * LANE, (pp + 1) * LANE)
        q = q_ref[:, cols]
        zq = jnp.zeros_like(q)
        qs = jnp.concatenate([jnp.where(lo, q, zq), jnp.where(lo, zq, q)], axis=0)
        s = _dot_nt(qs, k_ref[pl.ds(ks, WIN_C), cols])
        s = s + jnp.concatenate([tab_ref[2 * pp, var], tab_ref[2 * pp + 1, var]], axis=0)
        m = jnp.max(s, axis=-1, keepdims=True)
        p = jnp.exp(s - m)
        l = jnp.sum(p, axis=-1, keepdims=True)
        pv = jnp.dot(p.astype(BF16), v_ref[pl.ds(ks, WIN_C), cols], preferred_element_type=F32) / l
        o_ref[:, cols] = jnp.where(lo, pv[:TQ_C], pv[TQ_C:]).astype(o_ref.dtype)


def _attn_c(proj, tab):
    b, s, _ = proj.shape
    assert s >= WIN_C and s % TQ_C == 0
    n_grp = C_HEADS // 2 // PAIRS_C
    n_var = WIN_C // TQ_C
    wblk = PAIRS_C * LANE
    tab = tab.reshape(n_grp, 2 * PAIRS_C, n_var, TQ_C, WIN_C)
    return pl.pallas_call(
        _attn_c_kernel,
        grid=(b, n_grp, s // TQ_C),
        in_specs=[
            pl.BlockSpec((None, TQ_C, wblk), lambda b_, h, i: (b_, i, h)),
            pl.BlockSpec((None, s, wblk), lambda b_, h, i: (b_, 0, n_grp + h)),
            pl.BlockSpec((None, s, wblk), lambda b_, h, i: (b_, 0, 2 * n_grp + h)),
            pl.BlockSpec((None, 2 * PAIRS_C, n_var, TQ_C, WIN_C), lambda b_, h, i: (h, 0, 0, 0, 0)),
        ],
        out_specs=pl.BlockSpec((None, TQ_C, wblk), lambda b_, h, i: (b_, i, h)),
        out_shape=jax.ShapeDtypeStruct((b, s, D_MODEL), BF16),
        compiler_params=_cparams(("parallel", "parallel", "arbitrary")),
    )(proj, proj, proj, tab)


def _router_kernel(h_ref, g_ref, wr_ref, br_ref, xn_ref, ei_ref, cw_ref):
    xn = _rms(h_ref[...], g_ref[...])
    xn_ref[...] = xn
    logits = jnp.dot(xn, wr_ref[...], preferred_element_type=F32,
                     precision=lax.Precision.HIGHEST) + br_ref[...]
    lane = lax.broadcasted_iota(I32, (1, LANE), 1)
    lane_f = lane.astype(F32)
    big = float(LANE)
    is_g = lane < N_GROUPS
    lg = jnp.where(is_g, logits, -jnp.inf)
    gmax = jnp.max(lg, axis=-1, keepdims=True)
    g_sel = jnp.min(jnp.where(lg == gmax, lane_f, big), axis=-1, keepdims=True)
    p_gsel = 1.0 / jnp.sum(jnp.where(is_g, jnp.exp(logits - gmax), 0.0), axis=-1, keepdims=True)
    e_grp = ((lane - N_GROUPS) // EXPERTS_PER_GROUP).astype(F32)
    in_sel = (lane >= N_GROUPS) & (lane < N_GROUPS + N_EXPERTS) & (e_grp == g_sel)
    le = jnp.where(in_sel, logits, -jnp.inf)
    m1 = jnp.max(le, axis=-1, keepdims=True)
    i1 = jnp.min(jnp.where(le == m1, lane_f, big), axis=-1, keepdims=True)
    le2 = jnp.where(lane_f == i1, -jnp.inf, le)
    m2 = jnp.max(le2, axis=-1, keepdims=True)
    i2 = jnp.min(jnp.where(le2 == m2, lane_f, big), axis=-1, keepdims=True)
    t = jnp.exp(m2 - m1)
    c1 = p_gsel / (1.0 + t)
    c2 = p_gsel * t / (1.0 + t)
    ei_ref[...] = jnp.where(lane == 0, i1, i2).astype(I32) - N_GROUPS
    cw_ref[...] = jnp.where(lane == 0, c1, c2)


def _router(h, g, w_rg, b_rg, w_re, b_re, tm=256):
    n, d = h.shape
    pad = LANE - N_GROUPS - N_EXPERTS
    wr = jnp.concatenate([w_rg, w_re, jnp.zeros((d, pad), F32)], axis=1)
    br = jnp.concatenate([b_rg, b_re, jnp.zeros((pad,), F32)]).reshape(1, LANE)
    return pl.pallas_call(
        _router_kernel,
        grid=(n // tm,),
        in_specs=[pl.BlockSpec((tm, d), lambda i: (i, 0)),
                  pl.BlockSpec((1, d), lambda i: (0, 0)),
                  pl.BlockSpec((d, LANE), lambda i: (0, 0)),
                  pl.BlockSpec((1, LANE), lambda i: (0, 0))],
        out_specs=[pl.BlockSpec((tm, d), lambda i: (i, 0)),
                   pl.BlockSpec((tm, LANE), lambda i: (i, 0)),
                   pl.BlockSpec((tm, LANE), lambda i: (i, 0))],
        out_shape=[jax.ShapeDtypeStruct((n, d), F32),
                   jax.ShapeDtypeStruct((n, LANE), I32),
                   jax.ShapeDtypeStruct((n, LANE), F32)],
        compiler_params=_cparams(("parallel",)),
    )(h, g.reshape(1, d), wr, br)


def _moe_schedule(ei, n_tok, tm):
    ids = jnp.concatenate([ei[:, 0], ei[:, 1]])
    n_asg = 2 * n_tok
    order = jnp.argsort(ids, stable=True).astype(I32)
    counts = jnp.sum((ids[:, None] == jnp.arange(N_EXPERTS, dtype=I32)[None, :]).astype(I32), axis=0)
    tiles_per = (counts + tm - 1) // tm
    tile_end = jnp.cumsum(tiles_per)
    tile_start = tile_end - tiles_per
    row_start = jnp.cumsum(counts) - counts
    n_tiles = n_asg // tm + N_EXPERTS
    t = jnp.arange(n_tiles, dtype=I32)
    te = jnp.searchsorted(tile_end, t, side="right").astype(I32)
    live = te < N_EXPERTS
    te = jnp.minimum(te, N_EXPERTS - 1)
    rank0 = (t - tile_start[te]) * tm
    nv = jnp.where(live, jnp.clip(counts[te] - rank0, 0, tm), 0).astype(I32)
    r = jnp.arange(tm, dtype=I32)[None, :]
    pos = jnp.clip(row_start[te][:, None] + rank0[:, None] + r, 0, n_asg - 1)
    asg = order[pos]
    valid = r < nv[:, None]
    src = jnp.where(valid, asg % n_tok, 0).astype(I32)
    n_pad = n_tiles * tm - n_asg
    pad_before = jnp.cumsum(tm - nv) - (tm - nv)
    dst = jnp.where(valid, asg, n_asg + pad_before[:, None] + (r - nv[:, None])).astype(I32)
    prime = (n_asg + n_pad + jnp.arange(2 * tm, dtype=I32)).reshape(2, tm)
    src_ext = jnp.concatenate([src, src[-1:]], axis=0)
    dst_ext = jnp.concatenate([prime[:1], dst, prime[1:]], axis=0)
    return te, src_ext.reshape(n_tiles + 1, 1, tm), dst_ext.reshape(n_tiles + 2, 1, tm)


def _moe_kernel(te_ref, src_cur, src_next, dst_prev, dst_cur, dst_prime, x_hbm, wg_ref, wu_ref, wd_ref, o_hbm,
                xbuf, ybuf, gsem, ssem, *, tm, n_tiles):
    t = pl.program_id(0)
    slot = t % 2
    other = 1 - slot

    def gather(idx_ref, r, sl):
        return pltpu.make_async_copy(x_hbm.at[pl.ds(idx_ref[0, r], 1)], xbuf.at[sl, pl.ds(r, 1)], gsem.at[sl])

    def scatter(idx_ref, r, sl):
        return pltpu.make_async_copy(ybuf.at[sl, pl.ds(r, 1)], o_hbm.at[pl.ds(idx_ref[0, r], 1)], ssem.at[sl])

    def wait_gather(sl):
        pltpu.make_async_copy(x_hbm.at[pl.ds(0, tm)], xbuf.at[sl], gsem.at[sl]).wait()

    def wait_scatter(sl):
        pltpu.make_async_copy(ybuf.at[sl], o_hbm.at[pl.ds(0, tm)], ssem.at[sl]).wait()

    def each(fn):
        def body(r, c):
            fn(r)
            return c
        lax.fori_loop(0, tm, body, 0)

    @pl.when(t == 0)
    def _():
        ybuf[...] = jnp.zeros(ybuf.shape, F32)
        each(lambda r: gather(src_cur, r, 0).start())
        each(lambda r: scatter(dst_prime, r, 0).start())

    wait_gather(slot)
    wait_scatter(slot)
    for r in range(tm):
        gather(src_next, r, other).start()
    for r in range(tm):
        scatter(dst_prev, r, other).start()
    x = xbuf[slot].astype(BF16)
    g = jnp.dot(x, wg_ref[...], preferred_element_type=F32)
    u = jnp.dot(x, wu_ref[...], preferred_element_type=F32)
    act = (g * jax.nn.sigmoid(g)) * u
    ybuf[slot] = jnp.dot(act.astype(BF16), wd_ref[...], preferred_element_type=F32)

    @pl.when(t == n_tiles - 1)
    def _():
        each(lambda r: scatter(dst_cur, r, slot).start())
        wait_scatter(other)
        wait_scatter(slot)
        wait_gather(other)


def _moe_experts(xn, sched, wg, wu, wd, tm=TM_MOE):
    n, d = xn.shape
    te, src, dst = sched
    n_tiles = te.shape[0]

    def idx_spec(fn):
        return pl.BlockSpec((None, 1, tm), lambda t, te_: (fn(t), 0, 0), memory_space=pltpu.SMEM)

    return pl.pallas_call(
        functools.partial(_moe_kernel, tm=tm, n_tiles=n_tiles),
        grid_spec=pltpu.PrefetchScalarGridSpec(
            num_scalar_prefetch=1,
            grid=(n_tiles,),
            in_specs=[
                idx_spec(lambda t: t), idx_spec(lambda t: t + 1),
                idx_spec(lambda t: t), idx_spec(lambda t: t + 1), idx_spec(lambda t: n_tiles + 1),
                pl.BlockSpec(memory_space=pl.ANY),
                pl.BlockSpec((None, d, D_EXPERT), lambda t, te_: (te_[t], 0, 0)),
                pl.BlockSpec((None, d, D_EXPERT), lambda t, te_: (te_[t], 0, 0)),
                pl.BlockSpec((None, D_EXPERT, d), lambda t, te_: (te_[t], 0, 0)),
            ],
            out_specs=pl.BlockSpec(memory_space=pl.ANY),
            scratch_shapes=[pltpu.VMEM((2, tm, d), F32), pltpu.VMEM((2, tm, d), F32),
                            pltpu.SemaphoreType.DMA((2,)), pltpu.SemaphoreType.DMA((2,))],
        ),
        out_shape=jax.ShapeDtypeStruct(((n_tiles + 2) * tm, d), F32),
        compiler_params=_cparams(("arbitrary",)),
    )(te, src, src, dst, dst, dst, xn, wg, wu, wd)


def _combine_kernel(h_ref, y0_ref, y1_ref, cw_ref, g_ref, o_ref, *, final):
    cw = cw_ref[...]
    out = h_ref[...] + cw[:, 0:1] * y0_ref[...] + cw[:, 1:2] * y1_ref[...]
    if final:
        out = _rms(out, g_ref[...])
    o_ref[...] = out


def _combine(h, y, cw, g, final, tm=512):
    n, d = h.shape
    nb = n // tm
    return pl.pallas_call(
        functools.partial(_combine_kernel, final=final),
        grid=(nb,),
        in_specs=[pl.BlockSpec((tm, d), lambda i: (i, 0)),
                  pl.BlockSpec((tm, d), lambda i: (i, 0)),
                  pl.BlockSpec((tm, d), lambda i: (nb + i, 0)),
                  pl.BlockSpec((tm, LANE), lambda i: (i, 0)),
                  pl.BlockSpec((1, d), lambda i: (0, 0))],
        out_specs=pl.BlockSpec((tm, d), lambda i: (i, 0)),
        out_shape=jax.ShapeDtypeStruct((n, d), F32),
        compiler_params=_cparams(("parallel",)),
    )(h, y, y, cw, g.reshape(1, d))


def _moe_layer(h, ln_g, w_rg, b_rg, w_re, b_re, w_gate, w_up, w_down, final_g, final):
    n = h.shape[0]
    xn, ei, cw = _router(h, ln_g, w_rg, b_rg, w_re, b_re)
    sched = _moe_schedule(ei, n, TM_MOE)
    y = _moe_experts(xn, sched, w_gate.astype(BF16), w_up.astype(BF16), w_down.astype(BF16))
    return _combine(h, y, cw, final_g, final)


def _ab_weights(w_in):
    scale = HEAD_DIM ** -0.5
    widths = (512, 512, 512, 512, 64, 64, 256, 64, 4)
    offs = [sum(widths[:j]) for j in range(len(widths) + 1)]
    q_a, k_a, v_a, q_b, k_b, v_b, q_i, k_i, w_i = [w_in[:, offs[j]:offs[j + 1]] for j in range(len(widths))]
    pad = jnp.zeros((w_in.shape[0], LANE - IDX_HEADS), w_in.dtype)
    main = jnp.concatenate([q_a * scale, k_a, v_a, q_b * scale, k_b, k_b, v_b, v_b, q_i, k_i, k_i], axis=1)
    return jnp.concatenate([main, w_i, pad], axis=1).astype(BF16)


def kernel(x, t5_bias, ln_mix, ln_ffn, ln_final, ab_w_in, ab_w_out, diff_lambda, diff_subln,
           c_w_in, c_w_out, c_rel_bias, moe_w_rg, moe_b_rg, moe_w_re, moe_b_re,
           moe_w_gate, moe_w_up, moe_w_down):
    b, s, d = x.shape
    n = b * s
    depth = ln_mix.shape[0]
    h = x.reshape(n, d)
    near, far = _t5_tables(t5_bias)
    for l in range(depth):
        if l % 2 == 0:
            e = l // 2
            lam_init = 0.8 - 0.6 * math.exp(-0.3 * l)
            proj, wi = _norm_proj(h, ln_mix[l], _ab_weights(ab_w_in[e]), AB_COLS)
            proj = proj.reshape(b, s, AB_COLS)
            y_a = _attn_a(proj, near[:A_HEADS], far, diff_lambda[e], diff_subln[e], lam_init)
            y_b = _attn_b(proj, wi.reshape(b, s, LANE), near[A_HEADS:], far)
            h = _out_proj(h, y_a.reshape(n, -1), 0, y_b.reshape(n, -1), 0, ab_w_out[e].astype(BF16))
        else:
            o = l // 2
            scale = HEAD_DIM ** -0.5
            w_in = jnp.concatenate([c_w_in[o][:, :d] * scale, c_w_in[o][:, d:]], axis=1).astype(BF16)
            (proj,) = _norm_proj(h, ln_mix[l], w_in, 3 * d)
            y = _attn_c(proj.reshape(b, s, 3 * d), _c_tables(c_rel_bias[o])).reshape(n, d)
            h = _out_proj(h, y, 0, y, 1, c_w_out[o].astype(BF16))
        h = _moe_layer(h, ln_ffn[l], moe_w_rg[l], moe_b_rg[l], moe_w_re[l], moe_b_re[l],
                       moe_w_gate[l], moe_w_up[l], moe_w_down[l], ln_final, l == depth - 1)
    return h.reshape(b, s, d)
```

```python
import functools
import math

import jax
import jax.numpy as jnp
from jax import lax
from jax.experimental import pallas as pl
from jax.experimental.pallas import tpu as pltpu

F32 = jnp.float32
BF16 = jnp.bfloat16
I32 = jnp.int32

D_MODEL = 1024
CHUNK = 64
HEAD_DIM = 64
RMS_EPS = 1e-6
A_HEADS = 4
A_V_DIM = 2 * HEAD_DIM
B_HEADS = 8
IDX_HEADS = 4
TOPK_MAX = 256
C_HEADS = D_MODEL // HEAD_DIM
C_LEFT_CHUNKS = 8
C_REL_CLIP = 128
T5_BUCKETS = 32
T5_MAX_DIST = 1024
N_GROUPS = 4
EXPERTS_PER_GROUP = 8
N_EXPERTS = N_GROUPS * EXPERTS_PER_GROUP
D_EXPERT = D_MODEL // 2
NEG = -1e30

LANE = 128
VMEM_LIMIT = 56 * 1024 * 1024

TQ = 256
NEAR_TILES = 4
TQ_C = 2 * CHUNK
WIN_C = (C_LEFT_CHUNKS + 2) * CHUNK
TM_MOE = 256

COL_QA, COL_KA, COL_VA, COL_QB, COL_KB, COL_VB, COL_QI, COL_KI, COL_WI = (
    0, 512, 1024, 1536, 2048, 2176, 2304, 2560, 2688)
AB_COLS = 2688


def _cparams(sem, vmem=VMEM_LIMIT):
    return pltpu.CompilerParams(dimension_semantics=sem, vmem_limit_bytes=vmem)


def _dot_nt(a, b):
    return lax.dot_general(a, b, (((1,), (1,)), ((), ())), preferred_element_type=F32)


def _rms(x, g):
    return (x * lax.rsqrt(jnp.mean(x * x, axis=-1, keepdims=True) + RMS_EPS)) * g


def _norm_proj_kernel(x_ref, g_ref, w_ref, o_ref, *tail_ref, n_main):
    xn = _rms(x_ref[...], g_ref[...]).astype(BF16)
    acc = jnp.dot(xn, w_ref[...], preferred_element_type=F32)
    o_ref[...] = acc[:, :n_main].astype(o_ref.dtype)
    if tail_ref:
        tail_ref[0][...] = acc[:, n_main:]


def _norm_proj(x, g, w, n_main, tm=256):
    n, d = x.shape
    n_all = w.shape[1]
    out_shape = [jax.ShapeDtypeStruct((n, n_main), BF16)]
    out_specs = [pl.BlockSpec((tm, n_main), lambda i: (i, 0))]
    if n_all > n_main:
        out_shape.append(jax.ShapeDtypeStruct((n, n_all - n_main), F32))
        out_specs.append(pl.BlockSpec((tm, n_all - n_main), lambda i: (i, 0)))
    return pl.pallas_call(
        functools.partial(_norm_proj_kernel, n_main=n_main),
        grid=(n // tm,),
        in_specs=[pl.BlockSpec((tm, d), lambda i: (i, 0)),
                  pl.BlockSpec((1, d), lambda i: (0, 0)),
                  pl.BlockSpec((d, n_all), lambda i: (0, 0))],
        out_specs=out_specs, out_shape=out_shape,
        compiler_params=_cparams(("parallel",)),
    )(x, g.reshape(1, d), w)


def _out_proj_kernel(h_ref, y0_ref, y1_ref, w0_ref, w1_ref, o_ref):
    acc = jnp.dot(y0_ref[...], w0_ref[...], preferred_element_type=F32)
    acc = acc + jnp.dot(y1_ref[...], w1_ref[...], preferred_element_type=F32)
    o_ref[...] = h_ref[...] + acc


def _out_proj(h, y0, c0, y1, c1, w, tm=512):
    n, d = h.shape
    half = d // 2
    return pl.pallas_call(
        _out_proj_kernel,
        grid=(n // tm,),
        in_specs=[pl.BlockSpec((tm, d), lambda i: (i, 0)),
                  pl.BlockSpec((tm, half), lambda i: (i, c0)),
                  pl.BlockSpec((tm, half), lambda i: (i, c1)),
                  pl.BlockSpec((half, d), lambda i: (0, 0)),
                  pl.BlockSpec((half, d), lambda i: (1, 0))],
        out_specs=pl.BlockSpec((tm, d), lambda i: (i, 0)),
        out_shape=jax.ShapeDtypeStruct((n, d), F32),
        compiler_params=_cparams(("parallel",)),
    )(h, y0, y1, w, w)


def _t5_bucket(rel):
    half = T5_BUCKETS // 2
    max_exact = half // 2
    ret = jnp.where(rel > 0, half, 0)
    n = jnp.abs(rel)
    nf = jnp.maximum(n, 1).astype(F32)
    large = max_exact + (jnp.log(nf / max_exact) / math.log(T5_MAX_DIST / max_exact)
                         * (half - max_exact)).astype(I32)
    large = jnp.minimum(large, half - 1)
    return (ret + jnp.where(n < max_exact, n, large)).astype(I32)


def _toeplitz(g, rows, width):
    length = g.shape[-1]
    assert width < length and rows <= length
    flat = jnp.tile(g, (1,) * (g.ndim - 1) + (rows,))[..., :rows * (length - 1)]
    return flat.reshape(g.shape[:-1] + (rows, length - 1))[..., :width]


def _t5_tables(t5_bias):
    span = NEAR_TILES * TQ
    length = span + TQ
    k = jnp.arange(length, dtype=I32)
    delta = jnp.where(k < span, k, k - length) - (span - TQ)
    g = t5_bias[_t5_bucket(delta)].T
    bias = _toeplitz(g, TQ, span)
    r = jnp.arange(TQ, dtype=I32)[:, None]
    kr = jnp.arange(span, dtype=I32)[None, :] - (span - TQ)
    ok = jnp.floor_divide(kr, CHUNK) <= (r // CHUNK)
    bias = jnp.where(ok[None], bias, NEG)
    near = bias.reshape(-1, TQ, NEAR_TILES, TQ).transpose(0, 2, 1, 3)
    far = t5_bias[_t5_bucket(jnp.asarray(-(span - TQ) - 1, I32))]
    return near, far


N_GRP = 4
STACK = 2 * TQ


def _attend(i, qs_ref, k_fn, v_fn, mask_fn, near_fn, far_cols, mf_ref, mn_ref, l_ref, acc_ref):
    n_far = jnp.maximum(i - (NEAR_TILES - 1), 0)
    mf_ref[...] = jnp.full(mf_ref.shape, NEG, F32)
    mn_ref[...] = jnp.full(mn_ref.shape, NEG, F32)

    def rows(g):
        return slice(g * STACK, (g + 1) * STACK)

    def logits(g, kt, jt):
        off = pl.multiple_of(kt * TQ, TQ)
        s = _dot_nt(qs_ref[rows(g), :], k_fn(g, off))
        extra = None if mask_fn is None else mask_fn(kt)[None]
        if jt is not None:
            extra = near_fn(g, jt) if extra is None else extra + near_fn(g, jt)
        if extra is not None:
            s = (s.reshape(2, TQ, TQ) + extra).reshape(STACK, TQ)
        return s

    def loops(far_fn, near_fn_):
        def far_body(kt, c):
            far_fn(kt)
            return c

        def near_body(kt, c):
            near_fn_(kt)
            return c

        lax.fori_loop(0, n_far, far_body, 0)
        lax.fori_loop(n_far, i + 1, near_body, 0)

    def pass1(m_ref, near):
        def fn(kt):
            jt = kt - (i - (NEAR_TILES - 1)) if near else None
            for g in range(N_GRP):
                s = logits(g, kt, jt)
                m_ref[rows(g), :] = jnp.maximum(m_ref[rows(g), :], jnp.maximum(s[:, :LANE], s[:, LANE:]))
        return fn

    loops(pass1(mf_ref, False), pass1(mn_ref, True))

    for g in range(N_GRP):
        m = jnp.maximum(jnp.max(mf_ref[rows(g), :], axis=-1, keepdims=True) + far_cols[g],
                        jnp.max(mn_ref[rows(g), :], axis=-1, keepdims=True))
        m_near = jnp.broadcast_to(m, (STACK, LANE))
        mn_ref[rows(g), :] = m_near
        mf_ref[rows(g), :] = m_near - far_cols[g]
    l_ref[...] = jnp.zeros(l_ref.shape, F32)
    acc_ref[...] = jnp.zeros(acc_ref.shape, F32)

    def pass2(shift_ref, near):
        def fn(kt):
            jt = kt - (i - (NEAR_TILES - 1)) if near else None
            off = pl.multiple_of(kt * TQ, TQ)
            for g in range(N_GRP):
                s = logits(g, kt, jt)
                sh = shift_ref[rows(g), :]
                p0 = jnp.exp(s[:, :LANE] - sh)
                p1 = jnp.exp(s[:, LANE:] - sh)
                l_ref[rows(g), :] += p0 + p1
                p = jnp.concatenate([p0, p1], axis=1).astype(BF16)
                acc_ref[rows(g), :] += jnp.dot(p, v_fn(g, off), preferred_element_type=F32)
        return fn

    loops(pass2(mf_ref, False), pass2(mn_ref, True))
    return [acc_ref[rows(g), :] / jnp.sum(l_ref[rows(g), :], axis=-1, keepdims=True) for g in range(N_GRP)]


def _attend_scratch():
    return [pltpu.VMEM((N_GRP * STACK, LANE), BF16)] + [pltpu.VMEM((N_GRP * STACK, LANE), F32)] * 4


def _split_lanes(x):
    lo = lax.broadcasted_iota(I32, (1, LANE), 1) < HEAD_DIM
    z = jnp.zeros_like(x)
    return jnp.concatenate([jnp.where(lo, x, z), jnp.where(lo, z, x)], axis=0)


def _attn_a_kernel(far_ref, q_ref, k_ref, v_ref, tab_ref, lam_ref, sub_ref, o_ref,
                   qs_ref, mf_ref, mn_ref, l_ref, acc_ref, *, lam_init):
    i = pl.program_id(1)
    for g in range(A_HEADS):
        qs_ref[g * STACK:(g + 1) * STACK, :] = _split_lanes(q_ref[:, g * LANE:(g + 1) * LANE])
    outs = _attend(
        i, qs_ref,
        lambda g, off: k_ref[pl.ds(off, TQ), g * LANE:(g + 1) * LANE],
        lambda g, off: v_ref[pl.ds(off, TQ), g * LANE:(g + 1) * LANE],
        None,
        lambda g, jt: tab_ref[g, jt][None],
        [jnp.full((STACK, 1), far_ref[g], F32) for g in range(A_HEADS)],
        mf_ref, mn_ref, l_ref, acc_ref)
    lam = lam_ref[...]
    lam_full = (jnp.exp(jnp.sum(lam[0:1] * lam[1:2], axis=-1, keepdims=True))
                - jnp.exp(jnp.sum(lam[2:3] * lam[3:4], axis=-1, keepdims=True)) + lam_init)
    for g, o in enumerate(outs):
        d = o[:TQ] - lam_full * o[TQ:]
        o_ref[:, g * LANE:(g + 1) * LANE] = (_rms(d, sub_ref[...]) * (1.0 - lam_init)).astype(o_ref.dtype)


def _attn_a(proj, near, far, lam, subln, lam_init):
    b, s, _ = proj.shape
    assert A_HEADS == N_GRP
    wide = A_HEADS * A_V_DIM
    return pl.pallas_call(
        functools.partial(_attn_a_kernel, lam_init=lam_init),
        grid_spec=pltpu.PrefetchScalarGridSpec(
            num_scalar_prefetch=1,
            grid=(b, s // TQ),
            in_specs=[
                pl.BlockSpec((None, TQ, wide), lambda b_, i, f: (b_, i, COL_QA // wide)),
                pl.BlockSpec((None, s, wide), lambda b_, i, f: (b_, 0, COL_KA // wide)),
                pl.BlockSpec((None, s, wide), lambda b_, i, f: (b_, 0, COL_VA // wide)),
                pl.BlockSpec((A_HEADS, NEAR_TILES, TQ, TQ), lambda b_, i, f: (0, 0, 0, 0)),
                pl.BlockSpec((4, HEAD_DIM), lambda b_, i, f: (0, 0)),
                pl.BlockSpec((1, A_V_DIM), lambda b_, i, f: (0, 0)),
            ],
            out_specs=pl.BlockSpec((None, TQ, wide), lambda b_, i, f: (b_, i, 0)),
            scratch_shapes=_attend_scratch(),
        ),
        out_shape=jax.ShapeDtypeStruct((b, s, wide), BF16),
        compiler_params=_cparams(("parallel", "arbitrary")),
    )(far, proj, proj, proj, near, lam, subln.reshape(1, A_V_DIM))


def _attn_b_kernel(far_ref, qb_ref, kb_ref, vb_ref, qi_ref, ki_ref, wi_ref, tab_ref, o_ref,
                   keys_ref, selm_ref, cand_ref, tu_ref, trial_ref, cnt_ref, qs_ref, mf_ref, mn_ref, l_ref, acc_ref,
                   *, top_k, idx_bits):
    i = pl.program_id(1)
    n_kt = i + 1
    lane = lax.broadcasted_iota(I32, (1, LANE), 1)
    lo = lane < HEAD_DIM
    int_min = jnp.int32(-2 ** 31)
    split_heads = _split_lanes

    row = lax.broadcasted_iota(I32, (TQ, TQ), 0)
    col = lax.broadcasted_iota(I32, (TQ, TQ), 1)
    chunk_gap = (col // CHUNK) - (row // CHUNK)

    def causal(kt):
        return chunk_gap <= jnp.where(kt < i, jnp.int32(TQ), jnp.int32(0))

    qi = qi_ref[...]
    qis = jnp.concatenate([split_heads(qi[:, :LANE]), split_heads(qi[:, LANE:])], axis=0)
    wi = wi_ref[...] * ((IDX_HEADS ** -0.5) * (HEAD_DIM ** -0.5))

    def score_body(kt, c):
        off = pl.multiple_of(kt * TQ, TQ)
        r = jnp.maximum(_dot_nt(qis, ki_ref[pl.ds(off, TQ), :]), 0.0)
        isc = r[0:TQ] * wi[:, 0:1]
        for hh in range(1, IDX_HEADS):
            isc = isc + r[hh * TQ:(hh + 1) * TQ] * wi[:, hh:hh + 1]
        isc = jnp.where(isc == 0.0, 0.0, isc)
        isc = jnp.where(causal(kt), isc, -jnp.inf)
        bits = lax.bitcast_convert_type(isc, I32)
        keys_ref[kt] = bits ^ ((bits >> 31) & jnp.int32(0x7FFFFFFF))
        return c

    lax.fori_loop(0, n_kt, score_body, 0)

    ones = jnp.ones((LANE, LANE), BF16)
    halves = (slice(0, LANE), slice(LANE, TQ))

    def count(pred_fn):
        cnt_ref[...] = jnp.zeros(cnt_ref.shape, F32)

        def body(kt, c):
            kk = keys_ref[kt]
            cnt_ref[...] += sum(jnp.where(pred_fn(kt, kk[:, hs], hs), 1.0, 0.0) for hs in halves)
            return c

        lax.fori_loop(0, n_kt, body, 0)
        return jnp.dot(cnt_ref[...].astype(BF16), ones, preferred_element_type=F32)

    tu_ref[...] = jnp.zeros(tu_ref.shape, I32)

    def thr_body(bi, c):
        cand_u = tu_ref[...] | (jnp.int32(1) << (31 - bi))
        cand_ref[...] = cand_u ^ int_min
        cnt = count(lambda kt, kk, hs: kk >= cand_ref[...])
        tu_ref[...] = jnp.where(cnt >= top_k, cand_u, tu_ref[...])
        return c

    lax.fori_loop(0, 32, thr_body, 0)
    tu_ref[...] = tu_ref[...] ^ int_min
    n_ge = count(lambda kt, kk, hs: kk >= tu_ref[...])
    n_gt = count(lambda kt, kk, hs: kk > tu_ref[...])

    def key_index(kt, hs):
        return lax.broadcasted_iota(I32, (TQ, LANE), 1) + (kt * TQ + hs.start)

    cand_ref[...] = jnp.full(cand_ref.shape, 2 ** idx_bits - 1, I32)

    @pl.when(jnp.max(n_ge) > top_k)
    def _():
        cnt_hi = top_k - n_gt
        cand_ref[...] = jnp.zeros(cand_ref.shape, I32)

        def tie_body(bi, c):
            trial_ref[...] = cand_ref[...] | (jnp.int32(1) << (idx_bits - 1 - bi))
            cnt = count(lambda kt, kk, hs: (kk == tu_ref[...]) & (key_index(kt, hs) <= trial_ref[...]))
            cand_ref[...] = jnp.where(cnt <= cnt_hi, trial_ref[...], cand_ref[...])
            return c

        lax.fori_loop(0, idx_bits, tie_body, 0)

    def mask_body(kt, c):
        kk = keys_ref[kt]
        lim = jnp.where(kt < i, jnp.int32(TQ), jnp.int32(0))
        for hs in halves:
            k_ = kk[:, hs]
            sel = (k_ > tu_ref[...]) | ((k_ == tu_ref[...]) & (key_index(kt, hs) <= cand_ref[...]))
            gap = ((lax.broadcasted_iota(I32, (TQ, LANE), 1) + hs.start) // CHUNK
                   - lax.broadcasted_iota(I32, (TQ, LANE), 0) // CHUNK)
            selm_ref[kt, :, hs] = jnp.where(sel & (gap <= lim), 0.0, NEG)
        return c

    lax.fori_loop(0, n_kt, mask_body, 0)

    for g in range(N_GRP):
        qs_ref[g * STACK:(g + 1) * STACK, :] = split_heads(qb_ref[:, g * LANE:(g + 1) * LANE])
    top_rows = lax.broadcasted_iota(I32, (STACK, 1), 0) < TQ
    outs = _attend(
        i, qs_ref,
        lambda g, off: kb_ref[pl.ds(off, TQ), :],
        lambda g, off: vb_ref[pl.ds(off, TQ), :],
        lambda kt: selm_ref[kt],
        lambda g, jt: tab_ref[2 * g:2 * g + 2, jt],
        [jnp.where(top_rows, far_ref[A_HEADS + 2 * g], far_ref[A_HEADS + 2 * g + 1]) for g in range(N_GRP)],
        mf_ref, mn_ref, l_ref, acc_ref)
    for g, o in enumerate(outs):
        o_ref[:, g * LANE:(g + 1) * LANE] = jnp.where(lo, o[:TQ], o[TQ:]).astype(o_ref.dtype)


def _attn_b(proj, wi, near_b, far):
    b, s, _ = proj.shape
    top_k = min(TOPK_MAX, s // 4)
    assert top_k <= TQ and s % TQ == 0 and (s & (s - 1)) == 0
    n_kt = s // TQ
    assert 2 * n_kt <= 256 and B_HEADS == 2 * N_GRP
    qb_w = B_HEADS * HEAD_DIM
    qi_w = IDX_HEADS * HEAD_DIM
    return pl.pallas_call(
        functools.partial(_attn_b_kernel, top_k=float(top_k), idx_bits=int(math.log2(s))),
        grid_spec=pltpu.PrefetchScalarGridSpec(
            num_scalar_prefetch=1,
            grid=(b, n_kt),
            in_specs=[
                pl.BlockSpec((None, TQ, qb_w), lambda b_, i, f: (b_, i, COL_QB // qb_w)),
                pl.BlockSpec((None, s, LANE), lambda b_, i, f: (b_, 0, COL_KB // LANE)),
                pl.BlockSpec((None, s, LANE), lambda b_, i, f: (b_, 0, COL_VB // LANE)),
                pl.BlockSpec((None, TQ, qi_w), lambda b_, i, f: (b_, i, COL_QI // qi_w)),
                pl.BlockSpec((None, s, LANE), lambda b_, i, f: (b_, 0, COL_KI // LANE)),
                pl.BlockSpec((None, TQ, LANE), lambda b_, i, f: (b_, i, 0)),
                pl.BlockSpec((B_HEADS, NEAR_TILES, TQ, TQ), lambda b_, i, f: (0, 0, 0, 0)),
            ],
            out_specs=pl.BlockSpec((None, TQ, qb_w), lambda b_, i, f: (b_, i, 0)),
            scratch_shapes=[pltpu.VMEM((n_kt, TQ, TQ), I32), pltpu.VMEM((n_kt, TQ, TQ), F32)]
            + [pltpu.VMEM((TQ, LANE), I32)] * 3 + [pltpu.VMEM((TQ, LANE), F32)] + _attend_scratch(),
        ),
        out_shape=jax.ShapeDtypeStruct((b, s, qb_w), BF16),
        compiler_params=_cparams(("parallel", "arbitrary")),
    )(far, proj, proj, proj, proj, proj, wi, near_b)


PAIRS_C = 2


def _c_tables(rel_table):
    n_var = WIN_C // TQ_C
    length = WIN_C + TQ_C
    k = jnp.arange(length, dtype=I32)
    cr = jnp.where(k < WIN_C, k, k - length)
    off = (jnp.arange(n_var, dtype=I32) * TQ_C)[:, None]
    rel_idx = jnp.clip(off - cr[None, :], -C_REL_CLIP, C_REL_CLIP) + C_REL_CLIP
    g = rel_table[rel_idx].transpose(2, 0, 1)
    bias = _toeplitz(g, TQ_C, WIN_C)
    r = jnp.arange(TQ_C, dtype=I32)[None, :, None]
    c = jnp.arange(WIN_C, dtype=I32)[None, None, :]
    kc = jnp.floor_divide(c - off[:, :, None], CHUNK)
    ok = (kc <= r // CHUNK) & (kc >= r // CHUNK - C_LEFT_CHUNKS)
    return jnp.where(ok[None], bias, NEG)


def _attn_c_kernel(q_ref, k_ref, v_ref, tab_ref, o_ref):
    i = pl.program_id(2)
    n_var = WIN_C // TQ_C
    lane = lax.broadcasted_iota(I32, (1, LANE), 1)
    lo = lane < HEAD_DIM
    var = jnp.minimum(i, n_var - 1)
    ks = pl.multiple_of(jnp.maximum(i - (n_var - 1), 0) * TQ_C, TQ_C)
    for pp in range(PAIRS_C):
        cols = slice(pp * LANE, (pp + 1) * LANE)
        q = q_ref[:, cols]
        zq = jnp.zeros_like(q)
        qs = jnp.concatenate([jnp.where(lo, q, zq), jnp.where(lo, zq, q)], axis=0)
        s = _dot_nt(qs, k_ref[pl.ds(ks, WIN_C), cols])
        s = s + jnp.concatenate([tab_ref[2 * pp, var], tab_ref[2 * pp + 1, var]], axis=0)
        m = jnp.max(s, axis=-1, keepdims=True)
        p = jnp.exp(s - m)
        l = jnp.sum(p, axis=-1, keepdims=True)
        pv = jnp.dot(p.astype(BF16), v_ref[pl.ds(ks, WIN_C), cols], preferred_element_type=F32) / l
        o_ref[:, cols] = jnp.where(lo, pv[:TQ_C], pv[TQ_C:]).astype(o_ref.dtype)


def _attn_c(proj, tab):
    b, s, _ = proj.shape
    assert s >= WIN_C and s % TQ_C == 0
    n_grp = C_HEADS // 2 // PAIRS_C
    n_var = WIN_C // TQ_C
    wblk = PAIRS_C * LANE
    tab = tab.reshape(n_grp, 2 * PAIRS_C, n_var, TQ_C, WIN_C)
    return pl.pallas_call(
        _attn_c_kernel,
        grid=(b, n_grp, s // TQ_C),
        in_specs=[
            pl.BlockSpec((None, TQ_C, wblk), lambda b_, h, i: (b_, i, h)),
            pl.BlockSpec((None, s, wblk), lambda b_, h, i: (b_, 0, n_grp + h)),
            pl.BlockSpec((None, s, wblk), lambda b_, h, i: (b_, 0, 2 * n_grp + h)),
            pl.BlockSpec((None, 2 * PAIRS_C, n_var, TQ_C, WIN_C), lambda b_, h, i: (h, 0, 0, 0, 0)),
        ],
        out_specs=pl.BlockSpec((None, TQ_C, wblk), lambda b_, h, i: (b_, i, h)),
        out_shape=jax.ShapeDtypeStruct((b, s, D_MODEL), BF16),
        compiler_params=_cparams(("parallel", "parallel", "arbitrary")),
    )(proj, proj, proj, tab)


def _router_kernel(h_ref, g_ref, wr_ref, br_ref, xn_ref, ei_ref, cw_ref):
    xn = _rms(h_ref[...], g_ref[...])
    xn_ref[...] = xn
    logits = jnp.dot(xn, wr_ref[...], preferred_element_type=F32,
                     precision=lax.Precision.HIGHEST) + br_ref[...]
    lane = lax.broadcasted_iota(I32, (1, LANE), 1)
    lane_f = lane.astype(F32)
    big = float(LANE)
    is_g = lane < N_GROUPS
    lg = jnp.where(is_g, logits, -jnp.inf)
    gmax = jnp.max(lg, axis=-1, keepdims=True)
    g_sel = jnp.min(jnp.where(lg == gmax, lane_f, big), axis=-1, keepdims=True)
    p_gsel = 1.0 / jnp.sum(jnp.where(is_g, jnp.exp(logits - gmax), 0.0), axis=-1, keepdims=True)
    e_grp = ((lane - N_GROUPS) // EXPERTS_PER_GROUP).astype(F32)
    in_sel = (lane >= N_GROUPS) & (lane < N_GROUPS + N_EXPERTS) & (e_grp == g_sel)
    le = jnp.where(in_sel, logits, -jnp.inf)
    m1 = jnp.max(le, axis=-1, keepdims=True)
    i1 = jnp.min(jnp.where(le == m1, lane_f, big), axis=-1, keepdims=True)
    le2 = jnp.where(lane_f == i1, -jnp.inf, le)
    m2 = jnp.max(le2, axis=-1, keepdims=True)
    i2 = jnp.min(jnp.where(le2 == m2, lane_f, big), axis=-1, keepdims=True)
    t = jnp.exp(m2 - m1)
    c1 = p_gsel / (1.0 + t)
    c2 = p_gsel * t / (1.0 + t)
    ei_ref[...] = jnp.where(lane == 0, i1, i2).astype(I32) - N_GROUPS
    cw_ref[...] = jnp.where(lane == 0, c1, c2)


def _router(h, g, w_rg, b_rg, w_re, b_re, tm=256):
    n, d = h.shape
    pad = LANE - N_GROUPS - N_EXPERTS
    wr = jnp.concatenate([w_rg, w_re, jnp.zeros((d, pad), F32)], axis=1)
    br = jnp.concatenate([b_rg, b_re, jnp.zeros((pad,), F32)]).reshape(1, LANE)
    return pl.pallas_call(
        _router_kernel,
        grid=(n // tm,),
        in_specs=[pl.BlockSpec((tm, d), lambda i: (i, 0)),
                  pl.BlockSpec((1, d), lambda i: (0, 0)),
                  pl.BlockSpec((d, LANE), lambda i: (0, 0)),
                  pl.BlockSpec((1, LANE), lambda i: (0, 0))],
        out_specs=[pl.BlockSpec((tm, d), lambda i: (i, 0)),
                   pl.BlockSpec((tm, LANE), lambda i: (i, 0)),
                   pl.BlockSpec((tm, LANE), lambda i: (i, 0))],
        out_shape=[jax.ShapeDtypeStruct((n, d), F32),
                   jax.ShapeDtypeStruct((n, LANE), I32),
                   jax.ShapeDtypeStruct((n, LANE), F32)],
        compiler_params=_cparams(("parallel",)),
    )(h, g.reshape(1, d), wr, br)


def _moe_schedule(ei, n_tok, tm):
    ids = jnp.concatenate([ei[:, 0], ei[:, 1]])
    n_asg = 2 * n_tok
    order = jnp.argsort(ids, stable=True).astype(I32)
    counts = jnp.sum((ids[:, None] == jnp.arange(N_EXPERTS, dtype=I32)[None, :]).astype(I32), axis=0)
    tiles_per = (counts + tm - 1) // tm
    tile_end = jnp.cumsum(tiles_per)
    tile_start = tile_end - tiles_per
    row_start = jnp.cumsum(counts) - counts
    n_tiles = n_asg // tm + N_EXPERTS
    t = jnp.arange(n_tiles, dtype=I32)
    te = jnp.searchsorted(tile_end, t, side="right").astype(I32)
    live = te < N_EXPERTS
    te = jnp.minimum(te, N_EXPERTS - 1)
    rank0 = (t - tile_start[te]) * tm
    nv = jnp.where(live, jnp.clip(counts[te] - rank0, 0, tm), 0).astype(I32)
    r = jnp.arange(tm, dtype=I32)[None, :]
    pos = jnp.clip(row_start[te][:, None] + rank0[:, None] + r, 0, n_asg - 1)
    asg = order[pos]
    valid = r < nv[:, None]
    src = jnp.where(valid, asg % n_tok, 0).astype(I32)
    n_pad = n_tiles * tm - n_asg
    pad_before = jnp.cumsum(tm - nv) - (tm - nv)
    dst = jnp.where(valid, asg, n_asg + pad_before[:, None] + (r - nv[:, None])).astype(I32)
    prime = (n_asg + n_pad + jnp.arange(2 * tm, dtype=I32)).reshape(2, tm)
    src_ext = jnp.concatenate([src, src[-1:]], axis=0)
    dst_ext = jnp.concatenate([prime[:1], dst, prime[1:]], axis=0)
    return te, src_ext.reshape(n_tiles + 1, 1, tm), dst_ext.reshape(n_tiles + 2, 1, tm)


def _moe_kernel(te_ref, src_cur, src_next, dst_prev, dst_cur, dst_prime, x_hbm, wg_ref, wu_ref, wd_ref, o_hbm,
                xbuf, ybuf, gsem, ssem, *, tm, n_tiles):
    t = pl.program_id(0)
    slot = t % 2
    other = 1 - slot

    def gather(idx_ref, r, sl):
        return pltpu.make_async_copy(x_hbm.at[pl.ds(idx_ref[0, r], 1)], xbuf.at[sl, pl.ds(r, 1)], gsem.at[sl])

    def scatter(idx_ref, r, sl):
        return pltpu.make_async_copy(ybuf.at[sl, pl.ds(r, 1)], o_hbm.at[pl.ds(idx_ref[0, r], 1)], ssem.at[sl])

    def wait_gather(sl):
        pltpu.make_async_copy(x_hbm.at[pl.ds(0, tm)], xbuf.at[sl], gsem.at[sl]).wait()

    def wait_scatter(sl):
        pltpu.make_async_copy(ybuf.at[sl], o_hbm.at[pl.ds(0, tm)], ssem.at[sl]).wait()

    def each(fn):
        def body(r, c):
            fn(r)
            return c
        lax.fori_loop(0, tm, body, 0)

    @pl.when(t == 0)
    def _():
        ybuf[...] = jnp.zeros(ybuf.shape, F32)
        each(lambda r: gather(src_cur, r, 0).start())
        each(lambda r: scatter(dst_prime, r, 0).start())

    wait_gather(slot)
    wait_scatter(slot)
    for r in range(tm):
        gather(src_next, r, other).start(priority=r % 2)
    for r in range(tm):
        scatter(dst_prev, r, other).start(priority=r % 2)
    x = xbuf[slot].astype(BF16)
    g = jnp.dot(x, wg_ref[...], preferred_element_type=F32)
    u = jnp.dot(x, wu_ref[...], preferred_element_type=F32)
    act = (g * jax.nn.sigmoid(g)) * u
    ybuf[slot] = jnp.dot(act.astype(BF16), wd_ref[...], preferred_element_type=F32)

    @pl.when(t == n_tiles - 1)
    def _():
        each(lambda r: scatter(dst_cur, r, slot).start())
        wait_scatter(other)
        wait_scatter(slot)
        wait_gather(other)


def _moe_experts(xn, sched, wg, wu, wd, tm=TM_MOE):
    n, d = xn.shape
    te, src, dst = sched
    n_tiles = te.shape[0]

    def idx_spec(fn):
        return pl.BlockSpec((None, 1, tm), lambda t, te_: (fn(t), 0, 0), memory_space=pltpu.SMEM)

    return pl.pallas_call(
        functools.partial(_moe_kernel, tm=tm, n_tiles=n_tiles),
        grid_spec=pltpu.PrefetchScalarGridSpec(
            num_scalar_prefetch=1,
            grid=(n_tiles,),
            in_specs=[
                idx_spec(lambda t: t), idx_spec(lambda t: t + 1),
                idx_spec(lambda t: t), idx_spec(lambda t: t + 1), idx_spec(lambda t: n_tiles + 1),
                pl.BlockSpec(memory_space=pl.ANY),
                pl.BlockSpec((None, d, D_EXPERT), lambda t, te_: (te_[t], 0, 0)),
                pl.BlockSpec((None, d, D_EXPERT), lambda t, te_: (te_[t], 0, 0)),
                pl.BlockSpec((None, D_EXPERT, d), lambda t, te_: (te_[t], 0, 0)),
            ],
            out_specs=pl.BlockSpec(memory_space=pl.ANY),
            scratch_shapes=[pltpu.VMEM((2, tm, d), F32), pltpu.VMEM((2, tm, d), F32),
                            pltpu.SemaphoreType.DMA((2,)), pltpu.SemaphoreType.DMA((2,))],
        ),
        out_shape=jax.ShapeDtypeStruct(((n_tiles + 2) * tm, d), F32),
        compiler_params=_cparams(("arbitrary",)),
    )(te, src, src, dst, dst, dst, xn, wg, wu, wd)


def _combine_kernel(h_ref, y0_ref, y1_ref, cw_ref, g_ref, o_ref, *, final):
    cw = cw_ref[...]
    out = h_ref[...] + cw[:, 0:1] * y0_ref[...] + cw[:, 1:2] * y1_ref[...]
    if final:
        out = _rms(out, g_ref[...])
    o_ref[...] = out


def _combine(h, y, cw, g, final, tm=512):
    n, d = h.shape
    nb = n // tm
    return pl.pallas_call(
        functools.partial(_combine_kernel, final=final),
        grid=(nb,),
        in_specs=[pl.BlockSpec((tm, d), lambda i: (i, 0)),
                  pl.BlockSpec((tm, d), lambda i: (i, 0)),
                  pl.BlockSpec((tm, d), lambda i: (nb + i, 0)),
                  pl.BlockSpec((tm, LANE), lambda i: (i, 0)),
                  pl.BlockSpec((1, d), lambda i: (0, 0))],
        out_specs=pl.BlockSpec((tm, d), lambda i: (i, 0)),
        out_shape=jax.ShapeDtypeStruct((n, d), F32),
        compiler_params=_cparams(("parallel",)),
    )(h, y, y, cw, g.reshape(1, d))


def _moe_layer(h, ln_g, w_rg, b_rg, w_re, b_re, w_gate, w_up, w_down, final_g, final):
    n = h.shape[0]
    xn, ei, cw = _router(h, ln_g, w_rg, b_rg, w_re, b_re)
    sched = _moe_schedule(ei, n, TM_MOE)
    y = _moe_experts(xn, sched, w_gate.astype(BF16), w_up.astype(BF16), w_down.astype(BF16))
    return _combine(h, y, cw, final_g, final)


def _ab_weights(w_in):
    scale = HEAD_DIM ** -0.5
    widths = (512, 512, 512, 512, 64, 64, 256, 64, 4)
    offs = [sum(widths[:j]) for j in range(len(widths) + 1)]
    q_a, k_a, v_a, q_b, k_b, v_b, q_i, k_i, w_i = [w_in[:, offs[j]:offs[j + 1]] for j in range(len(widths))]
    pad = jnp.zeros((w_in.shape[0], LANE - IDX_HEADS), w_in.dtype)
    main = jnp.concatenate([q_a * scale, k_a, v_a, q_b * scale, k_b, k_b, v_b, v_b, q_i, k_i, k_i], axis=1)
    return jnp.concatenate([main, w_i, pad], axis=1).astype(BF16)


def kernel(x, t5_bias, ln_mix, ln_ffn, ln_final, ab_w_in, ab_w_out, diff_lambda, diff_subln,
           c_w_in, c_w_out, c_rel_bias, moe_w_rg, moe_b_rg, moe_w_re, moe_b_re,
           moe_w_gate, moe_w_up, moe_w_down):
    b, s, d = x.shape
    n = b * s
    depth = ln_mix.shape[0]
    h = x.reshape(n, d)
    near, far = _t5_tables(t5_bias)
    for l in range(depth):
        if l % 2 == 0:
            e = l // 2
            lam_init = 0.8 - 0.6 * math.exp(-0.3 * l)
            proj, wi = _norm_proj(h, ln_mix[l], _ab_weights(ab_w_in[e]), AB_COLS)
            proj = proj.reshape(b, s, AB_COLS)
            y_a = _attn_a(proj, near[:A_HEADS], far, diff_lambda[e], diff_subln[e], lam_init)
            y_b = _attn_b(proj, wi.reshape(b, s, LANE), near[A_HEADS:], far)
            h = _out_proj(h, y_a.reshape(n, -1), 0, y_b.reshape(n, -1), 0, ab_w_out[e].astype(BF16))
        else:
            o = l // 2
            scale = HEAD_DIM ** -0.5
            w_in = jnp.concatenate([c_w_in[o][:, :d] * scale, c_w_in[o][:, d:]], axis=1).astype(BF16)
            (proj,) = _norm_proj(h, ln_mix[l], w_in, 3 * d)
            y = _attn_c(proj.reshape(b, s, 3 * d), _c_tables(c_rel_bias[o])).reshape(n, d)
            h = _out_proj(h, y, 0, y, 1, c_w_out[o].astype(BF16))
        h = _moe_layer(h, ln_ffn[l], moe_w_rg[l], moe_b_rg[l], moe_w_re[l], moe_b_re[l],
                       moe_w_gate[l], moe_w_up[l], moe_w_down[l], ln_final, l == depth - 1)
    return h.reshape(b, s, d)
```

```python
import functools
import math

import jax
import jax.numpy as jnp
from jax import lax
from jax.experimental import pallas as pl
from jax.experimental.pallas import tpu as pltpu

F32 = jnp.float32
BF16 = jnp.bfloat16
I32 = jnp.int32

D_MODEL = 1024
CHUNK = 64
HEAD_DIM = 64
RMS_EPS = 1e-6
A_HEADS = 4
A_V_DIM = 2 * HEAD_DIM
B_HEADS = 8
IDX_HEADS = 4
TOPK_MAX = 256
C_HEADS = D_MODEL // HEAD_DIM
C_LEFT_CHUNKS = 8
C_REL_CLIP = 128
T5_BUCKETS = 32
T5_MAX_DIST = 1024
N_GROUPS = 4
EXPERTS_PER_GROUP = 8
N_EXPERTS = N_GROUPS * EXPERTS_PER_GROUP
D_EXPERT = D_MODEL // 2
NEG = -1e30

LANE = 128
VMEM_LIMIT = 56 * 1024 * 1024

TQ = 256
NEAR_TILES = 4
TQ_C = 2 * CHUNK
WIN_C = (C_LEFT_CHUNKS + 2) * CHUNK
TM_MOE = 256

COL_QA, COL_KA, COL_VA, COL_QB, COL_KB, COL_VB, COL_QI, COL_KI, COL_WI = (
    0, 512, 1024, 1536, 2048, 2176, 2304, 2560, 2688)
AB_COLS = 2688


def _cparams(sem, vmem=VMEM_LIMIT):
    return pltpu.CompilerParams(dimension_semantics=sem, vmem_limit_bytes=vmem)


def _dot_nt(a, b):
    return lax.dot_general(a, b, (((1,), (1,)), ((), ())), preferred_element_type=F32)


def _rms(x, g):
    return (x * lax.rsqrt(jnp.mean(x * x, axis=-1, keepdims=True) + RMS_EPS)) * g


def _norm_proj_kernel(x_ref, g_ref, w_ref, o_ref, *tail_ref, n_main):
    xn = _rms(x_ref[...], g_ref[...]).astype(BF16)
    acc = jnp.dot(xn, w_ref[...], preferred_element_type=F32)
    o_ref[...] = acc[:, :n_main].astype(o_ref.dtype)
    if tail_ref:
        tail_ref[0][...] = acc[:, n_main:]


def _norm_proj(x, g, w, n_main, tm=256):
    n, d = x.shape
    n_all = w.shape[1]
    out_shape = [jax.ShapeDtypeStruct((n, n_main), BF16)]
    out_specs = [pl.BlockSpec((tm, n_main), lambda i: (i, 0))]
    if n_all > n_main:
        out_shape.append(jax.ShapeDtypeStruct((n, n_all - n_main), F32))
        out_specs.append(pl.BlockSpec((tm, n_all - n_main), lambda i: (i, 0)))
    return pl.pallas_call(
        functools.partial(_norm_proj_kernel, n_main=n_main),
        grid=(n // tm,),
        in_specs=[pl.BlockSpec((tm, d), lambda i: (i, 0)),
                  pl.BlockSpec((1, d), lambda i: (0, 0)),
                  pl.BlockSpec((d, n_all), lambda i: (0, 0))],
        out_specs=out_specs, out_shape=out_shape,
        compiler_params=_cparams(("parallel",)),
    )(x, g.reshape(1, d), w)


def _out_proj_kernel(h_ref, y0_ref, y1_ref, w0_ref, w1_ref, o_ref):
    acc = jnp.dot(y0_ref[...], w0_ref[...], preferred_element_type=F32)
    acc = acc + jnp.dot(y1_ref[...], w1_ref[...], preferred_element_type=F32)
    o_ref[...] = h_ref[...] + acc


def _out_proj(h, y0, c0, y1, c1, w, tm=512):
    n, d = h.shape
    half = d // 2
    return pl.pallas_call(
        _out_proj_kernel,
        grid=(n // tm,),
        in_specs=[pl.BlockSpec((tm, d), lambda i: (i, 0)),
                  pl.BlockSpec((tm, half), lambda i: (i, c0)),
                  pl.BlockSpec((tm, half), lambda i: (i, c1)),
                  pl.BlockSpec((half, d), lambda i: (0, 0)),
                  pl.BlockSpec((half, d), lambda i: (1, 0))],
        out_specs=pl.BlockSpec((tm, d), lambda i: (i, 0)),
        out_shape=jax.ShapeDtypeStruct((n, d), F32),
        compiler_params=_cparams(("parallel",)),
    )(h, y0, y1, w, w)


def _t5_bucket(rel):
    half = T5_BUCKETS // 2
    max_exact = half // 2
    ret = jnp.where(rel > 0, half, 0)
    n = jnp.abs(rel)
    nf = jnp.maximum(n, 1).astype(F32)
    large = max_exact + (jnp.log(nf / max_exact) / math.log(T5_MAX_DIST / max_exact)
                         * (half - max_exact)).astype(I32)
    large = jnp.minimum(large, half - 1)
    return (ret + jnp.where(n < max_exact, n, large)).astype(I32)


def _toeplitz(g, rows, width):
    length = g.shape[-1]
    assert width < length and rows <= length
    flat = jnp.tile(g, (1,) * (g.ndim - 1) + (rows,))[..., :rows * (length - 1)]
    return flat.reshape(g.shape[:-1] + (rows, length - 1))[..., :width]


def _t5_tables(t5_bias):
    span = NEAR_TILES * TQ
    length = span + TQ
    k = jnp.arange(length, dtype=I32)
    delta = jnp.where(k < span, k, k - length) - (span - TQ)
    g = t5_bias[_t5_bucket(delta)].T
    bias = _toeplitz(g, TQ, span)
    r = jnp.arange(TQ, dtype=I32)[:, None]
    kr = jnp.arange(span, dtype=I32)[None, :] - (span - TQ)
    ok = jnp.floor_divide(kr, CHUNK) <= (r // CHUNK)
    bias = jnp.where(ok[None], bias, NEG)
    near = bias.reshape(-1, TQ, NEAR_TILES, TQ).transpose(0, 2, 1, 3)
    far = t5_bias[_t5_bucket(jnp.asarray(-(span - TQ) - 1, I32))]
    return near, far


N_GRP = 4
STACK = 2 * TQ


def _attend(i, qs_ref, k_fn, v_fn, mask_fn, near_fn, far_cols, mf_ref, mn_ref, l_ref, acc_ref):
    n_far = jnp.maximum(i - (NEAR_TILES - 1), 0)
    mf_ref[...] = jnp.full(mf_ref.shape, NEG, F32)
    mn_ref[...] = jnp.full(mn_ref.shape, NEG, F32)

    def rows(g):
        return slice(g * STACK, (g + 1) * STACK)

    def logits(g, kt, jt):
        off = pl.multiple_of(kt * TQ, TQ)
        s = _dot_nt(qs_ref[rows(g), :], k_fn(g, off))
        extra = None if mask_fn is None else mask_fn(kt)[None]
        if jt is not None:
            extra = near_fn(g, jt) if extra is None else extra + near_fn(g, jt)
        if extra is not None:
            s = (s.reshape(2, TQ, TQ) + extra).reshape(STACK, TQ)
        return s

    def loops(far_fn, near_fn_):
        def far_body(kt, c):
            far_fn(kt)
            return c

        def near_body(kt, c):
            near_fn_(kt)
            return c

        lax.fori_loop(0, n_far, far_body, 0)
        lax.fori_loop(n_far, i + 1, near_body, 0)

    def pass1(m_ref, near):
        def fn(kt):
            jt = kt - (i - (NEAR_TILES - 1)) if near else None
            for g in range(N_GRP):
                s = logits(g, kt, jt)
                m_ref[rows(g), :] = jnp.maximum(m_ref[rows(g), :], jnp.maximum(s[:, :LANE], s[:, LANE:]))
        return fn

    loops(pass1(mf_ref, False), pass1(mn_ref, True))

    for g in range(N_GRP):
        m = jnp.maximum(jnp.max(mf_ref[rows(g), :], axis=-1, keepdims=True) + far_cols[g],
                        jnp.max(mn_ref[rows(g), :], axis=-1, keepdims=True))
        m_near = jnp.broadcast_to(m, (STACK, LANE))
        mn_ref[rows(g), :] = m_near
        mf_ref[rows(g), :] = m_near - far_cols[g]
    l_ref[...] = jnp.zeros(l_ref.shape, F32)
    acc_ref[...] = jnp.zeros(acc_ref.shape, F32)

    def pass2(shift_ref, near):
        def fn(kt):
            jt = kt - (i - (NEAR_TILES - 1)) if near else None
            off = pl.multiple_of(kt * TQ, TQ)
            for g in range(N_GRP):
                s = logits(g, kt, jt)
                sh = shift_ref[rows(g), :]
                p0 = jnp.exp(s[:, :LANE] - sh)
                p1 = jnp.exp(s[:, LANE:] - sh)
                l_ref[rows(g), :] += p0 + p1
                p = jnp.concatenate([p0, p1], axis=1).astype(BF16)
                acc_ref[rows(g), :] += jnp.dot(p, v_fn(g, off), preferred_element_type=F32)
        return fn

    loops(pass2(mf_ref, False), pass2(mn_ref, True))
    return [acc_ref[rows(g), :] / jnp.sum(l_ref[rows(g), :], axis=-1, keepdims=True) for g in range(N_GRP)]


def _attend_scratch():
    return [pltpu.VMEM((N_GRP * STACK, LANE), BF16)] + [pltpu.VMEM((N_GRP * STACK, LANE), F32)] * 4


def _split_lanes(x):
    lo = lax.broadcasted_iota(I32, (1, LANE), 1) < HEAD_DIM
    z = jnp.zeros_like(x)
    return jnp.concatenate([jnp.where(lo, x, z), jnp.where(lo, z, x)], axis=0)


def _attn_a_kernel(far_ref, q_ref, k_ref, v_ref, tab_ref, lam_ref, sub_ref, o_ref,
                   qs_ref, mf_ref, mn_ref, l_ref, acc_ref, *, lam_init):
    i = pl.program_id(1)
    for g in range(A_HEADS):
        qs_ref[g * STACK:(g + 1) * STACK, :] = _split_lanes(q_ref[:, g * LANE:(g + 1) * LANE])
    outs = _attend(
        i, qs_ref,
        lambda g, off: k_ref[pl.ds(off, TQ), g * LANE:(g + 1) * LANE],
        lambda g, off: v_ref[pl.ds(off, TQ), g * LANE:(g + 1) * LANE],
        None,
        lambda g, jt: tab_ref[g, jt][None],
        [jnp.full((STACK, 1), far_ref[g], F32) for g in range(A_HEADS)],
        mf_ref, mn_ref, l_ref, acc_ref)
    lam = lam_ref[...]
    lam_full = (jnp.exp(jnp.sum(lam[0:1] * lam[1:2], axis=-1, keepdims=True))
                - jnp.exp(jnp.sum(lam[2:3] * lam[3:4], axis=-1, keepdims=True)) + lam_init)
    for g, o in enumerate(outs):
        d = o[:TQ] - lam_full * o[TQ:]
        o_ref[:, g * LANE:(g + 1) * LANE] = (_rms(d, sub_ref[...]) * (1.0 - lam_init)).astype(o_ref.dtype)


def _attn_a(proj, near, far, lam, subln, lam_init):
    b, s, _ = proj.shape
    assert A_HEADS == N_GRP
    wide = A_HEADS * A_V_DIM
    return pl.pallas_call(
        functools.partial(_attn_a_kernel, lam_init=lam_init),
        grid_spec=pltpu.PrefetchScalarGridSpec(
            num_scalar_prefetch=1,
            grid=(b, s // TQ),
            in_specs=[
                pl.BlockSpec((None, TQ, wide), lambda b_, i, f: (b_, i, COL_QA // wide)),
                pl.BlockSpec((None, s, wide), lambda b_, i, f: (b_, 0, COL_KA // wide)),
                pl.BlockSpec((None, s, wide), lambda b_, i, f: (b_, 0, COL_VA // wide)),
                pl.BlockSpec((A_HEADS, NEAR_TILES, TQ, TQ), lambda b_, i, f: (0, 0, 0, 0)),
                pl.BlockSpec((4, HEAD_DIM), lambda b_, i, f: (0, 0)),
                pl.BlockSpec((1, A_V_DIM), lambda b_, i, f: (0, 0)),
            ],
            out_specs=pl.BlockSpec((None, TQ, wide), lambda b_, i, f: (b_, i, 0)),
            scratch_shapes=_attend_scratch(),
        ),
        out_shape=jax.ShapeDtypeStruct((b, s, wide), BF16),
        compiler_params=_cparams(("parallel", "arbitrary")),
    )(far, proj, proj, proj, near, lam, subln.reshape(1, A_V_DIM))


def _attn_b_kernel(far_ref, qb_ref, kb_ref, vb_ref, qi_ref, ki_ref, wi_ref, tab_ref, o_ref,
                   keys_ref, selm_ref, cand_ref, tu_ref, trial_ref, cnt_ref, qs_ref, mf_ref, mn_ref, l_ref, acc_ref,
                   *, top_k, idx_bits):
    i = pl.program_id(1)
    n_kt = i + 1
    lane = lax.broadcasted_iota(I32, (1, LANE), 1)
    lo = lane < HEAD_DIM
    int_min = jnp.int32(-2 ** 31)
    split_heads = _split_lanes

    row = lax.broadcasted_iota(I32, (TQ, TQ), 0)
    col = lax.broadcasted_iota(I32, (TQ, TQ), 1)
    chunk_gap = (col // CHUNK) - (row // CHUNK)

    def causal(kt):
        return chunk_gap <= jnp.where(kt < i, jnp.int32(TQ), jnp.int32(0))

    qi = qi_ref[...]
    qis = jnp.concatenate([split_heads(qi[:, :LANE]), split_heads(qi[:, LANE:])], axis=0)
    wi = wi_ref[...] * ((IDX_HEADS ** -0.5) * (HEAD_DIM ** -0.5))

    def score_body(kt, c):
        off = pl.multiple_of(kt * TQ, TQ)
        r = jnp.maximum(_dot_nt(qis, ki_ref[pl.ds(off, TQ), :]), 0.0)
        isc = r[0:TQ] * wi[:, 0:1]
        for hh in range(1, IDX_HEADS):
            isc = isc + r[hh * TQ:(hh + 1) * TQ] * wi[:, hh:hh + 1]
        isc = jnp.where(isc == 0.0, 0.0, isc)
        isc = jnp.where(causal(kt), isc, -jnp.inf)
        bits = lax.bitcast_convert_type(isc, I32)
        keys_ref[kt] = bits ^ ((bits >> 31) & jnp.int32(0x7FFFFFFF))
        return c

    lax.fori_loop(0, n_kt, score_body, 0)

    ones = jnp.ones((LANE, LANE), BF16)
    halves = (slice(0, LANE), slice(LANE, TQ))

    def count(pred_fn):
        cnt_ref[...] = jnp.zeros(cnt_ref.shape, F32)

        def body(kt, c):
            kk = keys_ref[kt]
            cnt_ref[...] += sum(jnp.where(pred_fn(kt, kk[:, hs], hs), 1.0, 0.0) for hs in halves)
            return c

        lax.fori_loop(0, n_kt, body, 0)
        return jnp.dot(cnt_ref[...].astype(BF16), ones, preferred_element_type=F32)

    tu_ref[...] = jnp.zeros(tu_ref.shape, I32)

    def thr_body(bi, c):
        cand_u = tu_ref[...] | (jnp.int32(1) << (31 - bi))
        cand_ref[...] = cand_u ^ int_min
        cnt = count(lambda kt, kk, hs: kk >= cand_ref[...])
        tu_ref[...] = jnp.where(cnt >= top_k, cand_u, tu_ref[...])
        return c

    lax.fori_loop(0, 32, thr_body, 0)
    tu_ref[...] = tu_ref[...] ^ int_min
    n_ge = count(lambda kt, kk, hs: kk >= tu_ref[...])
    n_gt = count(lambda kt, kk, hs: kk > tu_ref[...])

    def key_index(kt, hs):
        return lax.broadcasted_iota(I32, (TQ, LANE), 1) + (kt * TQ + hs.start)

    cand_ref[...] = jnp.full(cand_ref.shape, 2 ** idx_bits - 1, I32)

    @pl.when(jnp.max(n_ge) > top_k)
    def _():
        cnt_hi = top_k - n_gt
        cand_ref[...] = jnp.zeros(cand_ref.shape, I32)

        def tie_body(bi, c):
            trial_ref[...] = cand_ref[...] | (jnp.int32(1) << (idx_bits - 1 - bi))
            cnt = count(lambda kt, kk, hs: (kk == tu_ref[...]) & (key_index(kt, hs) <= trial_ref[...]))
            cand_ref[...] = jnp.where(cnt <= cnt_hi, trial_ref[...], cand_ref[...])
            return c

        lax.fori_loop(0, idx_bits, tie_body, 0)

    def mask_body(kt, c):
        kk = keys_ref[kt]
        lim = jnp.where(kt < i, jnp.int32(TQ), jnp.int32(0))
        for hs in halves:
            k_ = kk[:, hs]
            sel = (k_ > tu_ref[...]) | ((k_ == tu_ref[...]) & (key_index(kt, hs) <= cand_ref[...]))
            gap = ((lax.broadcasted_iota(I32, (TQ, LANE), 1) + hs.start) // CHUNK
                   - lax.broadcasted_iota(I32, (TQ, LANE), 0) // CHUNK)
            selm_ref[kt, :, hs] = jnp.where(sel & (gap <= lim), 0.0, NEG)
        return c

    lax.fori_loop(0, n_kt, mask_body, 0)

    for g in range(N_GRP):
        qs_ref[g * STACK:(g + 1) * STACK, :] = split_heads(qb_ref[:, g * LANE:(g + 1) * LANE])
    top_rows = lax.broadcasted_iota(I32, (STACK, 1), 0) < TQ
    outs = _attend(
        i, qs_ref,
        lambda g, off: kb_ref[pl.ds(off, TQ), :],
        lambda g, off: vb_ref[pl.ds(off, TQ), :],
        lambda kt: selm_ref[kt],
        lambda g, jt: tab_ref[2 * g:2 * g + 2, jt],
        [jnp.where(top_rows, far_ref[A_HEADS + 2 * g], far_ref[A_HEADS + 2 * g + 1]) for g in range(N_GRP)],
        mf_ref, mn_ref, l_ref, acc_ref)
    for g, o in enumerate(outs):
        o_ref[:, g * LANE:(g + 1) * LANE] = jnp.where(lo, o[:TQ], o[TQ:]).astype(o_ref.dtype)


def _attn_b(proj, wi, near_b, far):
    b, s, _ = proj.shape
    top_k = min(TOPK_MAX, s // 4)
    assert top_k <= TQ and s % TQ == 0 and (s & (s - 1)) == 0
    n_kt = s // TQ
    assert 2 * n_kt <= 256 and B_HEADS == 2 * N_GRP
    qb_w = B_HEADS * HEAD_DIM
    qi_w = IDX_HEADS * HEAD_DIM
    return pl.pallas_call(
        functools.partial(_attn_b_kernel, top_k=float(top_k), idx_bits=int(math.log2(s))),
        grid_spec=pltpu.PrefetchScalarGridSpec(
            num_scalar_prefetch=1,
            grid=(b, n_kt),
            in_specs=[
                pl.BlockSpec((None, TQ, qb_w), lambda b_, i, f: (b_, i, COL_QB // qb_w)),
                pl.BlockSpec((None, s, LANE), lambda b_, i, f: (b_, 0, COL_KB // LANE)),
                pl.BlockSpec((None, s, LANE), lambda b_, i, f: (b_, 0, COL_VB // LANE)),
                pl.BlockSpec((None, TQ, qi_w), lambda b_, i, f: (b_, i, COL_QI // qi_w)),
                pl.BlockSpec((None, s, LANE), lambda b_, i, f: (b_, 0, COL_KI // LANE)),
                pl.BlockSpec((None, TQ, LANE), lambda b_, i, f: (b_, i, 0)),
                pl.BlockSpec((B_HEADS, NEAR_TILES, TQ, TQ), lambda b_, i, f: (0, 0, 0, 0)),
            ],
            out_specs=pl.BlockSpec((None, TQ, qb_w), lambda b_, i, f: (b_, i, 0)),
            scratch_shapes=[pltpu.VMEM((n_kt, TQ, TQ), I32), pltpu.VMEM((n_kt, TQ, TQ), F32)]
            + [pltpu.VMEM((TQ, LANE), I32)] * 3 + [pltpu.VMEM((TQ, LANE), F32)] + _attend_scratch(),
        ),
        out_shape=jax.ShapeDtypeStruct((b, s, qb_w), BF16),
        compiler_params=_cparams(("parallel", "arbitrary")),
    )(far, proj, proj, proj, proj, proj, wi, near_b)


PAIRS_C = 2


def _c_tables(rel_table):
    n_var = WIN_C // TQ_C
    length = WIN_C + TQ_C
    k = jnp.arange(length, dtype=I32)
    cr = jnp.where(k < WIN_C, k, k - length)
    off = (jnp.arange(n_var, dtype=I32) * TQ_C)[:, None]
    rel_idx = jnp.clip(off - cr[None, :], -C_REL_CLIP, C_REL_CLIP) + C_REL_CLIP
    g = rel_table[rel_idx].transpose(2, 0, 1)
    bias = _toeplitz(g, TQ_C, WIN_C)
    r = jnp.arange(TQ_C, dtype=I32)[None, :, None]
    c = jnp.arange(WIN_C, dtype=I32)[None, None, :]
    kc = jnp.floor_divide(c - off[:, :, None], CHUNK)
    ok = (kc <= r // CHUNK) & (kc >= r // CHUNK - C_LEFT_CHUNKS)
    return jnp.where(ok[None], bias, NEG)


def _attn_c_kernel(q_ref, k_ref, v_ref, tab_ref, o_ref):
    i = pl.program_id(2)
    n_var = WIN_C // TQ_C
    lane = lax.broadcasted_iota(I32, (1, LANE), 1)
    lo = lane < HEAD_DIM
    var = jnp.minimum(i, n_var - 1)
    ks = pl.multiple_of(jnp.maximum(i - (n_var - 1), 0) * TQ_C, TQ_C)
    for pp in range(PAIRS_C):
        cols = slice(pp * LANE, (pp + 1) * LANE)
        q = q_ref[:, cols]
        zq = jnp.zeros_like(q)
        qs = jnp.concatenate([jnp.where(lo, q, zq), jnp.where(lo, zq, q)], axis=0)
        s = _dot_nt(qs, k_ref[pl.ds(ks, WIN_C), cols])
        s = s + jnp.concatenate([tab_ref[2 * pp, var], tab_ref[2 * pp + 1, var]], axis=0)
        m = jnp.max(s, axis=-1, keepdims=True)
        p = jnp.exp(s - m)
        l = jnp.sum(p, axis=-1, keepdims=True)
        pv = jnp.dot(p.astype(BF16), v_ref[pl.ds(ks, WIN_C), cols], preferred_element_type=F32) / l
        o_ref[:, cols] = jnp.where(lo, pv[:TQ_C], pv[TQ_C:]).astype(o_ref.dtype)


def _attn_c(proj, tab):
    b, s, _ = proj.shape
    assert s >= WIN_C and s % TQ_C == 0
    n_grp = C_HEADS // 2 // PAIRS_C
    n_var = WIN_C // TQ_C
    wblk = PAIRS_C * LANE
    tab = tab.reshape(n_grp, 2 * PAIRS_C, n_var, TQ_C, WIN_C)
    return pl.pallas_call(
        _attn_c_kernel,
        grid=(b, n_grp, s // TQ_C),
        in_specs=[
            pl.BlockSpec((None, TQ_C, wblk), lambda b_, h, i: (b_, i, h)),
            pl.BlockSpec((None, s, wblk), lambda b_, h, i: (b_, 0, n_grp + h)),
            pl.BlockSpec((None, s, wblk), lambda b_, h, i: (b_, 0, 2 * n_grp + h)),
            pl.BlockSpec((None, 2 * PAIRS_C, n_var, TQ_C, WIN_C), lambda b_, h, i: (h, 0, 0, 0, 0)),
        ],
        out_specs=pl.BlockSpec((None, TQ_C, wblk), lambda b_, h, i: (b_, i, h)),
        out_shape=jax.ShapeDtypeStruct((b, s, D_MODEL), BF16),
        compiler_params=_cparams(("parallel", "parallel", "arbitrary")),
    )(proj, proj, proj, tab)


ROW_TILE = 8


def _rows_to_tiles(ref, x):
    rows = x.shape[0]
    for c in range(ROW_TILE):
        ref[pl.ds(c, rows, stride=ROW_TILE), :] = x[:, c * LANE:(c + 1) * LANE]


def _tiles_to_rows(ref, rows):
    return jnp.concatenate([ref[pl.ds(c, rows, stride=ROW_TILE), :] for c in range(ROW_TILE)], axis=1)


def _router_kernel(h_ref, g_ref, wr_ref, br_ref, xn_ref, ei_ref, cw_ref):
    xn = _rms(h_ref[...], g_ref[...])
    _rows_to_tiles(xn_ref, xn)
    logits = jnp.dot(xn, wr_ref[...], preferred_element_type=F32,
                     precision=lax.Precision.HIGHEST) + br_ref[...]
    lane = lax.broadcasted_iota(I32, (1, LANE), 1)
    lane_f = lane.astype(F32)
    big = float(LANE)
    is_g = lane < N_GROUPS
    lg = jnp.where(is_g, logits, -jnp.inf)
    gmax = jnp.max(lg, axis=-1, keepdims=True)
    g_sel = jnp.min(jnp.where(lg == gmax, lane_f, big), axis=-1, keepdims=True)
    p_gsel = 1.0 / jnp.sum(jnp.where(is_g, jnp.exp(logits - gmax), 0.0), axis=-1, keepdims=True)
    e_grp = ((lane - N_GROUPS) // EXPERTS_PER_GROUP).astype(F32)
    in_sel = (lane >= N_GROUPS) & (lane < N_GROUPS + N_EXPERTS) & (e_grp == g_sel)
    le = jnp.where(in_sel, logits, -jnp.inf)
    m1 = jnp.max(le, axis=-1, keepdims=True)
    i1 = jnp.min(jnp.where(le == m1, lane_f, big), axis=-1, keepdims=True)
    le2 = jnp.where(lane_f == i1, -jnp.inf, le)
    m2 = jnp.max(le2, axis=-1, keepdims=True)
    i2 = jnp.min(jnp.where(le2 == m2, lane_f, big), axis=-1, keepdims=True)
    t = jnp.exp(m2 - m1)
    c1 = p_gsel / (1.0 + t)
    c2 = p_gsel * t / (1.0 + t)
    ei_ref[...] = jnp.where(lane == 0, i1, i2).astype(I32) - N_GROUPS
    cw_ref[...] = jnp.where(lane == 0, c1, c2)


def _router(h, g, w_rg, b_rg, w_re, b_re, tm=256):
    n, d = h.shape
    pad = LANE - N_GROUPS - N_EXPERTS
    wr = jnp.concatenate([w_rg, w_re, jnp.zeros((d, pad), F32)], axis=1)
    br = jnp.concatenate([b_rg, b_re, jnp.zeros((pad,), F32)]).reshape(1, LANE)
    return pl.pallas_call(
        _router_kernel,
        grid=(n // tm,),
        in_specs=[pl.BlockSpec((tm, d), lambda i: (i, 0)),
                  pl.BlockSpec((1, d), lambda i: (0, 0)),
                  pl.BlockSpec((d, LANE), lambda i: (0, 0)),
                  pl.BlockSpec((1, LANE), lambda i: (0, 0))],
        out_specs=[pl.BlockSpec((tm * ROW_TILE, LANE), lambda i: (i, 0)),
                   pl.BlockSpec((tm, LANE), lambda i: (i, 0)),
                   pl.BlockSpec((tm, LANE), lambda i: (i, 0))],
        out_shape=[jax.ShapeDtypeStruct((n * ROW_TILE, LANE), F32),
                   jax.ShapeDtypeStruct((n, LANE), I32),
                   jax.ShapeDtypeStruct((n, LANE), F32)],
        compiler_params=_cparams(("parallel",)),
    )(h, g.reshape(1, d), wr, br)


def _moe_schedule(ei, n_tok, tm):
    ids = jnp.concatenate([ei[:, 0], ei[:, 1]])
    n_asg = 2 * n_tok
    assert N_EXPERTS * n_asg < 2 ** 31
    order = jnp.sort(ids * n_asg + jnp.arange(n_asg, dtype=I32)) % n_asg
    counts = jnp.sum((ids[:, None] == jnp.arange(N_EXPERTS, dtype=I32)[None, :]).astype(I32), axis=0)
    tiles_per = (counts + tm - 1) // tm
    tile_end = jnp.cumsum(tiles_per)
    tile_start = tile_end - tiles_per
    row_start = jnp.cumsum(counts) - counts
    n_tiles = n_asg // tm + N_EXPERTS
    t = jnp.arange(n_tiles, dtype=I32)
    te = jnp.searchsorted(tile_end, t, side="right").astype(I32)
    live = te < N_EXPERTS
    te = jnp.minimum(te, N_EXPERTS - 1)
    rank0 = (t - tile_start[te]) * tm
    nv = jnp.where(live, jnp.clip(counts[te] - rank0, 0, tm), 0).astype(I32)
    r = jnp.arange(tm, dtype=I32)[None, :]
    pos = jnp.clip(row_start[te][:, None] + rank0[:, None] + r, 0, n_asg - 1)
    asg = order[pos]
    valid = r < nv[:, None]
    src = jnp.where(valid, asg % n_tok, 0).astype(I32)
    n_pad = n_tiles * tm - n_asg
    pad_before = jnp.cumsum(tm - nv) - (tm - nv)
    dst = jnp.where(valid, asg, n_asg + pad_before[:, None] + (r - nv[:, None])).astype(I32)
    prime = (n_asg + n_pad + jnp.arange(2 * tm, dtype=I32)).reshape(2, tm)
    src_ext = jnp.concatenate([src, src[-1:]], axis=0)
    dst_ext = jnp.concatenate([prime[:1], dst, prime[1:]], axis=0)
    return te, src_ext.reshape(n_tiles + 1, 1, tm), dst_ext.reshape(n_tiles + 2, 1, tm)


def _moe_kernel(te_ref, src_cur, src_next, dst_prev, dst_cur, dst_prime, x_hbm, wg_ref, wu_ref, wd_ref, o_hbm,
                xbuf, ybuf, wg_bf, wu_bf, wd_bf, gsem, ssem, *, tm, n_tiles):
    t = pl.program_id(0)
    slot = t % 2
    other = 1 - slot

    def tok(i):
        return pl.ds(i * ROW_TILE, ROW_TILE)

    def gather(idx_ref, r, sl):
        return pltpu.make_async_copy(x_hbm.at[tok(idx_ref[0, r])], xbuf.at[sl, tok(r)], gsem.at[sl])

    def scatter(idx_ref, r, sl):
        return pltpu.make_async_copy(ybuf.at[sl, tok(r)], o_hbm.at[tok(idx_ref[0, r])], ssem.at[sl])

    def wait_gather(sl):
        pltpu.make_async_copy(x_hbm.at[pl.ds(0, tm * ROW_TILE)], xbuf.at[sl], gsem.at[sl]).wait()

    def wait_scatter(sl):
        pltpu.make_async_copy(ybuf.at[sl], o_hbm.at[pl.ds(0, tm * ROW_TILE)], ssem.at[sl]).wait()

    def each(fn):
        def body(r, c):
            fn(r)
            return c
        lax.fori_loop(0, tm, body, 0)

    @pl.when(t == 0)
    def _():
        ybuf[...] = jnp.zeros(ybuf.shape, F32)
        each(lambda r: gather(src_cur, r, 0).start())
        each(lambda r: scatter(dst_prime, r, 0).start())

    @pl.when((t == 0) | (te_ref[t] != te_ref[jnp.maximum(t - 1, 0)]))
    def _():
        wg_bf[...] = wg_ref[...].astype(BF16)
        wu_bf[...] = wu_ref[...].astype(BF16)
        wd_bf[...] = wd_ref[...].astype(BF16)

    wait_gather(slot)
    wait_scatter(slot)
    for r in range(tm):
        gather(src_next, r, other).start()
    for r in range(tm):
        scatter(dst_prev, r, other).start()
    x = _tiles_to_rows(xbuf.at[slot], tm).astype(BF16)
    g = jnp.dot(x, wg_bf[...], preferred_element_type=F32)
    u = jnp.dot(x, wu_bf[...], preferred_element_type=F32)
    act = (g * jax.nn.sigmoid(g)) * u
    _rows_to_tiles(ybuf.at[slot], jnp.dot(act.astype(BF16), wd_bf[...], preferred_element_type=F32))

    @pl.when(t == n_tiles - 1)
    def _():
        each(lambda r: scatter(dst_cur, r, slot).start())
        wait_scatter(other)
        wait_scatter(slot)
        wait_gather(other)


def _moe_experts(xn, sched, layer, wg, wu, wd, tm=TM_MOE):
    d = wg.shape[2]
    assert d == ROW_TILE * LANE
    te, src, dst = sched
    n_tiles = te.shape[0]

    def idx_spec(fn):
        return pl.BlockSpec((None, 1, tm), lambda t, te_: (fn(t), 0, 0), memory_space=pltpu.SMEM)

    def w_spec(rows, cols):
        return pl.BlockSpec((None, None, rows, cols), lambda t, te_: (layer, te_[t], 0, 0))

    return pl.pallas_call(
        functools.partial(_moe_kernel, tm=tm, n_tiles=n_tiles),
        grid_spec=pltpu.PrefetchScalarGridSpec(
            num_scalar_prefetch=1,
            grid=(n_tiles,),
            in_specs=[
                idx_spec(lambda t: t), idx_spec(lambda t: t + 1),
                idx_spec(lambda t: t), idx_spec(lambda t: t + 1), idx_spec(lambda t: n_tiles + 1),
                pl.BlockSpec(memory_space=pl.ANY),
                w_spec(d, D_EXPERT), w_spec(d, D_EXPERT), w_spec(D_EXPERT, d),
            ],
            out_specs=pl.BlockSpec(memory_space=pl.ANY),
            scratch_shapes=[pltpu.VMEM((2, tm * ROW_TILE, LANE), F32), pltpu.VMEM((2, tm * ROW_TILE, LANE), F32),
                            pltpu.VMEM((d, D_EXPERT), BF16), pltpu.VMEM((d, D_EXPERT), BF16),
                            pltpu.VMEM((D_EXPERT, d), BF16),
                            pltpu.SemaphoreType.DMA((2,)), pltpu.SemaphoreType.DMA((2,))],
        ),
        out_shape=jax.ShapeDtypeStruct(((n_tiles + 2) * tm * ROW_TILE, LANE), F32),
        compiler_params=_cparams(("arbitrary",)),
    )(te, src, src, dst, dst, dst, xn, wg, wu, wd)


def _combine_kernel(h_ref, y0_ref, y1_ref, cw_ref, g_ref, o_ref, *, final):
    cw = cw_ref[...]
    tm = h_ref.shape[0]
    out = h_ref[...] + cw[:, 0:1] * _tiles_to_rows(y0_ref, tm) + cw[:, 1:2] * _tiles_to_rows(y1_ref, tm)
    if final:
        out = _rms(out, g_ref[...])
    o_ref[...] = out


def _combine(h, y, cw, g, final, tm=512):
    n, d = h.shape
    nb = n // tm
    return pl.pallas_call(
        functools.partial(_combine_kernel, final=final),
        grid=(nb,),
        in_specs=[pl.BlockSpec((tm, d), lambda i: (i, 0)),
                  pl.BlockSpec((tm * ROW_TILE, LANE), lambda i: (i, 0)),
                  pl.BlockSpec((tm * ROW_TILE, LANE), lambda i: (nb + i, 0)),
                  pl.BlockSpec((tm, LANE), lambda i: (i, 0)),
                  pl.BlockSpec((1, d), lambda i: (0, 0))],
        out_specs=pl.BlockSpec((tm, d), lambda i: (i, 0)),
        out_shape=jax.ShapeDtypeStruct((n, d), F32),
        compiler_params=_cparams(("parallel",)),
    )(h, y, y, cw, g.reshape(1, d))


def _moe_layer(h, layer, ln_g, w_rg, b_rg, w_re, b_re, w_gate, w_up, w_down, final_g, final):
    n = h.shape[0]
    xn, ei, cw = _router(h, ln_g, w_rg, b_rg, w_re, b_re)
    sched = _moe_schedule(ei, n, TM_MOE)
    y = _moe_experts(xn, sched, layer, w_gate, w_up, w_down)
    return _combine(h, y, cw, final_g, final)


def _ab_weights(w_in):
    scale = HEAD_DIM ** -0.5
    widths = (512, 512, 512, 512, 64, 64, 256, 64, 4)
    offs = [sum(widths[:j]) for j in range(len(widths) + 1)]
    q_a, k_a, v_a, q_b, k_b, v_b, q_i, k_i, w_i = [w_in[:, offs[j]:offs[j + 1]] for j in range(len(widths))]
    pad = jnp.zeros((w_in.shape[0], LANE - IDX_HEADS), w_in.dtype)
    main = jnp.concatenate([q_a * scale, k_a, v_a, q_b * scale, k_b, k_b, v_b, v_b, q_i, k_i, k_i], axis=1)
    return jnp.concatenate([main, w_i, pad], axis=1).astype(BF16)


def kernel(x, t5_bias, ln_mix, ln_ffn, ln_final, ab_w_in, ab_w_out, diff_lambda, diff_subln,
           c_w_in, c_w_out, c_rel_bias, moe_w_rg, moe_b_rg, moe_w_re, moe_b_re,
           moe_w_gate, moe_w_up, moe_w_down):
    b, s, d = x.shape
    n = b * s
    depth = ln_mix.shape[0]
    h = x.reshape(n, d)
    near, far = _t5_tables(t5_bias)
    for l in range(depth):
        if l % 2 == 0:
            e = l // 2
            lam_init = 0.8 - 0.6 * math.exp(-0.3 * l)
            proj, wi = _norm_proj(h, ln_mix[l], _ab_weights(ab_w_in[e]), AB_COLS)
            proj = proj.reshape(b, s, AB_COLS)
            y_a = _attn_a(proj, near[:A_HEADS], far, diff_lambda[e], diff_subln[e], lam_init)
            y_b = _attn_b(proj, wi.reshape(b, s, LANE), near[A_HEADS:], far)
            h = _out_proj(h, y_a.reshape(n, -1), 0, y_b.reshape(n, -1), 0, ab_w_out[e].astype(BF16))
        else:
            o = l // 2
            scale = HEAD_DIM ** -0.5
            w_in = jnp.concatenate([c_w_in[o][:, :d] * scale, c_w_in[o][:, d:]], axis=1).astype(BF16)
            (proj,) = _norm_proj(h, ln_mix[l], w_in, 3 * d)
            y = _attn_c(proj.reshape(b, s, 3 * d), _c_tables(c_rel_bias[o])).reshape(n, d)
            h = _out_proj(h, y, 0, y, 1, c_w_out[o].astype(BF16))
        h = _moe_layer(h, l, ln_ffn[l], moe_w_rg[l], moe_b_rg[l], moe_w_re[l], moe_b_re[l],
                       moe_w_gate, moe_w_up, moe_w_down, ln_final, l == depth - 1)
    return h.reshape(b, s, d)
```

```python
import functools
import math

import jax
import jax.numpy as jnp
from jax import lax
from jax.experimental import pallas as pl
from jax.experimental.pallas import tpu as pltpu

F32 = jnp.float32
BF16 = jnp.bfloat16
I32 = jnp.int32

D_MODEL = 1024
CHUNK = 64
HEAD_DIM = 64
RMS_EPS = 1e-6
A_HEADS = 4
A_V_DIM = 2 * HEAD_DIM
B_HEADS = 8
IDX_HEADS = 4
TOPK_MAX = 256
C_HEADS = D_MODEL // HEAD_DIM
C_LEFT_CHUNKS = 8
C_REL_CLIP = 128
T5_BUCKETS = 32
T5_MAX_DIST = 1024
N_GROUPS = 4
EXPERTS_PER_GROUP = 8
N_EXPERTS = N_GROUPS * EXPERTS_PER_GROUP
PAIRS_PER_GROUP = EXPERTS_PER_GROUP * (EXPERTS_PER_GROUP - 1) // 2
D_EXPERT = D_MODEL // 2
NEG = -1e30

LANE = 128
VMEM_LIMIT = 56 * 1024 * 1024

TQ = 256
NEAR_TILES = 4
TQ_C = 2 * CHUNK
WIN_C = (C_LEFT_CHUNKS + 2) * CHUNK
TM_MOE = 256

COL_QA, COL_KA, COL_VA, COL_QB, COL_KB, COL_VB, COL_QI, COL_KI, COL_WI = (
    0, 512, 1024, 1536, 2048, 2176, 2304, 2560, 2688)
AB_COLS = 2688


def _cparams(sem, vmem=VMEM_LIMIT):
    return pltpu.CompilerParams(dimension_semantics=sem, vmem_limit_bytes=vmem)


def _dot_nt(a, b):
    return lax.dot_general(a, b, (((1,), (1,)), ((), ())), preferred_element_type=F32)


def _rms(x, g):
    return (x * lax.rsqrt(jnp.mean(x * x, axis=-1, keepdims=True) + RMS_EPS)) * g


def _norm_proj_kernel(x_ref, g_ref, w_ref, o_ref, *tail_ref, n_main):
    xn = _rms(x_ref[...], g_ref[...]).astype(BF16)
    acc = jnp.dot(xn, w_ref[...], preferred_element_type=F32)
    o_ref[...] = acc[:, :n_main].astype(o_ref.dtype)
    if tail_ref:
        tail_ref[0][...] = acc[:, n_main:]


def _norm_proj(x, g, w, n_main, tm=256):
    n, d = x.shape
    n_all = w.shape[1]
    out_shape = [jax.ShapeDtypeStruct((n, n_main), BF16)]
    out_specs = [pl.BlockSpec((tm, n_main), lambda i: (i, 0))]
    if n_all > n_main:
        out_shape.append(jax.ShapeDtypeStruct((n, n_all - n_main), F32))
        out_specs.append(pl.BlockSpec((tm, n_all - n_main), lambda i: (i, 0)))
    return pl.pallas_call(
        functools.partial(_norm_proj_kernel, n_main=n_main),
        grid=(n // tm,),
        in_specs=[pl.BlockSpec((tm, d), lambda i: (i, 0)),
                  pl.BlockSpec((1, d), lambda i: (0, 0)),
                  pl.BlockSpec((d, n_all), lambda i: (0, 0))],
        out_specs=out_specs, out_shape=out_shape,
        compiler_params=_cparams(("parallel",)),
    )(x, g.reshape(1, d), w)


def _out_proj_kernel(h_ref, y0_ref, y1_ref, w0_ref, w1_ref, o_ref):
    acc = jnp.dot(y0_ref[...], w0_ref[...], preferred_element_type=F32)
    acc = acc + jnp.dot(y1_ref[...], w1_ref[...], preferred_element_type=F32)
    o_ref[...] = h_ref[...] + acc


def _out_proj(h, y0, c0, y1, c1, w, tm=512):
    n, d = h.shape
    half = d // 2
    return pl.pallas_call(
        _out_proj_kernel,
        grid=(n // tm,),
        in_specs=[pl.BlockSpec((tm, d), lambda i: (i, 0)),
                  pl.BlockSpec((tm, half), lambda i: (i, c0)),
                  pl.BlockSpec((tm, half), lambda i: (i, c1)),
                  pl.BlockSpec((half, d), lambda i: (0, 0)),
                  pl.BlockSpec((half, d), lambda i: (1, 0))],
        out_specs=pl.BlockSpec((tm, d), lambda i: (i, 0)),
        out_shape=jax.ShapeDtypeStruct((n, d), F32),
        compiler_params=_cparams(("parallel",)),
    )(h, y0, y1, w, w)


def _t5_bucket(rel):
    half = T5_BUCKETS // 2
    max_exact = half // 2
    ret = jnp.where(rel > 0, half, 0)
    n = jnp.abs(rel)
    nf = jnp.maximum(n, 1).astype(F32)
    large = max_exact + (jnp.log(nf / max_exact) / math.log(T5_MAX_DIST / max_exact)
                         * (half - max_exact)).astype(I32)
    large = jnp.minimum(large, half - 1)
    return (ret + jnp.where(n < max_exact, n, large)).astype(I32)


def _toeplitz(g, rows, width):
    length = g.shape[-1]
    assert width < length and rows <= length
    flat = jnp.tile(g, (1,) * (g.ndim - 1) + (rows,))[..., :rows * (length - 1)]
    return flat.reshape(g.shape[:-1] + (rows, length - 1))[..., :width]


def _t5_tables(t5_bias):
    span = NEAR_TILES * TQ
    length = span + TQ
    k = jnp.arange(length, dtype=I32)
    delta = jnp.where(k < span, k, k - length) - (span - TQ)
    g = t5_bias[_t5_bucket(delta)].T
    bias = _toeplitz(g, TQ, span)
    r = jnp.arange(TQ, dtype=I32)[:, None]
    kr = jnp.arange(span, dtype=I32)[None, :] - (span - TQ)
    ok = jnp.floor_divide(kr, CHUNK) <= (r // CHUNK)
    bias = jnp.where(ok[None], bias, NEG)
    near = bias.reshape(-1, TQ, NEAR_TILES, TQ).transpose(0, 2, 1, 3)
    far = t5_bias[_t5_bucket(jnp.asarray(-(span - TQ) - 1, I32))]
    return near, far


N_GRP = 4
STACK = 2 * TQ


def _attend(i, qs_ref, k_fn, v_fn, mask_fn, near_fn, far_cols, mf_ref, mn_ref, l_ref, acc_ref):
    n_far = jnp.maximum(i - (NEAR_TILES - 1), 0)
    mf_ref[...] = jnp.full(mf_ref.shape, NEG, F32)
    mn_ref[...] = jnp.full(mn_ref.shape, NEG, F32)

    def rows(g):
        return slice(g * STACK, (g + 1) * STACK)

    def logits(g, kt, jt):
        off = pl.multiple_of(kt * TQ, TQ)
        s = _dot_nt(qs_ref[rows(g), :], k_fn(g, off))
        extra = None if mask_fn is None else mask_fn(kt)[None]
        if jt is not None:
            extra = near_fn(g, jt) if extra is None else extra + near_fn(g, jt)
        if extra is not None:
            s = (s.reshape(2, TQ, TQ) + extra).reshape(STACK, TQ)
        return s

    def loops(far_fn, near_fn_):
        def far_body(kt, c):
            far_fn(kt)
            return c

        def near_body(kt, c):
            near_fn_(kt)
            return c

        lax.fori_loop(0, n_far, far_body, 0)
        lax.fori_loop(n_far, i + 1, near_body, 0)

    def pass1(m_ref, near):
        def fn(kt):
            jt = kt - (i - (NEAR_TILES - 1)) if near else None
            for g in range(N_GRP):
                s = logits(g, kt, jt)
                m_ref[rows(g), :] = jnp.maximum(m_ref[rows(g), :], jnp.maximum(s[:, :LANE], s[:, LANE:]))
        return fn

    loops(pass1(mf_ref, False), pass1(mn_ref, True))

    for g in range(N_GRP):
        m = jnp.maximum(jnp.max(mf_ref[rows(g), :], axis=-1, keepdims=True) + far_cols[g],
                        jnp.max(mn_ref[rows(g), :], axis=-1, keepdims=True))
        m_near = jnp.broadcast_to(m, (STACK, LANE))
        mn_ref[rows(g), :] = m_near
        mf_ref[rows(g), :] = m_near - far_cols[g]
    l_ref[...] = jnp.zeros(l_ref.shape, F32)
    acc_ref[...] = jnp.zeros(acc_ref.shape, F32)

    def pass2(shift_ref, near):
        def fn(kt):
            jt = kt - (i - (NEAR_TILES - 1)) if near else None
            off = pl.multiple_of(kt * TQ, TQ)
            for g in range(N_GRP):
                s = logits(g, kt, jt)
                sh = shift_ref[rows(g), :]
                p0 = jnp.exp(s[:, :LANE] - sh)
                p1 = jnp.exp(s[:, LANE:] - sh)
                l_ref[rows(g), :] += p0 + p1
                p = jnp.concatenate([p0, p1], axis=1).astype(BF16)
                acc_ref[rows(g), :] += jnp.dot(p, v_fn(g, off), preferred_element_type=F32)
        return fn

    loops(pass2(mf_ref, False), pass2(mn_ref, True))
    return [acc_ref[rows(g), :] / jnp.sum(l_ref[rows(g), :], axis=-1, keepdims=True) for g in range(N_GRP)]


def _attend_scratch():
    return [pltpu.VMEM((N_GRP * STACK, LANE), BF16)] + [pltpu.VMEM((N_GRP * STACK, LANE), F32)] * 4


def _split_lanes(x):
    lo = lax.broadcasted_iota(I32, (1, LANE), 1) < HEAD_DIM
    z = jnp.zeros_like(x)
    return jnp.concatenate([jnp.where(lo, x, z), jnp.where(lo, z, x)], axis=0)


def _attn_a_kernel(far_ref, q_ref, k_ref, v_ref, tab_ref, lam_ref, sub_ref, o_ref,
                   qs_ref, mf_ref, mn_ref, l_ref, acc_ref, *, lam_init):
    i = pl.program_id(1)
    for g in range(A_HEADS):
        qs_ref[g * STACK:(g + 1) * STACK, :] = _split_lanes(q_ref[:, g * LANE:(g + 1) * LANE])
    outs = _attend(
        i, qs_ref,
        lambda g, off: k_ref[pl.ds(off, TQ), g * LANE:(g + 1) * LANE],
        lambda g, off: v_ref[pl.ds(off, TQ), g * LANE:(g + 1) * LANE],
        None,
        lambda g, jt: tab_ref[g, jt][None],
        [jnp.full((STACK, 1), far_ref[g], F32) for g in range(A_HEADS)],
        mf_ref, mn_ref, l_ref, acc_ref)
    lam = lam_ref[...]
    lam_full = (jnp.exp(jnp.sum(lam[0:1] * lam[1:2], axis=-1, keepdims=True))
                - jnp.exp(jnp.sum(lam[2:3] * lam[3:4], axis=-1, keepdims=True)) + lam_init)
    for g, o in enumerate(outs):
        d = o[:TQ] - lam_full * o[TQ:]
        o_ref[:, g * LANE:(g + 1) * LANE] = (_rms(d, sub_ref[...]) * (1.0 - lam_init)).astype(o_ref.dtype)


def _attn_a(proj, near, far, lam, subln, lam_init):
    b, s, _ = proj.shape
    assert A_HEADS == N_GRP
    wide = A_HEADS * A_V_DIM
    return pl.pallas_call(
        functools.partial(_attn_a_kernel, lam_init=lam_init),
        grid_spec=pltpu.PrefetchScalarGridSpec(
            num_scalar_prefetch=1,
            grid=(b, s // TQ),
            in_specs=[
                pl.BlockSpec((None, TQ, wide), lambda b_, i, f: (b_, i, COL_QA // wide)),
                pl.BlockSpec((None, s, wide), lambda b_, i, f: (b_, 0, COL_KA // wide)),
                pl.BlockSpec((None, s, wide), lambda b_, i, f: (b_, 0, COL_VA // wide)),
                pl.BlockSpec((A_HEADS, NEAR_TILES, TQ, TQ), lambda b_, i, f: (0, 0, 0, 0)),
                pl.BlockSpec((4, HEAD_DIM), lambda b_, i, f: (0, 0)),
                pl.BlockSpec((1, A_V_DIM), lambda b_, i, f: (0, 0)),
            ],
            out_specs=pl.BlockSpec((None, TQ, wide), lambda b_, i, f: (b_, i, 0)),
            scratch_shapes=_attend_scratch(),
        ),
        out_shape=jax.ShapeDtypeStruct((b, s, wide), BF16),
        compiler_params=_cparams(("parallel", "arbitrary")),
    )(far, proj, proj, proj, near, lam, subln.reshape(1, A_V_DIM))


def _attn_b_kernel(far_ref, qb_ref, kb_ref, vb_ref, qi_ref, ki_ref, wi_ref, tab_ref, o_ref,
                   keys_ref, selm_ref, cand_ref, tu_ref, trial_ref, cnt_ref, qs_ref, mf_ref, mn_ref, l_ref, acc_ref,
                   *, top_k, idx_bits):
    i = pl.program_id(1)
    n_kt = i + 1
    lane = lax.broadcasted_iota(I32, (1, LANE), 1)
    lo = lane < HEAD_DIM
    int_min = jnp.int32(-2 ** 31)
    split_heads = _split_lanes

    row = lax.broadcasted_iota(I32, (TQ, TQ), 0)
    col = lax.broadcasted_iota(I32, (TQ, TQ), 1)
    chunk_gap = (col // CHUNK) - (row // CHUNK)

    def causal(kt):
        return chunk_gap <= jnp.where(kt < i, jnp.int32(TQ), jnp.int32(0))

    qi = qi_ref[...]
    qis = jnp.concatenate([split_heads(qi[:, :LANE]), split_heads(qi[:, LANE:])], axis=0)
    wi = wi_ref[...] * ((IDX_HEADS ** -0.5) * (HEAD_DIM ** -0.5))

    def score_body(kt, c):
        off = pl.multiple_of(kt * TQ, TQ)
        r = jnp.maximum(_dot_nt(qis, ki_ref[pl.ds(off, TQ), :]), 0.0)
        isc = r[0:TQ] * wi[:, 0:1]
        for hh in range(1, IDX_HEADS):
            isc = isc + r[hh * TQ:(hh + 1) * TQ] * wi[:, hh:hh + 1]
        isc = jnp.where(isc == 0.0, 0.0, isc)
        isc = jnp.where(causal(kt), isc, -jnp.inf)
        bits = lax.bitcast_convert_type(isc, I32)
        keys_ref[kt] = bits ^ ((bits >> 31) & jnp.int32(0x7FFFFFFF))
        return c

    lax.fori_loop(0, n_kt, score_body, 0)

    ones = jnp.ones((LANE, LANE), BF16)
    halves = (slice(0, LANE), slice(LANE, TQ))

    def count(pred_fn):
        cnt_ref[...] = jnp.zeros(cnt_ref.shape, F32)

        def body(kt, c):
            kk = keys_ref[kt]
            cnt_ref[...] += sum(jnp.where(pred_fn(kt, kk[:, hs], hs), 1.0, 0.0) for hs in halves)
            return c

        lax.fori_loop(0, n_kt, body, 0)
        return jnp.dot(cnt_ref[...].astype(BF16), ones, preferred_element_type=F32)

    tu_ref[...] = jnp.zeros(tu_ref.shape, I32)

    def thr_body(bi, c):
        cand_u = tu_ref[...] | (jnp.int32(1) << (31 - bi))
        cand_ref[...] = cand_u ^ int_min
        cnt = count(lambda kt, kk, hs: kk >= cand_ref[...])
        tu_ref[...] = jnp.where(cnt >= top_k, cand_u, tu_ref[...])
        return c

    lax.fori_loop(0, 32, thr_body, 0)
    tu_ref[...] = tu_ref[...] ^ int_min
    n_ge = count(lambda kt, kk, hs: kk >= tu_ref[...])
    n_gt = count(lambda kt, kk, hs: kk > tu_ref[...])

    def key_index(kt, hs):
        return lax.broadcasted_iota(I32, (TQ, LANE), 1) + (kt * TQ + hs.start)

    cand_ref[...] = jnp.full(cand_ref.shape, 2 ** idx_bits - 1, I32)

    @pl.when(jnp.max(n_ge) > top_k)
    def _():
        cnt_hi = top_k - n_gt
        cand_ref[...] = jnp.zeros(cand_ref.shape, I32)

        def tie_body(bi, c):
            trial_ref[...] = cand_ref[...] | (jnp.int32(1) << (idx_bits - 1 - bi))
            cnt = count(lambda kt, kk, hs: (kk == tu_ref[...]) & (key_index(kt, hs) <= trial_ref[...]))
            cand_ref[...] = jnp.where(cnt <= cnt_hi, trial_ref[...], cand_ref[...])
            return c

        lax.fori_loop(0, idx_bits, tie_body, 0)

    def mask_body(kt, c):
        kk = keys_ref[kt]
        lim = jnp.where(kt < i, jnp.int32(TQ), jnp.int32(0))
        for hs in halves:
            k_ = kk[:, hs]
            sel = (k_ > tu_ref[...]) | ((k_ == tu_ref[...]) & (key_index(kt, hs) <= cand_ref[...]))
            gap = ((lax.broadcasted_iota(I32, (TQ, LANE), 1) + hs.start) // CHUNK
                   - lax.broadcasted_iota(I32, (TQ, LANE), 0) // CHUNK)
            selm_ref[kt, :, hs] = jnp.where(sel & (gap <= lim), 0.0, NEG)
        return c

    lax.fori_loop(0, n_kt, mask_body, 0)

    for g in range(N_GRP):
        qs_ref[g * STACK:(g + 1) * STACK, :] = split_heads(qb_ref[:, g * LANE:(g + 1) * LANE])
    top_rows = lax.broadcasted_iota(I32, (STACK, 1), 0) < TQ
    outs = _attend(
        i, qs_ref,
        lambda g, off: kb_ref[pl.ds(off, TQ), :],
        lambda g, off: vb_ref[pl.ds(off, TQ), :],
        lambda kt: selm_ref[kt],
        lambda g, jt: tab_ref[2 * g:2 * g + 2, jt],
        [jnp.where(top_rows, far_ref[A_HEADS + 2 * g], far_ref[A_HEADS + 2 * g + 1]) for g in range(N_GRP)],
        mf_ref, mn_ref, l_ref, acc_ref)
    for g, o in enumerate(outs):
        o_ref[:, g * LANE:(g + 1) * LANE] = jnp.where(lo, o[:TQ], o[TQ:]).astype(o_ref.dtype)


def _attn_b(proj, wi, near_b, far):
    b, s, _ = proj.shape
    top_k = min(TOPK_MAX, s // 4)
    assert top_k <= TQ and s % TQ == 0 and (s & (s - 1)) == 0
    n_kt = s // TQ
    assert 2 * n_kt <= 256 and B_HEADS == 2 * N_GRP
    qb_w = B_HEADS * HEAD_DIM
    qi_w = IDX_HEADS * HEAD_DIM
    return pl.pallas_call(
        functools.partial(_attn_b_kernel, top_k=float(top_k), idx_bits=int(math.log2(s))),
        grid_spec=pltpu.PrefetchScalarGridSpec(
            num_scalar_prefetch=1,
            grid=(b, n_kt),
            in_specs=[
                pl.BlockSpec((None, TQ, qb_w), lambda b_, i, f: (b_, i, COL_QB // qb_w)),
                pl.BlockSpec((None, s, LANE), lambda b_, i, f: (b_, 0, COL_KB // LANE)),
                pl.BlockSpec((None, s, LANE), lambda b_, i, f: (b_, 0, COL_VB // LANE)),
                pl.BlockSpec((None, TQ, qi_w), lambda b_, i, f: (b_, i, COL_QI // qi_w)),
                pl.BlockSpec((None, s, LANE), lambda b_, i, f: (b_, 0, COL_KI // LANE)),
                pl.BlockSpec((None, TQ, LANE), lambda b_, i, f: (b_, i, 0)),
                pl.BlockSpec((B_HEADS, NEAR_TILES, TQ, TQ), lambda b_, i, f: (0, 0, 0, 0)),
            ],
            out_specs=pl.BlockSpec((None, TQ, qb_w), lambda b_, i, f: (b_, i, 0)),
            scratch_shapes=[pltpu.VMEM((n_kt, TQ, TQ), I32), pltpu.VMEM((n_kt, TQ, TQ), F32)]
            + [pltpu.VMEM((TQ, LANE), I32)] * 3 + [pltpu.VMEM((TQ, LANE), F32)] + _attend_scratch(),
        ),
        out_shape=jax.ShapeDtypeStruct((b, s, qb_w), BF16),
        compiler_params=_cparams(("parallel", "arbitrary")),
    )(far, proj, proj, proj, proj, proj, wi, near_b)


PAIRS_C = 4


def _c_tables(rel_table):
    n_var = WIN_C // TQ_C
    length = WIN_C + TQ_C
    k = jnp.arange(length, dtype=I32)
    cr = jnp.where(k < WIN_C, k, k - length)
    off = (jnp.arange(n_var, dtype=I32) * TQ_C)[:, None]
    rel_idx = jnp.clip(off - cr[None, :], -C_REL_CLIP, C_REL_CLIP) + C_REL_CLIP
    g = rel_table[rel_idx].transpose(2, 0, 1)
    bias = _toeplitz(g, TQ_C, WIN_C)
    r = jnp.arange(TQ_C, dtype=I32)[None, :, None]
    c = jnp.arange(WIN_C, dtype=I32)[None, None, :]
    kc = jnp.floor_divide(c - off[:, :, None], CHUNK)
    ok = (kc <= r // CHUNK) & (kc >= r // CHUNK - C_LEFT_CHUNKS)
    return jnp.where(ok[None], bias, NEG)


def _attn_c_kernel(q_ref, k_ref, v_ref, tab_ref, o_ref):
    i = pl.program_id(2)
    n_var = WIN_C // TQ_C
    lane = lax.broadcasted_iota(I32, (1, LANE), 1)
    lo = lane < HEAD_DIM
    var = jnp.minimum(i, n_var - 1)
    ks = pl.multiple_of(jnp.maximum(i - (n_var - 1), 0) * TQ_C, TQ_C)
    for pp in range(PAIRS_C):
        cols = slice(pp * LANE, (pp + 1) * LANE)
        q = q_ref[:, cols]
        zq = jnp.zeros_like(q)
        qs = jnp.concatenate([jnp.where(lo, q, zq), jnp.where(lo, zq, q)], axis=0)
        s = _dot_nt(qs, k_ref[pl.ds(ks, WIN_C), cols])
        s = s + jnp.concatenate([tab_ref[2 * pp, var], tab_ref[2 * pp + 1, var]], axis=0)
        m = jnp.max(s, axis=-1, keepdims=True)
        p = jnp.exp(s - m)
        l = jnp.sum(p, axis=-1, keepdims=True)
        pv = jnp.dot(p.astype(BF16), v_ref[pl.ds(ks, WIN_C), cols], preferred_element_type=F32) / l
        o_ref[:, cols] = jnp.where(lo, pv[:TQ_C], pv[TQ_C:]).astype(o_ref.dtype)


def _attn_c(proj, tab):
    b, s, _ = proj.shape
    assert s >= WIN_C and s % TQ_C == 0
    n_grp = C_HEADS // 2 // PAIRS_C
    n_var = WIN_C // TQ_C
    wblk = PAIRS_C * LANE
    tab = tab.reshape(n_grp, 2 * PAIRS_C, n_var, TQ_C, WIN_C)
    return pl.pallas_call(
        _attn_c_kernel,
        grid=(n_grp, b, s // TQ_C),
        in_specs=[
            pl.BlockSpec((None, TQ_C, wblk), lambda h, b_, i: (b_, i, h)),
            pl.BlockSpec((None, s, wblk), lambda h, b_, i: (b_, 0, n_grp + h)),
            pl.BlockSpec((None, s, wblk), lambda h, b_, i: (b_, 0, 2 * n_grp + h)),
            pl.BlockSpec((None, 2 * PAIRS_C, n_var, TQ_C, WIN_C), lambda h, b_, i: (h, 0, 0, 0, 0),
                         pipeline_mode=pl.Buffered(1)),
        ],
        out_specs=pl.BlockSpec((None, TQ_C, wblk), lambda h, b_, i: (b_, i, h)),
        out_shape=jax.ShapeDtypeStruct((b, s, D_MODEL), BF16),
        compiler_params=_cparams(("parallel", "parallel", "arbitrary")),
    )(proj, proj, proj, tab)


ROW_TILE = 8


def _rows_to_tiles(ref, x):
    rows = x.shape[0]
    for c in range(ROW_TILE):
        ref[pl.ds(c, rows, stride=ROW_TILE), :] = x[:, c * LANE:(c + 1) * LANE]


def _tiles_to_rows(ref, rows):
    return jnp.concatenate([ref[pl.ds(c, rows, stride=ROW_TILE), :] for c in range(ROW_TILE)], axis=1)


def _router_kernel(h_ref, g_ref, wr_ref, br_ref, xn_ref, ei_ref, cw_ref):
    xn = _rms(h_ref[...], g_ref[...])
    _rows_to_tiles(xn_ref, xn)
    logits = jnp.dot(xn, wr_ref[...], preferred_element_type=F32,
                     precision=lax.Precision.HIGHEST) + br_ref[...]
    lane = lax.broadcasted_iota(I32, (1, LANE), 1)
    lane_f = lane.astype(F32)
    big = float(LANE)
    is_g = lane < N_GROUPS
    lg = jnp.where(is_g, logits, -jnp.inf)
    gmax = jnp.max(lg, axis=-1, keepdims=True)
    g_sel = jnp.min(jnp.where(lg == gmax, lane_f, big), axis=-1, keepdims=True)
    p_gsel = 1.0 / jnp.sum(jnp.where(is_g, jnp.exp(logits - gmax), 0.0), axis=-1, keepdims=True)
    e_grp = ((lane - N_GROUPS) // EXPERTS_PER_GROUP).astype(F32)
    in_sel = (lane >= N_GROUPS) & (lane < N_GROUPS + N_EXPERTS) & (e_grp == g_sel)
    le = jnp.where(in_sel, logits, -jnp.inf)
    m1 = jnp.max(le, axis=-1, keepdims=True)
    i1 = jnp.min(jnp.where(le == m1, lane_f, big), axis=-1, keepdims=True)
    le2 = jnp.where(lane_f == i1, -jnp.inf, le)
    m2 = jnp.max(le2, axis=-1, keepdims=True)
    i2 = jnp.min(jnp.where(le2 == m2, lane_f, big), axis=-1, keepdims=True)
    t = jnp.exp(m2 - m1)
    c1 = p_gsel / (1.0 + t)
    c2 = p_gsel * t / (1.0 + t)
    a = jnp.minimum(i1, i2) - N_GROUPS - EXPERTS_PER_GROUP * g_sel
    b = jnp.maximum(i1, i2) - N_GROUPS - EXPERTS_PER_GROUP * g_sel
    pair = a * (2 * EXPERTS_PER_GROUP - 1 - a) * 0.5 + (b - a - 1.0)
    cls = g_sel * PAIRS_PER_GROUP + pair
    ei_ref[...] = jnp.broadcast_to(cls, ei_ref.shape).astype(I32)
    first_is_lo = i1 < i2
    cw_ref[...] = jnp.where(lane == 0, jnp.where(first_is_lo, c1, c2), jnp.where(first_is_lo, c2, c1))


def _router(h, g, w_rg, b_rg, w_re, b_re, tm=256):
    n, d = h.shape
    pad = LANE - N_GROUPS - N_EXPERTS
    wr = jnp.concatenate([w_rg, w_re, jnp.zeros((d, pad), F32)], axis=1)
    br = jnp.concatenate([b_rg, b_re, jnp.zeros((pad,), F32)]).reshape(1, LANE)
    return pl.pallas_call(
        _router_kernel,
        grid=(n // tm,),
        in_specs=[pl.BlockSpec((tm, d), lambda i: (i, 0)),
                  pl.BlockSpec((1, d), lambda i: (0, 0)),
                  pl.BlockSpec((d, LANE), lambda i: (0, 0)),
                  pl.BlockSpec((1, LANE), lambda i: (0, 0))],
        out_specs=[pl.BlockSpec((tm * ROW_TILE, LANE), lambda i: (i, 0)),
                   pl.BlockSpec((tm, LANE), lambda i: (i, 0)),
                   pl.BlockSpec((tm, LANE), lambda i: (i, 0))],
        out_shape=[jax.ShapeDtypeStruct((n * ROW_TILE, LANE), F32),
                   jax.ShapeDtypeStruct((n, LANE), I32),
                   jax.ShapeDtypeStruct((n, LANE), F32)],
        compiler_params=_cparams(("parallel",)),
    )(h, g.reshape(1, d), wr, br)


N_CLASSES = N_GROUPS * PAIRS_PER_GROUP
_PAIR_AB = [(a, b) for a in range(EXPERTS_PER_GROUP) for b in range(a + 1, EXPERTS_PER_GROUP)]
CLASS_EXPERT_LO = [g * EXPERTS_PER_GROUP + a for g in range(N_GROUPS) for a, _ in _PAIR_AB]
CLASS_EXPERT_HI = [g * EXPERTS_PER_GROUP + b for g in range(N_GROUPS) for _, b in _PAIR_AB]
TB_MOE = 256


def _rank_kernel(cls_ref, rank_ref, cnt_ref, carry_ref):
    tb = cls_ref.shape[0]

    @pl.when(pl.program_id(0) == 0)
    def _():
        carry_ref[...] = jnp.zeros(carry_ref.shape, F32)

    onehot = cls_ref[...] == lax.broadcasted_iota(I32, (tb, LANE), 1)
    tril = lax.broadcasted_iota(I32, (tb, tb), 0) >= lax.broadcasted_iota(I32, (tb, tb), 1)
    pref = jnp.dot(jnp.where(tril, 1.0, 0.0).astype(BF16), jnp.where(onehot, 1.0, 0.0).astype(BF16),
                   preferred_element_type=F32)
    carry = carry_ref[...]
    rank = jnp.sum(jnp.where(onehot, pref + carry, 0.0), axis=-1, keepdims=True) - 1.0
    rank_ref[...] = jnp.broadcast_to(rank, rank_ref.shape).astype(I32)
    carry = carry + pref[tb - 1:tb, :]
    carry_ref[...] = carry
    cnt_ref[...] = carry


def _class_rank(cls, tb=512):
    n = cls.shape[0]
    return pl.pallas_call(
        _rank_kernel,
        grid=(n // tb,),
        in_specs=[pl.BlockSpec((tb, LANE), lambda i: (i, 0))],
        out_specs=[pl.BlockSpec((tb, LANE), lambda i: (i, 0)), pl.BlockSpec((1, LANE), lambda i: (0, 0))],
        out_shape=[jax.ShapeDtypeStruct((n, LANE), I32), jax.ShapeDtypeStruct((1, LANE), F32)],
        scratch_shapes=[pltpu.VMEM((1, LANE), F32)],
        compiler_params=_cparams(("arbitrary",)),
    )(cls)


def _moe_schedule(counts, n_tok, tm):
    counts = counts[:N_CLASSES].astype(I32)
    tiles_per = (counts + tm - 1) // tm
    tile_end = jnp.cumsum(tiles_per)
    tile_start = tile_end - tiles_per
    start_row = jnp.zeros((LANE,), I32).at[:N_CLASSES].set(tile_start * tm)
    n_tiles = n_tok // tm + N_CLASSES
    t = jnp.arange(n_tiles, dtype=I32)
    tc = jnp.searchsorted(tile_end, t, side="right").astype(I32)
    live = tc < N_CLASSES
    tc = jnp.minimum(tc, N_CLASSES - 1)
    nv = jnp.where(live, jnp.clip(counts[tc] - (t - tile_start[tc]) * tm, 0, tm), 0).astype(I32)
    e_lo = jnp.asarray(CLASS_EXPERT_LO, I32)[tc]
    e_hi = jnp.asarray(CLASS_EXPERT_HI, I32)[tc]
    return start_row, e_lo, e_hi, nv


def _tok(i, width=ROW_TILE):
    return pl.ds(i * width, width)


def _slot_of(start_ref, cls_ref, rank_ref, r):
    return start_ref[cls_ref[0, r]] + rank_ref[0, r]


def _dispatch_kernel(start_ref, cls_ref, rank_ref, x_hbm, xs_init, xs_hbm, sem, *, tb, n_blk):
    del xs_init
    i = pl.program_id(0)
    slot = i % 2

    def wait(sl):
        pltpu.make_async_copy(x_hbm.at[pl.ds(0, tb * ROW_TILE)], xs_hbm.at[pl.ds(0, tb * ROW_TILE)],
                              sem.at[sl]).wait()

    for r in range(tb):
        pltpu.make_async_copy(x_hbm.at[_tok(i * tb + r)], xs_hbm.at[_tok(_slot_of(start_ref, cls_ref, rank_ref, r))],
                              sem.at[slot]).start()

    @pl.when(i > 0)
    def _():
        wait(1 - slot)

    @pl.when(i == n_blk - 1)
    def _():
        wait(slot)


def _dispatch(xn, start_row, cls_sm, rank_sm, n_slots, tb=TB_MOE):
    n_blk = cls_sm.shape[0]
    xs0 = jnp.zeros((n_slots * ROW_TILE, LANE), F32)
    idx_spec = pl.BlockSpec((None, 1, tb), lambda i, st: (i, 0, 0), memory_space=pltpu.SMEM)
    return pl.pallas_call(
        functools.partial(_dispatch_kernel, tb=tb, n_blk=n_blk),
        grid_spec=pltpu.PrefetchScalarGridSpec(
            num_scalar_prefetch=1,
            grid=(n_blk,),
            in_specs=[idx_spec, idx_spec, pl.BlockSpec(memory_space=pl.ANY), pl.BlockSpec(memory_space=pl.ANY)],
            out_specs=pl.BlockSpec(memory_space=pl.ANY),
            scratch_shapes=[pltpu.SemaphoreType.DMA((2,))],
        ),
        out_shape=jax.ShapeDtypeStruct(xs0.shape, F32),
        input_output_aliases={4: 0},
        compiler_params=_cparams(("arbitrary",)),
    )(start_row, cls_sm, rank_sm, xn, xs0)


def _expert_kernel(elo_ref, ehi_ref, nv_ref, xs_ref, wga, wua, wda, wgb, wub, wdb, ys_ref,
                   wga_bf, wua_bf, wda_bf, wgb_bf, wub_bf, wdb_bf, *, tm):
    t = pl.program_id(0)
    prev = jnp.maximum(t - 1, 0)

    @pl.when(nv_ref[t] > 0)
    def _():
        @pl.when((t == 0) | (elo_ref[t] != elo_ref[prev]))
        def _():
            wga_bf[...] = wga[...].astype(BF16)
            wua_bf[...] = wua[...].astype(BF16)
            wda_bf[...] = wda[...].astype(BF16)

        @pl.when((t == 0) | (ehi_ref[t] != ehi_ref[prev]))
        def _():
            wgb_bf[...] = wgb[...].astype(BF16)
            wub_bf[...] = wub[...].astype(BF16)
            wdb_bf[...] = wdb[...].astype(BF16)

        x = _tiles_to_rows(xs_ref, tm).astype(BF16)
        for half, (wg, wu, wd) in enumerate(((wga_bf, wua_bf, wda_bf), (wgb_bf, wub_bf, wdb_bf))):
            g = jnp.dot(x, wg[...], preferred_element_type=F32)
            u = jnp.dot(x, wu[...], preferred_element_type=F32)
            act = (g * jax.nn.sigmoid(g)) * u
            y = jnp.dot(act.astype(BF16), wd[...], preferred_element_type=F32)
            for c in range(ROW_TILE):
                ys_ref[pl.ds(half * ROW_TILE + c, tm, stride=2 * ROW_TILE), :] = y[:, c * LANE:(c + 1) * LANE]

    @pl.when(nv_ref[t] == 0)
    def _():
        ys_ref[...] = jnp.zeros(ys_ref.shape, F32)


def _moe_experts(xs, e_lo, e_hi, nv, layer, wg, wu, wd, tm=TM_MOE):
    d = wg.shape[2]
    assert d == ROW_TILE * LANE
    n_tiles = nv.shape[0]

    def w_spec(rows, cols, which):
        return pl.BlockSpec((None, None, rows, cols), lambda t, lo, hi, nv_: (layer, (lo, hi)[which][t], 0, 0))

    w_specs = [w_spec(d, D_EXPERT, k) for k in (0, 0)] + [w_spec(D_EXPERT, d, 0)]
    w_specs += [w_spec(d, D_EXPERT, k) for k in (1, 1)] + [w_spec(D_EXPERT, d, 1)]
    w_bf = [pltpu.VMEM((d, D_EXPERT), BF16), pltpu.VMEM((d, D_EXPERT), BF16), pltpu.VMEM((D_EXPERT, d), BF16)]
    return pl.pallas_call(
        functools.partial(_expert_kernel, tm=tm),
        grid_spec=pltpu.PrefetchScalarGridSpec(
            num_scalar_prefetch=3,
            grid=(n_tiles,),
            in_specs=[pl.BlockSpec((tm * ROW_TILE, LANE), lambda t, lo, hi, nv_: (t, 0))] + w_specs,
            out_specs=pl.BlockSpec((tm * 2 * ROW_TILE, LANE), lambda t, lo, hi, nv_: (t, 0)),
            scratch_shapes=w_bf + w_bf,
        ),
        out_shape=jax.ShapeDtypeStruct((n_tiles * tm * 2 * ROW_TILE, LANE), F32),
        compiler_params=_cparams(("arbitrary",)),
    )(e_lo, e_hi, nv, xs, wg, wu, wd, wg, wu, wd)


def _combine_kernel(start_ref, cls_cur, rank_cur, cls_next, rank_next, ys_hbm, h_ref, cw_ref, g_ref, o_ref,
                    ybuf, sem, *, tb, n_blk, final):
    i = pl.program_id(0)
    slot = i % 2
    other = 1 - slot
    pair = 2 * ROW_TILE

    def fetch(cls_ref, rank_ref, r, sl):
        return pltpu.make_async_copy(ys_hbm.at[_tok(_slot_of(start_ref, cls_ref, rank_ref, r), pair)],
                                     ybuf.at[sl, _tok(r, pair)], sem.at[sl])

    def wait(sl):
        pltpu.make_async_copy(ys_hbm.at[pl.ds(0, tb * pair)], ybuf.at[sl], sem.at[sl]).wait()

    @pl.when(i == 0)
    def _():
        def body(r, c):
            fetch(cls_cur, rank_cur, r, 0).start()
            return c
        lax.fori_loop(0, tb, body, 0)

    wait(slot)
    for r in range(tb):
        fetch(cls_next, rank_next, r, other).start()
    yb = ybuf.at[slot]
    y_lo = jnp.concatenate([yb[pl.ds(c, tb, stride=pair), :] for c in range(ROW_TILE)], axis=1)
    y_hi = jnp.concatenate([yb[pl.ds(ROW_TILE + c, tb, stride=pair), :] for c in range(ROW_TILE)], axis=1)
    cw = cw_ref[...]
    out = h_ref[...] + cw[:, 0:1] * y_lo + cw[:, 1:2] * y_hi
    if final:
        out = _rms(out, g_ref[...])
    o_ref[...] = out

    @pl.when(i == n_blk - 1)
    def _():
        wait(other)


def _combine(h, ys, cw, g, start_row, cls_sm, rank_sm, final, tb=TB_MOE):
    n, d = h.shape
    n_blk = n // tb
    cls_x = jnp.concatenate([cls_sm, cls_sm[-1:]], axis=0)
    rank_x = jnp.concatenate([rank_sm, rank_sm[-1:]], axis=0)

    def idx_spec(shift):
        return pl.BlockSpec((None, 1, tb), lambda i, st: (i + shift, 0, 0), memory_space=pltpu.SMEM)

    return pl.pallas_call(
        functools.partial(_combine_kernel, tb=tb, n_blk=n_blk, final=final),
        grid_spec=pltpu.PrefetchScalarGridSpec(
            num_scalar_prefetch=1,
            grid=(n_blk,),
            in_specs=[idx_spec(0), idx_spec(0), idx_spec(1), idx_spec(1),
                      pl.BlockSpec(memory_space=pl.ANY),
                      pl.BlockSpec((tb, d), lambda i, st: (i, 0)),
                      pl.BlockSpec((tb, LANE), lambda i, st: (i, 0)),
                      pl.BlockSpec((1, d), lambda i, st: (0, 0))],
            out_specs=pl.BlockSpec((tb, d), lambda i, st: (i, 0)),
            scratch_shapes=[pltpu.VMEM((2, tb * 2 * ROW_TILE, LANE), F32), pltpu.SemaphoreType.DMA((2,))],
        ),
        out_shape=jax.ShapeDtypeStruct((n, d), F32),
        compiler_params=_cparams(("arbitrary",)),
    )(start_row, cls_x, rank_x, cls_x, rank_x, ys, h, cw, g.reshape(1, d))


def _moe_layer(h, layer, ln_g, w_rg, b_rg, w_re, b_re, w_gate, w_up, w_down, final_g, final):
    n = h.shape[0]
    xn, cls, cw = _router(h, ln_g, w_rg, b_rg, w_re, b_re)
    rank, counts = _class_rank(cls)
    start_row, e_lo, e_hi, nv = _moe_schedule(counts[0], n, TM_MOE)
    cls_sm = cls[:, 0].reshape(n // TB_MOE, 1, TB_MOE)
    rank_sm = rank[:, 0].reshape(n // TB_MOE, 1, TB_MOE)
    xs = _dispatch(xn, start_row, cls_sm, rank_sm, nv.shape[0] * TM_MOE)
    ys = _moe_experts(xs, e_lo, e_hi, nv, layer, w_gate, w_up, w_down)
    return _combine(h, ys, cw, final_g, start_row, cls_sm, rank_sm, final)


def _ab_weights(w_in):
    scale = HEAD_DIM ** -0.5
    widths = (512, 512, 512, 512, 64, 64, 256, 64, 4)
    offs = [sum(widths[:j]) for j in range(len(widths) + 1)]
    q_a, k_a, v_a, q_b, k_b, v_b, q_i, k_i, w_i = [w_in[:, offs[j]:offs[j + 1]] for j in range(len(widths))]
    pad = jnp.zeros((w_in.shape[0], LANE - IDX_HEADS), w_in.dtype)
    main = jnp.concatenate([q_a * scale, k_a, v_a, q_b * scale, k_b, k_b, v_b, v_b, q_i, k_i, k_i], axis=1)
    return jnp.concatenate([main, w_i, pad], axis=1).astype(BF16)


def kernel(x, t5_bias, ln_mix, ln_ffn, ln_final, ab_w_in, ab_w_out, diff_lambda, diff_subln,
           c_w_in, c_w_out, c_rel_bias, moe_w_rg, moe_b_rg, moe_w_re, moe_b_re,
           moe_w_gate, moe_w_up, moe_w_down):
    b, s, d = x.shape
    n = b * s
    depth = ln_mix.shape[0]
    h = x.reshape(n, d)
    near, far = _t5_tables(t5_bias)
    for l in range(depth):
        if l % 2 == 0:
            e = l // 2
            lam_init = 0.8 - 0.6 * math.exp(-0.3 * l)
            proj, wi = _norm_proj(h, ln_mix[l], _ab_weights(ab_w_in[e]), AB_COLS)
            proj = proj.reshape(b, s, AB_COLS)
            y_a = _attn_a(proj, near[:A_HEADS], far, diff_lambda[e], diff_subln[e], lam_init)
            y_b = _attn_b(proj, wi.reshape(b, s, LANE), near[A_HEADS:], far)
            h = _out_proj(h, y_a.reshape(n, -1), 0, y_b.reshape(n, -1), 0, ab_w_out[e].astype(BF16))
        else:
            o = l // 2
            scale = HEAD_DIM ** -0.5
            w_in = jnp.concatenate([c_w_in[o][:, :d] * scale, c_w_in[o][:, d:]], axis=1).astype(BF16)
            (proj,) = _norm_proj(h, ln_mix[l], w_in, 3 * d)
            y = _attn_c(proj.reshape(b, s, 3 * d), _c_tables(c_rel_bias[o])).reshape(n, d)
            h = _out_proj(h, y, 0, y, 1, c_w_out[o].astype(BF16))
        h = _moe_layer(h, l, ln_ffn[l], moe_w_rg[l], moe_b_rg[l], moe_w_re[l], moe_b_re[l],
                       moe_w_gate, moe_w_up, moe_w_down, ln_final, l == depth - 1)
    return h.reshape(b, s, d)
```

```python
import functools
import math

import jax
import jax.numpy as jnp
from jax import lax
from jax.experimental import pallas as pl
from jax.experimental.pallas import tpu as pltpu

F32 = jnp.float32
BF16 = jnp.bfloat16
I32 = jnp.int32

D_MODEL = 1024
CHUNK = 64
HEAD_DIM = 64
RMS_EPS = 1e-6
A_HEADS = 4
A_V_DIM = 2 * HEAD_DIM
B_HEADS = 8
IDX_HEADS = 4
TOPK_MAX = 256
C_HEADS = D_MODEL // HEAD_DIM
C_LEFT_CHUNKS = 8
C_REL_CLIP = 128
T5_BUCKETS = 32
T5_MAX_DIST = 1024
N_GROUPS = 4
EXPERTS_PER_GROUP = 8
N_EXPERTS = N_GROUPS * EXPERTS_PER_GROUP
PAIRS_PER_GROUP = EXPERTS_PER_GROUP * (EXPERTS_PER_GROUP - 1) // 2
D_EXPERT = D_MODEL // 2
NEG = -1e30

LANE = 128
VMEM_LIMIT = 56 * 1024 * 1024

TQ = 256
NEAR_TILES = 4
TQ_C = 2 * CHUNK
WIN_C = (C_LEFT_CHUNKS + 2) * CHUNK
TM_MOE = 256

COL_QA, COL_KA, COL_VA, COL_QB, COL_KB, COL_VB, COL_QI, COL_KI, COL_WI = (
    0, 512, 1024, 1536, 2048, 2176, 2304, 2560, 2688)
AB_COLS = 2688


def _cparams(sem, vmem=VMEM_LIMIT):
    return pltpu.CompilerParams(dimension_semantics=sem, vmem_limit_bytes=vmem)


def _dot_nt(a, b):
    return lax.dot_general(a, b, (((1,), (1,)), ((), ())), preferred_element_type=F32)


def _rms(x, g):
    return (x * lax.rsqrt(jnp.mean(x * x, axis=-1, keepdims=True) + RMS_EPS)) * g


def _norm_proj_kernel(x_ref, g_ref, w_ref, o_ref, *tail_ref, n_main):
    xn = _rms(x_ref[...], g_ref[...]).astype(BF16)
    acc = jnp.dot(xn, w_ref[...], preferred_element_type=F32)
    o_ref[...] = acc[:, :n_main].astype(o_ref.dtype)
    if tail_ref:
        tail_ref[0][...] = acc[:, n_main:]


def _norm_proj(x, g, w, n_main, tm=256):
    n, d = x.shape
    n_all = w.shape[1]
    out_shape = [jax.ShapeDtypeStruct((n, n_main), BF16)]
    out_specs = [pl.BlockSpec((tm, n_main), lambda i: (i, 0))]
    if n_all > n_main:
        out_shape.append(jax.ShapeDtypeStruct((n, n_all - n_main), F32))
        out_specs.append(pl.BlockSpec((tm, n_all - n_main), lambda i: (i, 0)))
    return pl.pallas_call(
        functools.partial(_norm_proj_kernel, n_main=n_main),
        grid=(n // tm,),
        in_specs=[pl.BlockSpec((tm, d), lambda i: (i, 0)),
                  pl.BlockSpec((1, d), lambda i: (0, 0)),
                  pl.BlockSpec((d, n_all), lambda i: (0, 0))],
        out_specs=out_specs, out_shape=out_shape,
        compiler_params=_cparams(("parallel",)),
    )(x, g.reshape(1, d), w)


def _out_proj_kernel(h_ref, y0_ref, y1_ref, w0_ref, w1_ref, o_ref):
    acc = jnp.dot(y0_ref[...], w0_ref[...], preferred_element_type=F32)
    acc = acc + jnp.dot(y1_ref[...], w1_ref[...], preferred_element_type=F32)
    o_ref[...] = h_ref[...] + acc


def _out_proj(h, y0, c0, y1, c1, w, tm=512):
    n, d = h.shape
    half = d // 2
    return pl.pallas_call(
        _out_proj_kernel,
        grid=(n // tm,),
        in_specs=[pl.BlockSpec((tm, d), lambda i: (i, 0)),
                  pl.BlockSpec((tm, half), lambda i: (i, c0)),
                  pl.BlockSpec((tm, half), lambda i: (i, c1)),
                  pl.BlockSpec((half, d), lambda i: (0, 0)),
                  pl.BlockSpec((half, d), lambda i: (1, 0))],
        out_specs=pl.BlockSpec((tm, d), lambda i: (i, 0)),
        out_shape=jax.ShapeDtypeStruct((n, d), F32),
        compiler_params=_cparams(("parallel",)),
    )(h, y0, y1, w, w)


def _t5_bucket(rel):
    half = T5_BUCKETS // 2
    max_exact = half // 2
    ret = jnp.where(rel > 0, half, 0)
    n = jnp.abs(rel)
    nf = jnp.maximum(n, 1).astype(F32)
    large = max_exact + (jnp.log(nf / max_exact) / math.log(T5_MAX_DIST / max_exact)
                         * (half - max_exact)).astype(I32)
    large = jnp.minimum(large, half - 1)
    return (ret + jnp.where(n < max_exact, n, large)).astype(I32)


def _toeplitz(g, rows, width):
    length = g.shape[-1]
    assert width < length and rows <= length
    flat = jnp.tile(g, (1,) * (g.ndim - 1) + (rows,))[..., :rows * (length - 1)]
    return flat.reshape(g.shape[:-1] + (rows, length - 1))[..., :width]


def _t5_tables(t5_bias):
    span = NEAR_TILES * TQ
    length = span + TQ
    k = jnp.arange(length, dtype=I32)
    delta = jnp.where(k < span, k, k - length) - (span - TQ)
    g = t5_bias[_t5_bucket(delta)].T
    bias = _toeplitz(g, TQ, span)
    r = jnp.arange(TQ, dtype=I32)[:, None]
    kr = jnp.arange(span, dtype=I32)[None, :] - (span - TQ)
    ok = jnp.floor_divide(kr, CHUNK) <= (r // CHUNK)
    bias = jnp.where(ok[None], bias, NEG)
    near = bias.reshape(-1, TQ, NEAR_TILES, TQ).transpose(0, 2, 1, 3)
    far = t5_bias[_t5_bucket(jnp.asarray(-(span - TQ) - 1, I32))]
    return near, far


N_GRP = 4
STACK = 2 * TQ


def _attend(i, qs_ref, k_fn, v_fn, mask_fn, near_fn, far_cols, mf_ref, mn_ref, l_ref, acc_ref):
    n_far = jnp.maximum(i - (NEAR_TILES - 1), 0)
    mf_ref[...] = jnp.full(mf_ref.shape, NEG, F32)
    mn_ref[...] = jnp.full(mn_ref.shape, NEG, F32)

    def rows(g):
        return slice(g * STACK, (g + 1) * STACK)

    def logits(g, kt, jt):
        off = pl.multiple_of(kt * TQ, TQ)
        s = _dot_nt(qs_ref[rows(g), :], k_fn(g, off))
        extra = None if mask_fn is None else mask_fn(kt)[None]
        if jt is not None:
            extra = near_fn(g, jt) if extra is None else extra + near_fn(g, jt)
        if extra is not None:
            s = (s.reshape(2, TQ, TQ) + extra).reshape(STACK, TQ)
        return s

    def loops(far_fn, near_fn_):
        def far_body(kt, c):
            far_fn(kt)
            return c

        def near_body(kt, c):
            near_fn_(kt)
            return c

        lax.fori_loop(0, n_far, far_body, 0)
        lax.fori_loop(n_far, i + 1, near_body, 0)

    def pass1(m_ref, near):
        def fn(kt):
            jt = kt - (i - (NEAR_TILES - 1)) if near else None
            for g in range(N_GRP):
                s = logits(g, kt, jt)
                m_ref[rows(g), :] = jnp.maximum(m_ref[rows(g), :], jnp.maximum(s[:, :LANE], s[:, LANE:]))
        return fn

    loops(pass1(mf_ref, False), pass1(mn_ref, True))

    for g in range(N_GRP):
        m = jnp.maximum(jnp.max(mf_ref[rows(g), :], axis=-1, keepdims=True) + far_cols[g],
                        jnp.max(mn_ref[rows(g), :], axis=-1, keepdims=True))
        m_near = jnp.broadcast_to(m, (STACK, LANE))
        mn_ref[rows(g), :] = m_near
        mf_ref[rows(g), :] = m_near - far_cols[g]
    l_ref[...] = jnp.zeros(l_ref.shape, F32)
    acc_ref[...] = jnp.zeros(acc_ref.shape, F32)

    def pass2(shift_ref, near):
        def fn(kt):
            jt = kt - (i - (NEAR_TILES - 1)) if near else None
            off = pl.multiple_of(kt * TQ, TQ)
            for g in range(N_GRP):
                s = logits(g, kt, jt)
                sh = shift_ref[rows(g), :]
                p0 = jnp.exp(s[:, :LANE] - sh)
                p1 = jnp.exp(s[:, LANE:] - sh)
                l_ref[rows(g), :] += p0 + p1
                p = jnp.concatenate([p0, p1], axis=1).astype(BF16)
                acc_ref[rows(g), :] += jnp.dot(p, v_fn(g, off), preferred_element_type=F32)
        return fn

    loops(pass2(mf_ref, False), pass2(mn_ref, True))
    return [acc_ref[rows(g), :] / jnp.sum(l_ref[rows(g), :], axis=-1, keepdims=True) for g in range(N_GRP)]


def _attend_scratch():
    return [pltpu.VMEM((N_GRP * STACK, LANE), BF16)] + [pltpu.VMEM((N_GRP * STACK, LANE), F32)] * 4


def _split_lanes(x):
    lo = lax.broadcasted_iota(I32, (1, LANE), 1) < HEAD_DIM
    z = jnp.zeros_like(x)
    return jnp.concatenate([jnp.where(lo, x, z), jnp.where(lo, z, x)], axis=0)


def _attn_a_kernel(far_ref, q_ref, k_ref, v_ref, tab_ref, lam_ref, sub_ref, o_ref,
                   qs_ref, mf_ref, mn_ref, l_ref, acc_ref, *, lam_init):
    i = pl.program_id(1)
    for g in range(A_HEADS):
        qs_ref[g * STACK:(g + 1) * STACK, :] = _split_lanes(q_ref[:, g * LANE:(g + 1) * LANE])
    outs = _attend(
        i, qs_ref,
        lambda g, off: k_ref[pl.ds(off, TQ), g * LANE:(g + 1) * LANE],
        lambda g, off: v_ref[pl.ds(off, TQ), g * LANE:(g + 1) * LANE],
        None,
        lambda g, jt: tab_ref[g, jt][None],
        [jnp.full((STACK, 1), far_ref[g], F32) for g in range(A_HEADS)],
        mf_ref, mn_ref, l_ref, acc_ref)
    lam = lam_ref[...]
    lam_full = (jnp.exp(jnp.sum(lam[0:1] * lam[1:2], axis=-1, keepdims=True))
                - jnp.exp(jnp.sum(lam[2:3] * lam[3:4], axis=-1, keepdims=True)) + lam_init)
    for g, o in enumerate(outs):
        d = o[:TQ] - lam_full * o[TQ:]
        o_ref[:, g * LANE:(g + 1) * LANE] = (_rms(d, sub_ref[...]) * (1.0 - lam_init)).astype(o_ref.dtype)


def _attn_a(proj, near, far, lam, subln, lam_init):
    b, s, _ = proj.shape
    assert A_HEADS == N_GRP
    wide = A_HEADS * A_V_DIM
    return pl.pallas_call(
        functools.partial(_attn_a_kernel, lam_init=lam_init),
        grid_spec=pltpu.PrefetchScalarGridSpec(
            num_scalar_prefetch=1,
            grid=(b, s // TQ),
            in_specs=[
                pl.BlockSpec((None, TQ, wide), lambda b_, i, f: (b_, i, COL_QA // wide)),
                pl.BlockSpec((None, s, wide), lambda b_, i, f: (b_, 0, COL_KA // wide)),
                pl.BlockSpec((None, s, wide), lambda b_, i, f: (b_, 0, COL_VA // wide)),
                pl.BlockSpec((A_HEADS, NEAR_TILES, TQ, TQ), lambda b_, i, f: (0, 0, 0, 0)),
                pl.BlockSpec((4, HEAD_DIM), lambda b_, i, f: (0, 0)),
                pl.BlockSpec((1, A_V_DIM), lambda b_, i, f: (0, 0)),
            ],
            out_specs=pl.BlockSpec((None, TQ, wide), lambda b_, i, f: (b_, i, 0)),
            scratch_shapes=_attend_scratch(),
        ),
        out_shape=jax.ShapeDtypeStruct((b, s, wide), BF16),
        compiler_params=_cparams(("parallel", "arbitrary")),
    )(far, proj, proj, proj, near, lam, subln.reshape(1, A_V_DIM))


def _attn_b_kernel(far_ref, qb_ref, kb_ref, vb_ref, qi_ref, ki_ref, wi_ref, tab_ref, o_ref,
                   keys_ref, selm_ref, cand_ref, tu_ref, trial_ref, cnt_ref, qs_ref, mf_ref, mn_ref, l_ref, acc_ref,
                   *, top_k, idx_bits):
    i = pl.program_id(1)
    n_kt = i + 1
    lane = lax.broadcasted_iota(I32, (1, LANE), 1)
    lo = lane < HEAD_DIM
    int_min = jnp.int32(-2 ** 31)
    split_heads = _split_lanes

    row = lax.broadcasted_iota(I32, (TQ, TQ), 0)
    col = lax.broadcasted_iota(I32, (TQ, TQ), 1)
    chunk_gap = (col // CHUNK) - (row // CHUNK)

    def causal(kt):
        return chunk_gap <= jnp.where(kt < i, jnp.int32(TQ), jnp.int32(0))

    qi = qi_ref[...]
    qis = jnp.concatenate([split_heads(qi[:, :LANE]), split_heads(qi[:, LANE:])], axis=0)
    wi = wi_ref[...] * ((IDX_HEADS ** -0.5) * (HEAD_DIM ** -0.5))

    def score_body(kt, c):
        off = pl.multiple_of(kt * TQ, TQ)
        r = jnp.maximum(_dot_nt(qis, ki_ref[pl.ds(off, TQ), :]), 0.0)
        isc = r[0:TQ] * wi[:, 0:1]
        for hh in range(1, IDX_HEADS):
            isc = isc + r[hh * TQ:(hh + 1) * TQ] * wi[:, hh:hh + 1]
        isc = jnp.where(isc == 0.0, 0.0, isc)
        isc = jnp.where(causal(kt), isc, -jnp.inf)
        bits = lax.bitcast_convert_type(isc, I32)
        keys_ref[kt] = bits ^ ((bits >> 31) & jnp.int32(0x7FFFFFFF))
        return c

    lax.fori_loop(0, n_kt, score_body, 0)

    ones = jnp.ones((LANE, LANE), BF16)
    halves = (slice(0, LANE), slice(LANE, TQ))

    def count(pred_fn):
        cnt_ref[...] = jnp.zeros(cnt_ref.shape, F32)

        def body(kt, c):
            kk = keys_ref[kt]
            cnt_ref[...] += sum(jnp.where(pred_fn(kt, kk[:, hs], hs), 1.0, 0.0) for hs in halves)
            return c

        lax.fori_loop(0, n_kt, body, 0)
        return jnp.dot(cnt_ref[...].astype(BF16), ones, preferred_element_type=F32)

    tu_ref[...] = jnp.zeros(tu_ref.shape, I32)

    def thr_body(bi, c):
        cand_u = tu_ref[...] | (jnp.int32(1) << (31 - bi))
        cand_ref[...] = cand_u ^ int_min
        cnt = count(lambda kt, kk, hs: kk >= cand_ref[...])
        tu_ref[...] = jnp.where(cnt >= top_k, cand_u, tu_ref[...])
        return c

    lax.fori_loop(0, 32, thr_body, 0)
    tu_ref[...] = tu_ref[...] ^ int_min
    n_ge = count(lambda kt, kk, hs: kk >= tu_ref[...])
    n_gt = count(lambda kt, kk, hs: kk > tu_ref[...])

    def key_index(kt, hs):
        return lax.broadcasted_iota(I32, (TQ, LANE), 1) + (kt * TQ + hs.start)

    cand_ref[...] = jnp.full(cand_ref.shape, 2 ** idx_bits - 1, I32)

    @pl.when(jnp.max(n_ge) > top_k)
    def _():
        cnt_hi = top_k - n_gt
        cand_ref[...] = jnp.zeros(cand_ref.shape, I32)

        def tie_body(bi, c):
            trial_ref[...] = cand_ref[...] | (jnp.int32(1) << (idx_bits - 1 - bi))
            cnt = count(lambda kt, kk, hs: (kk == tu_ref[...]) & (key_index(kt, hs) <= trial_ref[...]))
            cand_ref[...] = jnp.where(cnt <= cnt_hi, trial_ref[...], cand_ref[...])
            return c

        lax.fori_loop(0, idx_bits, tie_body, 0)

    def mask_body(kt, c):
        kk = keys_ref[kt]
        lim = jnp.where(kt < i, jnp.int32(TQ), jnp.int32(0))
        for hs in halves:
            k_ = kk[:, hs]
            sel = (k_ > tu_ref[...]) | ((k_ == tu_ref[...]) & (key_index(kt, hs) <= cand_ref[...]))
            gap = ((lax.broadcasted_iota(I32, (TQ, LANE), 1) + hs.start) // CHUNK
                   - lax.broadcasted_iota(I32, (TQ, LANE), 0) // CHUNK)
            selm_ref[kt, :, hs] = jnp.where(sel & (gap <= lim), 0.0, NEG)
        return c

    lax.fori_loop(0, n_kt, mask_body, 0)

    for g in range(N_GRP):
        qs_ref[g * STACK:(g + 1) * STACK, :] = split_heads(qb_ref[:, g * LANE:(g + 1) * LANE])
    top_rows = lax.broadcasted_iota(I32, (STACK, 1), 0) < TQ
    outs = _attend(
        i, qs_ref,
        lambda g, off: kb_ref[pl.ds(off, TQ), :],
        lambda g, off: vb_ref[pl.ds(off, TQ), :],
        lambda kt: selm_ref[kt],
        lambda g, jt: tab_ref[2 * g:2 * g + 2, jt],
        [jnp.where(top_rows, far_ref[A_HEADS + 2 * g], far_ref[A_HEADS + 2 * g + 1]) for g in range(N_GRP)],
        mf_ref, mn_ref, l_ref, acc_ref)
    for g, o in enumerate(outs):
        o_ref[:, g * LANE:(g + 1) * LANE] = jnp.where(lo, o[:TQ], o[TQ:]).astype(o_ref.dtype)


def _attn_b(proj, wi, near_b, far):
    b, s, _ = proj.shape
    top_k = min(TOPK_MAX, s // 4)
    assert top_k <= TQ and s % TQ == 0 and (s & (s - 1)) == 0
    n_kt = s // TQ
    assert 2 * n_kt <= 256 and B_HEADS == 2 * N_GRP
    qb_w = B_HEADS * HEAD_DIM
    qi_w = IDX_HEADS * HEAD_DIM
    return pl.pallas_call(
        functools.partial(_attn_b_kernel, top_k=float(top_k), idx_bits=int(math.log2(s))),
        grid_spec=pltpu.PrefetchScalarGridSpec(
            num_scalar_prefetch=1,
            grid=(b, n_kt),
            in_specs=[
                pl.BlockSpec((None, TQ, qb_w), lambda b_, i, f: (b_, i, COL_QB // qb_w)),
                pl.BlockSpec((None, s, LANE), lambda b_, i, f: (b_, 0, COL_KB // LANE)),
                pl.BlockSpec((None, s, LANE), lambda b_, i, f: (b_, 0, COL_VB // LANE)),
                pl.BlockSpec((None, TQ, qi_w), lambda b_, i, f: (b_, i, COL_QI // qi_w)),
                pl.BlockSpec((None, s, LANE), lambda b_, i, f: (b_, 0, COL_KI // LANE)),
                pl.BlockSpec((None, TQ, LANE), lambda b_, i, f: (b_, i, 0)),
                pl.BlockSpec((B_HEADS, NEAR_TILES, TQ, TQ), lambda b_, i, f: (0, 0, 0, 0)),
            ],
            out_specs=pl.BlockSpec((None, TQ, qb_w), lambda b_, i, f: (b_, i, 0)),
            scratch_shapes=[pltpu.VMEM((n_kt, TQ, TQ), I32), pltpu.VMEM((n_kt, TQ, TQ), F32)]
            + [pltpu.VMEM((TQ, LANE), I32)] * 3 + [pltpu.VMEM((TQ, LANE), F32)] + _attend_scratch(),
        ),
        out_shape=jax.ShapeDtypeStruct((b, s, qb_w), BF16),
        compiler_params=_cparams(("parallel", "arbitrary")),
    )(far, proj, proj, proj, proj, proj, wi, near_b)


PAIRS_C = 4


def _c_tables(rel_table):
    n_var = WIN_C // TQ_C
    length = WIN_C + TQ_C
    k = jnp.arange(length, dtype=I32)
    cr = jnp.where(k < WIN_C, k, k - length)
    off = (jnp.arange(n_var, dtype=I32) * TQ_C)[:, None]
    rel_idx = jnp.clip(off - cr[None, :], -C_REL_CLIP, C_REL_CLIP) + C_REL_CLIP
    g = rel_table[rel_idx].transpose(2, 0, 1)
    bias = _toeplitz(g, TQ_C, WIN_C)
    r = jnp.arange(TQ_C, dtype=I32)[None, :, None]
    c = jnp.arange(WIN_C, dtype=I32)[None, None, :]
    kc = jnp.floor_divide(c - off[:, :, None], CHUNK)
    ok = (kc <= r // CHUNK) & (kc >= r // CHUNK - C_LEFT_CHUNKS)
    return jnp.where(ok[None], bias, NEG)


def _attn_c_kernel(q_ref, k_ref, v_ref, tab_ref, o_ref):
    i = pl.program_id(2)
    n_var = WIN_C // TQ_C
    lane = lax.broadcasted_iota(I32, (1, LANE), 1)
    lo = lane < HEAD_DIM
    var = jnp.minimum(i, n_var - 1)
    ks = pl.multiple_of(jnp.maximum(i - (n_var - 1), 0) * TQ_C, TQ_C)
    for pp in range(PAIRS_C):
        cols = slice(pp * LANE, (pp + 1) * LANE)
        q = q_ref[:, cols]
        zq = jnp.zeros_like(q)
        qs = jnp.concatenate([jnp.where(lo, q, zq), jnp.where(lo, zq, q)], axis=0)
        s = _dot_nt(qs, k_ref[pl.ds(ks, WIN_C), cols])
        s = s + jnp.concatenate([tab_ref[2 * pp, var], tab_ref[2 * pp + 1, var]], axis=0)
        m = jnp.max(s, axis=-1, keepdims=True)
        p = jnp.exp(s - m)
        l = jnp.sum(p, axis=-1, keepdims=True)
        pv = jnp.dot(p.astype(BF16), v_ref[pl.ds(ks, WIN_C), cols], preferred_element_type=F32) / l
        o_ref[:, cols] = jnp.where(lo, pv[:TQ_C], pv[TQ_C:]).astype(o_ref.dtype)


def _attn_c(proj, tab):
    b, s, _ = proj.shape
    assert s >= WIN_C and s % TQ_C == 0
    n_grp = C_HEADS // 2 // PAIRS_C
    n_var = WIN_C // TQ_C
    wblk = PAIRS_C * LANE
    tab = tab.reshape(n_grp, 2 * PAIRS_C, n_var, TQ_C, WIN_C)
    return pl.pallas_call(
        _attn_c_kernel,
        grid=(n_grp, b, s // TQ_C),
        in_specs=[
            pl.BlockSpec((None, TQ_C, wblk), lambda h, b_, i: (b_, i, h)),
            pl.BlockSpec((None, s, wblk), lambda h, b_, i: (b_, 0, n_grp + h)),
            pl.BlockSpec((None, s, wblk), lambda h, b_, i: (b_, 0, 2 * n_grp + h)),
            pl.BlockSpec((None, 2 * PAIRS_C, n_var, TQ_C, WIN_C), lambda h, b_, i: (h, 0, 0, 0, 0),
                         pipeline_mode=pl.Buffered(1)),
        ],
        out_specs=pl.BlockSpec((None, TQ_C, wblk), lambda h, b_, i: (b_, i, h)),
        out_shape=jax.ShapeDtypeStruct((b, s, D_MODEL), BF16),
        compiler_params=_cparams(("parallel", "parallel", "arbitrary")),
    )(proj, proj, proj, tab)


ROW_TILE = 8


def _rows_to_tiles(ref, x):
    rows = x.shape[0]
    for c in range(ROW_TILE):
        ref[pl.ds(c, rows, stride=ROW_TILE), :] = x[:, c * LANE:(c + 1) * LANE]


def _tiles_to_rows(ref, rows):
    return jnp.concatenate([ref[pl.ds(c, rows, stride=ROW_TILE), :] for c in range(ROW_TILE)], axis=1)


def _router_kernel(h_ref, g_ref, wr_ref, br_ref, ei_ref, cw_ref):
    xn = _rms(h_ref[...], g_ref[...])
    logits = jnp.dot(xn, wr_ref[...], preferred_element_type=F32,
                     precision=lax.Precision.HIGHEST) + br_ref[...]
    lane = lax.broadcasted_iota(I32, (1, LANE), 1)
    lane_f = lane.astype(F32)
    big = float(LANE)
    is_g = lane < N_GROUPS
    lg = jnp.where(is_g, logits, -jnp.inf)
    gmax = jnp.max(lg, axis=-1, keepdims=True)
    g_sel = jnp.min(jnp.where(lg == gmax, lane_f, big), axis=-1, keepdims=True)
    p_gsel = 1.0 / jnp.sum(jnp.where(is_g, jnp.exp(logits - gmax), 0.0), axis=-1, keepdims=True)
    e_grp = ((lane - N_GROUPS) // EXPERTS_PER_GROUP).astype(F32)
    in_sel = (lane >= N_GROUPS) & (lane < N_GROUPS + N_EXPERTS) & (e_grp == g_sel)
    le = jnp.where(in_sel, logits, -jnp.inf)
    m1 = jnp.max(le, axis=-1, keepdims=True)
    i1 = jnp.min(jnp.where(le == m1, lane_f, big), axis=-1, keepdims=True)
    le2 = jnp.where(lane_f == i1, -jnp.inf, le)
    m2 = jnp.max(le2, axis=-1, keepdims=True)
    i2 = jnp.min(jnp.where(le2 == m2, lane_f, big), axis=-1, keepdims=True)
    t = jnp.exp(m2 - m1)
    c1 = p_gsel / (1.0 + t)
    c2 = p_gsel * t / (1.0 + t)
    a = jnp.minimum(i1, i2) - N_GROUPS - EXPERTS_PER_GROUP * g_sel
    b = jnp.maximum(i1, i2) - N_GROUPS - EXPERTS_PER_GROUP * g_sel
    pair = a * (2 * EXPERTS_PER_GROUP - 1 - a) * 0.5 + (b - a - 1.0)
    cls = g_sel * PAIRS_PER_GROUP + pair
    ei_ref[...] = jnp.broadcast_to(cls, ei_ref.shape).astype(I32)
    first_is_lo = i1 < i2
    cw_ref[...] = jnp.where(lane == 0, jnp.where(first_is_lo, c1, c2), jnp.where(first_is_lo, c2, c1))


def _router(h, g, w_rg, b_rg, w_re, b_re, tm=256):
    n, d = h.shape
    pad = LANE - N_GROUPS - N_EXPERTS
    wr = jnp.concatenate([w_rg, w_re, jnp.zeros((d, pad), F32)], axis=1)
    br = jnp.concatenate([b_rg, b_re, jnp.zeros((pad,), F32)]).reshape(1, LANE)
    return pl.pallas_call(
        _router_kernel,
        grid=(n // tm,),
        in_specs=[pl.BlockSpec((tm, d), lambda i: (i, 0)),
                  pl.BlockSpec((1, d), lambda i: (0, 0)),
                  pl.BlockSpec((d, LANE), lambda i: (0, 0)),
                  pl.BlockSpec((1, LANE), lambda i: (0, 0))],
        out_specs=[pl.BlockSpec((tm, LANE), lambda i: (i, 0)),
                   pl.BlockSpec((tm, LANE), lambda i: (i, 0))],
        out_shape=[jax.ShapeDtypeStruct((n, LANE), I32),
                   jax.ShapeDtypeStruct((n, LANE), F32)],
        compiler_params=_cparams(("parallel",)),
    )(h, g.reshape(1, d), wr, br)


N_CLASSES = N_GROUPS * PAIRS_PER_GROUP
_PAIR_AB = [(a, b) for a in range(EXPERTS_PER_GROUP) for b in range(a + 1, EXPERTS_PER_GROUP)]
CLASS_EXPERT_LO = [g * EXPERTS_PER_GROUP + a for g in range(N_GROUPS) for a, _ in _PAIR_AB]
CLASS_EXPERT_HI = [g * EXPERTS_PER_GROUP + b for g in range(N_GROUPS) for _, b in _PAIR_AB]
TB_MOE = 256


def _rank_kernel(cls_ref, rank_ref, cnt_ref, carry_ref):
    tb = cls_ref.shape[0]

    @pl.when(pl.program_id(0) == 0)
    def _():
        carry_ref[...] = jnp.zeros(carry_ref.shape, F32)

    onehot = cls_ref[...] == lax.broadcasted_iota(I32, (tb, LANE), 1)
    tril = lax.broadcasted_iota(I32, (tb, tb), 0) >= lax.broadcasted_iota(I32, (tb, tb), 1)
    pref = jnp.dot(jnp.where(tril, 1.0, 0.0).astype(BF16), jnp.where(onehot, 1.0, 0.0).astype(BF16),
                   preferred_element_type=F32)
    carry = carry_ref[...]
    rank = jnp.sum(jnp.where(onehot, pref + carry, 0.0), axis=-1, keepdims=True) - 1.0
    rank_ref[...] = jnp.broadcast_to(rank, rank_ref.shape).astype(I32)
    carry = carry + pref[tb - 1:tb, :]
    carry_ref[...] = carry
    cnt_ref[...] = carry


def _class_rank(cls, tb=512):
    n = cls.shape[0]
    return pl.pallas_call(
        _rank_kernel,
        grid=(n // tb,),
        in_specs=[pl.BlockSpec((tb, LANE), lambda i: (i, 0))],
        out_specs=[pl.BlockSpec((tb, LANE), lambda i: (i, 0)), pl.BlockSpec((1, LANE), lambda i: (0, 0))],
        out_shape=[jax.ShapeDtypeStruct((n, LANE), I32), jax.ShapeDtypeStruct((1, LANE), F32)],
        scratch_shapes=[pltpu.VMEM((1, LANE), F32)],
        compiler_params=_cparams(("arbitrary",)),
    )(cls)


def _moe_schedule(counts, n_tok, tm):
    counts = counts[:N_CLASSES].astype(I32)
    tiles_per = (counts + tm - 1) // tm
    tile_end = jnp.cumsum(tiles_per)
    tile_start = tile_end - tiles_per
    start_row = jnp.zeros((LANE,), I32).at[:N_CLASSES].set(tile_start * tm)
    n_tiles = n_tok // tm + N_CLASSES
    t = jnp.arange(n_tiles, dtype=I32)
    tc = jnp.searchsorted(tile_end, t, side="right").astype(I32)
    live = tc < N_CLASSES
    tc = jnp.minimum(tc, N_CLASSES - 1)
    nv = jnp.where(live, jnp.clip(counts[tc] - (t - tile_start[tc]) * tm, 0, tm), 0).astype(I32)
    e_lo = jnp.asarray(CLASS_EXPERT_LO, I32)[tc]
    e_hi = jnp.asarray(CLASS_EXPERT_HI, I32)[tc]
    return start_row, e_lo, e_hi, nv


def _tok(i, width=ROW_TILE):
    return pl.ds(i * width, width)


def _slot_of(start_ref, cls_ref, rank_ref, r):
    return start_ref[cls_ref[0, r]] + rank_ref[0, r]


def _dispatch_kernel(start_ref, cls_ref, rank_ref, h_ref, g_ref, xs_init, xs_hbm, sbuf, sem, *, tb, n_blk):
    del xs_init
    i = pl.program_id(0)
    slot = i % 2

    def wait(sl):
        pltpu.make_async_copy(sbuf.at[sl], xs_hbm.at[pl.ds(0, tb * ROW_TILE)], sem.at[sl]).wait()

    _rows_to_tiles(sbuf.at[slot], _rms(h_ref[...], g_ref[...]))
    for r in range(tb):
        pltpu.make_async_copy(sbuf.at[slot, _tok(r)], xs_hbm.at[_tok(_slot_of(start_ref, cls_ref, rank_ref, r))],
                              sem.at[slot]).start()

    @pl.when(i > 0)
    def _():
        wait(1 - slot)

    @pl.when(i == n_blk - 1)
    def _():
        wait(slot)


def _dispatch(h, g, start_row, cls_sm, rank_sm, n_slots, tb=TB_MOE):
    n, d = h.shape
    n_blk = cls_sm.shape[0]
    xs0 = jnp.zeros((n_slots * ROW_TILE, LANE), F32)
    idx_spec = pl.BlockSpec((None, 1, tb), lambda i, st: (i, 0, 0), memory_space=pltpu.SMEM)
    return pl.pallas_call(
        functools.partial(_dispatch_kernel, tb=tb, n_blk=n_blk),
        grid_spec=pltpu.PrefetchScalarGridSpec(
            num_scalar_prefetch=1,
            grid=(n_blk,),
            in_specs=[idx_spec, idx_spec,
                      pl.BlockSpec((tb, d), lambda i, st: (i, 0)),
                      pl.BlockSpec((1, d), lambda i, st: (0, 0)),
                      pl.BlockSpec(memory_space=pl.ANY)],
            out_specs=pl.BlockSpec(memory_space=pl.ANY),
            scratch_shapes=[pltpu.VMEM((2, tb * ROW_TILE, LANE), F32), pltpu.SemaphoreType.DMA((2,))],
        ),
        out_shape=jax.ShapeDtypeStruct(xs0.shape, F32),
        input_output_aliases={5: 0},
        compiler_params=_cparams(("arbitrary",)),
    )(start_row, cls_sm, rank_sm, h, g.reshape(1, d), xs0)


def _expert_kernel(elo_ref, ehi_ref, nv_ref, xs_ref, wga, wua, wda, wgb, wub, wdb, ys_ref, *, tm):
    t = pl.program_id(0)

    @pl.when(nv_ref[t] > 0)
    def _():
        x = _tiles_to_rows(xs_ref, tm).astype(BF16)
        for half, (wg, wu, wd) in enumerate(((wga, wua, wda), (wgb, wub, wdb))):
            g = jnp.dot(x, wg[...], preferred_element_type=F32)
            u = jnp.dot(x, wu[...], preferred_element_type=F32)
            act = (g * jax.nn.sigmoid(g)) * u
            y = jnp.dot(act.astype(BF16), wd[...], preferred_element_type=F32)
            for c in range(ROW_TILE):
                ys_ref[pl.ds(half * ROW_TILE + c, tm, stride=2 * ROW_TILE), :] = y[:, c * LANE:(c + 1) * LANE]

    @pl.when(nv_ref[t] == 0)
    def _():
        ys_ref[...] = jnp.zeros(ys_ref.shape, F32)


def _moe_experts(xs, e_lo, e_hi, nv, layer, wg, wu, wd, tm=TM_MOE):
    d = wg.shape[2]
    assert d == ROW_TILE * LANE
    n_tiles = nv.shape[0]

    def w_spec(rows, cols, which):
        return pl.BlockSpec((None, None, rows, cols), lambda t, lo, hi, nv_: (layer, (lo, hi)[which][t], 0, 0))

    w_specs = [w_spec(d, D_EXPERT, k) for k in (0, 0)] + [w_spec(D_EXPERT, d, 0)]
    w_specs += [w_spec(d, D_EXPERT, k) for k in (1, 1)] + [w_spec(D_EXPERT, d, 1)]
    return pl.pallas_call(
        functools.partial(_expert_kernel, tm=tm),
        grid_spec=pltpu.PrefetchScalarGridSpec(
            num_scalar_prefetch=3,
            grid=(n_tiles,),
            in_specs=[pl.BlockSpec((tm * ROW_TILE, LANE), lambda t, lo, hi, nv_: (t, 0))] + w_specs,
            out_specs=pl.BlockSpec((tm * 2 * ROW_TILE, LANE), lambda t, lo, hi, nv_: (t, 0)),
        ),
        out_shape=jax.ShapeDtypeStruct((n_tiles * tm * 2 * ROW_TILE, LANE), F32),
        compiler_params=_cparams(("arbitrary",)),
    )(e_lo, e_hi, nv, xs, wg, wu, wd, wg, wu, wd)


def _combine_kernel(start_ref, cls_cur, rank_cur, cls_next, rank_next, ys_hbm, h_ref, cw_ref, g_ref, o_ref,
                    ybuf, sem, *, tb, n_blk, final):
    i = pl.program_id(0)
    slot = i % 2
    other = 1 - slot
    pair = 2 * ROW_TILE

    def fetch(cls_ref, rank_ref, r, sl):
        return pltpu.make_async_copy(ys_hbm.at[_tok(_slot_of(start_ref, cls_ref, rank_ref, r), pair)],
                                     ybuf.at[sl, _tok(r, pair)], sem.at[sl])

    def wait(sl):
        pltpu.make_async_copy(ys_hbm.at[pl.ds(0, tb * pair)], ybuf.at[sl], sem.at[sl]).wait()

    @pl.when(i == 0)
    def _():
        def body(r, c):
            fetch(cls_cur, rank_cur, r, 0).start()
            return c
        lax.fori_loop(0, tb, body, 0)

    wait(slot)
    for r in range(tb):
        fetch(cls_next, rank_next, r, other).start()
    yb = ybuf.at[slot]
    y_lo = jnp.concatenate([yb[pl.ds(c, tb, stride=pair), :] for c in range(ROW_TILE)], axis=1)
    y_hi = jnp.concatenate([yb[pl.ds(ROW_TILE + c, tb, stride=pair), :] for c in range(ROW_TILE)], axis=1)
    cw = cw_ref[...]
    out = h_ref[...] + cw[:, 0:1] * y_lo + cw[:, 1:2] * y_hi
    if final:
        out = _rms(out, g_ref[...])
    o_ref[...] = out

    @pl.when(i == n_blk - 1)
    def _():
        wait(other)


def _combine(h, ys, cw, g, start_row, cls_sm, rank_sm, final, tb=TB_MOE):
    n, d = h.shape
    n_blk = n // tb
    cls_x = jnp.concatenate([cls_sm, cls_sm[-1:]], axis=0)
    rank_x = jnp.concatenate([rank_sm, rank_sm[-1:]], axis=0)

    def idx_spec(shift):
        return pl.BlockSpec((None, 1, tb), lambda i, st: (i + shift, 0, 0), memory_space=pltpu.SMEM)

    return pl.pallas_call(
        functools.partial(_combine_kernel, tb=tb, n_blk=n_blk, final=final),
        grid_spec=pltpu.PrefetchScalarGridSpec(
            num_scalar_prefetch=1,
            grid=(n_blk,),
            in_specs=[idx_spec(0), idx_spec(0), idx_spec(1), idx_spec(1),
                      pl.BlockSpec(memory_space=pl.ANY),
                      pl.BlockSpec((tb, d), lambda i, st: (i, 0)),
                      pl.BlockSpec((tb, LANE), lambda i, st: (i, 0)),
                      pl.BlockSpec((1, d), lambda i, st: (0, 0))],
            out_specs=pl.BlockSpec((tb, d), lambda i, st: (i, 0)),
            scratch_shapes=[pltpu.VMEM((2, tb * 2 * ROW_TILE, LANE), F32), pltpu.SemaphoreType.DMA((2,))],
        ),
        out_shape=jax.ShapeDtypeStruct((n, d), F32),
        compiler_params=_cparams(("arbitrary",)),
    )(start_row, cls_x, rank_x, cls_x, rank_x, ys, h, cw, g.reshape(1, d))


def _moe_layer(h, layer, ln_g, w_rg, b_rg, w_re, b_re, w_gate, w_up, w_down, final_g, final):
    n = h.shape[0]
    cls, cw = _router(h, ln_g, w_rg, b_rg, w_re, b_re)
    rank, counts = _class_rank(cls)
    start_row, e_lo, e_hi, nv = _moe_schedule(counts[0], n, TM_MOE)
    cls_sm = cls[:, 0].reshape(n // TB_MOE, 1, TB_MOE)
    rank_sm = rank[:, 0].reshape(n // TB_MOE, 1, TB_MOE)
    xs = _dispatch(h, ln_g, start_row, cls_sm, rank_sm, nv.shape[0] * TM_MOE)
    ys = _moe_experts(xs, e_lo, e_hi, nv, layer, w_gate, w_up, w_down)
    return _combine(h, ys, cw, final_g, start_row, cls_sm, rank_sm, final)


def _ab_weights(w_in):
    scale = HEAD_DIM ** -0.5
    widths = (512, 512, 512, 512, 64, 64, 256, 64, 4)
    offs = [sum(widths[:j]) for j in range(len(widths) + 1)]
    q_a, k_a, v_a, q_b, k_b, v_b, q_i, k_i, w_i = [w_in[:, offs[j]:offs[j + 1]] for j in range(len(widths))]
    pad = jnp.zeros((w_in.shape[0], LANE - IDX_HEADS), w_in.dtype)
    main = jnp.concatenate([q_a * scale, k_a, v_a, q_b * scale, k_b, k_b, v_b, v_b, q_i, k_i, k_i], axis=1)
    return jnp.concatenate([main, w_i, pad], axis=1).astype(BF16)


def kernel(x, t5_bias, ln_mix, ln_ffn, ln_final, ab_w_in, ab_w_out, diff_lambda, diff_subln,
           c_w_in, c_w_out, c_rel_bias, moe_w_rg, moe_b_rg, moe_w_re, moe_b_re,
           moe_w_gate, moe_w_up, moe_w_down):
    b, s, d = x.shape
    n = b * s
    depth = ln_mix.shape[0]
    h = x.reshape(n, d)
    near, far = _t5_tables(t5_bias)
    w_gate, w_up, w_down = (w.astype(BF16) for w in (moe_w_gate, moe_w_up, moe_w_down))
    for l in range(depth):
        if l % 2 == 0:
            e = l // 2
            lam_init = 0.8 - 0.6 * math.exp(-0.3 * l)
            proj, wi = _norm_proj(h, ln_mix[l], _ab_weights(ab_w_in[e]), AB_COLS)
            proj = proj.reshape(b, s, AB_COLS)
            y_a = _attn_a(proj, near[:A_HEADS], far, diff_lambda[e], diff_subln[e], lam_init)
            y_b = _attn_b(proj, wi.reshape(b, s, LANE), near[A_HEADS:], far)
            h = _out_proj(h, y_a.reshape(n, -1), 0, y_b.reshape(n, -1), 0, ab_w_out[e].astype(BF16))
        else:
            o = l // 2
            scale = HEAD_DIM ** -0.5
            w_in = jnp.concatenate([c_w_in[o][:, :d] * scale, c_w_in[o][:, d:]], axis=1).astype(BF16)
            (proj,) = _norm_proj(h, ln_mix[l], w_in, 3 * d)
            y = _attn_c(proj.reshape(b, s, 3 * d), _c_tables(c_rel_bias[o])).reshape(n, d)
            h = _out_proj(h, y, 0, y, 1, c_w_out[o].astype(BF16))
        h = _moe_layer(h, l, ln_ffn[l], moe_w_rg[l], moe_b_rg[l], moe_w_re[l], moe_b_re[l],
                       w_gate, w_up, w_down, ln_final, l == depth - 1)
    return h.reshape(b, s, d)
```

```python
import functools
import math

import jax
import jax.numpy as jnp
from jax import lax
from jax.experimental import pallas as pl
from jax.experimental.pallas import tpu as pltpu

F32 = jnp.float32
BF16 = jnp.bfloat16
I32 = jnp.int32

D_MODEL = 1024
CHUNK = 64
HEAD_DIM = 64
RMS_EPS = 1e-6
A_HEADS = 4
A_V_DIM = 2 * HEAD_DIM
B_HEADS = 8
IDX_HEADS = 4
TOPK_MAX = 256
C_HEADS = D_MODEL // HEAD_DIM
C_LEFT_CHUNKS = 8
C_REL_CLIP = 128
T5_BUCKETS = 32
T5_MAX_DIST = 1024
N_GROUPS = 4
EXPERTS_PER_GROUP = 8
N_EXPERTS = N_GROUPS * EXPERTS_PER_GROUP
PAIRS_PER_GROUP = EXPERTS_PER_GROUP * (EXPERTS_PER_GROUP - 1) // 2
D_EXPERT = D_MODEL // 2
NEG = -1e30

LANE = 128
VMEM_LIMIT = 56 * 1024 * 1024

TQ = 256
NEAR_TILES = 4
TQ_C = 2 * CHUNK
WIN_C = (C_LEFT_CHUNKS + 2) * CHUNK
TM_MOE = 256

COL_QA, COL_KA, COL_VA, COL_QB, COL_KB, COL_VB, COL_QI, COL_KI, COL_WI = (
    0, 512, 1024, 1536, 2048, 2176, 2304, 2560, 2688)
AB_COLS = 2688


def _cparams(sem, vmem=VMEM_LIMIT):
    return pltpu.CompilerParams(dimension_semantics=sem, vmem_limit_bytes=vmem)


def _dot_nt(a, b):
    return lax.dot_general(a, b, (((1,), (1,)), ((), ())), preferred_element_type=F32)


def _rms(x, g):
    return (x * lax.rsqrt(jnp.mean(x * x, axis=-1, keepdims=True) + RMS_EPS)) * g


def _norm_proj_kernel(x_ref, g_ref, w_ref, o_ref, *tail_ref, n_main):
    xn = _rms(x_ref[...], g_ref[...]).astype(BF16)
    acc = jnp.dot(xn, w_ref[...], preferred_element_type=F32)
    o_ref[...] = acc[:, :n_main].astype(o_ref.dtype)
    if tail_ref:
        tail_ref[0][...] = acc[:, n_main:]


def _norm_proj(x, g, w, n_main, tm=256):
    n, d = x.shape
    n_all = w.shape[1]
    out_shape = [jax.ShapeDtypeStruct((n, n_main), BF16)]
    out_specs = [pl.BlockSpec((tm, n_main), lambda i: (i, 0))]
    if n_all > n_main:
        out_shape.append(jax.ShapeDtypeStruct((n, n_all - n_main), F32))
        out_specs.append(pl.BlockSpec((tm, n_all - n_main), lambda i: (i, 0)))
    return pl.pallas_call(
        functools.partial(_norm_proj_kernel, n_main=n_main),
        grid=(n // tm,),
        in_specs=[pl.BlockSpec((tm, d), lambda i: (i, 0)),
                  pl.BlockSpec((1, d), lambda i: (0, 0)),
                  pl.BlockSpec((d, n_all), lambda i: (0, 0))],
        out_specs=out_specs, out_shape=out_shape,
        compiler_params=_cparams(("parallel",)),
    )(x, g.reshape(1, d), w)


def _out_proj_kernel(h_ref, y0_ref, y1_ref, w0_ref, w1_ref, o_ref):
    acc = jnp.dot(y0_ref[...], w0_ref[...], preferred_element_type=F32)
    acc = acc + jnp.dot(y1_ref[...], w1_ref[...], preferred_element_type=F32)
    o_ref[...] = h_ref[...] + acc


def _out_proj(h, y0, c0, y1, c1, w, tm=512):
    n, d = h.shape
    half = d // 2
    return pl.pallas_call(
        _out_proj_kernel,
        grid=(n // tm,),
        in_specs=[pl.BlockSpec((tm, d), lambda i: (i, 0)),
                  pl.BlockSpec((tm, half), lambda i: (i, c0)),
                  pl.BlockSpec((tm, half), lambda i: (i, c1)),
                  pl.BlockSpec((half, d), lambda i: (0, 0)),
                  pl.BlockSpec((half, d), lambda i: (1, 0))],
        out_specs=pl.BlockSpec((tm, d), lambda i: (i, 0)),
        out_shape=jax.ShapeDtypeStruct((n, d), F32),
        compiler_params=_cparams(("parallel",)),
    )(h, y0, y1, w, w)


def _t5_bucket(rel):
    half = T5_BUCKETS // 2
    max_exact = half // 2
    ret = jnp.where(rel > 0, half, 0)
    n = jnp.abs(rel)
    nf = jnp.maximum(n, 1).astype(F32)
    large = max_exact + (jnp.log(nf / max_exact) / math.log(T5_MAX_DIST / max_exact)
                         * (half - max_exact)).astype(I32)
    large = jnp.minimum(large, half - 1)
    return (ret + jnp.where(n < max_exact, n, large)).astype(I32)


def _toeplitz(g, rows, width):
    length = g.shape[-1]
    assert width < length and rows <= length
    flat = jnp.tile(g, (1,) * (g.ndim - 1) + (rows,))[..., :rows * (length - 1)]
    return flat.reshape(g.shape[:-1] + (rows, length - 1))[..., :width]


def _t5_tables(t5_bias):
    span = NEAR_TILES * TQ
    length = span + TQ
    k = jnp.arange(length, dtype=I32)
    delta = jnp.where(k < span, k, k - length) - (span - TQ)
    g = t5_bias[_t5_bucket(delta)].T
    bias = _toeplitz(g, TQ, span)
    r = jnp.arange(TQ, dtype=I32)[:, None]
    kr = jnp.arange(span, dtype=I32)[None, :] - (span - TQ)
    ok = jnp.floor_divide(kr, CHUNK) <= (r // CHUNK)
    bias = jnp.where(ok[None], bias, NEG)
    near = bias.reshape(-1, TQ, NEAR_TILES, TQ).transpose(0, 2, 1, 3)
    far = t5_bias[_t5_bucket(jnp.asarray(-(span - TQ) - 1, I32))]
    return near, far


N_GRP = 4
STACK = 2 * TQ


def _attend(i, qs_ref, k_fn, v_fn, mask_fn, near_fn, far_cols, mf_ref, mn_ref, l_ref, acc_ref, s_ref=None):
    n_far = jnp.maximum(i - (NEAR_TILES - 1), 0)
    mf_ref[...] = jnp.full(mf_ref.shape, NEG, F32)
    mn_ref[...] = jnp.full(mn_ref.shape, NEG, F32)

    def rows(g):
        return slice(g * STACK, (g + 1) * STACK)

    def logits(g, kt, jt):
        off = pl.multiple_of(kt * TQ, TQ)
        s = _dot_nt(qs_ref[rows(g), :], k_fn(g, off))
        extra = None if mask_fn is None else mask_fn(kt)[None]
        if jt is not None:
            extra = near_fn(g, jt) if extra is None else extra + near_fn(g, jt)
        if extra is not None:
            s = (s.reshape(2, TQ, TQ) + extra).reshape(STACK, TQ)
        return s

    def loops(far_fn, near_fn_):
        def far_body(kt, c):
            far_fn(kt)
            return c

        def near_body(kt, c):
            near_fn_(kt)
            return c

        lax.fori_loop(0, n_far, far_body, 0)
        lax.fori_loop(n_far, i + 1, near_body, 0)

    def pass1(m_ref, near):
        def fn(kt):
            jt = kt - (i - (NEAR_TILES - 1)) if near else None
            s = [logits(g, kt, jt) for g in range(N_GRP)]
            for g in range(N_GRP):
                if s_ref is not None:
                    s_ref[kt, rows(g), :] = s[g]
                m_ref[rows(g), :] = jnp.maximum(m_ref[rows(g), :], jnp.maximum(s[g][:, :LANE], s[g][:, LANE:]))
        return fn

    loops(pass1(mf_ref, False), pass1(mn_ref, True))

    for g in range(N_GRP):
        m = jnp.maximum(jnp.max(mf_ref[rows(g), :], axis=-1, keepdims=True) + far_cols[g],
                        jnp.max(mn_ref[rows(g), :], axis=-1, keepdims=True))
        m_near = jnp.broadcast_to(m, (STACK, LANE))
        mn_ref[rows(g), :] = m_near
        mf_ref[rows(g), :] = m_near - far_cols[g]
    l_ref[...] = jnp.zeros(l_ref.shape, F32)
    acc_ref[...] = jnp.zeros(acc_ref.shape, F32)

    def pass2(shift_ref, near):
        def fn(kt):
            jt = kt - (i - (NEAR_TILES - 1)) if near else None
            off = pl.multiple_of(kt * TQ, TQ)
            grp = range(N_GRP)
            s = [logits(g, kt, jt) if s_ref is None else s_ref[kt, rows(g), :] for g in grp]
            p0 = [jnp.exp(s[g][:, :LANE] - shift_ref[rows(g), :]) for g in grp]
            p1 = [jnp.exp(s[g][:, LANE:] - shift_ref[rows(g), :]) for g in grp]
            for g in grp:
                l_ref[rows(g), :] += p0[g] + p1[g]
            pv = [jnp.dot(jnp.concatenate([p0[g], p1[g]], axis=1).astype(BF16), v_fn(g, off),
                          preferred_element_type=F32) for g in grp]
            for g in grp:
                acc_ref[rows(g), :] += pv[g]
        return fn

    loops(pass2(mf_ref, False), pass2(mn_ref, True))
    return [acc_ref[rows(g), :] / jnp.sum(l_ref[rows(g), :], axis=-1, keepdims=True) for g in range(N_GRP)]


def _attend_scratch():
    return [pltpu.VMEM((N_GRP * STACK, LANE), BF16)] + [pltpu.VMEM((N_GRP * STACK, LANE), F32)] * 4


def _split_lanes(x):
    lo = lax.broadcasted_iota(I32, (1, LANE), 1) < HEAD_DIM
    z = jnp.zeros_like(x)
    return jnp.concatenate([jnp.where(lo, x, z), jnp.where(lo, z, x)], axis=0)


def _attn_a_kernel(far_ref, q_ref, k_ref, v_ref, tab_ref, lam_ref, sub_ref, o_ref,
                   qs_ref, mf_ref, mn_ref, l_ref, acc_ref, s_ref, *, lam_init):
    i = pl.program_id(1)
    for g in range(A_HEADS):
        qs_ref[g * STACK:(g + 1) * STACK, :] = _split_lanes(q_ref[:, g * LANE:(g + 1) * LANE])
    outs = _attend(
        i, qs_ref,
        lambda g, off: k_ref[pl.ds(off, TQ), g * LANE:(g + 1) * LANE],
        lambda g, off: v_ref[pl.ds(off, TQ), g * LANE:(g + 1) * LANE],
        None,
        lambda g, jt: tab_ref[g, jt][None],
        [jnp.full((STACK, 1), far_ref[g], F32) for g in range(A_HEADS)],
        mf_ref, mn_ref, l_ref, acc_ref, s_ref)
    lam = lam_ref[...]
    lam_full = (jnp.exp(jnp.sum(lam[0:1] * lam[1:2], axis=-1, keepdims=True))
                - jnp.exp(jnp.sum(lam[2:3] * lam[3:4], axis=-1, keepdims=True)) + lam_init)
    for g, o in enumerate(outs):
        d = o[:TQ] - lam_full * o[TQ:]
        o_ref[:, g * LANE:(g + 1) * LANE] = (_rms(d, sub_ref[...]) * (1.0 - lam_init)).astype(o_ref.dtype)


def _attn_a(proj, near, far, lam, subln, lam_init):
    b, s, _ = proj.shape
    assert A_HEADS == N_GRP
    wide = A_HEADS * A_V_DIM
    return pl.pallas_call(
        functools.partial(_attn_a_kernel, lam_init=lam_init),
        grid_spec=pltpu.PrefetchScalarGridSpec(
            num_scalar_prefetch=1,
            grid=(b, s // TQ),
            in_specs=[
                pl.BlockSpec((None, TQ, wide), lambda b_, i, f: (b_, i, COL_QA // wide)),
                pl.BlockSpec((None, s, wide), lambda b_, i, f: (b_, 0, COL_KA // wide), pipeline_mode=pl.Buffered(1)),
                pl.BlockSpec((None, s, wide), lambda b_, i, f: (b_, 0, COL_VA // wide), pipeline_mode=pl.Buffered(1)),
                pl.BlockSpec((A_HEADS, NEAR_TILES, TQ, TQ), lambda b_, i, f: (0, 0, 0, 0),
                             pipeline_mode=pl.Buffered(1)),
                pl.BlockSpec((4, HEAD_DIM), lambda b_, i, f: (0, 0)),
                pl.BlockSpec((1, A_V_DIM), lambda b_, i, f: (0, 0)),
            ],
            out_specs=pl.BlockSpec((None, TQ, wide), lambda b_, i, f: (b_, i, 0)),
            scratch_shapes=_attend_scratch() + [pltpu.VMEM((s // TQ, N_GRP * STACK, TQ), F32)],
        ),
        out_shape=jax.ShapeDtypeStruct((b, s, wide), BF16),
        compiler_params=_cparams(("parallel", "arbitrary")),
    )(far, proj, proj, proj, near, lam, subln.reshape(1, A_V_DIM))


def _attn_b_kernel(far_ref, qb_ref, kb_ref, vb_ref, qi_ref, ki_ref, wi_ref, tab_ref, o_ref,
                   keys_ref, selm_ref, cand_ref, tu_ref, trial_ref, cnt_ref, nge_ref, qs_ref, mf_ref, mn_ref, l_ref, acc_ref,
                   *, top_k, idx_bits):
    i = pl.program_id(1)
    n_kt = i + 1
    lane = lax.broadcasted_iota(I32, (1, LANE), 1)
    lo = lane < HEAD_DIM
    int_min = jnp.int32(-2 ** 31)
    split_heads = _split_lanes

    row = lax.broadcasted_iota(I32, (TQ, TQ), 0)
    col = lax.broadcasted_iota(I32, (TQ, TQ), 1)
    chunk_gap = (col // CHUNK) - (row // CHUNK)

    def causal(kt):
        return chunk_gap <= jnp.where(kt < i, jnp.int32(TQ), jnp.int32(0))

    qi = qi_ref[...]
    qis = jnp.concatenate([split_heads(qi[:, :LANE]), split_heads(qi[:, LANE:])], axis=0)
    wi = wi_ref[...] * ((IDX_HEADS ** -0.5) * (HEAD_DIM ** -0.5))

    def score_body(kt, c):
        off = pl.multiple_of(kt * TQ, TQ)
        r = jnp.maximum(_dot_nt(qis, ki_ref[pl.ds(off, TQ), :]), 0.0)
        isc = r[0:TQ] * wi[:, 0:1]
        for hh in range(1, IDX_HEADS):
            isc = isc + r[hh * TQ:(hh + 1) * TQ] * wi[:, hh:hh + 1]
        isc = jnp.where(isc == 0.0, 0.0, isc)
        isc = jnp.where(causal(kt), isc, -jnp.inf)
        bits = lax.bitcast_convert_type(isc, I32)
        keys_ref[kt] = bits ^ ((bits >> 31) & jnp.int32(0x7FFFFFFF))
        return c

    lax.fori_loop(0, n_kt, score_body, 0)

    ones = jnp.ones((LANE, LANE), BF16)
    halves = (slice(0, LANE), slice(LANE, TQ))

    def count(pred_fn):
        cnt_ref[...] = jnp.zeros(cnt_ref.shape, F32)

        def body(kt, c):
            kk = keys_ref[kt]
            cnt_ref[...] += sum(jnp.where(pred_fn(kt, kk[:, hs], hs), 1.0, 0.0) for hs in halves)
            return c

        lax.fori_loop(0, n_kt, body, 0)
        return jnp.dot(cnt_ref[...].astype(BF16), ones, preferred_element_type=F32)

    tu_ref[...] = jnp.zeros(tu_ref.shape, I32)
    nge_ref[...] = jnp.full(nge_ref.shape, (n_kt * TQ).astype(F32), F32)

    def thr_body(bi, c):
        cand_u = tu_ref[...] | (jnp.int32(1) << (31 - bi))
        cand_ref[...] = cand_u ^ int_min
        cnt = count(lambda kt, kk, hs: kk >= cand_ref[...])
        keep = cnt >= top_k
        tu_ref[...] = jnp.where(keep, cand_u, tu_ref[...])
        nge_ref[...] = jnp.where(keep, cnt, nge_ref[...])
        return c

    lax.fori_loop(0, 32, thr_body, 0)
    tu_ref[...] = tu_ref[...] ^ int_min

    def key_index(kt, hs):
        return lax.broadcasted_iota(I32, (TQ, LANE), 1) + (kt * TQ + hs.start)

    cand_ref[...] = jnp.full(cand_ref.shape, 2 ** idx_bits - 1, I32)

    @pl.when(jnp.max(nge_ref[...]) > top_k)
    def _():
        cnt_hi = top_k - count(lambda kt, kk, hs: kk > tu_ref[...])
        cand_ref[...] = jnp.zeros(cand_ref.shape, I32)

        def tie_body(bi, c):
            trial_ref[...] = cand_ref[...] | (jnp.int32(1) << (idx_bits - 1 - bi))
            cnt = count(lambda kt, kk, hs: (kk == tu_ref[...]) & (key_index(kt, hs) <= trial_ref[...]))
            cand_ref[...] = jnp.where(cnt <= cnt_hi, trial_ref[...], cand_ref[...])
            return c

        lax.fori_loop(0, idx_bits, tie_body, 0)

    def mask_body(kt, c):
        kk = keys_ref[kt]
        lim = jnp.where(kt < i, jnp.int32(TQ), jnp.int32(0))
        for hs in halves:
            k_ = kk[:, hs]
            sel = (k_ > tu_ref[...]) | ((k_ == tu_ref[...]) & (key_index(kt, hs) <= cand_ref[...]))
            gap = ((lax.broadcasted_iota(I32, (TQ, LANE), 1) + hs.start) // CHUNK
                   - lax.broadcasted_iota(I32, (TQ, LANE), 0) // CHUNK)
            selm_ref[kt, :, hs] = jnp.where(sel & (gap <= lim), 0.0, NEG)
        return c

    lax.fori_loop(0, n_kt, mask_body, 0)

    for g in range(N_GRP):
        qs_ref[g * STACK:(g + 1) * STACK, :] = split_heads(qb_ref[:, g * LANE:(g + 1) * LANE])
    top_rows = lax.broadcasted_iota(I32, (STACK, 1), 0) < TQ
    outs = _attend(
        i, qs_ref,
        lambda g, off: kb_ref[pl.ds(off, TQ), :],
        lambda g, off: vb_ref[pl.ds(off, TQ), :],
        lambda kt: selm_ref[kt],
        lambda g, jt: tab_ref[2 * g:2 * g + 2, jt],
        [jnp.where(top_rows, far_ref[A_HEADS + 2 * g], far_ref[A_HEADS + 2 * g + 1]) for g in range(N_GRP)],
        mf_ref, mn_ref, l_ref, acc_ref)
    for g, o in enumerate(outs):
        o_ref[:, g * LANE:(g + 1) * LANE] = jnp.where(lo, o[:TQ], o[TQ:]).astype(o_ref.dtype)


def _attn_b(proj, wi, near_b, far):
    b, s, _ = proj.shape
    top_k = min(TOPK_MAX, s // 4)
    assert top_k <= TQ and s % TQ == 0 and (s & (s - 1)) == 0
    n_kt = s // TQ
    assert 2 * n_kt <= 256 and B_HEADS == 2 * N_GRP
    qb_w = B_HEADS * HEAD_DIM
    qi_w = IDX_HEADS * HEAD_DIM
    return pl.pallas_call(
        functools.partial(_attn_b_kernel, top_k=float(top_k), idx_bits=int(math.log2(s))),
        grid_spec=pltpu.PrefetchScalarGridSpec(
            num_scalar_prefetch=1,
            grid=(b, n_kt),
            in_specs=[
                pl.BlockSpec((None, TQ, qb_w), lambda b_, i, f: (b_, i, COL_QB // qb_w)),
                pl.BlockSpec((None, s, LANE), lambda b_, i, f: (b_, 0, COL_KB // LANE)),
                pl.BlockSpec((None, s, LANE), lambda b_, i, f: (b_, 0, COL_VB // LANE)),
                pl.BlockSpec((None, TQ, qi_w), lambda b_, i, f: (b_, i, COL_QI // qi_w)),
                pl.BlockSpec((None, s, LANE), lambda b_, i, f: (b_, 0, COL_KI // LANE)),
                pl.BlockSpec((None, TQ, LANE), lambda b_, i, f: (b_, i, 0)),
                pl.BlockSpec((B_HEADS, NEAR_TILES, TQ, TQ), lambda b_, i, f: (0, 0, 0, 0)),
            ],
            out_specs=pl.BlockSpec((None, TQ, qb_w), lambda b_, i, f: (b_, i, 0)),
            scratch_shapes=[pltpu.VMEM((n_kt, TQ, TQ), I32), pltpu.VMEM((n_kt, TQ, TQ), F32)]
            + [pltpu.VMEM((TQ, LANE), I32)] * 3 + [pltpu.VMEM((TQ, LANE), F32)] * 2 + _attend_scratch(),
        ),
        out_shape=jax.ShapeDtypeStruct((b, s, qb_w), BF16),
        compiler_params=_cparams(("parallel", "arbitrary")),
    )(far, proj, proj, proj, proj, proj, wi, near_b)


PAIRS_C = 4


def _c_tables(rel_table):
    n_var = WIN_C // TQ_C
    length = WIN_C + TQ_C
    k = jnp.arange(length, dtype=I32)
    cr = jnp.where(k < WIN_C, k, k - length)
    off = (jnp.arange(n_var, dtype=I32) * TQ_C)[:, None]
    rel_idx = jnp.clip(off - cr[None, :], -C_REL_CLIP, C_REL_CLIP) + C_REL_CLIP
    g = rel_table[rel_idx].transpose(2, 0, 1)
    bias = _toeplitz(g, TQ_C, WIN_C)
    r = jnp.arange(TQ_C, dtype=I32)[None, :, None]
    c = jnp.arange(WIN_C, dtype=I32)[None, None, :]
    kc = jnp.floor_divide(c - off[:, :, None], CHUNK)
    ok = (kc <= r // CHUNK) & (kc >= r // CHUNK - C_LEFT_CHUNKS)
    return jnp.where(ok[None], bias, NEG)


def _attn_c_kernel(q_ref, k_ref, v_ref, tab_ref, o_ref):
    i = pl.program_id(2)
    n_var = WIN_C // TQ_C
    lane = lax.broadcasted_iota(I32, (1, LANE), 1)
    lo = lane < HEAD_DIM
    var = jnp.minimum(i, n_var - 1)
    ks = pl.multiple_of(jnp.maximum(i - (n_var - 1), 0) * TQ_C, TQ_C)
    pairs = range(PAIRS_C)
    cols = [slice(pp * LANE, (pp + 1) * LANE) for pp in pairs]
    s = [_dot_nt(_split_lanes(q_ref[:, cols[pp]]), k_ref[pl.ds(ks, WIN_C), cols[pp]]) for pp in pairs]
    s = [s[pp] + jnp.concatenate([tab_ref[2 * pp, var], tab_ref[2 * pp + 1, var]], axis=0) for pp in pairs]
    m = [jnp.max(s[pp], axis=-1, keepdims=True) for pp in pairs]
    p = [jnp.exp(s[pp] - m[pp]) for pp in pairs]
    l = [jnp.sum(p[pp], axis=-1, keepdims=True) for pp in pairs]
    pv = [jnp.dot(p[pp].astype(BF16), v_ref[pl.ds(ks, WIN_C), cols[pp]], preferred_element_type=F32) for pp in pairs]
    for pp in pairs:
        o = pv[pp] / l[pp]
        o_ref[:, cols[pp]] = jnp.where(lo, o[:TQ_C], o[TQ_C:]).astype(o_ref.dtype)


def _attn_c(proj, tab):
    b, s, _ = proj.shape
    assert s >= WIN_C and s % TQ_C == 0
    n_grp = C_HEADS // 2 // PAIRS_C
    n_var = WIN_C // TQ_C
    wblk = PAIRS_C * LANE
    tab = tab.reshape(n_grp, 2 * PAIRS_C, n_var, TQ_C, WIN_C)
    return pl.pallas_call(
        _attn_c_kernel,
        grid=(n_grp, b, s // TQ_C),
        in_specs=[
            pl.BlockSpec((None, TQ_C, wblk), lambda h, b_, i: (b_, i, h)),
            pl.BlockSpec((None, s, wblk), lambda h, b_, i: (b_, 0, n_grp + h)),
            pl.BlockSpec((None, s, wblk), lambda h, b_, i: (b_, 0, 2 * n_grp + h)),
            pl.BlockSpec((None, 2 * PAIRS_C, n_var, TQ_C, WIN_C), lambda h, b_, i: (h, 0, 0, 0, 0),
                         pipeline_mode=pl.Buffered(1)),
        ],
        out_specs=pl.BlockSpec((None, TQ_C, wblk), lambda h, b_, i: (b_, i, h)),
        out_shape=jax.ShapeDtypeStruct((b, s, D_MODEL), BF16),
        compiler_params=_cparams(("parallel", "parallel", "arbitrary")),
    )(proj, proj, proj, tab)


ROW_TILE = 8


def _rows_to_tiles(ref, x):
    rows = x.shape[0]
    for c in range(ROW_TILE):
        ref[pl.ds(c, rows, stride=ROW_TILE), :] = x[:, c * LANE:(c + 1) * LANE]


def _tiles_to_rows(ref, rows):
    return jnp.concatenate([ref[pl.ds(c, rows, stride=ROW_TILE), :] for c in range(ROW_TILE)], axis=1)


def _router_kernel(h_ref, g_ref, wr_ref, br_ref, ei_ref, cw_ref):
    xn = _rms(h_ref[...], g_ref[...])
    logits = jnp.dot(xn, wr_ref[...], preferred_element_type=F32,
                     precision=lax.Precision.HIGHEST) + br_ref[...]
    lane = lax.broadcasted_iota(I32, (1, LANE), 1)
    lane_f = lane.astype(F32)
    big = float(LANE)
    is_g = lane < N_GROUPS
    lg = jnp.where(is_g, logits, -jnp.inf)
    gmax = jnp.max(lg, axis=-1, keepdims=True)
    g_sel = jnp.min(jnp.where(lg == gmax, lane_f, big), axis=-1, keepdims=True)
    p_gsel = 1.0 / jnp.sum(jnp.where(is_g, jnp.exp(logits - gmax), 0.0), axis=-1, keepdims=True)
    e_grp = ((lane - N_GROUPS) // EXPERTS_PER_GROUP).astype(F32)
    in_sel = (lane >= N_GROUPS) & (lane < N_GROUPS + N_EXPERTS) & (e_grp == g_sel)
    le = jnp.where(in_sel, logits, -jnp.inf)
    m1 = jnp.max(le, axis=-1, keepdims=True)
    i1 = jnp.min(jnp.where(le == m1, lane_f, big), axis=-1, keepdims=True)
    le2 = jnp.where(lane_f == i1, -jnp.inf, le)
    m2 = jnp.max(le2, axis=-1, keepdims=True)
    i2 = jnp.min(jnp.where(le2 == m2, lane_f, big), axis=-1, keepdims=True)
    t = jnp.exp(m2 - m1)
    c1 = p_gsel / (1.0 + t)
    c2 = p_gsel * t / (1.0 + t)
    a = jnp.minimum(i1, i2) - N_GROUPS - EXPERTS_PER_GROUP * g_sel
    b = jnp.maximum(i1, i2) - N_GROUPS - EXPERTS_PER_GROUP * g_sel
    pair = a * (2 * EXPERTS_PER_GROUP - 1 - a) * 0.5 + (b - a - 1.0)
    cls = g_sel * PAIRS_PER_GROUP + pair
    ei_ref[...] = jnp.broadcast_to(cls, ei_ref.shape).astype(I32)
    first_is_lo = i1 < i2
    cw_ref[...] = jnp.where(lane == 0, jnp.where(first_is_lo, c1, c2), jnp.where(first_is_lo, c2, c1))


def _router(h, g, w_rg, b_rg, w_re, b_re, tm=256):
    n, d = h.shape
    pad = LANE - N_GROUPS - N_EXPERTS
    wr = jnp.concatenate([w_rg, w_re, jnp.zeros((d, pad), F32)], axis=1)
    br = jnp.concatenate([b_rg, b_re, jnp.zeros((pad,), F32)]).reshape(1, LANE)
    return pl.pallas_call(
        _router_kernel,
        grid=(n // tm,),
        in_specs=[pl.BlockSpec((tm, d), lambda i: (i, 0)),
                  pl.BlockSpec((1, d), lambda i: (0, 0)),
                  pl.BlockSpec((d, LANE), lambda i: (0, 0)),
                  pl.BlockSpec((1, LANE), lambda i: (0, 0))],
        out_specs=[pl.BlockSpec((tm, LANE), lambda i: (i, 0)),
                   pl.BlockSpec((tm, LANE), lambda i: (i, 0))],
        out_shape=[jax.ShapeDtypeStruct((n, LANE), I32),
                   jax.ShapeDtypeStruct((n, LANE), F32)],
        compiler_params=_cparams(("parallel",)),
    )(h, g.reshape(1, d), wr, br)


N_CLASSES = N_GROUPS * PAIRS_PER_GROUP
_PAIR_AB = [(a, b) for a in range(EXPERTS_PER_GROUP) for b in range(a + 1, EXPERTS_PER_GROUP)]
CLASS_EXPERT_LO = [g * EXPERTS_PER_GROUP + a for g in range(N_GROUPS) for a, _ in _PAIR_AB]
CLASS_EXPERT_HI = [g * EXPERTS_PER_GROUP + b for g in range(N_GROUPS) for _, b in _PAIR_AB]
TB_MOE = 256


def _rank_kernel(cls_ref, rank_ref, cnt_ref, carry_ref):
    tb = cls_ref.shape[0]

    @pl.when(pl.program_id(0) == 0)
    def _():
        carry_ref[...] = jnp.zeros(carry_ref.shape, F32)

    onehot = cls_ref[...] == lax.broadcasted_iota(I32, (tb, LANE), 1)
    tril = lax.broadcasted_iota(I32, (tb, tb), 0) >= lax.broadcasted_iota(I32, (tb, tb), 1)
    pref = jnp.dot(jnp.where(tril, 1.0, 0.0).astype(BF16), jnp.where(onehot, 1.0, 0.0).astype(BF16),
                   preferred_element_type=F32)
    carry = carry_ref[...]
    rank = jnp.sum(jnp.where(onehot, pref + carry, 0.0), axis=-1, keepdims=True) - 1.0
    rank_ref[...] = jnp.broadcast_to(rank, rank_ref.shape).astype(I32)
    carry = carry + pref[tb - 1:tb, :]
    carry_ref[...] = carry
    cnt_ref[...] = carry


def _class_rank(cls, tb=512):
    n = cls.shape[0]
    return pl.pallas_call(
        _rank_kernel,
        grid=(n // tb,),
        in_specs=[pl.BlockSpec((tb, LANE), lambda i: (i, 0))],
        out_specs=[pl.BlockSpec((tb, LANE), lambda i: (i, 0)), pl.BlockSpec((1, LANE), lambda i: (0, 0))],
        out_shape=[jax.ShapeDtypeStruct((n, LANE), I32), jax.ShapeDtypeStruct((1, LANE), F32)],
        scratch_shapes=[pltpu.VMEM((1, LANE), F32)],
        compiler_params=_cparams(("arbitrary",)),
    )(cls)


def _moe_schedule(counts, n_tok, tm):
    counts = counts[:N_CLASSES].astype(I32)
    tiles_per = (counts + tm - 1) // tm
    tile_end = jnp.cumsum(tiles_per)
    tile_start = tile_end - tiles_per
    start_row = jnp.zeros((LANE,), I32).at[:N_CLASSES].set(tile_start * tm)
    n_tiles = n_tok // tm + N_CLASSES
    t = jnp.arange(n_tiles, dtype=I32)
    tc = jnp.searchsorted(tile_end, t, side="right").astype(I32)
    live = tc < N_CLASSES
    tc = jnp.minimum(tc, N_CLASSES - 1)
    nv = jnp.where(live, jnp.clip(counts[tc] - (t - tile_start[tc]) * tm, 0, tm), 0).astype(I32)
    e_lo = jnp.asarray(CLASS_EXPERT_LO, I32)[tc]
    e_hi = jnp.asarray(CLASS_EXPERT_HI, I32)[tc]
    return start_row, e_lo, e_hi, nv


def _tok(i, width=ROW_TILE):
    return pl.ds(i * width, width)


def _slot_of(start_ref, cls_ref, rank_ref, r):
    return start_ref[cls_ref[0, r]] + rank_ref[0, r]


def _dispatch_kernel(start_ref, cls_ref, rank_ref, h_ref, g_ref, xs_init, xs_hbm, sbuf, sem, *, tb, n_blk):
    del xs_init
    i = pl.program_id(0)
    slot = i % 2

    def wait(sl):
        pltpu.make_async_copy(sbuf.at[sl], xs_hbm.at[pl.ds(0, tb * ROW_TILE)], sem.at[sl]).wait()

    _rows_to_tiles(sbuf.at[slot], _rms(h_ref[...], g_ref[...]))
    for r in range(tb):
        pltpu.make_async_copy(sbuf.at[slot, _tok(r)], xs_hbm.at[_tok(_slot_of(start_ref, cls_ref, rank_ref, r))],
                              sem.at[slot]).start()

    @pl.when(i > 0)
    def _():
        wait(1 - slot)

    @pl.when(i == n_blk - 1)
    def _():
        wait(slot)


def _dispatch(h, g, start_row, cls_sm, rank_sm, n_slots, tb=TB_MOE):
    n, d = h.shape
    n_blk = cls_sm.shape[0]
    xs0 = jnp.zeros((n_slots * ROW_TILE, LANE), F32)
    idx_spec = pl.BlockSpec((None, 1, tb), lambda i, st: (i, 0, 0), memory_space=pltpu.SMEM)
    return pl.pallas_call(
        functools.partial(_dispatch_kernel, tb=tb, n_blk=n_blk),
        grid_spec=pltpu.PrefetchScalarGridSpec(
            num_scalar_prefetch=1,
            grid=(n_blk,),
            in_specs=[idx_spec, idx_spec,
                      pl.BlockSpec((tb, d), lambda i, st: (i, 0)),
                      pl.BlockSpec((1, d), lambda i, st: (0, 0)),
                      pl.BlockSpec(memory_space=pl.ANY)],
            out_specs=pl.BlockSpec(memory_space=pl.ANY),
            scratch_shapes=[pltpu.VMEM((2, tb * ROW_TILE, LANE), F32), pltpu.SemaphoreType.DMA((2,))],
        ),
        out_shape=jax.ShapeDtypeStruct(xs0.shape, F32),
        input_output_aliases={5: 0},
        compiler_params=_cparams(("arbitrary",)),
    )(start_row, cls_sm, rank_sm, h, g.reshape(1, d), xs0)


def _expert_kernel(elo_ref, ehi_ref, nv_ref, xs_ref, wga, wua, wda, wgb, wub, wdb, ys_ref, *, tm):
    t = pl.program_id(0)

    @pl.when(nv_ref[t] > 0)
    def _():
        x = _tiles_to_rows(xs_ref, tm).astype(BF16)
        for half, (wg, wu, wd) in enumerate(((wga, wua, wda), (wgb, wub, wdb))):
            g = jnp.dot(x, wg[...], preferred_element_type=F32)
            u = jnp.dot(x, wu[...], preferred_element_type=F32)
            act = (g * jax.nn.sigmoid(g)) * u
            y = jnp.dot(act.astype(BF16), wd[...], preferred_element_type=F32)
            for c in range(ROW_TILE):
                ys_ref[pl.ds(half * ROW_TILE + c, tm, stride=2 * ROW_TILE), :] = y[:, c * LANE:(c + 1) * LANE]

    @pl.when(nv_ref[t] == 0)
    def _():
        ys_ref[...] = jnp.zeros(ys_ref.shape, F32)


def _moe_experts(xs, e_lo, e_hi, nv, layer, wg, wu, wd, tm=TM_MOE):
    d = wg.shape[2]
    assert d == ROW_TILE * LANE
    n_tiles = nv.shape[0]

    def w_spec(rows, cols, which):
        return pl.BlockSpec((None, None, rows, cols), lambda t, lo, hi, nv_: (layer, (lo, hi)[which][t], 0, 0))

    w_specs = [w_spec(d, D_EXPERT, k) for k in (0, 0)] + [w_spec(D_EXPERT, d, 0)]
    w_specs += [w_spec(d, D_EXPERT, k) for k in (1, 1)] + [w_spec(D_EXPERT, d, 1)]
    return pl.pallas_call(
        functools.partial(_expert_kernel, tm=tm),
        grid_spec=pltpu.PrefetchScalarGridSpec(
            num_scalar_prefetch=3,
            grid=(n_tiles,),
            in_specs=[pl.BlockSpec((tm * ROW_TILE, LANE), lambda t, lo, hi, nv_: (t, 0))] + w_specs,
            out_specs=pl.BlockSpec((tm * 2 * ROW_TILE, LANE), lambda t, lo, hi, nv_: (t, 0)),
        ),
        out_shape=jax.ShapeDtypeStruct((n_tiles * tm * 2 * ROW_TILE, LANE), F32),
        compiler_params=_cparams(("arbitrary",)),
    )(e_lo, e_hi, nv, xs, wg, wu, wd, wg, wu, wd)


def _combine_kernel(start_ref, cls_cur, rank_cur, cls_next, rank_next, ys_hbm, h_ref, cw_ref, g_ref, o_ref,
                    ybuf, sem, *, tb, n_blk, final):
    i = pl.program_id(0)
    slot = i % 2
    other = 1 - slot
    pair = 2 * ROW_TILE

    def fetch(cls_ref, rank_ref, r, sl):
        return pltpu.make_async_copy(ys_hbm.at[_tok(_slot_of(start_ref, cls_ref, rank_ref, r), pair)],
                                     ybuf.at[sl, _tok(r, pair)], sem.at[sl])

    def wait(sl):
        pltpu.make_async_copy(ys_hbm.at[pl.ds(0, tb * pair)], ybuf.at[sl], sem.at[sl]).wait()

    @pl.when(i == 0)
    def _():
        def body(r, c):
            fetch(cls_cur, rank_cur, r, 0).start()
            return c
        lax.fori_loop(0, tb, body, 0)

    wait(slot)
    for r in range(tb):
        fetch(cls_next, rank_next, r, other).start()
    yb = ybuf.at[slot]
    y_lo = jnp.concatenate([yb[pl.ds(c, tb, stride=pair), :] for c in range(ROW_TILE)], axis=1)
    y_hi = jnp.concatenate([yb[pl.ds(ROW_TILE + c, tb, stride=pair), :] for c in range(ROW_TILE)], axis=1)
    cw = cw_ref[...]
    out = h_ref[...] + cw[:, 0:1] * y_lo + cw[:, 1:2] * y_hi
    if final:
        out = _rms(out, g_ref[...])
    o_ref[...] = out

    @pl.when(i == n_blk - 1)
    def _():
        wait(other)


def _combine(h, ys, cw, g, start_row, cls_sm, rank_sm, final, tb=TB_MOE):
    n, d = h.shape
    n_blk = n // tb
    cls_x = jnp.concatenate([cls_sm, cls_sm[-1:]], axis=0)
    rank_x = jnp.concatenate([rank_sm, rank_sm[-1:]], axis=0)

    def idx_spec(shift):
        return pl.BlockSpec((None, 1, tb), lambda i, st: (i + shift, 0, 0), memory_space=pltpu.SMEM)

    return pl.pallas_call(
        functools.partial(_combine_kernel, tb=tb, n_blk=n_blk, final=final),
        grid_spec=pltpu.PrefetchScalarGridSpec(
            num_scalar_prefetch=1,
            grid=(n_blk,),
            in_specs=[idx_spec(0), idx_spec(0), idx_spec(1), idx_spec(1),
                      pl.BlockSpec(memory_space=pl.ANY),
                      pl.BlockSpec((tb, d), lambda i, st: (i, 0)),
                      pl.BlockSpec((tb, LANE), lambda i, st: (i, 0)),
                      pl.BlockSpec((1, d), lambda i, st: (0, 0))],
            out_specs=pl.BlockSpec((tb, d), lambda i, st: (i, 0)),
            scratch_shapes=[pltpu.VMEM((2, tb * 2 * ROW_TILE, LANE), F32), pltpu.SemaphoreType.DMA((2,))],
        ),
        out_shape=jax.ShapeDtypeStruct((n, d), F32),
        compiler_params=_cparams(("arbitrary",)),
    )(start_row, cls_x, rank_x, cls_x, rank_x, ys, h, cw, g.reshape(1, d))


def _moe_layer(h, layer, ln_g, w_rg, b_rg, w_re, b_re, w_gate, w_up, w_down, final_g, final):
    n = h.shape[0]
    cls, cw = _router(h, ln_g, w_rg, b_rg, w_re, b_re)
    rank, counts = _class_rank(cls)
    start_row, e_lo, e_hi, nv = _moe_schedule(counts[0], n, TM_MOE)
    cls_sm = cls[:, 0].reshape(n // TB_MOE, 1, TB_MOE)
    rank_sm = rank[:, 0].reshape(n // TB_MOE, 1, TB_MOE)
    xs = _dispatch(h, ln_g, start_row, cls_sm, rank_sm, nv.shape[0] * TM_MOE)
    ys = _moe_experts(xs, e_lo, e_hi, nv, layer, w_gate, w_up, w_down)
    return _combine(h, ys, cw, final_g, start_row, cls_sm, rank_sm, final)


def _ab_weights(w_in):
    scale = HEAD_DIM ** -0.5
    widths = (512, 512, 512, 512, 64, 64, 256, 64, 4)
    offs = [sum(widths[:j]) for j in range(len(widths) + 1)]
    q_a, k_a, v_a, q_b, k_b, v_b, q_i, k_i, w_i = [w_in[:, offs[j]:offs[j + 1]] for j in range(len(widths))]
    pad = jnp.zeros((w_in.shape[0], LANE - IDX_HEADS), w_in.dtype)
    main = jnp.concatenate([q_a * scale, k_a, v_a, q_b * scale, k_b, k_b, v_b, v_b, q_i, k_i, k_i], axis=1)
    return jnp.concatenate([main, w_i, pad], axis=1).astype(BF16)


def kernel(x, t5_bias, ln_mix, ln_ffn, ln_final, ab_w_in, ab_w_out, diff_lambda, diff_subln,
           c_w_in, c_w_out, c_rel_bias, moe_w_rg, moe_b_rg, moe_w_re, moe_b_re,
           moe_w_gate, moe_w_up, moe_w_down):
    b, s, d = x.shape
    n = b * s
    depth = ln_mix.shape[0]
    h = x.reshape(n, d)
    near, far = _t5_tables(t5_bias)
    w_gate, w_up, w_down = (w.astype(BF16) for w in (moe_w_gate, moe_w_up, moe_w_down))
    for l in range(depth):
        if l % 2 == 0:
            e = l // 2
            lam_init = 0.8 - 0.6 * math.exp(-0.3 * l)
            proj, wi = _norm_proj(h, ln_mix[l], _ab_weights(ab_w_in[e]), AB_COLS)
            proj = proj.reshape(b, s, AB_COLS)
            y_a = _attn_a(proj, near[:A_HEADS], far, diff_lambda[e], diff_subln[e], lam_init)
            y_b = _attn_b(proj, wi.reshape(b, s, LANE), near[A_HEADS:], far)
            h = _out_proj(h, y_a.reshape(n, -1), 0, y_b.reshape(n, -1), 0, ab_w_out[e].astype(BF16))
        else:
            o = l // 2
            scale = HEAD_DIM ** -0.5
            w_in = jnp.concatenate([c_w_in[o][:, :d] * scale, c_w_in[o][:, d:]], axis=1).astype(BF16)
            (proj,) = _norm_proj(h, ln_mix[l], w_in, 3 * d)
            y = _attn_c(proj.reshape(b, s, 3 * d), _c_tables(c_rel_bias[o])).reshape(n, d)
            h = _out_proj(h, y, 0, y, 1, c_w_out[o].astype(BF16))
        h = _moe_layer(h, l, ln_ffn[l], moe_w_rg[l], moe_b_rg[l], moe_w_re[l], moe_b_re[l],
                       w_gate, w_up, w_down, ln_final, l == depth - 1)
    return h.reshape(b, s, d)
```

```python
import functools
import math

import jax
import jax.numpy as jnp
from jax import lax
from jax.experimental import pallas as pl
from jax.experimental.pallas import tpu as pltpu

F32 = jnp.float32
BF16 = jnp.bfloat16
I32 = jnp.int32

D_MODEL = 1024
CHUNK = 64
HEAD_DIM = 64
RMS_EPS = 1e-6
A_HEADS = 4
A_V_DIM = 2 * HEAD_DIM
B_HEADS = 8
IDX_HEADS = 4
TOPK_MAX = 256
C_HEADS = D_MODEL // HEAD_DIM
C_LEFT_CHUNKS = 8
C_REL_CLIP = 128
T5_BUCKETS = 32
T5_MAX_DIST = 1024
N_GROUPS = 4
EXPERTS_PER_GROUP = 8
N_EXPERTS = N_GROUPS * EXPERTS_PER_GROUP
PAIRS_PER_GROUP = EXPERTS_PER_GROUP * (EXPERTS_PER_GROUP - 1) // 2
D_EXPERT = D_MODEL // 2
NEG = -1e30

LANE = 128
VMEM_LIMIT = 56 * 1024 * 1024

TQ = 256
NEAR_TILES = 4
TQ_C = 2 * CHUNK
WIN_C = (C_LEFT_CHUNKS + 2) * CHUNK
TM_MOE = 256

COL_QA, COL_KA, COL_VA, COL_QB, COL_KB, COL_VB, COL_QI, COL_KI, COL_WI = (
    0, 512, 1024, 1536, 2048, 2176, 2304, 2560, 2688)
AB_COLS = 2688


def _cparams(sem, vmem=VMEM_LIMIT):
    return pltpu.CompilerParams(dimension_semantics=sem, vmem_limit_bytes=vmem)


def _dot_nt(a, b):
    return lax.dot_general(a, b, (((1,), (1,)), ((), ())), preferred_element_type=F32)


def _rms(x, g):
    return (x * lax.rsqrt(jnp.mean(x * x, axis=-1, keepdims=True) + RMS_EPS)) * g


def _norm_proj_kernel(x_ref, g_ref, w_ref, o_ref, *tail_ref, n_main):
    xn = _rms(x_ref[...], g_ref[...]).astype(BF16)
    acc = jnp.dot(xn, w_ref[...], preferred_element_type=F32)
    o_ref[...] = acc[:, :n_main].astype(o_ref.dtype)
    if tail_ref:
        tail_ref[0][...] = acc[:, n_main:]


def _norm_proj(x, g, w, n_main, tm=256):
    n, d = x.shape
    n_all = w.shape[1]
    out_shape = [jax.ShapeDtypeStruct((n, n_main), BF16)]
    out_specs = [pl.BlockSpec((tm, n_main), lambda i: (i, 0))]
    if n_all > n_main:
        out_shape.append(jax.ShapeDtypeStruct((n, n_all - n_main), F32))
        out_specs.append(pl.BlockSpec((tm, n_all - n_main), lambda i: (i, 0)))
    return pl.pallas_call(
        functools.partial(_norm_proj_kernel, n_main=n_main),
        grid=(n // tm,),
        in_specs=[pl.BlockSpec((tm, d), lambda i: (i, 0)),
                  pl.BlockSpec((1, d), lambda i: (0, 0)),
                  pl.BlockSpec((d, n_all), lambda i: (0, 0))],
        out_specs=out_specs, out_shape=out_shape,
        compiler_params=_cparams(("parallel",)),
    )(x, g.reshape(1, d), w)


def _out_proj_kernel(h_ref, y0_ref, y1_ref, w0_ref, w1_ref, o_ref):
    acc = jnp.dot(y0_ref[...], w0_ref[...], preferred_element_type=F32)
    acc = acc + jnp.dot(y1_ref[...], w1_ref[...], preferred_element_type=F32)
    o_ref[...] = h_ref[...] + acc


def _out_proj(h, y0, c0, y1, c1, w, tm=512):
    n, d = h.shape
    half = d // 2
    return pl.pallas_call(
        _out_proj_kernel,
        grid=(n // tm,),
        in_specs=[pl.BlockSpec((tm, d), lambda i: (i, 0)),
                  pl.BlockSpec((tm, half), lambda i: (i, c0)),
                  pl.BlockSpec((tm, half), lambda i: (i, c1)),
                  pl.BlockSpec((half, d), lambda i: (0, 0)),
                  pl.BlockSpec((half, d), lambda i: (1, 0))],
        out_specs=pl.BlockSpec((tm, d), lambda i: (i, 0)),
        out_shape=jax.ShapeDtypeStruct((n, d), F32),
        compiler_params=_cparams(("parallel",)),
    )(h, y0, y1, w, w)


def _t5_bucket(rel):
    half = T5_BUCKETS // 2
    max_exact = half // 2
    ret = jnp.where(rel > 0, half, 0)
    n = jnp.abs(rel)
    nf = jnp.maximum(n, 1).astype(F32)
    large = max_exact + (jnp.log(nf / max_exact) / math.log(T5_MAX_DIST / max_exact)
                         * (half - max_exact)).astype(I32)
    large = jnp.minimum(large, half - 1)
    return (ret + jnp.where(n < max_exact, n, large)).astype(I32)


def _toeplitz(g, rows, width):
    length = g.shape[-1]
    assert width < length and rows <= length
    flat = jnp.tile(g, (1,) * (g.ndim - 1) + (rows,))[..., :rows * (length - 1)]
    return flat.reshape(g.shape[:-1] + (rows, length - 1))[..., :width]


def _t5_tables(t5_bias):
    span = NEAR_TILES * TQ
    length = span + TQ
    k = jnp.arange(length, dtype=I32)
    delta = jnp.where(k < span, k, k - length) - (span - TQ)
    g = t5_bias[_t5_bucket(delta)].T
    bias = _toeplitz(g, TQ, span)
    r = jnp.arange(TQ, dtype=I32)[:, None]
    kr = jnp.arange(span, dtype=I32)[None, :] - (span - TQ)
    ok = jnp.floor_divide(kr, CHUNK) <= (r // CHUNK)
    bias = jnp.where(ok[None], bias, NEG)
    near = bias.reshape(-1, TQ, NEAR_TILES, TQ).transpose(0, 2, 1, 3)
    far = t5_bias[_t5_bucket(jnp.asarray(-(span - TQ) - 1, I32))]
    return near, far


N_GRP = 4
STACK = 2 * TQ


def _attend(i, qs_ref, k_fn, v_fn, mask_fn, near_fn, far_cols, mf_ref, mn_ref, l_ref, acc_ref, s_ref=None):
    n_far = jnp.maximum(i - (NEAR_TILES - 1), 0)
    mf_ref[...] = jnp.full(mf_ref.shape, NEG, F32)
    mn_ref[...] = jnp.full(mn_ref.shape, NEG, F32)

    def rows(g):
        return slice(g * STACK, (g + 1) * STACK)

    def logits(g, kt, jt):
        off = pl.multiple_of(kt * TQ, TQ)
        s = _dot_nt(qs_ref[rows(g), :], k_fn(g, off))
        extra = None if mask_fn is None else mask_fn(kt)[None]
        if jt is not None:
            extra = near_fn(g, jt) if extra is None else extra + near_fn(g, jt)
        if extra is not None:
            s = (s.reshape(2, TQ, TQ) + extra).reshape(STACK, TQ)
        return s

    def loops(far_fn, near_fn_):
        def far_body(kt, c):
            far_fn(kt)
            return c

        def near_body(kt, c):
            near_fn_(kt)
            return c

        lax.fori_loop(0, n_far, far_body, 0)
        lax.fori_loop(n_far, i + 1, near_body, 0)

    def pass1(m_ref, near):
        def fn(kt):
            jt = kt - (i - (NEAR_TILES - 1)) if near else None
            s = [logits(g, kt, jt) for g in range(N_GRP)]
            for g in range(N_GRP):
                if s_ref is not None:
                    s_ref[kt, rows(g), :] = s[g]
                m_ref[rows(g), :] = jnp.maximum(m_ref[rows(g), :], jnp.maximum(s[g][:, :LANE], s[g][:, LANE:]))
        return fn

    loops(pass1(mf_ref, False), pass1(mn_ref, True))

    for g in range(N_GRP):
        m = jnp.maximum(jnp.max(mf_ref[rows(g), :], axis=-1, keepdims=True) + far_cols[g],
                        jnp.max(mn_ref[rows(g), :], axis=-1, keepdims=True))
        m_near = jnp.broadcast_to(m, (STACK, LANE))
        mn_ref[rows(g), :] = m_near
        mf_ref[rows(g), :] = m_near - far_cols[g]
    l_ref[...] = jnp.zeros(l_ref.shape, F32)
    acc_ref[...] = jnp.zeros(acc_ref.shape, F32)

    def pass2(shift_ref, near):
        def fn(kt):
            jt = kt - (i - (NEAR_TILES - 1)) if near else None
            off = pl.multiple_of(kt * TQ, TQ)
            grp = range(N_GRP)
            s = [logits(g, kt, jt) if s_ref is None else s_ref[kt, rows(g), :] for g in grp]
            p0 = [jnp.exp(s[g][:, :LANE] - shift_ref[rows(g), :]) for g in grp]
            p1 = [jnp.exp(s[g][:, LANE:] - shift_ref[rows(g), :]) for g in grp]
            for g in grp:
                l_ref[rows(g), :] += p0[g] + p1[g]
            pv = [jnp.dot(jnp.concatenate([p0[g], p1[g]], axis=1).astype(BF16), v_fn(g, off),
                          preferred_element_type=F32) for g in grp]
            for g in grp:
                acc_ref[rows(g), :] += pv[g]
        return fn

    loops(pass2(mf_ref, False), pass2(mn_ref, True))
    return [acc_ref[rows(g), :] / jnp.sum(l_ref[rows(g), :], axis=-1, keepdims=True) for g in range(N_GRP)]


def _attend_scratch():
    return [pltpu.VMEM((N_GRP * STACK, LANE), BF16)] + [pltpu.VMEM((N_GRP * STACK, LANE), F32)] * 4


def _split_lanes(x):
    lo = lax.broadcasted_iota(I32, (1, LANE), 1) < HEAD_DIM
    z = jnp.zeros_like(x)
    return jnp.concatenate([jnp.where(lo, x, z), jnp.where(lo, z, x)], axis=0)


def _attn_a_kernel(far_ref, q_ref, k_ref, v_ref, tab_ref, lam_ref, sub_ref, o_ref,
                   qs_ref, mf_ref, mn_ref, l_ref, acc_ref, s_ref, *, lam_init):
    i = pl.program_id(1)
    for g in range(A_HEADS):
        qs_ref[g * STACK:(g + 1) * STACK, :] = _split_lanes(q_ref[:, g * LANE:(g + 1) * LANE])
    outs = _attend(
        i, qs_ref,
        lambda g, off: k_ref[pl.ds(off, TQ), g * LANE:(g + 1) * LANE],
        lambda g, off: v_ref[pl.ds(off, TQ), g * LANE:(g + 1) * LANE],
        None,
        lambda g, jt: tab_ref[g, jt][None],
        [jnp.full((STACK, 1), far_ref[g], F32) for g in range(A_HEADS)],
        mf_ref, mn_ref, l_ref, acc_ref, s_ref)
    lam = lam_ref[...]
    lam_full = (jnp.exp(jnp.sum(lam[0:1] * lam[1:2], axis=-1, keepdims=True))
                - jnp.exp(jnp.sum(lam[2:3] * lam[3:4], axis=-1, keepdims=True)) + lam_init)
    for g, o in enumerate(outs):
        d = o[:TQ] - lam_full * o[TQ:]
        o_ref[:, g * LANE:(g + 1) * LANE] = (_rms(d, sub_ref[...]) * (1.0 - lam_init)).astype(o_ref.dtype)


def _attn_a(proj, near, far, lam, subln, lam_init):
    b, s, _ = proj.shape
    assert A_HEADS == N_GRP
    wide = A_HEADS * A_V_DIM
    return pl.pallas_call(
        functools.partial(_attn_a_kernel, lam_init=lam_init),
        grid_spec=pltpu.PrefetchScalarGridSpec(
            num_scalar_prefetch=1,
            grid=(b, s // TQ),
            in_specs=[
                pl.BlockSpec((None, TQ, wide), lambda b_, i, f: (b_, i, COL_QA // wide)),
                pl.BlockSpec((None, s, wide), lambda b_, i, f: (b_, 0, COL_KA // wide), pipeline_mode=pl.Buffered(1)),
                pl.BlockSpec((None, s, wide), lambda b_, i, f: (b_, 0, COL_VA // wide), pipeline_mode=pl.Buffered(1)),
                pl.BlockSpec((A_HEADS, NEAR_TILES, TQ, TQ), lambda b_, i, f: (0, 0, 0, 0),
                             pipeline_mode=pl.Buffered(1)),
                pl.BlockSpec((4, HEAD_DIM), lambda b_, i, f: (0, 0)),
                pl.BlockSpec((1, A_V_DIM), lambda b_, i, f: (0, 0)),
            ],
            out_specs=pl.BlockSpec((None, TQ, wide), lambda b_, i, f: (b_, i, 0)),
            scratch_shapes=_attend_scratch() + [pltpu.VMEM((s // TQ, N_GRP * STACK, TQ), F32)],
        ),
        out_shape=jax.ShapeDtypeStruct((b, s, wide), BF16),
        compiler_params=_cparams(("parallel", "arbitrary")),
    )(far, proj, proj, proj, near, lam, subln.reshape(1, A_V_DIM))


def _attn_b_kernel(far_ref, qb_ref, kb_ref, vb_ref, qi_ref, ki_ref, wi_ref, tab_ref, o_ref,
                   keys_ref, keyst_ref, selm_ref, cand_ref, tu_ref, jm_ref, qs_ref, mf_ref, mn_ref, l_ref, acc_ref,
                   *, top_k, idx_bits):
    i = pl.program_id(1)
    n_kt = i + 1
    lane = lax.broadcasted_iota(I32, (1, LANE), 1)
    lo = lane < HEAD_DIM
    int_min = jnp.int32(-2 ** 31)
    split_heads = _split_lanes

    row = lax.broadcasted_iota(I32, (TQ, TQ), 0)
    col = lax.broadcasted_iota(I32, (TQ, TQ), 1)
    chunk_gap = (col // CHUNK) - (row // CHUNK)

    def causal(kt):
        return chunk_gap <= jnp.where(kt < i, jnp.int32(TQ), jnp.int32(0))

    qi = qi_ref[...]
    qis = jnp.concatenate([split_heads(qi[:, :LANE]), split_heads(qi[:, LANE:])], axis=0)
    wi = wi_ref[...] * ((IDX_HEADS ** -0.5) * (HEAD_DIM ** -0.5))

    def score_body(kt, c):
        off = pl.multiple_of(kt * TQ, TQ)
        r = jnp.maximum(_dot_nt(qis, ki_ref[pl.ds(off, TQ), :]), 0.0)
        isc = r[0:TQ] * wi[:, 0:1]
        for hh in range(1, IDX_HEADS):
            isc = isc + r[hh * TQ:(hh + 1) * TQ] * wi[:, hh:hh + 1]
        isc = jnp.where(isc == 0.0, 0.0, isc)
        isc = jnp.where(causal(kt), isc, -jnp.inf)
        bits = lax.bitcast_convert_type(isc, I32)
        key = bits ^ ((bits >> 31) & jnp.int32(0x7FFFFFFF))
        keys_ref[kt] = key
        keyst_ref[kt] = key.T
        return c

    lax.fori_loop(0, n_kt, score_body, 0)

    halves = (slice(0, LANE), slice(LANE, TQ))

    def count(pred_fn):
        def body(kt, acc):
            c = jnp.where(pred_fn(kt, keyst_ref[kt]), 1.0, 0.0)
            return acc + jnp.sum(c.reshape(TQ // 32, 32, TQ), axis=0)
        acc = lax.fori_loop(0, n_kt, body, jnp.zeros((32, TQ), F32))
        return jnp.sum(acc, axis=0, keepdims=True)

    def thr_body(bi, state):
        tu, nge = state
        cand_u = tu | (jnp.int32(1) << (31 - bi))
        cand_s = cand_u ^ int_min
        cnt = count(lambda kt, kk: kk >= cand_s)
        keep = cnt >= top_k
        return jnp.where(keep, cand_u, tu), jnp.where(keep, cnt, nge)

    tu, nge = lax.fori_loop(0, 32, thr_body, (jnp.zeros((1, TQ), I32),
                                              jnp.full((1, TQ), (n_kt * TQ).astype(F32), F32)))
    thr = tu ^ int_min

    jm_ref[...] = jnp.full(jm_ref.shape, 2 ** idx_bits - 1, I32)

    @pl.when(jnp.max(nge) > top_k)
    def _():
        cnt_hi = top_k - count(lambda kt, kk: kk > thr)
        key_idx = lax.broadcasted_iota(I32, (TQ, TQ), 0)

        def tie_body(bi, jmax):
            trial = jmax | (jnp.int32(1) << (idx_bits - 1 - bi))
            cnt = count(lambda kt, kk: (kk == thr) & ((key_idx + kt * TQ) <= trial))
            return jnp.where(cnt <= cnt_hi, trial, jmax)

        jm_ref[...] = jnp.broadcast_to(lax.fori_loop(0, idx_bits, tie_body, jnp.zeros((1, TQ), I32)), jm_ref.shape)

    eye = lax.broadcasted_iota(I32, (TQ, TQ), 0) == lax.broadcasted_iota(I32, (TQ, TQ), 1)

    def to_col(row):
        col = jnp.sum(jnp.where(eye, row.astype(F32), 0.0), axis=-1, keepdims=True)
        return jnp.broadcast_to(col, (TQ, LANE)).astype(I32)

    tu_ref[...] = (to_col(thr >> 16) << 16) | to_col(thr & jnp.int32(0xFFFF))
    cand_ref[...] = to_col(jm_ref[0:1, :])

    def key_index(kt, hs):
        return lax.broadcasted_iota(I32, (TQ, LANE), 1) + (kt * TQ + hs.start)

    def mask_body(kt, c):
        kk = keys_ref[kt]
        lim = jnp.where(kt < i, jnp.int32(TQ), jnp.int32(0))
        for hs in halves:
            k_ = kk[:, hs]
            sel = (k_ > tu_ref[...]) | ((k_ == tu_ref[...]) & (key_index(kt, hs) <= cand_ref[...]))
            gap = ((lax.broadcasted_iota(I32, (TQ, LANE), 1) + hs.start) // CHUNK
                   - lax.broadcasted_iota(I32, (TQ, LANE), 0) // CHUNK)
            selm_ref[kt, :, hs] = jnp.where(sel & (gap <= lim), 0.0, NEG)
        return c

    lax.fori_loop(0, n_kt, mask_body, 0)

    for g in range(N_GRP):
        qs_ref[g * STACK:(g + 1) * STACK, :] = split_heads(qb_ref[:, g * LANE:(g + 1) * LANE])
    top_rows = lax.broadcasted_iota(I32, (STACK, 1), 0) < TQ
    outs = _attend(
        i, qs_ref,
        lambda g, off: kb_ref[pl.ds(off, TQ), :],
        lambda g, off: vb_ref[pl.ds(off, TQ), :],
        lambda kt: selm_ref[kt],
        lambda g, jt: tab_ref[2 * g:2 * g + 2, jt],
        [jnp.where(top_rows, far_ref[A_HEADS + 2 * g], far_ref[A_HEADS + 2 * g + 1]) for g in range(N_GRP)],
        mf_ref, mn_ref, l_ref, acc_ref)
    for g, o in enumerate(outs):
        o_ref[:, g * LANE:(g + 1) * LANE] = jnp.where(lo, o[:TQ], o[TQ:]).astype(o_ref.dtype)


def _attn_b(proj, wi, near_b, far):
    b, s, _ = proj.shape
    top_k = min(TOPK_MAX, s // 4)
    assert top_k <= TQ and s % TQ == 0 and (s & (s - 1)) == 0
    n_kt = s // TQ
    assert B_HEADS == 2 * N_GRP and s < 2 ** 16
    qb_w = B_HEADS * HEAD_DIM
    qi_w = IDX_HEADS * HEAD_DIM
    return pl.pallas_call(
        functools.partial(_attn_b_kernel, top_k=float(top_k), idx_bits=int(math.log2(s))),
        grid_spec=pltpu.PrefetchScalarGridSpec(
            num_scalar_prefetch=1,
            grid=(b, n_kt),
            in_specs=[
                pl.BlockSpec((None, TQ, qb_w), lambda b_, i, f: (b_, i, COL_QB // qb_w)),
                pl.BlockSpec((None, s, LANE), lambda b_, i, f: (b_, 0, COL_KB // LANE)),
                pl.BlockSpec((None, s, LANE), lambda b_, i, f: (b_, 0, COL_VB // LANE)),
                pl.BlockSpec((None, TQ, qi_w), lambda b_, i, f: (b_, i, COL_QI // qi_w)),
                pl.BlockSpec((None, s, LANE), lambda b_, i, f: (b_, 0, COL_KI // LANE)),
                pl.BlockSpec((None, TQ, LANE), lambda b_, i, f: (b_, i, 0)),
                pl.BlockSpec((B_HEADS, NEAR_TILES, TQ, TQ), lambda b_, i, f: (0, 0, 0, 0)),
            ],
            out_specs=pl.BlockSpec((None, TQ, qb_w), lambda b_, i, f: (b_, i, 0)),
            scratch_shapes=[pltpu.VMEM((n_kt, TQ, TQ), I32)] * 2 + [pltpu.VMEM((n_kt, TQ, TQ), F32)]
            + [pltpu.VMEM((TQ, LANE), I32)] * 2 + [pltpu.VMEM((8, TQ), I32)] + _attend_scratch(),
        ),
        out_shape=jax.ShapeDtypeStruct((b, s, qb_w), BF16),
        compiler_params=_cparams(("parallel", "arbitrary")),
    )(far, proj, proj, proj, proj, proj, wi, near_b)


PAIRS_C = 4


def _c_tables(rel_table):
    n_var = WIN_C // TQ_C
    length = WIN_C + TQ_C
    k = jnp.arange(length, dtype=I32)
    cr = jnp.where(k < WIN_C, k, k - length)
    off = (jnp.arange(n_var, dtype=I32) * TQ_C)[:, None]
    rel_idx = jnp.clip(off - cr[None, :], -C_REL_CLIP, C_REL_CLIP) + C_REL_CLIP
    g = rel_table[rel_idx].transpose(2, 0, 1)
    bias = _toeplitz(g, TQ_C, WIN_C)
    r = jnp.arange(TQ_C, dtype=I32)[None, :, None]
    c = jnp.arange(WIN_C, dtype=I32)[None, None, :]
    kc = jnp.floor_divide(c - off[:, :, None], CHUNK)
    ok = (kc <= r // CHUNK) & (kc >= r // CHUNK - C_LEFT_CHUNKS)
    return jnp.where(ok[None], bias, NEG)


def _attn_c_kernel(q_ref, k_ref, v_ref, tab_ref, o_ref):
    i = pl.program_id(2)
    n_var = WIN_C // TQ_C
    lane = lax.broadcasted_iota(I32, (1, LANE), 1)
    lo = lane < HEAD_DIM
    var = jnp.minimum(i, n_var - 1)
    ks = pl.multiple_of(jnp.maximum(i - (n_var - 1), 0) * TQ_C, TQ_C)
    pairs = range(PAIRS_C)
    cols = [slice(pp * LANE, (pp + 1) * LANE) for pp in pairs]
    s = [_dot_nt(_split_lanes(q_ref[:, cols[pp]]), k_ref[pl.ds(ks, WIN_C), cols[pp]]) for pp in pairs]
    s = [s[pp] + jnp.concatenate([tab_ref[2 * pp, var], tab_ref[2 * pp + 1, var]], axis=0) for pp in pairs]
    m = [jnp.max(s[pp], axis=-1, keepdims=True) for pp in pairs]
    p = [jnp.exp(s[pp] - m[pp]) for pp in pairs]
    l = [jnp.sum(p[pp], axis=-1, keepdims=True) for pp in pairs]
    pv = [jnp.dot(p[pp].astype(BF16), v_ref[pl.ds(ks, WIN_C), cols[pp]], preferred_element_type=F32) for pp in pairs]
    for pp in pairs:
        o = pv[pp] / l[pp]
        o_ref[:, cols[pp]] = jnp.where(lo, o[:TQ_C], o[TQ_C:]).astype(o_ref.dtype)


def _attn_c(proj, tab):
    b, s, _ = proj.shape
    assert s >= WIN_C and s % TQ_C == 0
    n_grp = C_HEADS // 2 // PAIRS_C
    n_var = WIN_C // TQ_C
    wblk = PAIRS_C * LANE
    tab = tab.reshape(n_grp, 2 * PAIRS_C, n_var, TQ_C, WIN_C)
    return pl.pallas_call(
        _attn_c_kernel,
        grid=(n_grp, b, s // TQ_C),
        in_specs=[
            pl.BlockSpec((None, TQ_C, wblk), lambda h, b_, i: (b_, i, h)),
            pl.BlockSpec((None, s, wblk), lambda h, b_, i: (b_, 0, n_grp + h)),
            pl.BlockSpec((None, s, wblk), lambda h, b_, i: (b_, 0, 2 * n_grp + h)),
            pl.BlockSpec((None, 2 * PAIRS_C, n_var, TQ_C, WIN_C), lambda h, b_, i: (h, 0, 0, 0, 0),
                         pipeline_mode=pl.Buffered(1)),
        ],
        out_specs=pl.BlockSpec((None, TQ_C, wblk), lambda h, b_, i: (b_, i, h)),
        out_shape=jax.ShapeDtypeStruct((b, s, D_MODEL), BF16),
        compiler_params=_cparams(("parallel", "parallel", "arbitrary")),
    )(proj, proj, proj, tab)


ROW_TILE = 8


def _rows_to_tiles(ref, x):
    rows = x.shape[0]
    for c in range(ROW_TILE):
        ref[pl.ds(c, rows, stride=ROW_TILE), :] = x[:, c * LANE:(c + 1) * LANE]


def _tiles_to_rows(ref, rows):
    return jnp.concatenate([ref[pl.ds(c, rows, stride=ROW_TILE), :] for c in range(ROW_TILE)], axis=1)


def _router_kernel(h_ref, g_ref, wr_ref, br_ref, ei_ref, cw_ref):
    xn = _rms(h_ref[...], g_ref[...])
    logits = jnp.dot(xn, wr_ref[...], preferred_element_type=F32,
                     precision=lax.Precision.HIGHEST) + br_ref[...]
    lane = lax.broadcasted_iota(I32, (1, LANE), 1)
    lane_f = lane.astype(F32)
    big = float(LANE)
    is_g = lane < N_GROUPS
    lg = jnp.where(is_g, logits, -jnp.inf)
    gmax = jnp.max(lg, axis=-1, keepdims=True)
    g_sel = jnp.min(jnp.where(lg == gmax, lane_f, big), axis=-1, keepdims=True)
    p_gsel = 1.0 / jnp.sum(jnp.where(is_g, jnp.exp(logits - gmax), 0.0), axis=-1, keepdims=True)
    e_grp = ((lane - N_GROUPS) // EXPERTS_PER_GROUP).astype(F32)
    in_sel = (lane >= N_GROUPS) & (lane < N_GROUPS + N_EXPERTS) & (e_grp == g_sel)
    le = jnp.where(in_sel, logits, -jnp.inf)
    m1 = jnp.max(le, axis=-1, keepdims=True)
    i1 = jnp.min(jnp.where(le == m1, lane_f, big), axis=-1, keepdims=True)
    le2 = jnp.where(lane_f == i1, -jnp.inf, le)
    m2 = jnp.max(le2, axis=-1, keepdims=True)
    i2 = jnp.min(jnp.where(le2 == m2, lane_f, big), axis=-1, keepdims=True)
    t = jnp.exp(m2 - m1)
    c1 = p_gsel / (1.0 + t)
    c2 = p_gsel * t / (1.0 + t)
    a = jnp.minimum(i1, i2) - N_GROUPS - EXPERTS_PER_GROUP * g_sel
    b = jnp.maximum(i1, i2) - N_GROUPS - EXPERTS_PER_GROUP * g_sel
    pair = a * (2 * EXPERTS_PER_GROUP - 1 - a) * 0.5 + (b - a - 1.0)
    cls = g_sel * PAIRS_PER_GROUP + pair
    ei_ref[...] = jnp.broadcast_to(cls, ei_ref.shape).astype(I32)
    first_is_lo = i1 < i2
    cw_ref[...] = jnp.where(lane == 0, jnp.where(first_is_lo, c1, c2), jnp.where(first_is_lo, c2, c1))


def _router(h, g, w_rg, b_rg, w_re, b_re, tm=256):
    n, d = h.shape
    pad = LANE - N_GROUPS - N_EXPERTS
    wr = jnp.concatenate([w_rg, w_re, jnp.zeros((d, pad), F32)], axis=1)
    br = jnp.concatenate([b_rg, b_re, jnp.zeros((pad,), F32)]).reshape(1, LANE)
    return pl.pallas_call(
        _router_kernel,
        grid=(n // tm,),
        in_specs=[pl.BlockSpec((tm, d), lambda i: (i, 0)),
                  pl.BlockSpec((1, d), lambda i: (0, 0)),
                  pl.BlockSpec((d, LANE), lambda i: (0, 0)),
                  pl.BlockSpec((1, LANE), lambda i: (0, 0))],
        out_specs=[pl.BlockSpec((tm, LANE), lambda i: (i, 0)),
                   pl.BlockSpec((tm, LANE), lambda i: (i, 0))],
        out_shape=[jax.ShapeDtypeStruct((n, LANE), I32),
                   jax.ShapeDtypeStruct((n, LANE), F32)],
        compiler_params=_cparams(("parallel",)),
    )(h, g.reshape(1, d), wr, br)


N_CLASSES = N_GROUPS * PAIRS_PER_GROUP
_PAIR_AB = [(a, b) for a in range(EXPERTS_PER_GROUP) for b in range(a + 1, EXPERTS_PER_GROUP)]
CLASS_EXPERT_LO = [g * EXPERTS_PER_GROUP + a for g in range(N_GROUPS) for a, _ in _PAIR_AB]
CLASS_EXPERT_HI = [g * EXPERTS_PER_GROUP + b for g in range(N_GROUPS) for _, b in _PAIR_AB]
TB_MOE = 256


def _rank_kernel(cls_ref, rank_ref, cnt_ref, carry_ref):
    tb = cls_ref.shape[0]

    @pl.when(pl.program_id(0) == 0)
    def _():
        carry_ref[...] = jnp.zeros(carry_ref.shape, F32)

    onehot = cls_ref[...] == lax.broadcasted_iota(I32, (tb, LANE), 1)
    tril = lax.broadcasted_iota(I32, (tb, tb), 0) >= lax.broadcasted_iota(I32, (tb, tb), 1)
    pref = jnp.dot(jnp.where(tril, 1.0, 0.0).astype(BF16), jnp.where(onehot, 1.0, 0.0).astype(BF16),
                   preferred_element_type=F32)
    carry = carry_ref[...]
    rank = jnp.sum(jnp.where(onehot, pref + carry, 0.0), axis=-1, keepdims=True) - 1.0
    rank_ref[...] = jnp.broadcast_to(rank, rank_ref.shape).astype(I32)
    carry = carry + pref[tb - 1:tb, :]
    carry_ref[...] = carry
    cnt_ref[...] = carry


def _class_rank(cls, tb=512):
    n = cls.shape[0]
    return pl.pallas_call(
        _rank_kernel,
        grid=(n // tb,),
        in_specs=[pl.BlockSpec((tb, LANE), lambda i: (i, 0))],
        out_specs=[pl.BlockSpec((tb, LANE), lambda i: (i, 0)), pl.BlockSpec((1, LANE), lambda i: (0, 0))],
        out_shape=[jax.ShapeDtypeStruct((n, LANE), I32), jax.ShapeDtypeStruct((1, LANE), F32)],
        scratch_shapes=[pltpu.VMEM((1, LANE), F32)],
        compiler_params=_cparams(("arbitrary",)),
    )(cls)


def _moe_schedule(counts, n_tok, tm):
    counts = counts[:N_CLASSES].astype(I32)
    tiles_per = (counts + tm - 1) // tm
    tile_end = jnp.cumsum(tiles_per)
    tile_start = tile_end - tiles_per
    start_row = jnp.zeros((LANE,), I32).at[:N_CLASSES].set(tile_start * tm)
    n_tiles = n_tok // tm + N_CLASSES
    t = jnp.arange(n_tiles, dtype=I32)
    tc = jnp.searchsorted(tile_end, t, side="right").astype(I32)
    live = tc < N_CLASSES
    tc = jnp.minimum(tc, N_CLASSES - 1)
    nv = jnp.where(live, jnp.clip(counts[tc] - (t - tile_start[tc]) * tm, 0, tm), 0).astype(I32)
    e_lo = jnp.asarray(CLASS_EXPERT_LO, I32)[tc]
    e_hi = jnp.asarray(CLASS_EXPERT_HI, I32)[tc]
    return start_row, e_lo, e_hi, nv


def _tok(i, width=ROW_TILE):
    return pl.ds(i * width, width)


def _slot_of(start_ref, cls_ref, rank_ref, r):
    return start_ref[cls_ref[0, r]] + rank_ref[0, r]


def _dispatch_kernel(start_ref, cls_ref, rank_ref, h_ref, g_ref, xs_init, xs_hbm, sbuf, sem, *, tb, n_blk):
    del xs_init
    i = pl.program_id(0)
    slot = i % 2

    def wait(sl):
        pltpu.make_async_copy(sbuf.at[sl], xs_hbm.at[pl.ds(0, tb * ROW_TILE)], sem.at[sl]).wait()

    _rows_to_tiles(sbuf.at[slot], _rms(h_ref[...], g_ref[...]))
    for r in range(tb):
        pltpu.make_async_copy(sbuf.at[slot, _tok(r)], xs_hbm.at[_tok(_slot_of(start_ref, cls_ref, rank_ref, r))],
                              sem.at[slot]).start()

    @pl.when(i > 0)
    def _():
        wait(1 - slot)

    @pl.when(i == n_blk - 1)
    def _():
        wait(slot)


def _dispatch(h, g, start_row, cls_sm, rank_sm, n_slots, tb=TB_MOE):
    n, d = h.shape
    n_blk = cls_sm.shape[0]
    xs0 = jnp.zeros((n_slots * ROW_TILE, LANE), F32)
    idx_spec = pl.BlockSpec((None, 1, tb), lambda i, st: (i, 0, 0), memory_space=pltpu.SMEM)
    return pl.pallas_call(
        functools.partial(_dispatch_kernel, tb=tb, n_blk=n_blk),
        grid_spec=pltpu.PrefetchScalarGridSpec(
            num_scalar_prefetch=1,
            grid=(n_blk,),
            in_specs=[idx_spec, idx_spec,
                      pl.BlockSpec((tb, d), lambda i, st: (i, 0)),
                      pl.BlockSpec((1, d), lambda i, st: (0, 0)),
                      pl.BlockSpec(memory_space=pl.ANY)],
            out_specs=pl.BlockSpec(memory_space=pl.ANY),
            scratch_shapes=[pltpu.VMEM((2, tb * ROW_TILE, LANE), F32), pltpu.SemaphoreType.DMA((2,))],
        ),
        out_shape=jax.ShapeDtypeStruct(xs0.shape, F32),
        input_output_aliases={5: 0},
        compiler_params=_cparams(("arbitrary",)),
    )(start_row, cls_sm, rank_sm, h, g.reshape(1, d), xs0)


def _expert_kernel(elo_ref, ehi_ref, nv_ref, xs_ref, wga, wua, wda, wgb, wub, wdb, ys_ref, *, tm):
    t = pl.program_id(0)

    @pl.when(nv_ref[t] > 0)
    def _():
        x = _tiles_to_rows(xs_ref, tm).astype(BF16)
        g = [jnp.dot(x, w[...], preferred_element_type=F32) for w in (wga, wgb)]
        u = [jnp.dot(x, w[...], preferred_element_type=F32) for w in (wua, wub)]
        act = [((g[k] * jax.nn.sigmoid(g[k])) * u[k]).astype(BF16) for k in range(2)]
        y = [jnp.dot(act[k], w[...], preferred_element_type=F32) for k, w in enumerate((wda, wdb))]
        for half in range(2):
            for c in range(ROW_TILE):
                ys_ref[pl.ds(half * ROW_TILE + c, tm, stride=2 * ROW_TILE), :] = y[half][:, c * LANE:(c + 1) * LANE]

    @pl.when(nv_ref[t] == 0)
    def _():
        ys_ref[...] = jnp.zeros(ys_ref.shape, F32)


def _moe_experts(xs, e_lo, e_hi, nv, layer, wg, wu, wd, tm=TM_MOE):
    d = wg.shape[2]
    assert d == ROW_TILE * LANE
    n_tiles = nv.shape[0]

    def w_spec(rows, cols, which):
        return pl.BlockSpec((None, None, rows, cols), lambda t, lo, hi, nv_: (layer, (lo, hi)[which][t], 0, 0))

    w_specs = [w_spec(d, D_EXPERT, k) for k in (0, 0)] + [w_spec(D_EXPERT, d, 0)]
    w_specs += [w_spec(d, D_EXPERT, k) for k in (1, 1)] + [w_spec(D_EXPERT, d, 1)]
    return pl.pallas_call(
        functools.partial(_expert_kernel, tm=tm),
        grid_spec=pltpu.PrefetchScalarGridSpec(
            num_scalar_prefetch=3,
            grid=(n_tiles,),
            in_specs=[pl.BlockSpec((tm * ROW_TILE, LANE), lambda t, lo, hi, nv_: (t, 0))] + w_specs,
            out_specs=pl.BlockSpec((tm * 2 * ROW_TILE, LANE), lambda t, lo, hi, nv_: (t, 0)),
        ),
        out_shape=jax.ShapeDtypeStruct((n_tiles * tm * 2 * ROW_TILE, LANE), F32),
        compiler_params=_cparams(("arbitrary",)),
    )(e_lo, e_hi, nv, xs, wg, wu, wd, wg, wu, wd)


def _combine_kernel(start_ref, cls_cur, rank_cur, cls_next, rank_next, ys_hbm, h_ref, cw_ref, g_ref, o_ref,
                    ybuf, sem, *, tb, n_blk, final):
    i = pl.program_id(0)
    slot = i % 2
    other = 1 - slot
    pair = 2 * ROW_TILE

    def fetch(cls_ref, rank_ref, r, sl):
        return pltpu.make_async_copy(ys_hbm.at[_tok(_slot_of(start_ref, cls_ref, rank_ref, r), pair)],
                                     ybuf.at[sl, _tok(r, pair)], sem.at[sl])

    def wait(sl):
        pltpu.make_async_copy(ys_hbm.at[pl.ds(0, tb * pair)], ybuf.at[sl], sem.at[sl]).wait()

    @pl.when(i == 0)
    def _():
        def body(r, c):
            fetch(cls_cur, rank_cur, r, 0).start()
            return c
        lax.fori_loop(0, tb, body, 0)

    wait(slot)
    for r in range(tb):
        fetch(cls_next, rank_next, r, other).start()
    yb = ybuf.at[slot]
    y_lo = jnp.concatenate([yb[pl.ds(c, tb, stride=pair), :] for c in range(ROW_TILE)], axis=1)
    y_hi = jnp.concatenate([yb[pl.ds(ROW_TILE + c, tb, stride=pair), :] for c in range(ROW_TILE)], axis=1)
    cw = cw_ref[...]
    out = h_ref[...] + cw[:, 0:1] * y_lo + cw[:, 1:2] * y_hi
    if final:
        out = _rms(out, g_ref[...])
    o_ref[...] = out

    @pl.when(i == n_blk - 1)
    def _():
        wait(other)


def _combine(h, ys, cw, g, start_row, cls_sm, rank_sm, final, tb=TB_MOE):
    n, d = h.shape
    n_blk = n // tb
    cls_x = jnp.concatenate([cls_sm, cls_sm[-1:]], axis=0)
    rank_x = jnp.concatenate([rank_sm, rank_sm[-1:]], axis=0)

    def idx_spec(shift):
        return pl.BlockSpec((None, 1, tb), lambda i, st: (i + shift, 0, 0), memory_space=pltpu.SMEM)

    return pl.pallas_call(
        functools.partial(_combine_kernel, tb=tb, n_blk=n_blk, final=final),
        grid_spec=pltpu.PrefetchScalarGridSpec(
            num_scalar_prefetch=1,
            grid=(n_blk,),
            in_specs=[idx_spec(0), idx_spec(0), idx_spec(1), idx_spec(1),
                      pl.BlockSpec(memory_space=pl.ANY),
                      pl.BlockSpec((tb, d), lambda i, st: (i, 0)),
                      pl.BlockSpec((tb, LANE), lambda i, st: (i, 0)),
                      pl.BlockSpec((1, d), lambda i, st: (0, 0))],
            out_specs=pl.BlockSpec((tb, d), lambda i, st: (i, 0)),
            scratch_shapes=[pltpu.VMEM((2, tb * 2 * ROW_TILE, LANE), F32), pltpu.SemaphoreType.DMA((2,))],
        ),
        out_shape=jax.ShapeDtypeStruct((n, d), F32),
        compiler_params=_cparams(("arbitrary",)),
    )(start_row, cls_x, rank_x, cls_x, rank_x, ys, h, cw, g.reshape(1, d))


def _moe_layer(h, layer, ln_g, w_rg, b_rg, w_re, b_re, w_gate, w_up, w_down, final_g, final):
    n = h.shape[0]
    cls, cw = _router(h, ln_g, w_rg, b_rg, w_re, b_re)
    rank, counts = _class_rank(cls)
    start_row, e_lo, e_hi, nv = _moe_schedule(counts[0], n, TM_MOE)
    cls_sm = cls[:, 0].reshape(n // TB_MOE, 1, TB_MOE)
    rank_sm = rank[:, 0].reshape(n // TB_MOE, 1, TB_MOE)
    xs = _dispatch(h, ln_g, start_row, cls_sm, rank_sm, nv.shape[0] * TM_MOE)
    ys = _moe_experts(xs, e_lo, e_hi, nv, layer, w_gate, w_up, w_down)
    return _combine(h, ys, cw, final_g, start_row, cls_sm, rank_sm, final)


def _ab_weights(w_in):
    scale = HEAD_DIM ** -0.5
    widths = (512, 512, 512, 512, 64, 64, 256, 64, 4)
    offs = [sum(widths[:j]) for j in range(len(widths) + 1)]
    q_a, k_a, v_a, q_b, k_b, v_b, q_i, k_i, w_i = [w_in[:, offs[j]:offs[j + 1]] for j in range(len(widths))]
    pad = jnp.zeros((w_in.shape[0], LANE - IDX_HEADS), w_in.dtype)
    main = jnp.concatenate([q_a * scale, k_a, v_a, q_b * scale, k_b, k_b, v_b, v_b, q_i, k_i, k_i], axis=1)
    return jnp.concatenate([main, w_i, pad], axis=1).astype(BF16)


def kernel(x, t5_bias, ln_mix, ln_ffn, ln_final, ab_w_in, ab_w_out, diff_lambda, diff_subln,
           c_w_in, c_w_out, c_rel_bias, moe_w_rg, moe_b_rg, moe_w_re, moe_b_re,
           moe_w_gate, moe_w_up, moe_w_down):
    b, s, d = x.shape
    n = b * s
    depth = ln_mix.shape[0]
    h = x.reshape(n, d)
    near, far = _t5_tables(t5_bias)
    w_gate, w_up, w_down = (w.astype(BF16) for w in (moe_w_gate, moe_w_up, moe_w_down))
    for l in range(depth):
        if l % 2 == 0:
            e = l // 2
            lam_init = 0.8 - 0.6 * math.exp(-0.3 * l)
            proj, wi = _norm_proj(h, ln_mix[l], _ab_weights(ab_w_in[e]), AB_COLS)
            proj = proj.reshape(b, s, AB_COLS)
            y_a = _attn_a(proj, near[:A_HEADS], far, diff_lambda[e], diff_subln[e], lam_init)
            y_b = _attn_b(proj, wi.reshape(b, s, LANE), near[A_HEADS:], far)
            h = _out_proj(h, y_a.reshape(n, -1), 0, y_b.reshape(n, -1), 0, ab_w_out[e].astype(BF16))
        else:
            o = l // 2
            scale = HEAD_DIM ** -0.5
            w_in = jnp.concatenate([c_w_in[o][:, :d] * scale, c_w_in[o][:, d:]], axis=1).astype(BF16)
            (proj,) = _norm_proj(h, ln_mix[l], w_in, 3 * d)
            y = _attn_c(proj.reshape(b, s, 3 * d), _c_tables(c_rel_bias[o])).reshape(n, d)
            h = _out_proj(h, y, 0, y, 1, c_w_out[o].astype(BF16))
        h = _moe_layer(h, l, ln_ffn[l], moe_w_rg[l], moe_b_rg[l], moe_w_re[l], moe_b_re[l],
                       w_gate, w_up, w_down, ln_final, l == depth - 1)
    return h.reshape(b, s, d)
```

```python
import functools
import math

import jax
import jax.numpy as jnp
from jax import lax
from jax.experimental import pallas as pl
from jax.experimental.pallas import tpu as pltpu

F32 = jnp.float32
BF16 = jnp.bfloat16
I32 = jnp.int32

D_MODEL = 1024
CHUNK = 64
HEAD_DIM = 64
RMS_EPS = 1e-6
A_HEADS = 4
A_V_DIM = 2 * HEAD_DIM
B_HEADS = 8
IDX_HEADS = 4
TOPK_MAX = 256
C_HEADS = D_MODEL // HEAD_DIM
C_LEFT_CHUNKS = 8
C_REL_CLIP = 128
T5_BUCKETS = 32
T5_MAX_DIST = 1024
N_GROUPS = 4
EXPERTS_PER_GROUP = 8
N_EXPERTS = N_GROUPS * EXPERTS_PER_GROUP
PAIRS_PER_GROUP = EXPERTS_PER_GROUP * (EXPERTS_PER_GROUP - 1) // 2
D_EXPERT = D_MODEL // 2
NEG = -1e30

LANE = 128
VMEM_LIMIT = 56 * 1024 * 1024

TQ = 256
NEAR_TILES = 4
TQ_C = 2 * CHUNK
WIN_C = (C_LEFT_CHUNKS + 2) * CHUNK
TM_MOE = 256

COL_QA, COL_KA, COL_VA, COL_QB, COL_KB, COL_VB, COL_QI, COL_KI, COL_WI = (
    0, 512, 1024, 1536, 2048, 2176, 2304, 2560, 2688)
AB_COLS = 2688


def _cparams(sem, vmem=VMEM_LIMIT):
    return pltpu.CompilerParams(dimension_semantics=sem, vmem_limit_bytes=vmem)


def _dot_nt(a, b):
    return lax.dot_general(a, b, (((1,), (1,)), ((), ())), preferred_element_type=F32)


def _rms(x, g):
    return (x * lax.rsqrt(jnp.mean(x * x, axis=-1, keepdims=True) + RMS_EPS)) * g


def _norm_proj_kernel(x_ref, g_ref, w_ref, o_ref, *tail_ref, n_main):
    xn = _rms(x_ref[...], g_ref[...]).astype(BF16)
    acc = jnp.dot(xn, w_ref[...], preferred_element_type=F32)
    o_ref[...] = acc[:, :n_main].astype(o_ref.dtype)
    if tail_ref:
        tail_ref[0][...] = acc[:, n_main:]


def _norm_proj(x, g, w, n_main, tm=256):
    n, d = x.shape
    n_all = w.shape[1]
    out_shape = [jax.ShapeDtypeStruct((n, n_main), BF16)]
    out_specs = [pl.BlockSpec((tm, n_main), lambda i: (i, 0))]
    if n_all > n_main:
        out_shape.append(jax.ShapeDtypeStruct((n, n_all - n_main), F32))
        out_specs.append(pl.BlockSpec((tm, n_all - n_main), lambda i: (i, 0)))
    return pl.pallas_call(
        functools.partial(_norm_proj_kernel, n_main=n_main),
        grid=(n // tm,),
        in_specs=[pl.BlockSpec((tm, d), lambda i: (i, 0)),
                  pl.BlockSpec((1, d), lambda i: (0, 0)),
                  pl.BlockSpec((d, n_all), lambda i: (0, 0))],
        out_specs=out_specs, out_shape=out_shape,
        compiler_params=_cparams(("parallel",)),
    )(x, g.reshape(1, d), w)


def _out_proj_kernel(h_ref, y0_ref, y1_ref, w0_ref, w1_ref, o_ref):
    acc = jnp.dot(y0_ref[...], w0_ref[...], preferred_element_type=F32)
    acc = acc + jnp.dot(y1_ref[...], w1_ref[...], preferred_element_type=F32)
    o_ref[...] = h_ref[...] + acc


def _out_proj(h, y0, c0, y1, c1, w, tm=512):
    n, d = h.shape
    half = d // 2
    return pl.pallas_call(
        _out_proj_kernel,
        grid=(n // tm,),
        in_specs=[pl.BlockSpec((tm, d), lambda i: (i, 0)),
                  pl.BlockSpec((tm, half), lambda i: (i, c0)),
                  pl.BlockSpec((tm, half), lambda i: (i, c1)),
                  pl.BlockSpec((half, d), lambda i: (0, 0)),
                  pl.BlockSpec((half, d), lambda i: (1, 0))],
        out_specs=pl.BlockSpec((tm, d), lambda i: (i, 0)),
        out_shape=jax.ShapeDtypeStruct((n, d), F32),
        compiler_params=_cparams(("parallel",)),
    )(h, y0, y1, w, w)


def _t5_bucket(rel):
    half = T5_BUCKETS // 2
    max_exact = half // 2
    ret = jnp.where(rel > 0, half, 0)
    n = jnp.abs(rel)
    nf = jnp.maximum(n, 1).astype(F32)
    large = max_exact + (jnp.log(nf / max_exact) / math.log(T5_MAX_DIST / max_exact)
                         * (half - max_exact)).astype(I32)
    large = jnp.minimum(large, half - 1)
    return (ret + jnp.where(n < max_exact, n, large)).astype(I32)


def _toeplitz(g, rows, width):
    length = g.shape[-1]
    assert width < length and rows <= length
    flat = jnp.tile(g, (1,) * (g.ndim - 1) + (rows,))[..., :rows * (length - 1)]
    return flat.reshape(g.shape[:-1] + (rows, length - 1))[..., :width]


def _t5_tables(t5_bias):
    span = NEAR_TILES * TQ
    length = span + TQ
    k = jnp.arange(length, dtype=I32)
    delta = jnp.where(k < span, k, k - length) - (span - TQ)
    g = t5_bias[_t5_bucket(delta)].T
    bias = _toeplitz(g, TQ, span)
    r = jnp.arange(TQ, dtype=I32)[:, None]
    kr = jnp.arange(span, dtype=I32)[None, :] - (span - TQ)
    ok = jnp.floor_divide(kr, CHUNK) <= (r // CHUNK)
    bias = jnp.where(ok[None], bias, NEG)
    near = bias.reshape(-1, TQ, NEAR_TILES, TQ).transpose(0, 2, 1, 3)
    far = t5_bias[_t5_bucket(jnp.asarray(-(span - TQ) - 1, I32))]
    return near, far


N_GRP = 4
STACK = 2 * TQ


def _attend(i, qs_ref, k_fn, v_fn, mask_fn, near_fn, far_cols, mf_ref, mn_ref, l_ref, acc_ref, s_ref=None):
    n_far = jnp.maximum(i - (NEAR_TILES - 1), 0)
    n_far_slots = 0 if s_ref is None else s_ref.shape[0] - NEAR_TILES
    n_far_cached = jnp.minimum(n_far, n_far_slots)

    mf_ref[...] = jnp.full(mf_ref.shape, NEG, F32)
    mn_ref[...] = jnp.full(mn_ref.shape, NEG, F32)

    def rows(g):
        return slice(g * STACK, (g + 1) * STACK)

    def logits(g, kt, jt):
        off = pl.multiple_of(kt * TQ, TQ)
        s = _dot_nt(qs_ref[rows(g), :], k_fn(g, off))
        extra = None if mask_fn is None else mask_fn(kt)[None]
        if jt is not None:
            extra = near_fn(g, jt) if extra is None else extra + near_fn(g, jt)
        if extra is not None:
            s = (s.reshape(2, TQ, TQ) + extra).reshape(STACK, TQ)
        return s

    def loops(make_fn):
        def run(lo, hi, fn):
            def body(kt, c):
                fn(kt)
                return c
            lax.fori_loop(lo, hi, body, 0)

        if n_far_slots > 0:
            run(0, n_far_cached, make_fn(False, True))
        run(n_far_cached, n_far, make_fn(False, False))
        run(n_far, i + 1, make_fn(True, s_ref is not None))

    def slot(kt, jt):
        return NEAR_TILES + kt if jt is None else jt

    def pass1(near, cached):
        m_ref = mn_ref if near else mf_ref

        def fn(kt):
            jt = kt - (i - (NEAR_TILES - 1)) if near else None
            s = [logits(g, kt, jt) for g in range(N_GRP)]
            for g in range(N_GRP):
                if cached:
                    s_ref[slot(kt, jt), rows(g), :] = s[g]
                m_ref[rows(g), :] = jnp.maximum(m_ref[rows(g), :], jnp.maximum(s[g][:, :LANE], s[g][:, LANE:]))
        return fn

    loops(pass1)

    for g in range(N_GRP):
        m = jnp.maximum(jnp.max(mf_ref[rows(g), :], axis=-1, keepdims=True) + far_cols[g],
                        jnp.max(mn_ref[rows(g), :], axis=-1, keepdims=True))
        m_near = jnp.broadcast_to(m, (STACK, LANE))
        mn_ref[rows(g), :] = m_near
        mf_ref[rows(g), :] = m_near - far_cols[g]
    l_ref[...] = jnp.zeros(l_ref.shape, F32)
    acc_ref[...] = jnp.zeros(acc_ref.shape, F32)

    def pass2(near, cached):
        shift_ref = mn_ref if near else mf_ref

        def fn(kt):
            jt = kt - (i - (NEAR_TILES - 1)) if near else None
            off = pl.multiple_of(kt * TQ, TQ)
            grp = range(N_GRP)
            s = [s_ref[slot(kt, jt), rows(g), :] if cached else logits(g, kt, jt) for g in grp]
            p0 = [jnp.exp(s[g][:, :LANE] - shift_ref[rows(g), :]) for g in grp]
            p1 = [jnp.exp(s[g][:, LANE:] - shift_ref[rows(g), :]) for g in grp]
            for g in grp:
                l_ref[rows(g), :] += p0[g] + p1[g]
            pv = [jnp.dot(jnp.concatenate([p0[g], p1[g]], axis=1).astype(BF16), v_fn(g, off),
                          preferred_element_type=F32) for g in grp]
            for g in grp:
                acc_ref[rows(g), :] += pv[g]
        return fn

    loops(pass2)
    return [acc_ref[rows(g), :] / jnp.sum(l_ref[rows(g), :], axis=-1, keepdims=True) for g in range(N_GRP)]


def _attend_scratch():
    return [pltpu.VMEM((N_GRP * STACK, LANE), BF16)] + [pltpu.VMEM((N_GRP * STACK, LANE), F32)] * 4


def _split_lanes(x):
    lo = lax.broadcasted_iota(I32, (1, LANE), 1) < HEAD_DIM
    z = jnp.zeros_like(x)
    return jnp.concatenate([jnp.where(lo, x, z), jnp.where(lo, z, x)], axis=0)


def _attn_a_kernel(far_ref, q_ref, k_ref, v_ref, tab_ref, lam_ref, sub_ref, o_ref,
                   qs_ref, mf_ref, mn_ref, l_ref, acc_ref, s_ref, *, lam_init):
    i = pl.program_id(1)
    for g in range(A_HEADS):
        qs_ref[g * STACK:(g + 1) * STACK, :] = _split_lanes(q_ref[:, g * LANE:(g + 1) * LANE])
    outs = _attend(
        i, qs_ref,
        lambda g, off: k_ref[pl.ds(off, TQ), g * LANE:(g + 1) * LANE],
        lambda g, off: v_ref[pl.ds(off, TQ), g * LANE:(g + 1) * LANE],
        None,
        lambda g, jt: tab_ref[g, jt][None],
        [jnp.full((STACK, 1), far_ref[g], F32) for g in range(A_HEADS)],
        mf_ref, mn_ref, l_ref, acc_ref, s_ref)
    lam = lam_ref[...]
    lam_full = (jnp.exp(jnp.sum(lam[0:1] * lam[1:2], axis=-1, keepdims=True))
                - jnp.exp(jnp.sum(lam[2:3] * lam[3:4], axis=-1, keepdims=True)) + lam_init)
    for g, o in enumerate(outs):
        d = o[:TQ] - lam_full * o[TQ:]
        o_ref[:, g * LANE:(g + 1) * LANE] = (_rms(d, sub_ref[...]) * (1.0 - lam_init)).astype(o_ref.dtype)


def _attn_a(proj, near, far, lam, subln, lam_init):
    b, s, _ = proj.shape
    assert A_HEADS == N_GRP
    wide = A_HEADS * A_V_DIM
    return pl.pallas_call(
        functools.partial(_attn_a_kernel, lam_init=lam_init),
        grid_spec=pltpu.PrefetchScalarGridSpec(
            num_scalar_prefetch=1,
            grid=(b, s // TQ),
            in_specs=[
                pl.BlockSpec((None, TQ, wide), lambda b_, i, f: (b_, i, COL_QA // wide)),
                pl.BlockSpec((None, s, wide), lambda b_, i, f: (b_, 0, COL_KA // wide), pipeline_mode=pl.Buffered(1)),
                pl.BlockSpec((None, s, wide), lambda b_, i, f: (b_, 0, COL_VA // wide), pipeline_mode=pl.Buffered(1)),
                pl.BlockSpec((A_HEADS, NEAR_TILES, TQ, TQ), lambda b_, i, f: (0, 0, 0, 0),
                             pipeline_mode=pl.Buffered(1)),
                pl.BlockSpec((4, HEAD_DIM), lambda b_, i, f: (0, 0)),
                pl.BlockSpec((1, A_V_DIM), lambda b_, i, f: (0, 0)),
            ],
            out_specs=pl.BlockSpec((None, TQ, wide), lambda b_, i, f: (b_, i, 0)),
            scratch_shapes=_attend_scratch() + [pltpu.VMEM((s // TQ, N_GRP * STACK, TQ), F32)],
        ),
        out_shape=jax.ShapeDtypeStruct((b, s, wide), BF16),
        compiler_params=_cparams(("parallel", "arbitrary")),
    )(far, proj, proj, proj, near, lam, subln.reshape(1, A_V_DIM))


def _attn_b_kernel(far_ref, qb_ref, kb_ref, vb_ref, qi_ref, ki_ref, wi_ref, tab_ref, o_ref,
                   keys_ref, keyst_ref, selm_ref, cand_ref, tu_ref, jm_ref, qs_ref, mf_ref, mn_ref, l_ref, acc_ref, s_ref,
                   *, top_k, idx_bits):
    i = pl.program_id(1)
    n_kt = i + 1
    lane = lax.broadcasted_iota(I32, (1, LANE), 1)
    lo = lane < HEAD_DIM
    int_min = jnp.int32(-2 ** 31)
    split_heads = _split_lanes

    row = lax.broadcasted_iota(I32, (TQ, TQ), 0)
    col = lax.broadcasted_iota(I32, (TQ, TQ), 1)
    chunk_gap = (col // CHUNK) - (row // CHUNK)

    def causal(kt):
        return chunk_gap <= jnp.where(kt < i, jnp.int32(TQ), jnp.int32(0))

    qi = qi_ref[...]
    qis = jnp.concatenate([split_heads(qi[:, :LANE]), split_heads(qi[:, LANE:])], axis=0)
    wi = wi_ref[...] * ((IDX_HEADS ** -0.5) * (HEAD_DIM ** -0.5))

    def score_body(kt, c):
        off = pl.multiple_of(kt * TQ, TQ)
        r = jnp.maximum(_dot_nt(qis, ki_ref[pl.ds(off, TQ), :]), 0.0)
        isc = r[0:TQ] * wi[:, 0:1]
        for hh in range(1, IDX_HEADS):
            isc = isc + r[hh * TQ:(hh + 1) * TQ] * wi[:, hh:hh + 1]
        isc = jnp.where(isc == 0.0, 0.0, isc)
        isc = jnp.where(causal(kt), isc, -jnp.inf)
        bits = lax.bitcast_convert_type(isc, I32)
        key = bits ^ ((bits >> 31) & jnp.int32(0x7FFFFFFF))
        keys_ref[kt] = key
        keyst_ref[kt] = key.T
        return c

    lax.fori_loop(0, n_kt, score_body, 0)

    halves = (slice(0, LANE), slice(LANE, TQ))

    def count(pred_fn):
        def body(kt, acc):
            c = jnp.where(pred_fn(kt, keyst_ref[kt]), 1.0, 0.0)
            return acc + jnp.sum(c.reshape(TQ // 32, 32, TQ), axis=0)
        acc = lax.fori_loop(0, n_kt, body, jnp.zeros((32, TQ), F32))
        return jnp.sum(acc, axis=0, keepdims=True)

    def thr_body(bi, state):
        tu, nge = state
        cand_u = tu | (jnp.int32(1) << (31 - bi))
        cand_s = cand_u ^ int_min
        cnt = count(lambda kt, kk: kk >= cand_s)
        keep = cnt >= top_k
        return jnp.where(keep, cand_u, tu), jnp.where(keep, cnt, nge)

    tu, nge = lax.fori_loop(0, 32, thr_body, (jnp.zeros((1, TQ), I32),
                                              jnp.full((1, TQ), (n_kt * TQ).astype(F32), F32)))
    thr = tu ^ int_min

    jm_ref[...] = jnp.full(jm_ref.shape, 2 ** idx_bits - 1, I32)

    @pl.when(jnp.max(nge) > top_k)
    def _():
        cnt_hi = top_k - count(lambda kt, kk: kk > thr)
        key_idx = lax.broadcasted_iota(I32, (TQ, TQ), 0)

        def tie_body(bi, jmax):
            trial = jmax | (jnp.int32(1) << (idx_bits - 1 - bi))
            cnt = count(lambda kt, kk: (kk == thr) & ((key_idx + kt * TQ) <= trial))
            return jnp.where(cnt <= cnt_hi, trial, jmax)

        jm_ref[...] = jnp.broadcast_to(lax.fori_loop(0, idx_bits, tie_body, jnp.zeros((1, TQ), I32)), jm_ref.shape)

    eye = lax.broadcasted_iota(I32, (TQ, TQ), 0) == lax.broadcasted_iota(I32, (TQ, TQ), 1)

    def to_col(row):
        col = jnp.sum(jnp.where(eye, row.astype(F32), 0.0), axis=-1, keepdims=True)
        return jnp.broadcast_to(col, (TQ, LANE)).astype(I32)

    tu_ref[...] = (to_col(thr >> 16) << 16) | to_col(thr & jnp.int32(0xFFFF))
    cand_ref[...] = to_col(jm_ref[0:1, :])

    def key_index(kt, hs):
        return lax.broadcasted_iota(I32, (TQ, LANE), 1) + (kt * TQ + hs.start)

    def mask_body(kt, c):
        kk = keys_ref[kt]
        lim = jnp.where(kt < i, jnp.int32(TQ), jnp.int32(0))
        for hs in halves:
            k_ = kk[:, hs]
            sel = (k_ > tu_ref[...]) | ((k_ == tu_ref[...]) & (key_index(kt, hs) <= cand_ref[...]))
            gap = ((lax.broadcasted_iota(I32, (TQ, LANE), 1) + hs.start) // CHUNK
                   - lax.broadcasted_iota(I32, (TQ, LANE), 0) // CHUNK)
            selm_ref[kt, :, hs] = jnp.where(sel & (gap <= lim), 0.0, NEG)
        return c

    lax.fori_loop(0, n_kt, mask_body, 0)

    for g in range(N_GRP):
        qs_ref[g * STACK:(g + 1) * STACK, :] = split_heads(qb_ref[:, g * LANE:(g + 1) * LANE])
    top_rows = lax.broadcasted_iota(I32, (STACK, 1), 0) < TQ
    outs = _attend(
        i, qs_ref,
        lambda g, off: kb_ref[pl.ds(off, TQ), :],
        lambda g, off: vb_ref[pl.ds(off, TQ), :],
        lambda kt: selm_ref[kt],
        lambda g, jt: tab_ref[2 * g:2 * g + 2, jt],
        [jnp.where(top_rows, far_ref[A_HEADS + 2 * g], far_ref[A_HEADS + 2 * g + 1]) for g in range(N_GRP)],
        mf_ref, mn_ref, l_ref, acc_ref, s_ref)
    for g, o in enumerate(outs):
        o_ref[:, g * LANE:(g + 1) * LANE] = jnp.where(lo, o[:TQ], o[TQ:]).astype(o_ref.dtype)


B_CACHE_TILES = NEAR_TILES + 6


def _attn_b(proj, wi, near_b, far):
    b, s, _ = proj.shape
    top_k = min(TOPK_MAX, s // 4)
    assert top_k <= TQ and s % TQ == 0 and (s & (s - 1)) == 0
    n_kt = s // TQ
    assert B_HEADS == 2 * N_GRP and s < 2 ** 16
    qb_w = B_HEADS * HEAD_DIM
    qi_w = IDX_HEADS * HEAD_DIM
    return pl.pallas_call(
        functools.partial(_attn_b_kernel, top_k=float(top_k), idx_bits=int(math.log2(s))),
        grid_spec=pltpu.PrefetchScalarGridSpec(
            num_scalar_prefetch=1,
            grid=(b, n_kt),
            in_specs=[
                pl.BlockSpec((None, TQ, qb_w), lambda b_, i, f: (b_, i, COL_QB // qb_w)),
                pl.BlockSpec((None, s, LANE), lambda b_, i, f: (b_, 0, COL_KB // LANE)),
                pl.BlockSpec((None, s, LANE), lambda b_, i, f: (b_, 0, COL_VB // LANE)),
                pl.BlockSpec((None, TQ, qi_w), lambda b_, i, f: (b_, i, COL_QI // qi_w)),
                pl.BlockSpec((None, s, LANE), lambda b_, i, f: (b_, 0, COL_KI // LANE)),
                pl.BlockSpec((None, TQ, LANE), lambda b_, i, f: (b_, i, 0)),
                pl.BlockSpec((B_HEADS, NEAR_TILES, TQ, TQ), lambda b_, i, f: (0, 0, 0, 0),
                             pipeline_mode=pl.Buffered(1)),
            ],
            out_specs=pl.BlockSpec((None, TQ, qb_w), lambda b_, i, f: (b_, i, 0)),
            scratch_shapes=[pltpu.VMEM((n_kt, TQ, TQ), I32)] * 2 + [pltpu.VMEM((n_kt, TQ, TQ), F32)]
            + [pltpu.VMEM((TQ, LANE), I32)] * 2 + [pltpu.VMEM((8, TQ), I32)] + _attend_scratch()
            + [pltpu.VMEM((min(n_kt, B_CACHE_TILES), N_GRP * STACK, TQ), F32)],
        ),
        out_shape=jax.ShapeDtypeStruct((b, s, qb_w), BF16),
        compiler_params=_cparams(("parallel", "arbitrary")),
    )(far, proj, proj, proj, proj, proj, wi, near_b)


PAIRS_C = 4


def _c_tables(rel_table):
    n_var = WIN_C // TQ_C
    length = WIN_C + TQ_C
    k = jnp.arange(length, dtype=I32)
    cr = jnp.where(k < WIN_C, k, k - length)
    off = (jnp.arange(n_var, dtype=I32) * TQ_C)[:, None]
    rel_idx = jnp.clip(off - cr[None, :], -C_REL_CLIP, C_REL_CLIP) + C_REL_CLIP
    g = rel_table[rel_idx].transpose(2, 0, 1)
    bias = _toeplitz(g, TQ_C, WIN_C)
    r = jnp.arange(TQ_C, dtype=I32)[None, :, None]
    c = jnp.arange(WIN_C, dtype=I32)[None, None, :]
    kc = jnp.floor_divide(c - off[:, :, None], CHUNK)
    ok = (kc <= r // CHUNK) & (kc >= r // CHUNK - C_LEFT_CHUNKS)
    return jnp.where(ok[None], bias, NEG)


def _attn_c_kernel(q_ref, k_ref, v_ref, tab_ref, o_ref):
    i = pl.program_id(2)
    n_var = WIN_C // TQ_C
    lane = lax.broadcasted_iota(I32, (1, LANE), 1)
    lo = lane < HEAD_DIM
    var = jnp.minimum(i, n_var - 1)
    ks = pl.multiple_of(jnp.maximum(i - (n_var - 1), 0) * TQ_C, TQ_C)
    pairs = range(PAIRS_C)
    cols = [slice(pp * LANE, (pp + 1) * LANE) for pp in pairs]
    s = [_dot_nt(_split_lanes(q_ref[:, cols[pp]]), k_ref[pl.ds(ks, WIN_C), cols[pp]]) for pp in pairs]
    s = [s[pp] + jnp.concatenate([tab_ref[2 * pp, var], tab_ref[2 * pp + 1, var]], axis=0) for pp in pairs]
    m = [jnp.max(s[pp], axis=-1, keepdims=True) for pp in pairs]
    p = [jnp.exp(s[pp] - m[pp]) for pp in pairs]
    l = [jnp.sum(p[pp], axis=-1, keepdims=True) for pp in pairs]
    pv = [jnp.dot(p[pp].astype(BF16), v_ref[pl.ds(ks, WIN_C), cols[pp]], preferred_element_type=F32) for pp in pairs]
    for pp in pairs:
        o = pv[pp] / l[pp]
        o_ref[:, cols[pp]] = jnp.where(lo, o[:TQ_C], o[TQ_C:]).astype(o_ref.dtype)


def _attn_c(proj, tab):
    b, s, _ = proj.shape
    assert s >= WIN_C and s % TQ_C == 0
    n_grp = C_HEADS // 2 // PAIRS_C
    n_var = WIN_C // TQ_C
    wblk = PAIRS_C * LANE
    tab = tab.reshape(n_grp, 2 * PAIRS_C, n_var, TQ_C, WIN_C)
    return pl.pallas_call(
        _attn_c_kernel,
        grid=(n_grp, b, s // TQ_C),
        in_specs=[
            pl.BlockSpec((None, TQ_C, wblk), lambda h, b_, i: (b_, i, h)),
            pl.BlockSpec((None, s, wblk), lambda h, b_, i: (b_, 0, n_grp + h)),
            pl.BlockSpec((None, s, wblk), lambda h, b_, i: (b_, 0, 2 * n_grp + h)),
            pl.BlockSpec((None, 2 * PAIRS_C, n_var, TQ_C, WIN_C), lambda h, b_, i: (h, 0, 0, 0, 0),
                         pipeline_mode=pl.Buffered(1)),
        ],
        out_specs=pl.BlockSpec((None, TQ_C, wblk), lambda h, b_, i: (b_, i, h)),
        out_shape=jax.ShapeDtypeStruct((b, s, D_MODEL), BF16),
        compiler_params=_cparams(("parallel", "parallel", "arbitrary")),
    )(proj, proj, proj, tab)


ROW_TILE = 8


def _rows_to_tiles(ref, x):
    rows = x.shape[0]
    for c in range(ROW_TILE):
        ref[pl.ds(c, rows, stride=ROW_TILE), :] = x[:, c * LANE:(c + 1) * LANE]


def _tiles_to_rows(ref, rows):
    return jnp.concatenate([ref[pl.ds(c, rows, stride=ROW_TILE), :] for c in range(ROW_TILE)], axis=1)


def _router_kernel(h_ref, g_ref, wr_ref, br_ref, ei_ref, cw_ref):
    xn = _rms(h_ref[...], g_ref[...])
    logits = jnp.dot(xn, wr_ref[...], preferred_element_type=F32,
                     precision=lax.Precision.HIGHEST) + br_ref[...]
    lane = lax.broadcasted_iota(I32, (1, LANE), 1)
    lane_f = lane.astype(F32)
    big = float(LANE)
    is_g = lane < N_GROUPS
    lg = jnp.where(is_g, logits, -jnp.inf)
    gmax = jnp.max(lg, axis=-1, keepdims=True)
    g_sel = jnp.min(jnp.where(lg == gmax, lane_f, big), axis=-1, keepdims=True)
    p_gsel = 1.0 / jnp.sum(jnp.where(is_g, jnp.exp(logits - gmax), 0.0), axis=-1, keepdims=True)
    e_grp = ((lane - N_GROUPS) // EXPERTS_PER_GROUP).astype(F32)
    in_sel = (lane >= N_GROUPS) & (lane < N_GROUPS + N_EXPERTS) & (e_grp == g_sel)
    le = jnp.where(in_sel, logits, -jnp.inf)
    m1 = jnp.max(le, axis=-1, keepdims=True)
    i1 = jnp.min(jnp.where(le == m1, lane_f, big), axis=-1, keepdims=True)
    le2 = jnp.where(lane_f == i1, -jnp.inf, le)
    m2 = jnp.max(le2, axis=-1, keepdims=True)
    i2 = jnp.min(jnp.where(le2 == m2, lane_f, big), axis=-1, keepdims=True)
    t = jnp.exp(m2 - m1)
    c1 = p_gsel / (1.0 + t)
    c2 = p_gsel * t / (1.0 + t)
    a = jnp.minimum(i1, i2) - N_GROUPS - EXPERTS_PER_GROUP * g_sel
    b = jnp.maximum(i1, i2) - N_GROUPS - EXPERTS_PER_GROUP * g_sel
    pair = a * (2 * EXPERTS_PER_GROUP - 1 - a) * 0.5 + (b - a - 1.0)
    cls = g_sel * PAIRS_PER_GROUP + pair
    ei_ref[...] = jnp.broadcast_to(cls, ei_ref.shape).astype(I32)
    first_is_lo = i1 < i2
    cw_ref[...] = jnp.where(lane == 0, jnp.where(first_is_lo, c1, c2), jnp.where(first_is_lo, c2, c1))


def _router(h, g, w_rg, b_rg, w_re, b_re, tm=256):
    n, d = h.shape
    pad = LANE - N_GROUPS - N_EXPERTS
    wr = jnp.concatenate([w_rg, w_re, jnp.zeros((d, pad), F32)], axis=1)
    br = jnp.concatenate([b_rg, b_re, jnp.zeros((pad,), F32)]).reshape(1, LANE)
    return pl.pallas_call(
        _router_kernel,
        grid=(n // tm,),
        in_specs=[pl.BlockSpec((tm, d), lambda i: (i, 0)),
                  pl.BlockSpec((1, d), lambda i: (0, 0)),
                  pl.BlockSpec((d, LANE), lambda i: (0, 0)),
                  pl.BlockSpec((1, LANE), lambda i: (0, 0))],
        out_specs=[pl.BlockSpec((tm, LANE), lambda i: (i, 0)),
                   pl.BlockSpec((tm, LANE), lambda i: (i, 0))],
        out_shape=[jax.ShapeDtypeStruct((n, LANE), I32),
                   jax.ShapeDtypeStruct((n, LANE), F32)],
        compiler_params=_cparams(("parallel",)),
    )(h, g.reshape(1, d), wr, br)


N_CLASSES = N_GROUPS * PAIRS_PER_GROUP
_PAIR_AB = [(a, b) for a in range(EXPERTS_PER_GROUP) for b in range(a + 1, EXPERTS_PER_GROUP)]
CLASS_EXPERT_LO = [g * EXPERTS_PER_GROUP + a for g in range(N_GROUPS) for a, _ in _PAIR_AB]
CLASS_EXPERT_HI = [g * EXPERTS_PER_GROUP + b for g in range(N_GROUPS) for _, b in _PAIR_AB]
TB_MOE = 256


def _rank_kernel(cls_ref, rank_ref, cnt_ref, carry_ref):
    tb = cls_ref.shape[0]

    @pl.when(pl.program_id(0) == 0)
    def _():
        carry_ref[...] = jnp.zeros(carry_ref.shape, F32)

    onehot = cls_ref[...] == lax.broadcasted_iota(I32, (tb, LANE), 1)
    tril = lax.broadcasted_iota(I32, (tb, tb), 0) >= lax.broadcasted_iota(I32, (tb, tb), 1)
    pref = jnp.dot(jnp.where(tril, 1.0, 0.0).astype(BF16), jnp.where(onehot, 1.0, 0.0).astype(BF16),
                   preferred_element_type=F32)
    carry = carry_ref[...]
    rank = jnp.sum(jnp.where(onehot, pref + carry, 0.0), axis=-1, keepdims=True) - 1.0
    rank_ref[...] = jnp.broadcast_to(rank, rank_ref.shape).astype(I32)
    carry = carry + pref[tb - 1:tb, :]
    carry_ref[...] = carry
    cnt_ref[...] = carry


def _class_rank(cls, tb=512):
    n = cls.shape[0]
    return pl.pallas_call(
        _rank_kernel,
        grid=(n // tb,),
        in_specs=[pl.BlockSpec((tb, LANE), lambda i: (i, 0))],
        out_specs=[pl.BlockSpec((tb, LANE), lambda i: (i, 0)), pl.BlockSpec((1, LANE), lambda i: (0, 0))],
        out_shape=[jax.ShapeDtypeStruct((n, LANE), I32), jax.ShapeDtypeStruct((1, LANE), F32)],
        scratch_shapes=[pltpu.VMEM((1, LANE), F32)],
        compiler_params=_cparams(("arbitrary",)),
    )(cls)


def _moe_schedule(counts, n_tok, tm):
    counts = counts[:N_CLASSES].astype(I32)
    tiles_per = (counts + tm - 1) // tm
    tile_end = jnp.cumsum(tiles_per)
    tile_start = tile_end - tiles_per
    start_row = jnp.zeros((LANE,), I32).at[:N_CLASSES].set(tile_start * tm)
    n_tiles = n_tok // tm + N_CLASSES
    t = jnp.arange(n_tiles, dtype=I32)
    tc = jnp.searchsorted(tile_end, t, side="right").astype(I32)
    live = tc < N_CLASSES
    tc = jnp.minimum(tc, N_CLASSES - 1)
    nv = jnp.where(live, jnp.clip(counts[tc] - (t - tile_start[tc]) * tm, 0, tm), 0).astype(I32)
    e_lo = jnp.asarray(CLASS_EXPERT_LO, I32)[tc]
    e_hi = jnp.asarray(CLASS_EXPERT_HI, I32)[tc]
    return start_row, e_lo, e_hi, nv


def _tok(i, width=ROW_TILE):
    return pl.ds(i * width, width)


def _slot_of(start_ref, cls_ref, rank_ref, r):
    return start_ref[cls_ref[0, r]] + rank_ref[0, r]


def _dispatch_kernel(start_ref, cls_ref, rank_ref, h_ref, g_ref, xs_init, xs_hbm, sbuf, sem, *, tb, n_blk):
    del xs_init
    i = pl.program_id(0)
    slot = i % 2

    def wait(sl):
        pltpu.make_async_copy(sbuf.at[sl], xs_hbm.at[pl.ds(0, tb * ROW_TILE)], sem.at[sl]).wait()

    _rows_to_tiles(sbuf.at[slot], _rms(h_ref[...], g_ref[...]))
    for r in range(tb):
        pltpu.make_async_copy(sbuf.at[slot, _tok(r)], xs_hbm.at[_tok(_slot_of(start_ref, cls_ref, rank_ref, r))],
                              sem.at[slot]).start()

    @pl.when(i > 0)
    def _():
        wait(1 - slot)

    @pl.when(i == n_blk - 1)
    def _():
        wait(slot)


def _dispatch(h, g, start_row, cls_sm, rank_sm, n_slots, tb=TB_MOE):
    n, d = h.shape
    n_blk = cls_sm.shape[0]
    xs0 = jnp.zeros((n_slots * ROW_TILE, LANE), F32)
    idx_spec = pl.BlockSpec((None, 1, tb), lambda i, st: (i, 0, 0), memory_space=pltpu.SMEM)
    return pl.pallas_call(
        functools.partial(_dispatch_kernel, tb=tb, n_blk=n_blk),
        grid_spec=pltpu.PrefetchScalarGridSpec(
            num_scalar_prefetch=1,
            grid=(n_blk,),
            in_specs=[idx_spec, idx_spec,
                      pl.BlockSpec((tb, d), lambda i, st: (i, 0)),
                      pl.BlockSpec((1, d), lambda i, st: (0, 0)),
                      pl.BlockSpec(memory_space=pl.ANY)],
            out_specs=pl.BlockSpec(memory_space=pl.ANY),
            scratch_shapes=[pltpu.VMEM((2, tb * ROW_TILE, LANE), F32), pltpu.SemaphoreType.DMA((2,))],
        ),
        out_shape=jax.ShapeDtypeStruct(xs0.shape, F32),
        input_output_aliases={5: 0},
        compiler_params=_cparams(("arbitrary",)),
    )(start_row, cls_sm, rank_sm, h, g.reshape(1, d), xs0)


def _expert_kernel(elo_ref, ehi_ref, nv_ref, xs_ref, wga, wua, wda, wgb, wub, wdb, ys_ref, *, tm):
    t = pl.program_id(0)

    @pl.when(nv_ref[t] > 0)
    def _():
        x = _tiles_to_rows(xs_ref, tm).astype(BF16)
        g = [jnp.dot(x, w[...], preferred_element_type=F32) for w in (wga, wgb)]
        u = [jnp.dot(x, w[...], preferred_element_type=F32) for w in (wua, wub)]
        act = [((g[k] * jax.nn.sigmoid(g[k])) * u[k]).astype(BF16) for k in range(2)]
        y = [jnp.dot(act[k], w[...], preferred_element_type=F32) for k, w in enumerate((wda, wdb))]
        for half in range(2):
            for c in range(ROW_TILE):
                ys_ref[pl.ds(half * ROW_TILE + c, tm, stride=2 * ROW_TILE), :] = y[half][:, c * LANE:(c + 1) * LANE]

    @pl.when(nv_ref[t] == 0)
    def _():
        ys_ref[...] = jnp.zeros(ys_ref.shape, F32)


def _moe_experts(xs, e_lo, e_hi, nv, layer, wg, wu, wd, tm=TM_MOE):
    d = wg.shape[2]
    assert d == ROW_TILE * LANE
    n_tiles = nv.shape[0]

    def w_spec(rows, cols, which):
        return pl.BlockSpec((None, None, rows, cols), lambda t, lo, hi, nv_: (layer, (lo, hi)[which][t], 0, 0))

    w_specs = [w_spec(d, D_EXPERT, k) for k in (0, 0)] + [w_spec(D_EXPERT, d, 0)]
    w_specs += [w_spec(d, D_EXPERT, k) for k in (1, 1)] + [w_spec(D_EXPERT, d, 1)]
    return pl.pallas_call(
        functools.partial(_expert_kernel, tm=tm),
        grid_spec=pltpu.PrefetchScalarGridSpec(
            num_scalar_prefetch=3,
            grid=(n_tiles,),
            in_specs=[pl.BlockSpec((tm * ROW_TILE, LANE), lambda t, lo, hi, nv_: (t, 0))] + w_specs,
            out_specs=pl.BlockSpec((tm * 2 * ROW_TILE, LANE), lambda t, lo, hi, nv_: (t, 0)),
        ),
        out_shape=jax.ShapeDtypeStruct((n_tiles * tm * 2 * ROW_TILE, LANE), F32),
        compiler_params=_cparams(("arbitrary",)),
    )(e_lo, e_hi, nv, xs, wg, wu, wd, wg, wu, wd)


def _combine_kernel(start_ref, cls_cur, rank_cur, cls_next, rank_next, ys_hbm, h_ref, cw_ref, g_ref, o_ref,
                    ybuf, sem, *, tb, n_blk, final):
    i = pl.program_id(0)
    slot = i % 2
    other = 1 - slot
    pair = 2 * ROW_TILE

    def fetch(cls_ref, rank_ref, r, sl):
        return pltpu.make_async_copy(ys_hbm.at[_tok(_slot_of(start_ref, cls_ref, rank_ref, r), pair)],
                                     ybuf.at[sl, _tok(r, pair)], sem.at[sl])

    def wait(sl):
        pltpu.make_async_copy(ys_hbm.at[pl.ds(0, tb * pair)], ybuf.at[sl], sem.at[sl]).wait()

    @pl.when(i == 0)
    def _():
        def body(r, c):
            fetch(cls_cur, rank_cur, r, 0).start()
            return c
        lax.fori_loop(0, tb, body, 0)

    wait(slot)
    for r in range(tb):
        fetch(cls_next, rank_next, r, other).start()
    yb = ybuf.at[slot]
    y_lo = jnp.concatenate([yb[pl.ds(c, tb, stride=pair), :] for c in range(ROW_TILE)], axis=1)
    y_hi = jnp.concatenate([yb[pl.ds(ROW_TILE + c, tb, stride=pair), :] for c in range(ROW_TILE)], axis=1)
    cw = cw_ref[...]
    out = h_ref[...] + cw[:, 0:1] * y_lo + cw[:, 1:2] * y_hi
    if final:
        out = _rms(out, g_ref[...])
    o_ref[...] = out

    @pl.when(i == n_blk - 1)
    def _():
        wait(other)


def _combine(h, ys, cw, g, start_row, cls_sm, rank_sm, final, tb=TB_MOE):
    n, d = h.shape
    n_blk = n // tb
    cls_x = jnp.concatenate([cls_sm, cls_sm[-1:]], axis=0)
    rank_x = jnp.concatenate([rank_sm, rank_sm[-1:]], axis=0)

    def idx_spec(shift):
        return pl.BlockSpec((None, 1, tb), lambda i, st: (i + shift, 0, 0), memory_space=pltpu.SMEM)

    return pl.pallas_call(
        functools.partial(_combine_kernel, tb=tb, n_blk=n_blk, final=final),
        grid_spec=pltpu.PrefetchScalarGridSpec(
            num_scalar_prefetch=1,
            grid=(n_blk,),
            in_specs=[idx_spec(0), idx_spec(0), idx_spec(1), idx_spec(1),
                      pl.BlockSpec(memory_space=pl.ANY),
                      pl.BlockSpec((tb, d), lambda i, st: (i, 0)),
                      pl.BlockSpec((tb, LANE), lambda i, st: (i, 0)),
                      pl.BlockSpec((1, d), lambda i, st: (0, 0))],
            out_specs=pl.BlockSpec((tb, d), lambda i, st: (i, 0)),
            scratch_shapes=[pltpu.VMEM((2, tb * 2 * ROW_TILE, LANE), F32), pltpu.SemaphoreType.DMA((2,))],
        ),
        out_shape=jax.ShapeDtypeStruct((n, d), F32),
        compiler_params=_cparams(("arbitrary",)),
    )(start_row, cls_x, rank_x, cls_x, rank_x, ys, h, cw, g.reshape(1, d))


def _moe_layer(h, layer, ln_g, w_rg, b_rg, w_re, b_re, w_gate, w_up, w_down, final_g, final):
    n = h.shape[0]
    cls, cw = _router(h, ln_g, w_rg, b_rg, w_re, b_re)
    rank, counts = _class_rank(cls)
    start_row, e_lo, e_hi, nv = _moe_schedule(counts[0], n, TM_MOE)
    cls_sm = cls[:, 0].reshape(n // TB_MOE, 1, TB_MOE)
    rank_sm = rank[:, 0].reshape(n // TB_MOE, 1, TB_MOE)
    xs = _dispatch(h, ln_g, start_row, cls_sm, rank_sm, nv.shape[0] * TM_MOE)
    ys = _moe_experts(xs, e_lo, e_hi, nv, layer, w_gate, w_up, w_down)
    return _combine(h, ys, cw, final_g, start_row, cls_sm, rank_sm, final)


def _ab_weights(w_in):
    scale = HEAD_DIM ** -0.5
    widths = (512, 512, 512, 512, 64, 64, 256, 64, 4)
    offs = [sum(widths[:j]) for j in range(len(widths) + 1)]
    q_a, k_a, v_a, q_b, k_b, v_b, q_i, k_i, w_i = [w_in[:, offs[j]:offs[j + 1]] for j in range(len(widths))]
    pad = jnp.zeros((w_in.shape[0], LANE - IDX_HEADS), w_in.dtype)
    main = jnp.concatenate([q_a * scale, k_a, v_a, q_b * scale, k_b, k_b, v_b, v_b, q_i, k_i, k_i], axis=1)
    return jnp.concatenate([main, w_i, pad], axis=1).astype(BF16)


def kernel(x, t5_bias, ln_mix, ln_ffn, ln_final, ab_w_in, ab_w_out, diff_lambda, diff_subln,
           c_w_in, c_w_out, c_rel_bias, moe_w_rg, moe_b_rg, moe_w_re, moe_b_re,
           moe_w_gate, moe_w_up, moe_w_down):
    b, s, d = x.shape
    n = b * s
    depth = ln_mix.shape[0]
    h = x.reshape(n, d)
    near, far = _t5_tables(t5_bias)
    w_gate, w_up, w_down = (w.astype(BF16) for w in (moe_w_gate, moe_w_up, moe_w_down))
    for l in range(depth):
        if l % 2 == 0:
            e = l // 2
            lam_init = 0.8 - 0.6 * math.exp(-0.3 * l)
            proj, wi = _norm_proj(h, ln_mix[l], _ab_weights(ab_w_in[e]), AB_COLS)
            proj = proj.reshape(b, s, AB_COLS)
            y_a = _attn_a(proj, near[:A_HEADS], far, diff_lambda[e], diff_subln[e], lam_init)
            y_b = _attn_b(proj, wi.reshape(b, s, LANE), near[A_HEADS:], far)
            h = _out_proj(h, y_a.reshape(n, -1), 0, y_b.reshape(n, -1), 0, ab_w_out[e].astype(BF16))
        else:
            o = l // 2
            scale = HEAD_DIM ** -0.5
            w_in = jnp.concatenate([c_w_in[o][:, :d] * scale, c_w_in[o][:, d:]], axis=1).astype(BF16)
            (proj,) = _norm_proj(h, ln_mix[l], w_in, 3 * d)
            y = _attn_c(proj.reshape(b, s, 3 * d), _c_tables(c_rel_bias[o])).reshape(n, d)
            h = _out_proj(h, y, 0, y, 1, c_w_out[o].astype(BF16))
        h = _moe_layer(h, l, ln_ffn[l], moe_w_rg[l], moe_b_rg[l], moe_w_re[l], moe_b_re[l],
                       w_gate, w_up, w_down, ln_final, l == depth - 1)
    return h.reshape(b, s, d)
```

```python
import functools
import math

import jax
import jax.numpy as jnp
from jax import lax
from jax.experimental import pallas as pl
from jax.experimental.pallas import tpu as pltpu

F32 = jnp.float32
BF16 = jnp.bfloat16
I32 = jnp.int32

D_MODEL = 1024
CHUNK = 64
HEAD_DIM = 64
RMS_EPS = 1e-6
A_HEADS = 4
A_V_DIM = 2 * HEAD_DIM
B_HEADS = 8
IDX_HEADS = 4
TOPK_MAX = 256
C_HEADS = D_MODEL // HEAD_DIM
C_LEFT_CHUNKS = 8
C_REL_CLIP = 128
T5_BUCKETS = 32
T5_MAX_DIST = 1024
N_GROUPS = 4
EXPERTS_PER_GROUP = 8
N_EXPERTS = N_GROUPS * EXPERTS_PER_GROUP
PAIRS_PER_GROUP = EXPERTS_PER_GROUP * (EXPERTS_PER_GROUP - 1) // 2
D_EXPERT = D_MODEL // 2
NEG = -1e30

LANE = 128
VMEM_LIMIT = 56 * 1024 * 1024

TQ = 256
NEAR_TILES = 4
TQ_C = 2 * CHUNK
WIN_C = (C_LEFT_CHUNKS + 2) * CHUNK
TM_MOE = 256

COL_QA, COL_KA, COL_VA, COL_QB, COL_KB, COL_VB, COL_QI, COL_KI, COL_WI = (
    0, 512, 1024, 1536, 2048, 2176, 2304, 2560, 2688)
AB_COLS = 2688


def _cparams(sem, vmem=VMEM_LIMIT):
    return pltpu.CompilerParams(dimension_semantics=sem, vmem_limit_bytes=vmem)


def _dot_nt(a, b):
    return lax.dot_general(a, b, (((1,), (1,)), ((), ())), preferred_element_type=F32)


def _rms(x, g):
    return (x * lax.rsqrt(jnp.mean(x * x, axis=-1, keepdims=True) + RMS_EPS)) * g


def _norm_proj_kernel(x_ref, g_ref, w_ref, o_ref, *tail_ref, n_main):
    xn = _rms(x_ref[...], g_ref[...]).astype(BF16)
    acc = jnp.dot(xn, w_ref[...], preferred_element_type=F32)
    o_ref[...] = acc[:, :n_main].astype(o_ref.dtype)
    if tail_ref:
        tail_ref[0][...] = acc[:, n_main:]


def _norm_proj(x, g, w, n_main, tm=256):
    n, d = x.shape
    n_all = w.shape[1]
    out_shape = [jax.ShapeDtypeStruct((n, n_main), BF16)]
    out_specs = [pl.BlockSpec((tm, n_main), lambda i: (i, 0))]
    if n_all > n_main:
        out_shape.append(jax.ShapeDtypeStruct((n, n_all - n_main), F32))
        out_specs.append(pl.BlockSpec((tm, n_all - n_main), lambda i: (i, 0)))
    return pl.pallas_call(
        functools.partial(_norm_proj_kernel, n_main=n_main),
        grid=(n // tm,),
        in_specs=[pl.BlockSpec((tm, d), lambda i: (i, 0)),
                  pl.BlockSpec((1, d), lambda i: (0, 0)),
                  pl.BlockSpec((d, n_all), lambda i: (0, 0))],
        out_specs=out_specs, out_shape=out_shape,
        compiler_params=_cparams(("parallel",)),
    )(x, g.reshape(1, d), w)


def _out_proj_kernel(h_ref, y0_ref, y1_ref, w0_ref, w1_ref, o_ref):
    acc = jnp.dot(y0_ref[...], w0_ref[...], preferred_element_type=F32)
    acc = acc + jnp.dot(y1_ref[...], w1_ref[...], preferred_element_type=F32)
    o_ref[...] = h_ref[...] + acc


def _out_proj(h, y0, c0, y1, c1, w, tm=512):
    n, d = h.shape
    half = d // 2
    return pl.pallas_call(
        _out_proj_kernel,
        grid=(n // tm,),
        in_specs=[pl.BlockSpec((tm, d), lambda i: (i, 0)),
                  pl.BlockSpec((tm, half), lambda i: (i, c0)),
                  pl.BlockSpec((tm, half), lambda i: (i, c1)),
                  pl.BlockSpec((half, d), lambda i: (0, 0)),
                  pl.BlockSpec((half, d), lambda i: (1, 0))],
        out_specs=pl.BlockSpec((tm, d), lambda i: (i, 0)),
        out_shape=jax.ShapeDtypeStruct((n, d), F32),
        compiler_params=_cparams(("parallel",)),
    )(h, y0, y1, w, w)


def _t5_bucket(rel):
    half = T5_BUCKETS // 2
    max_exact = half // 2
    ret = jnp.where(rel > 0, half, 0)
    n = jnp.abs(rel)
    nf = jnp.maximum(n, 1).astype(F32)
    large = max_exact + (jnp.log(nf / max_exact) / math.log(T5_MAX_DIST / max_exact)
                         * (half - max_exact)).astype(I32)
    large = jnp.minimum(large, half - 1)
    return (ret + jnp.where(n < max_exact, n, large)).astype(I32)


def _toeplitz(g, rows, width):
    length = g.shape[-1]
    assert width < length and rows <= length
    flat = jnp.tile(g, (1,) * (g.ndim - 1) + (rows,))[..., :rows * (length - 1)]
    return flat.reshape(g.shape[:-1] + (rows, length - 1))[..., :width]


def _t5_tables(t5_bias):
    span = NEAR_TILES * TQ
    length = span + TQ
    k = jnp.arange(length, dtype=I32)
    delta = jnp.where(k < span, k, k - length) - (span - TQ)
    g = t5_bias[_t5_bucket(delta)].T
    bias = _toeplitz(g, TQ, span)
    r = jnp.arange(TQ, dtype=I32)[:, None]
    kr = jnp.arange(span, dtype=I32)[None, :] - (span - TQ)
    ok = jnp.floor_divide(kr, CHUNK) <= (r // CHUNK)
    bias = jnp.where(ok[None], bias, NEG)
    near = bias.reshape(-1, TQ, NEAR_TILES, TQ).transpose(0, 2, 1, 3)
    far = t5_bias[_t5_bucket(jnp.asarray(-(span - TQ) - 1, I32))]
    return near, far


N_GRP = 4
STACK = 2 * TQ


def _attend(i, qs_ref, k_fn, v_fn, mask_fn, near_fn, far_cols, mf_ref, mn_ref, l_ref, acc_ref, s_ref=None):
    n_far = jnp.maximum(i - (NEAR_TILES - 1), 0)
    n_far_slots = 0 if s_ref is None else s_ref.shape[0] - NEAR_TILES
    n_far_cached = jnp.minimum(n_far, n_far_slots)

    mf_ref[...] = jnp.full(mf_ref.shape, NEG, F32)
    mn_ref[...] = jnp.full(mn_ref.shape, NEG, F32)

    def rows(g):
        return slice(g * STACK, (g + 1) * STACK)

    def logits(g, kt, jt):
        off = pl.multiple_of(kt * TQ, TQ)
        s = _dot_nt(qs_ref[rows(g), :], k_fn(g, off))
        extra = None if mask_fn is None else mask_fn(kt)[None]
        if jt is not None:
            extra = near_fn(g, jt) if extra is None else extra + near_fn(g, jt)
        if extra is not None:
            s = (s.reshape(2, TQ, TQ) + extra).reshape(STACK, TQ)
        return s

    def loops(make_fn):
        def run(lo, hi, fn):
            def body(kt, c):
                fn(kt)
                return c
            lax.fori_loop(lo, hi, body, 0)

        if n_far_slots > 0:
            run(0, n_far_cached, make_fn(False, True))
        run(n_far_cached, n_far, make_fn(False, False))
        run(n_far, i + 1, make_fn(True, s_ref is not None))

    def slot(kt, jt):
        return NEAR_TILES + kt if jt is None else jt

    def pass1(near, cached):
        m_ref = mn_ref if near else mf_ref

        def fn(kt):
            jt = kt - (i - (NEAR_TILES - 1)) if near else None
            s = [logits(g, kt, jt) for g in range(N_GRP)]
            for g in range(N_GRP):
                if cached:
                    s_ref[slot(kt, jt), rows(g), :] = s[g]
                m_ref[rows(g), :] = jnp.maximum(m_ref[rows(g), :], jnp.maximum(s[g][:, :LANE], s[g][:, LANE:]))
        return fn

    loops(pass1)

    for g in range(N_GRP):
        m = jnp.maximum(jnp.max(mf_ref[rows(g), :], axis=-1, keepdims=True) + far_cols[g],
                        jnp.max(mn_ref[rows(g), :], axis=-1, keepdims=True))
        m_near = jnp.broadcast_to(m, (STACK, LANE))
        mn_ref[rows(g), :] = m_near
        mf_ref[rows(g), :] = m_near - far_cols[g]
    l_ref[...] = jnp.zeros(l_ref.shape, F32)
    acc_ref[...] = jnp.zeros(acc_ref.shape, F32)

    def pass2(near, cached):
        shift_ref = mn_ref if near else mf_ref

        def fn(kt):
            jt = kt - (i - (NEAR_TILES - 1)) if near else None
            off = pl.multiple_of(kt * TQ, TQ)
            grp = range(N_GRP)
            s = [s_ref[slot(kt, jt), rows(g), :] if cached else logits(g, kt, jt) for g in grp]
            p0 = [jnp.exp(s[g][:, :LANE] - shift_ref[rows(g), :]) for g in grp]
            p1 = [jnp.exp(s[g][:, LANE:] - shift_ref[rows(g), :]) for g in grp]
            for g in grp:
                l_ref[rows(g), :] += p0[g] + p1[g]
            pv = [jnp.dot(jnp.concatenate([p0[g], p1[g]], axis=1).astype(BF16), v_fn(g, off),
                          preferred_element_type=F32) for g in grp]
            for g in grp:
                acc_ref[rows(g), :] += pv[g]
        return fn

    loops(pass2)
    return [acc_ref[rows(g), :] / jnp.sum(l_ref[rows(g), :], axis=-1, keepdims=True) for g in range(N_GRP)]


def _attend_scratch():
    return [pltpu.VMEM((N_GRP * STACK, LANE), BF16)] + [pltpu.VMEM((N_GRP * STACK, LANE), F32)] * 4


def _split_lanes(x):
    lo = lax.broadcasted_iota(I32, (1, LANE), 1) < HEAD_DIM
    z = jnp.zeros_like(x)
    return jnp.concatenate([jnp.where(lo, x, z), jnp.where(lo, z, x)], axis=0)


def _attn_a_kernel(far_ref, q_ref, k_ref, v_ref, tab_ref, lam_ref, sub_ref, o_ref,
                   qs_ref, mf_ref, mn_ref, l_ref, acc_ref, s_ref, *, lam_init):
    i = pl.program_id(1)
    for g in range(A_HEADS):
        qs_ref[g * STACK:(g + 1) * STACK, :] = _split_lanes(q_ref[:, g * LANE:(g + 1) * LANE])
    outs = _attend(
        i, qs_ref,
        lambda g, off: k_ref[pl.ds(off, TQ), g * LANE:(g + 1) * LANE],
        lambda g, off: v_ref[pl.ds(off, TQ), g * LANE:(g + 1) * LANE],
        None,
        lambda g, jt: tab_ref[g, jt][None],
        [jnp.full((STACK, 1), far_ref[g], F32) for g in range(A_HEADS)],
        mf_ref, mn_ref, l_ref, acc_ref, s_ref)
    lam = lam_ref[...]
    lam_full = (jnp.exp(jnp.sum(lam[0:1] * lam[1:2], axis=-1, keepdims=True))
                - jnp.exp(jnp.sum(lam[2:3] * lam[3:4], axis=-1, keepdims=True)) + lam_init)
    for g, o in enumerate(outs):
        d = o[:TQ] - lam_full * o[TQ:]
        o_ref[:, g * LANE:(g + 1) * LANE] = (_rms(d, sub_ref[...]) * (1.0 - lam_init)).astype(o_ref.dtype)


def _attn_a(proj, near, far, lam, subln, lam_init):
    b, s, _ = proj.shape
    assert A_HEADS == N_GRP
    wide = A_HEADS * A_V_DIM
    return pl.pallas_call(
        functools.partial(_attn_a_kernel, lam_init=lam_init),
        grid_spec=pltpu.PrefetchScalarGridSpec(
            num_scalar_prefetch=1,
            grid=(b, s // TQ),
            in_specs=[
                pl.BlockSpec((None, TQ, wide), lambda b_, i, f: (b_, i, COL_QA // wide)),
                pl.BlockSpec((None, s, wide), lambda b_, i, f: (b_, 0, COL_KA // wide), pipeline_mode=pl.Buffered(1)),
                pl.BlockSpec((None, s, wide), lambda b_, i, f: (b_, 0, COL_VA // wide), pipeline_mode=pl.Buffered(1)),
                pl.BlockSpec((A_HEADS, NEAR_TILES, TQ, TQ), lambda b_, i, f: (0, 0, 0, 0),
                             pipeline_mode=pl.Buffered(1)),
                pl.BlockSpec((4, HEAD_DIM), lambda b_, i, f: (0, 0)),
                pl.BlockSpec((1, A_V_DIM), lambda b_, i, f: (0, 0)),
            ],
            out_specs=pl.BlockSpec((None, TQ, wide), lambda b_, i, f: (b_, i, 0)),
            scratch_shapes=_attend_scratch() + [pltpu.VMEM((s // TQ, N_GRP * STACK, TQ), F32)],
        ),
        out_shape=jax.ShapeDtypeStruct((b, s, wide), BF16),
        compiler_params=_cparams(("parallel", "arbitrary")),
    )(far, proj, proj, proj, near, lam, subln.reshape(1, A_V_DIM))


def _attn_b_kernel(far_ref, qb_ref, kb_ref, vb_ref, qi_ref, ki_ref, wi_ref, tab_ref, o_ref,
                   keyst_ref, selm_ref, jm_ref, qs_ref, mf_ref, mn_ref, l_ref, acc_ref, s_ref,
                   *, top_k, idx_bits):
    i = pl.program_id(1)
    n_kt = i + 1
    lane = lax.broadcasted_iota(I32, (1, LANE), 1)
    lo = lane < HEAD_DIM
    int_min = jnp.int32(-2 ** 31)
    split_heads = _split_lanes

    key_row = lax.broadcasted_iota(I32, (TQ, TQ), 0)
    chunk_gap = (key_row // CHUNK) - (lax.broadcasted_iota(I32, (TQ, TQ), 1) // CHUNK)

    def causal(kt):
        return chunk_gap <= jnp.where(kt < i, jnp.int32(TQ), jnp.int32(0))

    qi = qi_ref[...]
    qis = jnp.concatenate([split_heads(qi[:, :LANE]), split_heads(qi[:, LANE:])], axis=0)
    wi = wi_ref[...] * ((IDX_HEADS ** -0.5) * (HEAD_DIM ** -0.5))

    def score_body(kt, c):
        off = pl.multiple_of(kt * TQ, TQ)
        r = jnp.maximum(_dot_nt(ki_ref[pl.ds(off, TQ), :], qis), 0.0)
        isc = r[:, 0:TQ] * wi[0:1, :]
        for hh in range(1, IDX_HEADS):
            isc = isc + r[:, hh * TQ:(hh + 1) * TQ] * wi[hh:hh + 1, :]
        isc = jnp.where(isc == 0.0, 0.0, isc)
        isc = jnp.where(causal(kt), isc, -jnp.inf)
        bits = lax.bitcast_convert_type(isc, I32)
        keyst_ref[kt] = bits ^ ((bits >> 31) & jnp.int32(0x7FFFFFFF))
        return c

    lax.fori_loop(0, n_kt, score_body, 0)


    def count(pred_fn):
        def body(kt, acc):
            c = jnp.where(pred_fn(kt, keyst_ref[kt]), 1.0, 0.0)
            return acc + jnp.sum(c.reshape(TQ // 32, 32, TQ), axis=0)
        acc = lax.fori_loop(0, n_kt, body, jnp.zeros((32, TQ), F32))
        return jnp.sum(acc, axis=0, keepdims=True)

    def thr_body(bi, state):
        tu, nge = state
        cand_u = tu | (jnp.int32(1) << (31 - bi))
        cand_s = cand_u ^ int_min
        cnt = count(lambda kt, kk: kk >= cand_s)
        keep = cnt >= top_k
        return jnp.where(keep, cand_u, tu), jnp.where(keep, cnt, nge)

    tu, nge = lax.fori_loop(0, 32, thr_body, (jnp.zeros((1, TQ), I32),
                                              jnp.full((1, TQ), (n_kt * TQ).astype(F32), F32)))
    thr = tu ^ int_min

    jm_ref[...] = jnp.full(jm_ref.shape, 2 ** idx_bits - 1, I32)

    @pl.when(jnp.max(nge) > top_k)
    def _():
        cnt_hi = top_k - count(lambda kt, kk: kk > thr)

        def tie_body(bi, jmax):
            trial = jmax | (jnp.int32(1) << (idx_bits - 1 - bi))
            cnt = count(lambda kt, kk: (kk == thr) & ((key_row + kt * TQ) <= trial))
            return jnp.where(cnt <= cnt_hi, trial, jmax)

        jm_ref[...] = jnp.broadcast_to(lax.fori_loop(0, idx_bits, tie_body, jnp.zeros((1, TQ), I32)), jm_ref.shape)

    jmax = jm_ref[0:1, :]

    def mask_body(kt, c):
        kk = keyst_ref[kt]
        sel = ((kk > thr) | ((kk == thr) & ((key_row + kt * TQ) <= jmax))) & causal(kt)
        selm_ref[kt] = jnp.where(sel, 0.0, NEG).T
        return c

    lax.fori_loop(0, n_kt, mask_body, 0)

    for g in range(N_GRP):
        qs_ref[g * STACK:(g + 1) * STACK, :] = split_heads(qb_ref[:, g * LANE:(g + 1) * LANE])
    top_rows = lax.broadcasted_iota(I32, (STACK, 1), 0) < TQ
    outs = _attend(
        i, qs_ref,
        lambda g, off: kb_ref[pl.ds(off, TQ), :],
        lambda g, off: vb_ref[pl.ds(off, TQ), :],
        lambda kt: selm_ref[kt],
        lambda g, jt: tab_ref[2 * g:2 * g + 2, jt],
        [jnp.where(top_rows, far_ref[A_HEADS + 2 * g], far_ref[A_HEADS + 2 * g + 1]) for g in range(N_GRP)],
        mf_ref, mn_ref, l_ref, acc_ref, s_ref)
    for g, o in enumerate(outs):
        o_ref[:, g * LANE:(g + 1) * LANE] = jnp.where(lo, o[:TQ], o[TQ:]).astype(o_ref.dtype)


B_CACHE_TILES = NEAR_TILES + 8


def _attn_b(proj, wi, near_b, far):
    b, s, _ = proj.shape
    top_k = min(TOPK_MAX, s // 4)
    assert top_k <= TQ and s % TQ == 0 and (s & (s - 1)) == 0
    n_kt = s // TQ
    assert B_HEADS == 2 * N_GRP and s < 2 ** 16
    qb_w = B_HEADS * HEAD_DIM
    qi_w = IDX_HEADS * HEAD_DIM
    return pl.pallas_call(
        functools.partial(_attn_b_kernel, top_k=float(top_k), idx_bits=int(math.log2(s))),
        grid_spec=pltpu.PrefetchScalarGridSpec(
            num_scalar_prefetch=1,
            grid=(b, n_kt),
            in_specs=[
                pl.BlockSpec((None, TQ, qb_w), lambda b_, i, f: (b_, i, COL_QB // qb_w)),
                pl.BlockSpec((None, s, LANE), lambda b_, i, f: (b_, 0, COL_KB // LANE)),
                pl.BlockSpec((None, s, LANE), lambda b_, i, f: (b_, 0, COL_VB // LANE)),
                pl.BlockSpec((None, TQ, qi_w), lambda b_, i, f: (b_, i, COL_QI // qi_w)),
                pl.BlockSpec((None, s, LANE), lambda b_, i, f: (b_, 0, COL_KI // LANE)),
                pl.BlockSpec((None, 8, TQ), lambda b_, i, f: (b_, 0, i)),
                pl.BlockSpec((B_HEADS, NEAR_TILES, TQ, TQ), lambda b_, i, f: (0, 0, 0, 0),
                             pipeline_mode=pl.Buffered(1)),
            ],
            out_specs=pl.BlockSpec((None, TQ, qb_w), lambda b_, i, f: (b_, i, 0)),
            scratch_shapes=[pltpu.VMEM((n_kt, TQ, TQ), I32), pltpu.VMEM((n_kt, TQ, TQ), F32),
                            pltpu.VMEM((8, TQ), I32)] + _attend_scratch()
            + [pltpu.VMEM((min(n_kt, B_CACHE_TILES), N_GRP * STACK, TQ), F32)],
        ),
        out_shape=jax.ShapeDtypeStruct((b, s, qb_w), BF16),
        compiler_params=_cparams(("parallel", "arbitrary")),
    )(far, proj, proj, proj, proj, proj, wi, near_b)


PAIRS_C = 4


def _c_tables(rel_table):
    n_var = WIN_C // TQ_C
    length = WIN_C + TQ_C
    k = jnp.arange(length, dtype=I32)
    cr = jnp.where(k < WIN_C, k, k - length)
    off = (jnp.arange(n_var, dtype=I32) * TQ_C)[:, None]
    rel_idx = jnp.clip(off - cr[None, :], -C_REL_CLIP, C_REL_CLIP) + C_REL_CLIP
    g = rel_table[rel_idx].transpose(2, 0, 1)
    bias = _toeplitz(g, TQ_C, WIN_C)
    r = jnp.arange(TQ_C, dtype=I32)[None, :, None]
    c = jnp.arange(WIN_C, dtype=I32)[None, None, :]
    kc = jnp.floor_divide(c - off[:, :, None], CHUNK)
    ok = (kc <= r // CHUNK) & (kc >= r // CHUNK - C_LEFT_CHUNKS)
    return jnp.where(ok[None], bias, NEG)


def _attn_c_kernel(q_ref, k_ref, v_ref, tab_ref, o_ref):
    i = pl.program_id(2)
    n_var = WIN_C // TQ_C
    lane = lax.broadcasted_iota(I32, (1, LANE), 1)
    lo = lane < HEAD_DIM
    var = jnp.minimum(i, n_var - 1)
    ks = pl.multiple_of(jnp.maximum(i - (n_var - 1), 0) * TQ_C, TQ_C)
    pairs = range(PAIRS_C)
    cols = [slice(pp * LANE, (pp + 1) * LANE) for pp in pairs]
    s = [_dot_nt(_split_lanes(q_ref[:, cols[pp]]), k_ref[pl.ds(ks, WIN_C), cols[pp]]) for pp in pairs]
    s = [s[pp] + jnp.concatenate([tab_ref[2 * pp, var], tab_ref[2 * pp + 1, var]], axis=0) for pp in pairs]
    m = [jnp.max(s[pp], axis=-1, keepdims=True) for pp in pairs]
    p = [jnp.exp(s[pp] - m[pp]) for pp in pairs]
    l = [jnp.sum(p[pp], axis=-1, keepdims=True) for pp in pairs]
    pv = [jnp.dot(p[pp].astype(BF16), v_ref[pl.ds(ks, WIN_C), cols[pp]], preferred_element_type=F32) for pp in pairs]
    for pp in pairs:
        o = pv[pp] / l[pp]
        o_ref[:, cols[pp]] = jnp.where(lo, o[:TQ_C], o[TQ_C:]).astype(o_ref.dtype)


def _attn_c(proj, tab):
    b, s, _ = proj.shape
    assert s >= WIN_C and s % TQ_C == 0
    n_grp = C_HEADS // 2 // PAIRS_C
    n_var = WIN_C // TQ_C
    wblk = PAIRS_C * LANE
    tab = tab.reshape(n_grp, 2 * PAIRS_C, n_var, TQ_C, WIN_C)
    return pl.pallas_call(
        _attn_c_kernel,
        grid=(n_grp, b, s // TQ_C),
        in_specs=[
            pl.BlockSpec((None, TQ_C, wblk), lambda h, b_, i: (b_, i, h)),
            pl.BlockSpec((None, s, wblk), lambda h, b_, i: (b_, 0, n_grp + h)),
            pl.BlockSpec((None, s, wblk), lambda h, b_, i: (b_, 0, 2 * n_grp + h)),
            pl.BlockSpec((None, 2 * PAIRS_C, n_var, TQ_C, WIN_C), lambda h, b_, i: (h, 0, 0, 0, 0),
                         pipeline_mode=pl.Buffered(1)),
        ],
        out_specs=pl.BlockSpec((None, TQ_C, wblk), lambda h, b_, i: (b_, i, h)),
        out_shape=jax.ShapeDtypeStruct((b, s, D_MODEL), BF16),
        compiler_params=_cparams(("parallel", "parallel", "arbitrary")),
    )(proj, proj, proj, tab)


ROW_TILE = 8


def _rows_to_tiles(ref, x):
    rows = x.shape[0]
    for c in range(ROW_TILE):
        ref[pl.ds(c, rows, stride=ROW_TILE), :] = x[:, c * LANE:(c + 1) * LANE]


def _tiles_to_rows(ref, rows):
    return jnp.concatenate([ref[pl.ds(c, rows, stride=ROW_TILE), :] for c in range(ROW_TILE)], axis=1)


def _router_kernel(h_ref, g_ref, wr_ref, br_ref, ei_ref, cw_ref):
    xn = _rms(h_ref[...], g_ref[...])
    logits = jnp.dot(xn, wr_ref[...], preferred_element_type=F32,
                     precision=lax.Precision.HIGHEST) + br_ref[...]
    lane = lax.broadcasted_iota(I32, (1, LANE), 1)
    lane_f = lane.astype(F32)
    big = float(LANE)
    is_g = lane < N_GROUPS
    lg = jnp.where(is_g, logits, -jnp.inf)
    gmax = jnp.max(lg, axis=-1, keepdims=True)
    g_sel = jnp.min(jnp.where(lg == gmax, lane_f, big), axis=-1, keepdims=True)
    p_gsel = 1.0 / jnp.sum(jnp.where(is_g, jnp.exp(logits - gmax), 0.0), axis=-1, keepdims=True)
    e_grp = ((lane - N_GROUPS) // EXPERTS_PER_GROUP).astype(F32)
    in_sel = (lane >= N_GROUPS) & (lane < N_GROUPS + N_EXPERTS) & (e_grp == g_sel)
    le = jnp.where(in_sel, logits, -jnp.inf)
    m1 = jnp.max(le, axis=-1, keepdims=True)
    i1 = jnp.min(jnp.where(le == m1, lane_f, big), axis=-1, keepdims=True)
    le2 = jnp.where(lane_f == i1, -jnp.inf, le)
    m2 = jnp.max(le2, axis=-1, keepdims=True)
    i2 = jnp.min(jnp.where(le2 == m2, lane_f, big), axis=-1, keepdims=True)
    t = jnp.exp(m2 - m1)
    c1 = p_gsel / (1.0 + t)
    c2 = p_gsel * t / (1.0 + t)
    a = jnp.minimum(i1, i2) - N_GROUPS - EXPERTS_PER_GROUP * g_sel
    b = jnp.maximum(i1, i2) - N_GROUPS - EXPERTS_PER_GROUP * g_sel
    pair = a * (2 * EXPERTS_PER_GROUP - 1 - a) * 0.5 + (b - a - 1.0)
    cls = g_sel * PAIRS_PER_GROUP + pair
    ei_ref[...] = jnp.broadcast_to(cls, ei_ref.shape).astype(I32)
    first_is_lo = i1 < i2
    cw_ref[...] = jnp.where(lane == 0, jnp.where(first_is_lo, c1, c2), jnp.where(first_is_lo, c2, c1))


def _router(h, g, w_rg, b_rg, w_re, b_re, tm=256):
    n, d = h.shape
    pad = LANE - N_GROUPS - N_EXPERTS
    wr = jnp.concatenate([w_rg, w_re, jnp.zeros((d, pad), F32)], axis=1)
    br = jnp.concatenate([b_rg, b_re, jnp.zeros((pad,), F32)]).reshape(1, LANE)
    return pl.pallas_call(
        _router_kernel,
        grid=(n // tm,),
        in_specs=[pl.BlockSpec((tm, d), lambda i: (i, 0)),
                  pl.BlockSpec((1, d), lambda i: (0, 0)),
                  pl.BlockSpec((d, LANE), lambda i: (0, 0)),
                  pl.BlockSpec((1, LANE), lambda i: (0, 0))],
        out_specs=[pl.BlockSpec((tm, LANE), lambda i: (i, 0)),
                   pl.BlockSpec((tm, LANE), lambda i: (i, 0))],
        out_shape=[jax.ShapeDtypeStruct((n, LANE), I32),
                   jax.ShapeDtypeStruct((n, LANE), F32)],
        compiler_params=_cparams(("parallel",)),
    )(h, g.reshape(1, d), wr, br)


N_CLASSES = N_GROUPS * PAIRS_PER_GROUP
_PAIR_AB = [(a, b) for a in range(EXPERTS_PER_GROUP) for b in range(a + 1, EXPERTS_PER_GROUP)]
CLASS_EXPERT_LO = [g * EXPERTS_PER_GROUP + a for g in range(N_GROUPS) for a, _ in _PAIR_AB]
CLASS_EXPERT_HI = [g * EXPERTS_PER_GROUP + b for g in range(N_GROUPS) for _, b in _PAIR_AB]
TB_MOE = 256


def _rank_kernel(cls_ref, rank_ref, cnt_ref, carry_ref):
    tb = cls_ref.shape[0]

    @pl.when(pl.program_id(0) == 0)
    def _():
        carry_ref[...] = jnp.zeros(carry_ref.shape, F32)

    onehot = cls_ref[...] == lax.broadcasted_iota(I32, (tb, LANE), 1)
    tril = lax.broadcasted_iota(I32, (tb, tb), 0) >= lax.broadcasted_iota(I32, (tb, tb), 1)
    pref = jnp.dot(jnp.where(tril, 1.0, 0.0).astype(BF16), jnp.where(onehot, 1.0, 0.0).astype(BF16),
                   preferred_element_type=F32)
    carry = carry_ref[...]
    rank = jnp.sum(jnp.where(onehot, pref + carry, 0.0), axis=-1, keepdims=True) - 1.0
    rank_ref[...] = jnp.broadcast_to(rank, rank_ref.shape).astype(I32)
    carry = carry + pref[tb - 1:tb, :]
    carry_ref[...] = carry
    cnt_ref[...] = carry


def _class_rank(cls, tb=512):
    n = cls.shape[0]
    return pl.pallas_call(
        _rank_kernel,
        grid=(n // tb,),
        in_specs=[pl.BlockSpec((tb, LANE), lambda i: (i, 0))],
        out_specs=[pl.BlockSpec((tb, LANE), lambda i: (i, 0)), pl.BlockSpec((1, LANE), lambda i: (0, 0))],
        out_shape=[jax.ShapeDtypeStruct((n, LANE), I32), jax.ShapeDtypeStruct((1, LANE), F32)],
        scratch_shapes=[pltpu.VMEM((1, LANE), F32)],
        compiler_params=_cparams(("arbitrary",)),
    )(cls)


def _moe_schedule(counts, n_tok, tm):
    counts = counts[:N_CLASSES].astype(I32)
    tiles_per = (counts + tm - 1) // tm
    tile_end = jnp.cumsum(tiles_per)
    tile_start = tile_end - tiles_per
    start_row = jnp.zeros((LANE,), I32).at[:N_CLASSES].set(tile_start * tm)
    n_tiles = n_tok // tm + N_CLASSES
    t = jnp.arange(n_tiles, dtype=I32)
    tc = jnp.searchsorted(tile_end, t, side="right").astype(I32)
    live = tc < N_CLASSES
    tc = jnp.minimum(tc, N_CLASSES - 1)
    nv = jnp.where(live, jnp.clip(counts[tc] - (t - tile_start[tc]) * tm, 0, tm), 0).astype(I32)
    e_lo = jnp.asarray(CLASS_EXPERT_LO, I32)[tc]
    e_hi = jnp.asarray(CLASS_EXPERT_HI, I32)[tc]
    return start_row, e_lo, e_hi, nv


def _tok(i, width=ROW_TILE):
    return pl.ds(i * width, width)


def _slot_of(start_ref, cls_ref, rank_ref, r):
    return start_ref[cls_ref[0, r]] + rank_ref[0, r]


def _dispatch_kernel(start_ref, cls_ref, rank_ref, h_ref, g_ref, xs_init, xs_hbm, sbuf, sem, *, tb, n_blk):
    del xs_init
    i = pl.program_id(0)
    slot = i % 2

    def wait(sl):
        pltpu.make_async_copy(sbuf.at[sl], xs_hbm.at[pl.ds(0, tb * ROW_TILE)], sem.at[sl]).wait()

    _rows_to_tiles(sbuf.at[slot], _rms(h_ref[...], g_ref[...]))
    for r in range(tb):
        pltpu.make_async_copy(sbuf.at[slot, _tok(r)], xs_hbm.at[_tok(_slot_of(start_ref, cls_ref, rank_ref, r))],
                              sem.at[slot]).start()

    @pl.when(i > 0)
    def _():
        wait(1 - slot)

    @pl.when(i == n_blk - 1)
    def _():
        wait(slot)


def _dispatch(h, g, start_row, cls_sm, rank_sm, n_slots, tb=TB_MOE):
    n, d = h.shape
    n_blk = cls_sm.shape[0]
    xs0 = jnp.zeros((n_slots * ROW_TILE, LANE), F32)
    idx_spec = pl.BlockSpec((None, 1, tb), lambda i, st: (i, 0, 0), memory_space=pltpu.SMEM)
    return pl.pallas_call(
        functools.partial(_dispatch_kernel, tb=tb, n_blk=n_blk),
        grid_spec=pltpu.PrefetchScalarGridSpec(
            num_scalar_prefetch=1,
            grid=(n_blk,),
            in_specs=[idx_spec, idx_spec,
                      pl.BlockSpec((tb, d), lambda i, st: (i, 0)),
                      pl.BlockSpec((1, d), lambda i, st: (0, 0)),
                      pl.BlockSpec(memory_space=pl.ANY)],
            out_specs=pl.BlockSpec(memory_space=pl.ANY),
            scratch_shapes=[pltpu.VMEM((2, tb * ROW_TILE, LANE), F32), pltpu.SemaphoreType.DMA((2,))],
        ),
        out_shape=jax.ShapeDtypeStruct(xs0.shape, F32),
        input_output_aliases={5: 0},
        compiler_params=_cparams(("arbitrary",)),
    )(start_row, cls_sm, rank_sm, h, g.reshape(1, d), xs0)


def _expert_kernel(elo_ref, ehi_ref, nv_ref, xs_ref, wga, wua, wda, wgb, wub, wdb, ys_ref, *, tm):
    t = pl.program_id(0)

    @pl.when(nv_ref[t] > 0)
    def _():
        x = _tiles_to_rows(xs_ref, tm).astype(BF16)
        g = [jnp.dot(x, w[...], preferred_element_type=F32) for w in (wga, wgb)]
        u = [jnp.dot(x, w[...], preferred_element_type=F32) for w in (wua, wub)]
        act = [((g[k] * jax.nn.sigmoid(g[k])) * u[k]).astype(BF16) for k in range(2)]
        y = [jnp.dot(act[k], w[...], preferred_element_type=F32) for k, w in enumerate((wda, wdb))]
        for half in range(2):
            for c in range(ROW_TILE):
                ys_ref[pl.ds(half * ROW_TILE + c, tm, stride=2 * ROW_TILE), :] = y[half][:, c * LANE:(c + 1) * LANE]

    @pl.when(nv_ref[t] == 0)
    def _():
        ys_ref[...] = jnp.zeros(ys_ref.shape, F32)


def _moe_experts(xs, e_lo, e_hi, nv, layer, wg, wu, wd, tm=TM_MOE):
    d = wg.shape[2]
    assert d == ROW_TILE * LANE
    n_tiles = nv.shape[0]

    def w_spec(rows, cols, which):
        return pl.BlockSpec((None, None, rows, cols), lambda t, lo, hi, nv_: (layer, (lo, hi)[which][t], 0, 0))

    w_specs = [w_spec(d, D_EXPERT, k) for k in (0, 0)] + [w_spec(D_EXPERT, d, 0)]
    w_specs += [w_spec(d, D_EXPERT, k) for k in (1, 1)] + [w_spec(D_EXPERT, d, 1)]
    return pl.pallas_call(
        functools.partial(_expert_kernel, tm=tm),
        grid_spec=pltpu.PrefetchScalarGridSpec(
            num_scalar_prefetch=3,
            grid=(n_tiles,),
            in_specs=[pl.BlockSpec((tm * ROW_TILE, LANE), lambda t, lo, hi, nv_: (t, 0))] + w_specs,
            out_specs=pl.BlockSpec((tm * 2 * ROW_TILE, LANE), lambda t, lo, hi, nv_: (t, 0)),
        ),
        out_shape=jax.ShapeDtypeStruct((n_tiles * tm * 2 * ROW_TILE, LANE), F32),
        compiler_params=_cparams(("arbitrary",)),
    )(e_lo, e_hi, nv, xs, wg, wu, wd, wg, wu, wd)


def _combine_kernel(start_ref, cls_cur, rank_cur, cls_next, rank_next, ys_hbm, h_ref, cw_ref, g_ref, o_ref,
                    ybuf, sem, *, tb, n_blk, final):
    i = pl.program_id(0)
    slot = i % 2
    other = 1 - slot
    pair = 2 * ROW_TILE

    def fetch(cls_ref, rank_ref, r, sl):
        return pltpu.make_async_copy(ys_hbm.at[_tok(_slot_of(start_ref, cls_ref, rank_ref, r), pair)],
                                     ybuf.at[sl, _tok(r, pair)], sem.at[sl])

    def wait(sl):
        pltpu.make_async_copy(ys_hbm.at[pl.ds(0, tb * pair)], ybuf.at[sl], sem.at[sl]).wait()

    @pl.when(i == 0)
    def _():
        def body(r, c):
            fetch(cls_cur, rank_cur, r, 0).start()
            return c
        lax.fori_loop(0, tb, body, 0)

    wait(slot)
    for r in range(tb):
        fetch(cls_next, rank_next, r, other).start()
    yb = ybuf.at[slot]
    y_lo = jnp.concatenate([yb[pl.ds(c, tb, stride=pair), :] for c in range(ROW_TILE)], axis=1)
    y_hi = jnp.concatenate([yb[pl.ds(ROW_TILE + c, tb, stride=pair), :] for c in range(ROW_TILE)], axis=1)
    cw = cw_ref[...]
    out = h_ref[...] + cw[:, 0:1] * y_lo + cw[:, 1:2] * y_hi
    if final:
        out = _rms(out, g_ref[...])
    o_ref[...] = out

    @pl.when(i == n_blk - 1)
    def _():
        wait(other)


def _combine(h, ys, cw, g, start_row, cls_sm, rank_sm, final, tb=TB_MOE):
    n, d = h.shape
    n_blk = n // tb
    cls_x = jnp.concatenate([cls_sm, cls_sm[-1:]], axis=0)
    rank_x = jnp.concatenate([rank_sm, rank_sm[-1:]], axis=0)

    def idx_spec(shift):
        return pl.BlockSpec((None, 1, tb), lambda i, st: (i + shift, 0, 0), memory_space=pltpu.SMEM)

    return pl.pallas_call(
        functools.partial(_combine_kernel, tb=tb, n_blk=n_blk, final=final),
        grid_spec=pltpu.PrefetchScalarGridSpec(
            num_scalar_prefetch=1,
            grid=(n_blk,),
            in_specs=[idx_spec(0), idx_spec(0), idx_spec(1), idx_spec(1),
                      pl.BlockSpec(memory_space=pl.ANY),
                      pl.BlockSpec((tb, d), lambda i, st: (i, 0)),
                      pl.BlockSpec((tb, LANE), lambda i, st: (i, 0)),
                      pl.BlockSpec((1, d), lambda i, st: (0, 0))],
            out_specs=pl.BlockSpec((tb, d), lambda i, st: (i, 0)),
            scratch_shapes=[pltpu.VMEM((2, tb * 2 * ROW_TILE, LANE), F32), pltpu.SemaphoreType.DMA((2,))],
        ),
        out_shape=jax.ShapeDtypeStruct((n, d), F32),
        compiler_params=_cparams(("arbitrary",)),
    )(start_row, cls_x, rank_x, cls_x, rank_x, ys, h, cw, g.reshape(1, d))


def _moe_layer(h, layer, ln_g, w_rg, b_rg, w_re, b_re, w_gate, w_up, w_down, final_g, final):
    n = h.shape[0]
    cls, cw = _router(h, ln_g, w_rg, b_rg, w_re, b_re)
    rank, counts = _class_rank(cls)
    start_row, e_lo, e_hi, nv = _moe_schedule(counts[0], n, TM_MOE)
    cls_sm = cls[:, 0].reshape(n // TB_MOE, 1, TB_MOE)
    rank_sm = rank[:, 0].reshape(n // TB_MOE, 1, TB_MOE)
    xs = _dispatch(h, ln_g, start_row, cls_sm, rank_sm, nv.shape[0] * TM_MOE)
    ys = _moe_experts(xs, e_lo, e_hi, nv, layer, w_gate, w_up, w_down)
    return _combine(h, ys, cw, final_g, start_row, cls_sm, rank_sm, final)


def _ab_weights(w_in):
    scale = HEAD_DIM ** -0.5
    widths = (512, 512, 512, 512, 64, 64, 256, 64, 4)
    offs = [sum(widths[:j]) for j in range(len(widths) + 1)]
    q_a, k_a, v_a, q_b, k_b, v_b, q_i, k_i, w_i = [w_in[:, offs[j]:offs[j + 1]] for j in range(len(widths))]
    pad = jnp.zeros((w_in.shape[0], LANE - IDX_HEADS), w_in.dtype)
    main = jnp.concatenate([q_a * scale, k_a, v_a, q_b * scale, k_b, k_b, v_b, v_b, q_i, k_i, k_i], axis=1)
    return jnp.concatenate([main, w_i, pad], axis=1).astype(BF16)


def kernel(x, t5_bias, ln_mix, ln_ffn, ln_final, ab_w_in, ab_w_out, diff_lambda, diff_subln,
           c_w_in, c_w_out, c_rel_bias, moe_w_rg, moe_b_rg, moe_w_re, moe_b_re,
           moe_w_gate, moe_w_up, moe_w_down):
    b, s, d = x.shape
    n = b * s
    depth = ln_mix.shape[0]
    h = x.reshape(n, d)
    near, far = _t5_tables(t5_bias)
    w_gate, w_up, w_down = (w.astype(BF16) for w in (moe_w_gate, moe_w_up, moe_w_down))
    for l in range(depth):
        if l % 2 == 0:
            e = l // 2
            lam_init = 0.8 - 0.6 * math.exp(-0.3 * l)
            proj, wi = _norm_proj(h, ln_mix[l], _ab_weights(ab_w_in[e]), AB_COLS)
            proj = proj.reshape(b, s, AB_COLS)
            y_a = _attn_a(proj, near[:A_HEADS], far, diff_lambda[e], diff_subln[e], lam_init)
            wi_rows = wi.reshape(b, s, LANE)[:, :, :8].transpose(0, 2, 1)
            y_b = _attn_b(proj, wi_rows, near[A_HEADS:], far)
            h = _out_proj(h, y_a.reshape(n, -1), 0, y_b.reshape(n, -1), 0, ab_w_out[e].astype(BF16))
        else:
            o = l // 2
            scale = HEAD_DIM ** -0.5
            w_in = jnp.concatenate([c_w_in[o][:, :d] * scale, c_w_in[o][:, d:]], axis=1).astype(BF16)
            (proj,) = _norm_proj(h, ln_mix[l], w_in, 3 * d)
            y = _attn_c(proj.reshape(b, s, 3 * d), _c_tables(c_rel_bias[o])).reshape(n, d)
            h = _out_proj(h, y, 0, y, 1, c_w_out[o].astype(BF16))
        h = _moe_layer(h, l, ln_ffn[l], moe_w_rg[l], moe_b_rg[l], moe_w_re[l], moe_b_re[l],
                       w_gate, w_up, w_down, ln_final, l == depth - 1)
    return h.reshape(b, s, d)
```

```python
import functools
import math

import jax
import jax.numpy as jnp
from jax import lax
from jax.experimental import pallas as pl
from jax.experimental.pallas import tpu as pltpu

F32 = jnp.float32
BF16 = jnp.bfloat16
I32 = jnp.int32

D_MODEL = 1024
CHUNK = 64
HEAD_DIM = 64
RMS_EPS = 1e-6
A_HEADS = 4
A_V_DIM = 2 * HEAD_DIM
B_HEADS = 8
IDX_HEADS = 4
TOPK_MAX = 256
C_HEADS = D_MODEL // HEAD_DIM
C_LEFT_CHUNKS = 8
C_REL_CLIP = 128
T5_BUCKETS = 32
T5_MAX_DIST = 1024
N_GROUPS = 4
EXPERTS_PER_GROUP = 8
N_EXPERTS = N_GROUPS * EXPERTS_PER_GROUP
PAIRS_PER_GROUP = EXPERTS_PER_GROUP * (EXPERTS_PER_GROUP - 1) // 2
D_EXPERT = D_MODEL // 2
NEG = -1e30

LANE = 128
VMEM_LIMIT = 56 * 1024 * 1024

TQ = 256
NEAR_TILES = 4
TQ_C = 2 * CHUNK
WIN_C = (C_LEFT_CHUNKS + 2) * CHUNK
TM_MOE = 256

COL_QA, COL_KA, COL_VA, COL_QB, COL_KB, COL_VB, COL_QI, COL_KI, COL_WI = (
    0, 512, 1024, 1536, 2048, 2176, 2304, 2560, 2688)
AB_COLS = 2688


def _cparams(sem, vmem=VMEM_LIMIT):
    return pltpu.CompilerParams(dimension_semantics=sem, vmem_limit_bytes=vmem)


def _dot_nt(a, b):
    return lax.dot_general(a, b, (((1,), (1,)), ((), ())), preferred_element_type=F32)


def _rms(x, g):
    return (x * lax.rsqrt(jnp.mean(x * x, axis=-1, keepdims=True) + RMS_EPS)) * g


def _norm_proj_kernel(x_ref, g_ref, w_ref, o_ref, *tail_ref, n_main):
    xn = _rms(x_ref[...], g_ref[...]).astype(BF16)
    acc = jnp.dot(xn, w_ref[...], preferred_element_type=F32)
    o_ref[...] = acc[:, :n_main].astype(o_ref.dtype)
    if tail_ref:
        tail_ref[0][...] = acc[:, n_main:]


def _norm_proj(x, g, w, n_main, tm=256):
    n, d = x.shape
    n_all = w.shape[1]
    out_shape = [jax.ShapeDtypeStruct((n, n_main), BF16)]
    out_specs = [pl.BlockSpec((tm, n_main), lambda i: (i, 0))]
    if n_all > n_main:
        out_shape.append(jax.ShapeDtypeStruct((n, n_all - n_main), F32))
        out_specs.append(pl.BlockSpec((tm, n_all - n_main), lambda i: (i, 0)))
    return pl.pallas_call(
        functools.partial(_norm_proj_kernel, n_main=n_main),
        grid=(n // tm,),
        in_specs=[pl.BlockSpec((tm, d), lambda i: (i, 0)),
                  pl.BlockSpec((1, d), lambda i: (0, 0)),
                  pl.BlockSpec((d, n_all), lambda i: (0, 0))],
        out_specs=out_specs, out_shape=out_shape,
        compiler_params=_cparams(("parallel",)),
    )(x, g.reshape(1, d), w)


def _out_proj_kernel(h_ref, y0_ref, y1_ref, w0_ref, w1_ref, o_ref):
    acc = jnp.dot(y0_ref[...], w0_ref[...], preferred_element_type=F32)
    acc = acc + jnp.dot(y1_ref[...], w1_ref[...], preferred_element_type=F32)
    o_ref[...] = h_ref[...] + acc


def _out_proj(h, y0, c0, y1, c1, w, tm=512):
    n, d = h.shape
    half = d // 2
    return pl.pallas_call(
        _out_proj_kernel,
        grid=(n // tm,),
        in_specs=[pl.BlockSpec((tm, d), lambda i: (i, 0)),
                  pl.BlockSpec((tm, half), lambda i: (i, c0)),
                  pl.BlockSpec((tm, half), lambda i: (i, c1)),
                  pl.BlockSpec((half, d), lambda i: (0, 0)),
                  pl.BlockSpec((half, d), lambda i: (1, 0))],
        out_specs=pl.BlockSpec((tm, d), lambda i: (i, 0)),
        out_shape=jax.ShapeDtypeStruct((n, d), F32),
        compiler_params=_cparams(("parallel",)),
    )(h, y0, y1, w, w)


def _t5_bucket(rel):
    half = T5_BUCKETS // 2
    max_exact = half // 2
    ret = jnp.where(rel > 0, half, 0)
    n = jnp.abs(rel)
    nf = jnp.maximum(n, 1).astype(F32)
    large = max_exact + (jnp.log(nf / max_exact) / math.log(T5_MAX_DIST / max_exact)
                         * (half - max_exact)).astype(I32)
    large = jnp.minimum(large, half - 1)
    return (ret + jnp.where(n < max_exact, n, large)).astype(I32)


def _toeplitz(g, rows, width):
    length = g.shape[-1]
    assert width < length and rows <= length
    flat = jnp.tile(g, (1,) * (g.ndim - 1) + (rows,))[..., :rows * (length - 1)]
    return flat.reshape(g.shape[:-1] + (rows, length - 1))[..., :width]


def _t5_tables(t5_bias):
    span = NEAR_TILES * TQ
    length = span + TQ
    k = jnp.arange(length, dtype=I32)
    delta = jnp.where(k < span, k, k - length) - (span - TQ)
    g = t5_bias[_t5_bucket(delta)].T
    bias = _toeplitz(g, TQ, span)
    r = jnp.arange(TQ, dtype=I32)[:, None]
    kr = jnp.arange(span, dtype=I32)[None, :] - (span - TQ)
    ok = jnp.floor_divide(kr, CHUNK) <= (r // CHUNK)
    bias = jnp.where(ok[None], bias, NEG)
    near = bias.reshape(-1, TQ, NEAR_TILES, TQ).transpose(0, 2, 1, 3)
    far = t5_bias[_t5_bucket(jnp.asarray(-(span - TQ) - 1, I32))]
    return near, far


N_GRP = 4
STACK = 2 * TQ


def _attend(i, qs_ref, k_fn, v_fn, mask_fn, near_fn, far_cols, mf_ref, mn_ref, l_ref, acc_ref, s_ref=None):
    n_far = jnp.maximum(i - (NEAR_TILES - 1), 0)
    n_far_slots = 0 if s_ref is None else s_ref.shape[0] - NEAR_TILES
    n_far_cached = jnp.minimum(n_far, n_far_slots)

    mf_ref[...] = jnp.full(mf_ref.shape, NEG, F32)
    mn_ref[...] = jnp.full(mn_ref.shape, NEG, F32)

    def rows(g):
        return slice(g * STACK, (g + 1) * STACK)

    def logits(g, kt, jt):
        off = pl.multiple_of(kt * TQ, TQ)
        s = _dot_nt(qs_ref[rows(g), :], k_fn(g, off))
        extra = None if mask_fn is None else mask_fn(kt)[None]
        if jt is not None:
            extra = near_fn(g, jt) if extra is None else extra + near_fn(g, jt)
        if extra is not None:
            s = (s.reshape(2, TQ, TQ) + extra).reshape(STACK, TQ)
        return s

    def loops(make_fn):
        def run(lo, hi, fn):
            pairs = (hi - lo) // 2

            def pair_body(j, c):
                fn(lo + 2 * j)
                fn(lo + 2 * j + 1)
                return c

            def body(kt, c):
                fn(kt)
                return c

            lax.fori_loop(0, pairs, pair_body, 0)
            lax.fori_loop(lo + 2 * pairs, hi, body, 0)

        if n_far_slots > 0:
            run(0, n_far_cached, make_fn(False, True))
        run(n_far_cached, n_far, make_fn(False, False))
        run(n_far, i + 1, make_fn(True, s_ref is not None))

    def slot(kt, jt):
        return NEAR_TILES + kt if jt is None else jt

    def pass1(near, cached):
        m_ref = mn_ref if near else mf_ref

        def fn(kt):
            jt = kt - (i - (NEAR_TILES - 1)) if near else None
            s = [logits(g, kt, jt) for g in range(N_GRP)]
            for g in range(N_GRP):
                if cached:
                    s_ref[slot(kt, jt), rows(g), :] = s[g]
                m_ref[rows(g), :] = jnp.maximum(m_ref[rows(g), :], jnp.maximum(s[g][:, :LANE], s[g][:, LANE:]))
        return fn

    loops(pass1)

    for g in range(N_GRP):
        m = jnp.maximum(jnp.max(mf_ref[rows(g), :], axis=-1, keepdims=True) + far_cols[g],
                        jnp.max(mn_ref[rows(g), :], axis=-1, keepdims=True))
        m_near = jnp.broadcast_to(m, (STACK, LANE))
        mn_ref[rows(g), :] = m_near
        mf_ref[rows(g), :] = m_near - far_cols[g]
    l_ref[...] = jnp.zeros(l_ref.shape, F32)
    acc_ref[...] = jnp.zeros(acc_ref.shape, F32)

    def pass2(near, cached):
        shift_ref = mn_ref if near else mf_ref

        def fn(kt):
            jt = kt - (i - (NEAR_TILES - 1)) if near else None
            off = pl.multiple_of(kt * TQ, TQ)
            grp = range(N_GRP)
            s = [s_ref[slot(kt, jt), rows(g), :] if cached else logits(g, kt, jt) for g in grp]
            p0 = [jnp.exp(s[g][:, :LANE] - shift_ref[rows(g), :]) for g in grp]
            p1 = [jnp.exp(s[g][:, LANE:] - shift_ref[rows(g), :]) for g in grp]
            for g in grp:
                l_ref[rows(g), :] += p0[g] + p1[g]
            pv = [jnp.dot(jnp.concatenate([p0[g], p1[g]], axis=1).astype(BF16), v_fn(g, off),
                          preferred_element_type=F32) for g in grp]
            for g in grp:
                acc_ref[rows(g), :] += pv[g]
        return fn

    loops(pass2)
    return [acc_ref[rows(g), :] / jnp.sum(l_ref[rows(g), :], axis=-1, keepdims=True) for g in range(N_GRP)]


def _attend_scratch():
    return [pltpu.VMEM((N_GRP * STACK, LANE), BF16)] + [pltpu.VMEM((N_GRP * STACK, LANE), F32)] * 4


def _split_lanes(x):
    lo = lax.broadcasted_iota(I32, (1, LANE), 1) < HEAD_DIM
    z = jnp.zeros_like(x)
    return jnp.concatenate([jnp.where(lo, x, z), jnp.where(lo, z, x)], axis=0)


def _attn_a_kernel(far_ref, q_ref, k_ref, v_ref, tab_ref, lam_ref, sub_ref, o_ref,
                   qs_ref, mf_ref, mn_ref, l_ref, acc_ref, s_ref, *, lam_init):
    i = pl.program_id(1)
    for g in range(A_HEADS):
        qs_ref[g * STACK:(g + 1) * STACK, :] = _split_lanes(q_ref[:, g * LANE:(g + 1) * LANE])
    outs = _attend(
        i, qs_ref,
        lambda g, off: k_ref[pl.ds(off, TQ), g * LANE:(g + 1) * LANE],
        lambda g, off: v_ref[pl.ds(off, TQ), g * LANE:(g + 1) * LANE],
        None,
        lambda g, jt: tab_ref[g, jt][None],
        [jnp.full((STACK, 1), far_ref[g], F32) for g in range(A_HEADS)],
        mf_ref, mn_ref, l_ref, acc_ref, s_ref)
    lam = lam_ref[...]
    lam_full = (jnp.exp(jnp.sum(lam[0:1] * lam[1:2], axis=-1, keepdims=True))
                - jnp.exp(jnp.sum(lam[2:3] * lam[3:4], axis=-1, keepdims=True)) + lam_init)
    for g, o in enumerate(outs):
        d = o[:TQ] - lam_full * o[TQ:]
        o_ref[:, g * LANE:(g + 1) * LANE] = (_rms(d, sub_ref[...]) * (1.0 - lam_init)).astype(o_ref.dtype)


def _attn_a(proj, near, far, lam, subln, lam_init):
    b, s, _ = proj.shape
    assert A_HEADS == N_GRP
    wide = A_HEADS * A_V_DIM
    return pl.pallas_call(
        functools.partial(_attn_a_kernel, lam_init=lam_init),
        grid_spec=pltpu.PrefetchScalarGridSpec(
            num_scalar_prefetch=1,
            grid=(b, s // TQ),
            in_specs=[
                pl.BlockSpec((None, TQ, wide), lambda b_, i, f: (b_, i, COL_QA // wide)),
                pl.BlockSpec((None, s, wide), lambda b_, i, f: (b_, 0, COL_KA // wide), pipeline_mode=pl.Buffered(1)),
                pl.BlockSpec((None, s, wide), lambda b_, i, f: (b_, 0, COL_VA // wide), pipeline_mode=pl.Buffered(1)),
                pl.BlockSpec((A_HEADS, NEAR_TILES, TQ, TQ), lambda b_, i, f: (0, 0, 0, 0),
                             pipeline_mode=pl.Buffered(1)),
                pl.BlockSpec((4, HEAD_DIM), lambda b_, i, f: (0, 0)),
                pl.BlockSpec((1, A_V_DIM), lambda b_, i, f: (0, 0)),
            ],
            out_specs=pl.BlockSpec((None, TQ, wide), lambda b_, i, f: (b_, i, 0)),
            scratch_shapes=_attend_scratch() + [pltpu.VMEM((s // TQ, N_GRP * STACK, TQ), F32)],
        ),
        out_shape=jax.ShapeDtypeStruct((b, s, wide), BF16),
        compiler_params=_cparams(("parallel", "arbitrary")),
    )(far, proj, proj, proj, near, lam, subln.reshape(1, A_V_DIM))


def _attn_b_kernel(far_ref, qb_ref, kb_ref, vb_ref, qi_ref, ki_ref, wi_ref, tab_ref, o_ref,
                   keyst_ref, selm_ref, jm_ref, qs_ref, mf_ref, mn_ref, l_ref, acc_ref, s_ref,
                   *, top_k, idx_bits):
    i = pl.program_id(1)
    n_kt = i + 1
    lane = lax.broadcasted_iota(I32, (1, LANE), 1)
    lo = lane < HEAD_DIM
    int_min = jnp.int32(-2 ** 31)
    split_heads = _split_lanes

    key_row = lax.broadcasted_iota(I32, (TQ, TQ), 0)
    chunk_gap = (key_row // CHUNK) - (lax.broadcasted_iota(I32, (TQ, TQ), 1) // CHUNK)

    def causal(kt):
        return chunk_gap <= jnp.where(kt < i, jnp.int32(TQ), jnp.int32(0))

    qi = qi_ref[...]
    qis = jnp.concatenate([split_heads(qi[:, :LANE]), split_heads(qi[:, LANE:])], axis=0)
    wi = wi_ref[...] * ((IDX_HEADS ** -0.5) * (HEAD_DIM ** -0.5))

    def score_body(kt, c):
        off = pl.multiple_of(kt * TQ, TQ)
        r = jnp.maximum(_dot_nt(ki_ref[pl.ds(off, TQ), :], qis), 0.0)
        isc = r[:, 0:TQ] * wi[0:1, :]
        for hh in range(1, IDX_HEADS):
            isc = isc + r[:, hh * TQ:(hh + 1) * TQ] * wi[hh:hh + 1, :]
        isc = jnp.where(isc == 0.0, 0.0, isc)
        isc = jnp.where(causal(kt), isc, -jnp.inf)
        bits = lax.bitcast_convert_type(isc, I32)
        keyst_ref[kt] = bits ^ ((bits >> 31) & jnp.int32(0x7FFFFFFF))
        return c

    lax.fori_loop(0, n_kt, score_body, 0)


    def count(pred_fn):
        def body(kt, acc):
            c = jnp.where(pred_fn(kt, keyst_ref[kt]), 1.0, 0.0)
            return acc + jnp.sum(c.reshape(TQ // 32, 32, TQ), axis=0)
        acc = lax.fori_loop(0, n_kt, body, jnp.zeros((32, TQ), F32))
        return jnp.sum(acc, axis=0, keepdims=True)

    def thr_body(bi, state):
        tu, nge = state
        cand_u = tu | (jnp.int32(1) << (31 - bi))
        cand_s = cand_u ^ int_min
        cnt = count(lambda kt, kk: kk >= cand_s)
        keep = cnt >= top_k
        return jnp.where(keep, cand_u, tu), jnp.where(keep, cnt, nge)

    tu, nge = lax.fori_loop(0, 32, thr_body, (jnp.zeros((1, TQ), I32),
                                              jnp.full((1, TQ), (n_kt * TQ).astype(F32), F32)))
    thr = tu ^ int_min

    jm_ref[...] = jnp.full(jm_ref.shape, 2 ** idx_bits - 1, I32)

    @pl.when(jnp.max(nge) > top_k)
    def _():
        cnt_hi = top_k - count(lambda kt, kk: kk > thr)

        def tie_body(bi, jmax):
            trial = jmax | (jnp.int32(1) << (idx_bits - 1 - bi))
            cnt = count(lambda kt, kk: (kk == thr) & ((key_row + kt * TQ) <= trial))
            return jnp.where(cnt <= cnt_hi, trial, jmax)

        jm_ref[...] = jnp.broadcast_to(lax.fori_loop(0, idx_bits, tie_body, jnp.zeros((1, TQ), I32)), jm_ref.shape)

    jmax = jm_ref[0:1, :]

    def mask_body(kt, c):
        kk = keyst_ref[kt]
        sel = ((kk > thr) | ((kk == thr) & ((key_row + kt * TQ) <= jmax))) & causal(kt)
        selm_ref[kt] = jnp.where(sel, 0.0, NEG).T
        return c

    lax.fori_loop(0, n_kt, mask_body, 0)

    for g in range(N_GRP):
        qs_ref[g * STACK:(g + 1) * STACK, :] = split_heads(qb_ref[:, g * LANE:(g + 1) * LANE])
    top_rows = lax.broadcasted_iota(I32, (STACK, 1), 0) < TQ
    outs = _attend(
        i, qs_ref,
        lambda g, off: kb_ref[pl.ds(off, TQ), :],
        lambda g, off: vb_ref[pl.ds(off, TQ), :],
        lambda kt: selm_ref[kt],
        lambda g, jt: tab_ref[2 * g:2 * g + 2, jt],
        [jnp.where(top_rows, far_ref[A_HEADS + 2 * g], far_ref[A_HEADS + 2 * g + 1]) for g in range(N_GRP)],
        mf_ref, mn_ref, l_ref, acc_ref, s_ref)
    for g, o in enumerate(outs):
        o_ref[:, g * LANE:(g + 1) * LANE] = jnp.where(lo, o[:TQ], o[TQ:]).astype(o_ref.dtype)


B_CACHE_TILES = NEAR_TILES + 8


def _attn_b(proj, wi, near_b, far):
    b, s, _ = proj.shape
    top_k = min(TOPK_MAX, s // 4)
    assert top_k <= TQ and s % TQ == 0 and (s & (s - 1)) == 0
    n_kt = s // TQ
    assert B_HEADS == 2 * N_GRP and s < 2 ** 16
    qb_w = B_HEADS * HEAD_DIM
    qi_w = IDX_HEADS * HEAD_DIM
    return pl.pallas_call(
        functools.partial(_attn_b_kernel, top_k=float(top_k), idx_bits=int(math.log2(s))),
        grid_spec=pltpu.PrefetchScalarGridSpec(
            num_scalar_prefetch=1,
            grid=(b, n_kt),
            in_specs=[
                pl.BlockSpec((None, TQ, qb_w), lambda b_, i, f: (b_, i, COL_QB // qb_w)),
                pl.BlockSpec((None, s, LANE), lambda b_, i, f: (b_, 0, COL_KB // LANE)),
                pl.BlockSpec((None, s, LANE), lambda b_, i, f: (b_, 0, COL_VB // LANE)),
                pl.BlockSpec((None, TQ, qi_w), lambda b_, i, f: (b_, i, COL_QI // qi_w)),
                pl.BlockSpec((None, s, LANE), lambda b_, i, f: (b_, 0, COL_KI // LANE)),
                pl.BlockSpec((None, 8, TQ), lambda b_, i, f: (b_, 0, i)),
                pl.BlockSpec((B_HEADS, NEAR_TILES, TQ, TQ), lambda b_, i, f: (0, 0, 0, 0),
                             pipeline_mode=pl.Buffered(1)),
            ],
            out_specs=pl.BlockSpec((None, TQ, qb_w), lambda b_, i, f: (b_, i, 0)),
            scratch_shapes=[pltpu.VMEM((n_kt, TQ, TQ), I32), pltpu.VMEM((n_kt, TQ, TQ), F32),
                            pltpu.VMEM((8, TQ), I32)] + _attend_scratch()
            + [pltpu.VMEM((min(n_kt, B_CACHE_TILES), N_GRP * STACK, TQ), F32)],
        ),
        out_shape=jax.ShapeDtypeStruct((b, s, qb_w), BF16),
        compiler_params=_cparams(("parallel", "arbitrary")),
    )(far, proj, proj, proj, proj, proj, wi, near_b)


PAIRS_C = 4


def _c_tables(rel_table):
    n_var = WIN_C // TQ_C
    length = WIN_C + TQ_C
    k = jnp.arange(length, dtype=I32)
    cr = jnp.where(k < WIN_C, k, k - length)
    off = (jnp.arange(n_var, dtype=I32) * TQ_C)[:, None]
    rel_idx = jnp.clip(off - cr[None, :], -C_REL_CLIP, C_REL_CLIP) + C_REL_CLIP
    g = rel_table[rel_idx].transpose(2, 0, 1)
    bias = _toeplitz(g, TQ_C, WIN_C)
    r = jnp.arange(TQ_C, dtype=I32)[None, :, None]
    c = jnp.arange(WIN_C, dtype=I32)[None, None, :]
    kc = jnp.floor_divide(c - off[:, :, None], CHUNK)
    ok = (kc <= r // CHUNK) & (kc >= r // CHUNK - C_LEFT_CHUNKS)
    return jnp.where(ok[None], bias, NEG)


def _attn_c_kernel(q_ref, k_ref, v_ref, tab_ref, o_ref):
    i = pl.program_id(2)
    n_var = WIN_C // TQ_C
    lane = lax.broadcasted_iota(I32, (1, LANE), 1)
    lo = lane < HEAD_DIM
    var = jnp.minimum(i, n_var - 1)
    ks = pl.multiple_of(jnp.maximum(i - (n_var - 1), 0) * TQ_C, TQ_C)
    pairs = range(PAIRS_C)
    cols = [slice(pp * LANE, (pp + 1) * LANE) for pp in pairs]
    s = [_dot_nt(_split_lanes(q_ref[:, cols[pp]]), k_ref[pl.ds(ks, WIN_C), cols[pp]]) for pp in pairs]
    s = [s[pp] + jnp.concatenate([tab_ref[2 * pp, var], tab_ref[2 * pp + 1, var]], axis=0) for pp in pairs]
    m = [jnp.max(s[pp], axis=-1, keepdims=True) for pp in pairs]
    p = [jnp.exp(s[pp] - m[pp]) for pp in pairs]
    l = [jnp.sum(p[pp], axis=-1, keepdims=True) for pp in pairs]
    pv = [jnp.dot(p[pp].astype(BF16), v_ref[pl.ds(ks, WIN_C), cols[pp]], preferred_element_type=F32) for pp in pairs]
    for pp in pairs:
        o = pv[pp] / l[pp]
        o_ref[:, cols[pp]] = jnp.where(lo, o[:TQ_C], o[TQ_C:]).astype(o_ref.dtype)


def _attn_c(proj, tab):
    b, s, _ = proj.shape
    assert s >= WIN_C and s % TQ_C == 0
    n_grp = C_HEADS // 2 // PAIRS_C
    n_var = WIN_C // TQ_C
    wblk = PAIRS_C * LANE
    tab = tab.reshape(n_grp, 2 * PAIRS_C, n_var, TQ_C, WIN_C)
    return pl.pallas_call(
        _attn_c_kernel,
        grid=(n_grp, b, s // TQ_C),
        in_specs=[
            pl.BlockSpec((None, TQ_C, wblk), lambda h, b_, i: (b_, i, h)),
            pl.BlockSpec((None, s, wblk), lambda h, b_, i: (b_, 0, n_grp + h)),
            pl.BlockSpec((None, s, wblk), lambda h, b_, i: (b_, 0, 2 * n_grp + h)),
            pl.BlockSpec((None, 2 * PAIRS_C, n_var, TQ_C, WIN_C), lambda h, b_, i: (h, 0, 0, 0, 0),
                         pipeline_mode=pl.Buffered(1)),
        ],
        out_specs=pl.BlockSpec((None, TQ_C, wblk), lambda h, b_, i: (b_, i, h)),
        out_shape=jax.ShapeDtypeStruct((b, s, D_MODEL), BF16),
        compiler_params=_cparams(("parallel", "parallel", "arbitrary")),
    )(proj, proj, proj, tab)


ROW_TILE = 8


def _rows_to_tiles(ref, x):
    rows = x.shape[0]
    for c in range(ROW_TILE):
        ref[pl.ds(c, rows, stride=ROW_TILE), :] = x[:, c * LANE:(c + 1) * LANE]


def _tiles_to_rows(ref, rows):
    return jnp.concatenate([ref[pl.ds(c, rows, stride=ROW_TILE), :] for c in range(ROW_TILE)], axis=1)


def _router_kernel(h_ref, g_ref, wr_ref, br_ref, ei_ref, cw_ref):
    xn = _rms(h_ref[...], g_ref[...])
    logits = jnp.dot(xn, wr_ref[...], preferred_element_type=F32,
                     precision=lax.Precision.HIGHEST) + br_ref[...]
    lane = lax.broadcasted_iota(I32, (1, LANE), 1)
    lane_f = lane.astype(F32)
    big = float(LANE)
    is_g = lane < N_GROUPS
    lg = jnp.where(is_g, logits, -jnp.inf)
    gmax = jnp.max(lg, axis=-1, keepdims=True)
    g_sel = jnp.min(jnp.where(lg == gmax, lane_f, big), axis=-1, keepdims=True)
    p_gsel = 1.0 / jnp.sum(jnp.where(is_g, jnp.exp(logits - gmax), 0.0), axis=-1, keepdims=True)
    e_grp = ((lane - N_GROUPS) // EXPERTS_PER_GROUP).astype(F32)
    in_sel = (lane >= N_GROUPS) & (lane < N_GROUPS + N_EXPERTS) & (e_grp == g_sel)
    le = jnp.where(in_sel, logits, -jnp.inf)
    m1 = jnp.max(le, axis=-1, keepdims=True)
    i1 = jnp.min(jnp.where(le == m1, lane_f, big), axis=-1, keepdims=True)
    le2 = jnp.where(lane_f == i1, -jnp.inf, le)
    m2 = jnp.max(le2, axis=-1, keepdims=True)
    i2 = jnp.min(jnp.where(le2 == m2, lane_f, big), axis=-1, keepdims=True)
    t = jnp.exp(m2 - m1)
    c1 = p_gsel / (1.0 + t)
    c2 = p_gsel * t / (1.0 + t)
    a = jnp.minimum(i1, i2) - N_GROUPS - EXPERTS_PER_GROUP * g_sel
    b = jnp.maximum(i1, i2) - N_GROUPS - EXPERTS_PER_GROUP * g_sel
    pair = a * (2 * EXPERTS_PER_GROUP - 1 - a) * 0.5 + (b - a - 1.0)
    cls = g_sel * PAIRS_PER_GROUP + pair
    ei_ref[...] = jnp.broadcast_to(cls, ei_ref.shape).astype(I32)
    first_is_lo = i1 < i2
    cw_ref[...] = jnp.where(lane == 0, jnp.where(first_is_lo, c1, c2), jnp.where(first_is_lo, c2, c1))


def _router(h, g, w_rg, b_rg, w_re, b_re, tm=256):
    n, d = h.shape
    pad = LANE - N_GROUPS - N_EXPERTS
    wr = jnp.concatenate([w_rg, w_re, jnp.zeros((d, pad), F32)], axis=1)
    br = jnp.concatenate([b_rg, b_re, jnp.zeros((pad,), F32)]).reshape(1, LANE)
    return pl.pallas_call(
        _router_kernel,
        grid=(n // tm,),
        in_specs=[pl.BlockSpec((tm, d), lambda i: (i, 0)),
                  pl.BlockSpec((1, d), lambda i: (0, 0)),
                  pl.BlockSpec((d, LANE), lambda i: (0, 0)),
                  pl.BlockSpec((1, LANE), lambda i: (0, 0))],
        out_specs=[pl.BlockSpec((tm, LANE), lambda i: (i, 0)),
                   pl.BlockSpec((tm, LANE), lambda i: (i, 0))],
        out_shape=[jax.ShapeDtypeStruct((n, LANE), I32),
                   jax.ShapeDtypeStruct((n, LANE), F32)],
        compiler_params=_cparams(("parallel",)),
    )(h, g.reshape(1, d), wr, br)


N_CLASSES = N_GROUPS * PAIRS_PER_GROUP
_PAIR_AB = [(a, b) for a in range(EXPERTS_PER_GROUP) for b in range(a + 1, EXPERTS_PER_GROUP)]
CLASS_EXPERT_LO = [g * EXPERTS_PER_GROUP + a for g in range(N_GROUPS) for a, _ in _PAIR_AB]
CLASS_EXPERT_HI = [g * EXPERTS_PER_GROUP + b for g in range(N_GROUPS) for _, b in _PAIR_AB]
TB_MOE = 256


def _rank_kernel(cls_ref, rank_ref, cnt_ref, carry_ref):
    tb = cls_ref.shape[0]

    @pl.when(pl.program_id(0) == 0)
    def _():
        carry_ref[...] = jnp.zeros(carry_ref.shape, F32)

    onehot = cls_ref[...] == lax.broadcasted_iota(I32, (tb, LANE), 1)
    tril = lax.broadcasted_iota(I32, (tb, tb), 0) >= lax.broadcasted_iota(I32, (tb, tb), 1)
    pref = jnp.dot(jnp.where(tril, 1.0, 0.0).astype(BF16), jnp.where(onehot, 1.0, 0.0).astype(BF16),
                   preferred_element_type=F32)
    carry = carry_ref[...]
    rank = jnp.sum(jnp.where(onehot, pref + carry, 0.0), axis=-1, keepdims=True) - 1.0
    rank_ref[...] = jnp.broadcast_to(rank, rank_ref.shape).astype(I32)
    carry = carry + pref[tb - 1:tb, :]
    carry_ref[...] = carry
    cnt_ref[...] = carry


def _class_rank(cls, tb=512):
    n = cls.shape[0]
    return pl.pallas_call(
        _rank_kernel,
        grid=(n // tb,),
        in_specs=[pl.BlockSpec((tb, LANE), lambda i: (i, 0))],
        out_specs=[pl.BlockSpec((tb, LANE), lambda i: (i, 0)), pl.BlockSpec((1, LANE), lambda i: (0, 0))],
        out_shape=[jax.ShapeDtypeStruct((n, LANE), I32), jax.ShapeDtypeStruct((1, LANE), F32)],
        scratch_shapes=[pltpu.VMEM((1, LANE), F32)],
        compiler_params=_cparams(("arbitrary",)),
    )(cls)


def _moe_schedule(counts, n_tok, tm):
    counts = counts[:N_CLASSES].astype(I32)
    tiles_per = (counts + tm - 1) // tm
    tile_end = jnp.cumsum(tiles_per)
    tile_start = tile_end - tiles_per
    start_row = jnp.zeros((LANE,), I32).at[:N_CLASSES].set(tile_start * tm)
    n_tiles = n_tok // tm + N_CLASSES
    t = jnp.arange(n_tiles, dtype=I32)
    tc = jnp.searchsorted(tile_end, t, side="right").astype(I32)
    live = tc < N_CLASSES
    tc = jnp.minimum(tc, N_CLASSES - 1)
    nv = jnp.where(live, jnp.clip(counts[tc] - (t - tile_start[tc]) * tm, 0, tm), 0).astype(I32)
    e_lo = jnp.asarray(CLASS_EXPERT_LO, I32)[tc]
    e_hi = jnp.asarray(CLASS_EXPERT_HI, I32)[tc]
    return start_row, e_lo, e_hi, nv


def _tok(i, width=ROW_TILE):
    return pl.ds(i * width, width)


def _slot_of(start_ref, cls_ref, rank_ref, r):
    return start_ref[cls_ref[0, r]] + rank_ref[0, r]


def _dispatch_kernel(start_ref, cls_ref, rank_ref, h_ref, g_ref, xs_init, xs_hbm, sbuf, sem, *, tb, n_blk):
    del xs_init
    i = pl.program_id(0)
    slot = i % 2

    def wait(sl):
        pltpu.make_async_copy(sbuf.at[sl], xs_hbm.at[pl.ds(0, tb * ROW_TILE)], sem.at[sl]).wait()

    _rows_to_tiles(sbuf.at[slot], _rms(h_ref[...], g_ref[...]))
    for r in range(tb):
        pltpu.make_async_copy(sbuf.at[slot, _tok(r)], xs_hbm.at[_tok(_slot_of(start_ref, cls_ref, rank_ref, r))],
                              sem.at[slot]).start()

    @pl.when(i > 0)
    def _():
        wait(1 - slot)

    @pl.when(i == n_blk - 1)
    def _():
        wait(slot)


def _dispatch(h, g, start_row, cls_sm, rank_sm, n_slots, tb=TB_MOE):
    n, d = h.shape
    n_blk = cls_sm.shape[0]
    xs0 = jnp.zeros((n_slots * ROW_TILE, LANE), F32)
    idx_spec = pl.BlockSpec((None, 1, tb), lambda i, st: (i, 0, 0), memory_space=pltpu.SMEM)
    return pl.pallas_call(
        functools.partial(_dispatch_kernel, tb=tb, n_blk=n_blk),
        grid_spec=pltpu.PrefetchScalarGridSpec(
            num_scalar_prefetch=1,
            grid=(n_blk,),
            in_specs=[idx_spec, idx_spec,
                      pl.BlockSpec((tb, d), lambda i, st: (i, 0)),
                      pl.BlockSpec((1, d), lambda i, st: (0, 0)),
                      pl.BlockSpec(memory_space=pl.ANY)],
            out_specs=pl.BlockSpec(memory_space=pl.ANY),
            scratch_shapes=[pltpu.VMEM((2, tb * ROW_TILE, LANE), F32), pltpu.SemaphoreType.DMA((2,))],
        ),
        out_shape=jax.ShapeDtypeStruct(xs0.shape, F32),
        input_output_aliases={5: 0},
        compiler_params=_cparams(("arbitrary",)),
    )(start_row, cls_sm, rank_sm, h, g.reshape(1, d), xs0)


def _expert_kernel(elo_ref, ehi_ref, nv_ref, xs_ref, wga, wua, wda, wgb, wub, wdb, ys_ref, *, tm):
    t = pl.program_id(0)

    @pl.when(nv_ref[t] > 0)
    def _():
        x = _tiles_to_rows(xs_ref, tm).astype(BF16)
        g = [jnp.dot(x, w[...], preferred_element_type=F32) for w in (wga, wgb)]
        u = [jnp.dot(x, w[...], preferred_element_type=F32) for w in (wua, wub)]
        act = [((g[k] * jax.nn.sigmoid(g[k])) * u[k]).astype(BF16) for k in range(2)]
        y = [jnp.dot(act[k], w[...], preferred_element_type=F32) for k, w in enumerate((wda, wdb))]
        for half in range(2):
            for c in range(ROW_TILE):
                ys_ref[pl.ds(half * ROW_TILE + c, tm, stride=2 * ROW_TILE), :] = y[half][:, c * LANE:(c + 1) * LANE]

    @pl.when(nv_ref[t] == 0)
    def _():
        ys_ref[...] = jnp.zeros(ys_ref.shape, F32)


def _moe_experts(xs, e_lo, e_hi, nv, layer, wg, wu, wd, tm=TM_MOE):
    d = wg.shape[2]
    assert d == ROW_TILE * LANE
    n_tiles = nv.shape[0]

    def w_spec(rows, cols, which):
        return pl.BlockSpec((None, None, rows, cols), lambda t, lo, hi, nv_: (layer, (lo, hi)[which][t], 0, 0))

    w_specs = [w_spec(d, D_EXPERT, k) for k in (0, 0)] + [w_spec(D_EXPERT, d, 0)]
    w_specs += [w_spec(d, D_EXPERT, k) for k in (1, 1)] + [w_spec(D_EXPERT, d, 1)]
    return pl.pallas_call(
        functools.partial(_expert_kernel, tm=tm),
        grid_spec=pltpu.PrefetchScalarGridSpec(
            num_scalar_prefetch=3,
            grid=(n_tiles,),
            in_specs=[pl.BlockSpec((tm * ROW_TILE, LANE), lambda t, lo, hi, nv_: (t, 0))] + w_specs,
            out_specs=pl.BlockSpec((tm * 2 * ROW_TILE, LANE), lambda t, lo, hi, nv_: (t, 0)),
        ),
        out_shape=jax.ShapeDtypeStruct((n_tiles * tm * 2 * ROW_TILE, LANE), F32),
        compiler_params=_cparams(("arbitrary",)),
    )(e_lo, e_hi, nv, xs, wg, wu, wd, wg, wu, wd)


def _combine_kernel(start_ref, cls_cur, rank_cur, cls_next, rank_next, ys_hbm, h_ref, cw_ref, g_ref, o_ref,
                    ybuf, sem, *, tb, n_blk, final):
    i = pl.program_id(0)
    slot = i % 2
    other = 1 - slot
    pair = 2 * ROW_TILE

    def fetch(cls_ref, rank_ref, r, sl):
        return pltpu.make_async_copy(ys_hbm.at[_tok(_slot_of(start_ref, cls_ref, rank_ref, r), pair)],
                                     ybuf.at[sl, _tok(r, pair)], sem.at[sl])

    def wait(sl):
        pltpu.make_async_copy(ys_hbm.at[pl.ds(0, tb * pair)], ybuf.at[sl], sem.at[sl]).wait()

    @pl.when(i == 0)
    def _():
        def body(r, c):
            fetch(cls_cur, rank_cur, r, 0).start()
            return c
        lax.fori_loop(0, tb, body, 0)

    wait(slot)
    for r in range(tb):
        fetch(cls_next, rank_next, r, other).start()
    yb = ybuf.at[slot]
    y_lo = jnp.concatenate([yb[pl.ds(c, tb, stride=pair), :] for c in range(ROW_TILE)], axis=1)
    y_hi = jnp.concatenate([yb[pl.ds(ROW_TILE + c, tb, stride=pair), :] for c in range(ROW_TILE)], axis=1)
    cw = cw_ref[...]
    out = h_ref[...] + cw[:, 0:1] * y_lo + cw[:, 1:2] * y_hi
    if final:
        out = _rms(out, g_ref[...])
    o_ref[...] = out

    @pl.when(i == n_blk - 1)
    def _():
        wait(other)


def _combine(h, ys, cw, g, start_row, cls_sm, rank_sm, final, tb=TB_MOE):
    n, d = h.shape
    n_blk = n // tb
    cls_x = jnp.concatenate([cls_sm, cls_sm[-1:]], axis=0)
    rank_x = jnp.concatenate([rank_sm, rank_sm[-1:]], axis=0)

    def idx_spec(shift):
        return pl.BlockSpec((None, 1, tb), lambda i, st: (i + shift, 0, 0), memory_space=pltpu.SMEM)

    return pl.pallas_call(
        functools.partial(_combine_kernel, tb=tb, n_blk=n_blk, final=final),
        grid_spec=pltpu.PrefetchScalarGridSpec(
            num_scalar_prefetch=1,
            grid=(n_blk,),
            in_specs=[idx_spec(0), idx_spec(0), idx_spec(1), idx_spec(1),
                      pl.BlockSpec(memory_space=pl.ANY),
                      pl.BlockSpec((tb, d), lambda i, st: (i, 0)),
                      pl.BlockSpec((tb, LANE), lambda i, st: (i, 0)),
                      pl.BlockSpec((1, d), lambda i, st: (0, 0))],
            out_specs=pl.BlockSpec((tb, d), lambda i, st: (i, 0)),
            scratch_shapes=[pltpu.VMEM((2, tb * 2 * ROW_TILE, LANE), F32), pltpu.SemaphoreType.DMA((2,))],
        ),
        out_shape=jax.ShapeDtypeStruct((n, d), F32),
        compiler_params=_cparams(("arbitrary",)),
    )(start_row, cls_x, rank_x, cls_x, rank_x, ys, h, cw, g.reshape(1, d))


def _moe_layer(h, layer, ln_g, w_rg, b_rg, w_re, b_re, w_gate, w_up, w_down, final_g, final):
    n = h.shape[0]
    cls, cw = _router(h, ln_g, w_rg, b_rg, w_re, b_re)
    rank, counts = _class_rank(cls)
    start_row, e_lo, e_hi, nv = _moe_schedule(counts[0], n, TM_MOE)
    cls_sm = cls[:, 0].reshape(n // TB_MOE, 1, TB_MOE)
    rank_sm = rank[:, 0].reshape(n // TB_MOE, 1, TB_MOE)
    xs = _dispatch(h, ln_g, start_row, cls_sm, rank_sm, nv.shape[0] * TM_MOE)
    ys = _moe_experts(xs, e_lo, e_hi, nv, layer, w_gate, w_up, w_down)
    return _combine(h, ys, cw, final_g, start_row, cls_sm, rank_sm, final)


def _ab_weights(w_in):
    scale = HEAD_DIM ** -0.5
    widths = (512, 512, 512, 512, 64, 64, 256, 64, 4)
    offs = [sum(widths[:j]) for j in range(len(widths) + 1)]
    q_a, k_a, v_a, q_b, k_b, v_b, q_i, k_i, w_i = [w_in[:, offs[j]:offs[j + 1]] for j in range(len(widths))]
    pad = jnp.zeros((w_in.shape[0], LANE - IDX_HEADS), w_in.dtype)
    main = jnp.concatenate([q_a * scale, k_a, v_a, q_b * scale, k_b, k_b, v_b, v_b, q_i, k_i, k_i], axis=1)
    return jnp.concatenate([main, w_i, pad], axis=1).astype(BF16)


def kernel(x, t5_bias, ln_mix, ln_ffn, ln_final, ab_w_in, ab_w_out, diff_lambda, diff_subln,
           c_w_in, c_w_out, c_rel_bias, moe_w_rg, moe_b_rg, moe_w_re, moe_b_re,
           moe_w_gate, moe_w_up, moe_w_down):
    b, s, d = x.shape
    n = b * s
    depth = ln_mix.shape[0]
    h = x.reshape(n, d)
    near, far = _t5_tables(t5_bias)
    w_gate, w_up, w_down = (w.astype(BF16) for w in (moe_w_gate, moe_w_up, moe_w_down))
    for l in range(depth):
        if l % 2 == 0:
            e = l // 2
            lam_init = 0.8 - 0.6 * math.exp(-0.3 * l)
            proj, wi = _norm_proj(h, ln_mix[l], _ab_weights(ab_w_in[e]), AB_COLS)
            proj = proj.reshape(b, s, AB_COLS)
            y_a = _attn_a(proj, near[:A_HEADS], far, diff_lambda[e], diff_subln[e], lam_init)
            wi_rows = wi.reshape(b, s, LANE)[:, :, :8].transpose(0, 2, 1)
            y_b = _attn_b(proj, wi_rows, near[A_HEADS:], far)
            h = _out_proj(h, y_a.reshape(n, -1), 0, y_b.reshape(n, -1), 0, ab_w_out[e].astype(BF16))
        else:
            o = l // 2
            scale = HEAD_DIM ** -0.5
            w_in = jnp.concatenate([c_w_in[o][:, :d] * scale, c_w_in[o][:, d:]], axis=1).astype(BF16)
            (proj,) = _norm_proj(h, ln_mix[l], w_in, 3 * d)
            y = _attn_c(proj.reshape(b, s, 3 * d), _c_tables(c_rel_bias[o])).reshape(n, d)
            h = _out_proj(h, y, 0, y, 1, c_w_out[o].astype(BF16))
        h = _moe_layer(h, l, ln_ffn[l], moe_w_rg[l], moe_b_rg[l], moe_w_re[l], moe_b_re[l],
                       w_gate, w_up, w_down, ln_final, l == depth - 1)
    return h.reshape(b, s, d)
```

```python
import functools
import math

import jax
import jax.numpy as jnp
from jax import lax
from jax.experimental import pallas as pl
from jax.experimental.pallas import tpu as pltpu

F32 = jnp.float32
BF16 = jnp.bfloat16
I32 = jnp.int32

D_MODEL = 1024
CHUNK = 64
HEAD_DIM = 64
RMS_EPS = 1e-6
A_HEADS = 4
A_V_DIM = 2 * HEAD_DIM
B_HEADS = 8
IDX_HEADS = 4
TOPK_MAX = 256
C_HEADS = D_MODEL // HEAD_DIM
C_LEFT_CHUNKS = 8
C_REL_CLIP = 128
T5_BUCKETS = 32
T5_MAX_DIST = 1024
N_GROUPS = 4
EXPERTS_PER_GROUP = 8
N_EXPERTS = N_GROUPS * EXPERTS_PER_GROUP
PAIRS_PER_GROUP = EXPERTS_PER_GROUP * (EXPERTS_PER_GROUP - 1) // 2
D_EXPERT = D_MODEL // 2
NEG = -1e30

LANE = 128
VMEM_LIMIT = 56 * 1024 * 1024

TQ = 256
NEAR_TILES = 4
TQ_C = 2 * CHUNK
WIN_C = (C_LEFT_CHUNKS + 2) * CHUNK
TM_MOE = 256

COL_QA, COL_KA, COL_VA, COL_QB, COL_KB, COL_VB, COL_QI, COL_KI, COL_WI = (
    0, 512, 1024, 1536, 2048, 2176, 2304, 2560, 2688)
AB_COLS = 2688


def _cparams(sem, vmem=VMEM_LIMIT):
    return pltpu.CompilerParams(dimension_semantics=sem, vmem_limit_bytes=vmem)


def _dot_nt(a, b):
    return lax.dot_general(a, b, (((1,), (1,)), ((), ())), preferred_element_type=F32)


def _rms(x, g):
    return (x * lax.rsqrt(jnp.mean(x * x, axis=-1, keepdims=True) + RMS_EPS)) * g


def _norm_proj_kernel(x_ref, g_ref, w_ref, o_ref, *tail_ref, n_main):
    xn = _rms(x_ref[...], g_ref[...]).astype(BF16)
    acc = jnp.dot(xn, w_ref[...], preferred_element_type=F32)
    o_ref[...] = acc[:, :n_main].astype(o_ref.dtype)
    if tail_ref:
        tail_ref[0][...] = acc[:, n_main:]


def _norm_proj(x, g, w, n_main, tm=256):
    n, d = x.shape
    n_all = w.shape[1]
    out_shape = [jax.ShapeDtypeStruct((n, n_main), BF16)]
    out_specs = [pl.BlockSpec((tm, n_main), lambda i: (i, 0))]
    if n_all > n_main:
        out_shape.append(jax.ShapeDtypeStruct((n, n_all - n_main), F32))
        out_specs.append(pl.BlockSpec((tm, n_all - n_main), lambda i: (i, 0)))
    return pl.pallas_call(
        functools.partial(_norm_proj_kernel, n_main=n_main),
        grid=(n // tm,),
        in_specs=[pl.BlockSpec((tm, d), lambda i: (i, 0)),
                  pl.BlockSpec((1, d), lambda i: (0, 0)),
                  pl.BlockSpec((d, n_all), lambda i: (0, 0))],
        out_specs=out_specs, out_shape=out_shape,
        compiler_params=_cparams(("parallel",)),
    )(x, g.reshape(1, d), w)


def _out_proj_kernel(h_ref, y0_ref, y1_ref, w0_ref, w1_ref, o_ref):
    acc = jnp.dot(y0_ref[...], w0_ref[...], preferred_element_type=F32)
    acc = acc + jnp.dot(y1_ref[...], w1_ref[...], preferred_element_type=F32)
    o_ref[...] = h_ref[...] + acc


def _out_proj(h, y0, c0, y1, c1, w, tm=512):
    n, d = h.shape
    half = d // 2
    return pl.pallas_call(
        _out_proj_kernel,
        grid=(n // tm,),
        in_specs=[pl.BlockSpec((tm, d), lambda i: (i, 0)),
                  pl.BlockSpec((tm, half), lambda i: (i, c0)),
                  pl.BlockSpec((tm, half), lambda i: (i, c1)),
                  pl.BlockSpec((half, d), lambda i: (0, 0)),
                  pl.BlockSpec((half, d), lambda i: (1, 0))],
        out_specs=pl.BlockSpec((tm, d), lambda i: (i, 0)),
        out_shape=jax.ShapeDtypeStruct((n, d), F32),
        compiler_params=_cparams(("parallel",)),
    )(h, y0, y1, w, w)


def _t5_bucket(rel):
    half = T5_BUCKETS // 2
    max_exact = half // 2
    ret = jnp.where(rel > 0, half, 0)
    n = jnp.abs(rel)
    nf = jnp.maximum(n, 1).astype(F32)
    large = max_exact + (jnp.log(nf / max_exact) / math.log(T5_MAX_DIST / max_exact)
                         * (half - max_exact)).astype(I32)
    large = jnp.minimum(large, half - 1)
    return (ret + jnp.where(n < max_exact, n, large)).astype(I32)


def _toeplitz(g, rows, width):
    length = g.shape[-1]
    assert width < length and rows <= length
    flat = jnp.tile(g, (1,) * (g.ndim - 1) + (rows,))[..., :rows * (length - 1)]
    return flat.reshape(g.shape[:-1] + (rows, length - 1))[..., :width]


def _t5_tables(t5_bias):
    span = NEAR_TILES * TQ
    length = span + TQ
    k = jnp.arange(length, dtype=I32)
    delta = jnp.where(k < span, k, k - length) - (span - TQ)
    g = t5_bias[_t5_bucket(delta)].T
    bias = _toeplitz(g, TQ, span)
    r = jnp.arange(TQ, dtype=I32)[:, None]
    kr = jnp.arange(span, dtype=I32)[None, :] - (span - TQ)
    ok = jnp.floor_divide(kr, CHUNK) <= (r // CHUNK)
    bias = jnp.where(ok[None], bias, NEG)
    near = bias.reshape(-1, TQ, NEAR_TILES, TQ).transpose(0, 2, 1, 3)
    far = t5_bias[_t5_bucket(jnp.asarray(-(span - TQ) - 1, I32))]
    return near, far


N_GRP = 4
STACK = 2 * TQ
TRIP_WIDTHS = (4, 2, 1)


def _attend(i, qs_ref, k_fn, v_fn, mask_fn, near_fn, far_cols, mf_ref, mn_ref, l_ref, acc_ref, s_ref=None):
    n_far = jnp.maximum(i - (NEAR_TILES - 1), 0)
    n_far_slots = 0 if s_ref is None else s_ref.shape[0] - NEAR_TILES
    n_far_cached = jnp.minimum(n_far, n_far_slots)

    mf_ref[...] = jnp.full(mf_ref.shape, NEG, F32)
    mn_ref[...] = jnp.full(mn_ref.shape, NEG, F32)

    def rows(g):
        return slice(g * STACK, (g + 1) * STACK)

    def logits(g, kt, jt):
        off = pl.multiple_of(kt * TQ, TQ)
        s = _dot_nt(qs_ref[rows(g), :], k_fn(g, off))
        extra = None if mask_fn is None else mask_fn(kt)[None]
        if jt is not None:
            extra = near_fn(g, jt) if extra is None else extra + near_fn(g, jt)
        if extra is not None:
            s = (s.reshape(2, TQ, TQ) + extra).reshape(STACK, TQ)
        return s

    def loops(make_fn):
        def run(lo, hi, fn):
            for width in TRIP_WIDTHS:
                groups = (hi - lo) // width

                def body(j, c, lo=lo, width=width):
                    for u in range(width):
                        fn(lo + width * j + u)
                    return c

                lax.fori_loop(0, groups, body, 0)
                lo = lo + width * groups

        if n_far_slots > 0:
            run(0, n_far_cached, make_fn(False, True))
        run(n_far_cached, n_far, make_fn(False, False))
        run(n_far, i + 1, make_fn(True, s_ref is not None))

    def slot(kt, jt):
        return NEAR_TILES + kt if jt is None else jt

    def pass1(near, cached):
        m_ref = mn_ref if near else mf_ref

        def fn(kt):
            jt = kt - (i - (NEAR_TILES - 1)) if near else None
            s = [logits(g, kt, jt) for g in range(N_GRP)]
            for g in range(N_GRP):
                if cached:
                    s_ref[slot(kt, jt), rows(g), :] = s[g]
                m_ref[rows(g), :] = jnp.maximum(m_ref[rows(g), :], jnp.maximum(s[g][:, :LANE], s[g][:, LANE:]))
        return fn

    loops(pass1)

    for g in range(N_GRP):
        m = jnp.maximum(jnp.max(mf_ref[rows(g), :], axis=-1, keepdims=True) + far_cols[g],
                        jnp.max(mn_ref[rows(g), :], axis=-1, keepdims=True))
        m_near = jnp.broadcast_to(m, (STACK, LANE))
        mn_ref[rows(g), :] = m_near
        mf_ref[rows(g), :] = m_near - far_cols[g]
    l_ref[...] = jnp.zeros(l_ref.shape, F32)
    acc_ref[...] = jnp.zeros(acc_ref.shape, F32)

    def pass2(near, cached):
        shift_ref = mn_ref if near else mf_ref

        def fn(kt):
            jt = kt - (i - (NEAR_TILES - 1)) if near else None
            off = pl.multiple_of(kt * TQ, TQ)
            grp = range(N_GRP)
            s = [s_ref[slot(kt, jt), rows(g), :] if cached else logits(g, kt, jt) for g in grp]
            p0 = [jnp.exp(s[g][:, :LANE] - shift_ref[rows(g), :]) for g in grp]
            p1 = [jnp.exp(s[g][:, LANE:] - shift_ref[rows(g), :]) for g in grp]
            for g in grp:
                l_ref[rows(g), :] += p0[g] + p1[g]
            pv = [jnp.dot(jnp.concatenate([p0[g], p1[g]], axis=1).astype(BF16), v_fn(g, off),
                          preferred_element_type=F32) for g in grp]
            for g in grp:
                acc_ref[rows(g), :] += pv[g]
        return fn

    loops(pass2)
    return [acc_ref[rows(g), :] / jnp.sum(l_ref[rows(g), :], axis=-1, keepdims=True) for g in range(N_GRP)]


def _attend_scratch():
    return [pltpu.VMEM((N_GRP * STACK, LANE), BF16)] + [pltpu.VMEM((N_GRP * STACK, LANE), F32)] * 4


def _split_lanes(x):
    lo = lax.broadcasted_iota(I32, (1, LANE), 1) < HEAD_DIM
    z = jnp.zeros_like(x)
    return jnp.concatenate([jnp.where(lo, x, z), jnp.where(lo, z, x)], axis=0)


def _attn_a_kernel(far_ref, q_ref, k_ref, v_ref, tab_ref, lam_ref, sub_ref, o_ref,
                   qs_ref, mf_ref, mn_ref, l_ref, acc_ref, s_ref, *, lam_init):
    i = pl.program_id(1)
    for g in range(A_HEADS):
        qs_ref[g * STACK:(g + 1) * STACK, :] = _split_lanes(q_ref[:, g * LANE:(g + 1) * LANE])
    outs = _attend(
        i, qs_ref,
        lambda g, off: k_ref[pl.ds(off, TQ), g * LANE:(g + 1) * LANE],
        lambda g, off: v_ref[pl.ds(off, TQ), g * LANE:(g + 1) * LANE],
        None,
        lambda g, jt: tab_ref[g, jt][None],
        [jnp.full((STACK, 1), far_ref[g], F32) for g in range(A_HEADS)],
        mf_ref, mn_ref, l_ref, acc_ref, s_ref)
    lam = lam_ref[...]
    lam_full = (jnp.exp(jnp.sum(lam[0:1] * lam[1:2], axis=-1, keepdims=True))
                - jnp.exp(jnp.sum(lam[2:3] * lam[3:4], axis=-1, keepdims=True)) + lam_init)
    for g, o in enumerate(outs):
        d = o[:TQ] - lam_full * o[TQ:]
        o_ref[:, g * LANE:(g + 1) * LANE] = (_rms(d, sub_ref[...]) * (1.0 - lam_init)).astype(o_ref.dtype)


def _attn_a(proj, near, far, lam, subln, lam_init):
    b, s, _ = proj.shape
    assert A_HEADS == N_GRP
    wide = A_HEADS * A_V_DIM
    return pl.pallas_call(
        functools.partial(_attn_a_kernel, lam_init=lam_init),
        grid_spec=pltpu.PrefetchScalarGridSpec(
            num_scalar_prefetch=1,
            grid=(b, s // TQ),
            in_specs=[
                pl.BlockSpec((None, TQ, wide), lambda b_, i, f: (b_, i, COL_QA // wide)),
                pl.BlockSpec((None, s, wide), lambda b_, i, f: (b_, 0, COL_KA // wide), pipeline_mode=pl.Buffered(1)),
                pl.BlockSpec((None, s, wide), lambda b_, i, f: (b_, 0, COL_VA // wide), pipeline_mode=pl.Buffered(1)),
                pl.BlockSpec((A_HEADS, NEAR_TILES, TQ, TQ), lambda b_, i, f: (0, 0, 0, 0),
                             pipeline_mode=pl.Buffered(1)),
                pl.BlockSpec((4, HEAD_DIM), lambda b_, i, f: (0, 0)),
                pl.BlockSpec((1, A_V_DIM), lambda b_, i, f: (0, 0)),
            ],
            out_specs=pl.BlockSpec((None, TQ, wide), lambda b_, i, f: (b_, i, 0)),
            scratch_shapes=_attend_scratch() + [pltpu.VMEM((s // TQ, N_GRP * STACK, TQ), F32)],
        ),
        out_shape=jax.ShapeDtypeStruct((b, s, wide), BF16),
        compiler_params=_cparams(("parallel", "arbitrary")),
    )(far, proj, proj, proj, near, lam, subln.reshape(1, A_V_DIM))


def _attn_b_kernel(far_ref, qb_ref, kb_ref, vb_ref, qi_ref, ki_ref, wi_ref, tab_ref, o_ref,
                   keyst_ref, selm_ref, jm_ref, qs_ref, mf_ref, mn_ref, l_ref, acc_ref, s_ref,
                   *, top_k, idx_bits):
    i = pl.program_id(1)
    n_kt = i + 1
    lane = lax.broadcasted_iota(I32, (1, LANE), 1)
    lo = lane < HEAD_DIM
    int_min = jnp.int32(-2 ** 31)
    split_heads = _split_lanes

    key_row = lax.broadcasted_iota(I32, (TQ, TQ), 0)
    chunk_gap = (key_row // CHUNK) - (lax.broadcasted_iota(I32, (TQ, TQ), 1) // CHUNK)

    def causal(kt):
        return chunk_gap <= jnp.where(kt < i, jnp.int32(TQ), jnp.int32(0))

    qi = qi_ref[...]
    qis = jnp.concatenate([split_heads(qi[:, :LANE]), split_heads(qi[:, LANE:])], axis=0)
    wi = wi_ref[...] * ((IDX_HEADS ** -0.5) * (HEAD_DIM ** -0.5))

    def score_body(kt, c):
        off = pl.multiple_of(kt * TQ, TQ)
        r = jnp.maximum(_dot_nt(ki_ref[pl.ds(off, TQ), :], qis), 0.0)
        isc = r[:, 0:TQ] * wi[0:1, :]
        for hh in range(1, IDX_HEADS):
            isc = isc + r[:, hh * TQ:(hh + 1) * TQ] * wi[hh:hh + 1, :]
        isc = jnp.where(isc == 0.0, 0.0, isc)
        isc = jnp.where(causal(kt), isc, -jnp.inf)
        bits = lax.bitcast_convert_type(isc, I32)
        keyst_ref[kt] = bits ^ ((bits >> 31) & jnp.int32(0x7FFFFFFF))
        return c

    lax.fori_loop(0, n_kt, score_body, 0)


    def count(pred_fn):
        def body(kt, acc):
            c = jnp.where(pred_fn(kt, keyst_ref[kt]), 1.0, 0.0)
            return acc + jnp.sum(c.reshape(TQ // 32, 32, TQ), axis=0)
        acc = lax.fori_loop(0, n_kt, body, jnp.zeros((32, TQ), F32))
        return jnp.sum(acc, axis=0, keepdims=True)

    def thr_body(bi, state):
        tu, nge = state
        cand_u = tu | (jnp.int32(1) << (31 - bi))
        cand_s = cand_u ^ int_min
        cnt = count(lambda kt, kk: kk >= cand_s)
        keep = cnt >= top_k
        return jnp.where(keep, cand_u, tu), jnp.where(keep, cnt, nge)

    tu, nge = lax.fori_loop(0, 32, thr_body, (jnp.zeros((1, TQ), I32),
                                              jnp.full((1, TQ), (n_kt * TQ).astype(F32), F32)))
    thr = tu ^ int_min

    jm_ref[...] = jnp.full(jm_ref.shape, 2 ** idx_bits - 1, I32)

    @pl.when(jnp.max(nge) > top_k)
    def _():
        cnt_hi = top_k - count(lambda kt, kk: kk > thr)

        def tie_body(bi, jmax):
            trial = jmax | (jnp.int32(1) << (idx_bits - 1 - bi))
            cnt = count(lambda kt, kk: (kk == thr) & ((key_row + kt * TQ) <= trial))
            return jnp.where(cnt <= cnt_hi, trial, jmax)

        jm_ref[...] = jnp.broadcast_to(lax.fori_loop(0, idx_bits, tie_body, jnp.zeros((1, TQ), I32)), jm_ref.shape)

    jmax = jm_ref[0:1, :]

    def mask_body(kt, c):
        kk = keyst_ref[kt]
        sel = ((kk > thr) | ((kk == thr) & ((key_row + kt * TQ) <= jmax))) & causal(kt)
        selm_ref[kt] = jnp.where(sel, 0.0, NEG).T
        return c

    lax.fori_loop(0, n_kt, mask_body, 0)

    for g in range(N_GRP):
        qs_ref[g * STACK:(g + 1) * STACK, :] = split_heads(qb_ref[:, g * LANE:(g + 1) * LANE])
    top_rows = lax.broadcasted_iota(I32, (STACK, 1), 0) < TQ
    outs = _attend(
        i, qs_ref,
        lambda g, off: kb_ref[pl.ds(off, TQ), :],
        lambda g, off: vb_ref[pl.ds(off, TQ), :],
        lambda kt: selm_ref[kt],
        lambda g, jt: tab_ref[2 * g:2 * g + 2, jt],
        [jnp.where(top_rows, far_ref[A_HEADS + 2 * g], far_ref[A_HEADS + 2 * g + 1]) for g in range(N_GRP)],
        mf_ref, mn_ref, l_ref, acc_ref, s_ref)
    for g, o in enumerate(outs):
        o_ref[:, g * LANE:(g + 1) * LANE] = jnp.where(lo, o[:TQ], o[TQ:]).astype(o_ref.dtype)


B_CACHE_TILES = NEAR_TILES + 8


def _attn_b(proj, wi, near_b, far):
    b, s, _ = proj.shape
    top_k = min(TOPK_MAX, s // 4)
    assert top_k <= TQ and s % TQ == 0 and (s & (s - 1)) == 0
    n_kt = s // TQ
    assert B_HEADS == 2 * N_GRP and s < 2 ** 16
    qb_w = B_HEADS * HEAD_DIM
    qi_w = IDX_HEADS * HEAD_DIM
    return pl.pallas_call(
        functools.partial(_attn_b_kernel, top_k=float(top_k), idx_bits=int(math.log2(s))),
        grid_spec=pltpu.PrefetchScalarGridSpec(
            num_scalar_prefetch=1,
            grid=(b, n_kt),
            in_specs=[
                pl.BlockSpec((None, TQ, qb_w), lambda b_, i, f: (b_, i, COL_QB // qb_w)),
                pl.BlockSpec((None, s, LANE), lambda b_, i, f: (b_, 0, COL_KB // LANE)),
                pl.BlockSpec((None, s, LANE), lambda b_, i, f: (b_, 0, COL_VB // LANE)),
                pl.BlockSpec((None, TQ, qi_w), lambda b_, i, f: (b_, i, COL_QI // qi_w)),
                pl.BlockSpec((None, s, LANE), lambda b_, i, f: (b_, 0, COL_KI // LANE)),
                pl.BlockSpec((None, 8, TQ), lambda b_, i, f: (b_, 0, i)),
                pl.BlockSpec((B_HEADS, NEAR_TILES, TQ, TQ), lambda b_, i, f: (0, 0, 0, 0),
                             pipeline_mode=pl.Buffered(1)),
            ],
            out_specs=pl.BlockSpec((None, TQ, qb_w), lambda b_, i, f: (b_, i, 0)),
            scratch_shapes=[pltpu.VMEM((n_kt, TQ, TQ), I32), pltpu.VMEM((n_kt, TQ, TQ), F32),
                            pltpu.VMEM((8, TQ), I32)] + _attend_scratch()
            + [pltpu.VMEM((min(n_kt, B_CACHE_TILES), N_GRP * STACK, TQ), F32)],
        ),
        out_shape=jax.ShapeDtypeStruct((b, s, qb_w), BF16),
        compiler_params=_cparams(("parallel", "arbitrary")),
    )(far, proj, proj, proj, proj, proj, wi, near_b)


PAIRS_C = 4


def _c_tables(rel_table):
    n_var = WIN_C // TQ_C
    length = WIN_C + TQ_C
    k = jnp.arange(length, dtype=I32)
    cr = jnp.where(k < WIN_C, k, k - length)
    off = (jnp.arange(n_var, dtype=I32) * TQ_C)[:, None]
    rel_idx = jnp.clip(off - cr[None, :], -C_REL_CLIP, C_REL_CLIP) + C_REL_CLIP
    g = rel_table[rel_idx].transpose(2, 0, 1)
    bias = _toeplitz(g, TQ_C, WIN_C)
    r = jnp.arange(TQ_C, dtype=I32)[None, :, None]
    c = jnp.arange(WIN_C, dtype=I32)[None, None, :]
    kc = jnp.floor_divide(c - off[:, :, None], CHUNK)
    ok = (kc <= r // CHUNK) & (kc >= r // CHUNK - C_LEFT_CHUNKS)
    return jnp.where(ok[None], bias, NEG)


def _attn_c_kernel(q_ref, k_ref, v_ref, tab_ref, o_ref):
    i = pl.program_id(2)
    n_var = WIN_C // TQ_C
    lane = lax.broadcasted_iota(I32, (1, LANE), 1)
    lo = lane < HEAD_DIM
    var = jnp.minimum(i, n_var - 1)
    ks = pl.multiple_of(jnp.maximum(i - (n_var - 1), 0) * TQ_C, TQ_C)
    pairs = range(PAIRS_C)
    cols = [slice(pp * LANE, (pp + 1) * LANE) for pp in pairs]
    s = [_dot_nt(_split_lanes(q_ref[:, cols[pp]]), k_ref[pl.ds(ks, WIN_C), cols[pp]]) for pp in pairs]
    s = [s[pp] + jnp.concatenate([tab_ref[2 * pp, var], tab_ref[2 * pp + 1, var]], axis=0) for pp in pairs]
    m = [jnp.max(s[pp], axis=-1, keepdims=True) for pp in pairs]
    p = [jnp.exp(s[pp] - m[pp]) for pp in pairs]
    l = [jnp.sum(p[pp], axis=-1, keepdims=True) for pp in pairs]
    pv = [jnp.dot(p[pp].astype(BF16), v_ref[pl.ds(ks, WIN_C), cols[pp]], preferred_element_type=F32) for pp in pairs]
    for pp in pairs:
        o = pv[pp] / l[pp]
        o_ref[:, cols[pp]] = jnp.where(lo, o[:TQ_C], o[TQ_C:]).astype(o_ref.dtype)


def _attn_c(proj, tab):
    b, s, _ = proj.shape
    assert s >= WIN_C and s % TQ_C == 0
    n_grp = C_HEADS // 2 // PAIRS_C
    n_var = WIN_C // TQ_C
    wblk = PAIRS_C * LANE
    tab = tab.reshape(n_grp, 2 * PAIRS_C, n_var, TQ_C, WIN_C)
    return pl.pallas_call(
        _attn_c_kernel,
        grid=(n_grp, b, s // TQ_C),
        in_specs=[
            pl.BlockSpec((None, TQ_C, wblk), lambda h, b_, i: (b_, i, h)),
            pl.BlockSpec((None, s, wblk), lambda h, b_, i: (b_, 0, n_grp + h)),
            pl.BlockSpec((None, s, wblk), lambda h, b_, i: (b_, 0, 2 * n_grp + h)),
            pl.BlockSpec((None, 2 * PAIRS_C, n_var, TQ_C, WIN_C), lambda h, b_, i: (h, 0, 0, 0, 0),
                         pipeline_mode=pl.Buffered(1)),
        ],
        out_specs=pl.BlockSpec((None, TQ_C, wblk), lambda h, b_, i: (b_, i, h)),
        out_shape=jax.ShapeDtypeStruct((b, s, D_MODEL), BF16),
        compiler_params=_cparams(("parallel", "parallel", "arbitrary")),
    )(proj, proj, proj, tab)


ROW_TILE = 8


def _rows_to_tiles(ref, x):
    rows = x.shape[0]
    for c in range(ROW_TILE):
        ref[pl.ds(c, rows, stride=ROW_TILE), :] = x[:, c * LANE:(c + 1) * LANE]


def _tiles_to_rows(ref, rows):
    return jnp.concatenate([ref[pl.ds(c, rows, stride=ROW_TILE), :] for c in range(ROW_TILE)], axis=1)


def _router_kernel(h_ref, g_ref, wr_ref, br_ref, ei_ref, cw_ref):
    xn = _rms(h_ref[...], g_ref[...])
    logits = jnp.dot(xn, wr_ref[...], preferred_element_type=F32,
                     precision=lax.Precision.HIGHEST) + br_ref[...]
    lane = lax.broadcasted_iota(I32, (1, LANE), 1)
    lane_f = lane.astype(F32)
    big = float(LANE)
    is_g = lane < N_GROUPS
    lg = jnp.where(is_g, logits, -jnp.inf)
    gmax = jnp.max(lg, axis=-1, keepdims=True)
    g_sel = jnp.min(jnp.where(lg == gmax, lane_f, big), axis=-1, keepdims=True)
    p_gsel = 1.0 / jnp.sum(jnp.where(is_g, jnp.exp(logits - gmax), 0.0), axis=-1, keepdims=True)
    e_grp = ((lane - N_GROUPS) // EXPERTS_PER_GROUP).astype(F32)
    in_sel = (lane >= N_GROUPS) & (lane < N_GROUPS + N_EXPERTS) & (e_grp == g_sel)
    le = jnp.where(in_sel, logits, -jnp.inf)
    m1 = jnp.max(le, axis=-1, keepdims=True)
    i1 = jnp.min(jnp.where(le == m1, lane_f, big), axis=-1, keepdims=True)
    le2 = jnp.where(lane_f == i1, -jnp.inf, le)
    m2 = jnp.max(le2, axis=-1, keepdims=True)
    i2 = jnp.min(jnp.where(le2 == m2, lane_f, big), axis=-1, keepdims=True)
    t = jnp.exp(m2 - m1)
    c1 = p_gsel / (1.0 + t)
    c2 = p_gsel * t / (1.0 + t)
    a = jnp.minimum(i1, i2) - N_GROUPS - EXPERTS_PER_GROUP * g_sel
    b = jnp.maximum(i1, i2) - N_GROUPS - EXPERTS_PER_GROUP * g_sel
    pair = a * (2 * EXPERTS_PER_GROUP - 1 - a) * 0.5 + (b - a - 1.0)
    cls = g_sel * PAIRS_PER_GROUP + pair
    ei_ref[...] = jnp.broadcast_to(cls, ei_ref.shape).astype(I32)
    first_is_lo = i1 < i2
    cw_ref[...] = jnp.where(lane == 0, jnp.where(first_is_lo, c1, c2), jnp.where(first_is_lo, c2, c1))


def _router(h, g, w_rg, b_rg, w_re, b_re, tm=256):
    n, d = h.shape
    pad = LANE - N_GROUPS - N_EXPERTS
    wr = jnp.concatenate([w_rg, w_re, jnp.zeros((d, pad), F32)], axis=1)
    br = jnp.concatenate([b_rg, b_re, jnp.zeros((pad,), F32)]).reshape(1, LANE)
    return pl.pallas_call(
        _router_kernel,
        grid=(n // tm,),
        in_specs=[pl.BlockSpec((tm, d), lambda i: (i, 0)),
                  pl.BlockSpec((1, d), lambda i: (0, 0)),
                  pl.BlockSpec((d, LANE), lambda i: (0, 0)),
                  pl.BlockSpec((1, LANE), lambda i: (0, 0))],
        out_specs=[pl.BlockSpec((tm, LANE), lambda i: (i, 0)),
                   pl.BlockSpec((tm, LANE), lambda i: (i, 0))],
        out_shape=[jax.ShapeDtypeStruct((n, LANE), I32),
                   jax.ShapeDtypeStruct((n, LANE), F32)],
        compiler_params=_cparams(("parallel",)),
    )(h, g.reshape(1, d), wr, br)


N_CLASSES = N_GROUPS * PAIRS_PER_GROUP
_PAIR_AB = [(a, b) for a in range(EXPERTS_PER_GROUP) for b in range(a + 1, EXPERTS_PER_GROUP)]
CLASS_EXPERT_LO = [g * EXPERTS_PER_GROUP + a for g in range(N_GROUPS) for a, _ in _PAIR_AB]
CLASS_EXPERT_HI = [g * EXPERTS_PER_GROUP + b for g in range(N_GROUPS) for _, b in _PAIR_AB]
TB_MOE = 256


def _rank_kernel(cls_ref, rank_ref, cnt_ref, carry_ref):
    tb = cls_ref.shape[0]

    @pl.when(pl.program_id(0) == 0)
    def _():
        carry_ref[...] = jnp.zeros(carry_ref.shape, F32)

    onehot = cls_ref[...] == lax.broadcasted_iota(I32, (tb, LANE), 1)
    tril = lax.broadcasted_iota(I32, (tb, tb), 0) >= lax.broadcasted_iota(I32, (tb, tb), 1)
    pref = jnp.dot(jnp.where(tril, 1.0, 0.0).astype(BF16), jnp.where(onehot, 1.0, 0.0).astype(BF16),
                   preferred_element_type=F32)
    carry = carry_ref[...]
    rank = jnp.sum(jnp.where(onehot, pref + carry, 0.0), axis=-1, keepdims=True) - 1.0
    rank_ref[...] = jnp.broadcast_to(rank, rank_ref.shape).astype(I32)
    carry = carry + pref[tb - 1:tb, :]
    carry_ref[...] = carry
    cnt_ref[...] = carry


def _class_rank(cls, tb=512):
    n = cls.shape[0]
    return pl.pallas_call(
        _rank_kernel,
        grid=(n // tb,),
        in_specs=[pl.BlockSpec((tb, LANE), lambda i: (i, 0))],
        out_specs=[pl.BlockSpec((tb, LANE), lambda i: (i, 0)), pl.BlockSpec((1, LANE), lambda i: (0, 0))],
        out_shape=[jax.ShapeDtypeStruct((n, LANE), I32), jax.ShapeDtypeStruct((1, LANE), F32)],
        scratch_shapes=[pltpu.VMEM((1, LANE), F32)],
        compiler_params=_cparams(("arbitrary",)),
    )(cls)


def _moe_schedule(counts, n_tok, tm):
    counts = counts[:N_CLASSES].astype(I32)
    tiles_per = (counts + tm - 1) // tm
    tile_end = jnp.cumsum(tiles_per)
    tile_start = tile_end - tiles_per
    start_row = jnp.zeros((LANE,), I32).at[:N_CLASSES].set(tile_start * tm)
    n_tiles = n_tok // tm + N_CLASSES
    t = jnp.arange(n_tiles, dtype=I32)
    tc = jnp.searchsorted(tile_end, t, side="right").astype(I32)
    live = tc < N_CLASSES
    tc = jnp.minimum(tc, N_CLASSES - 1)
    nv = jnp.where(live, jnp.clip(counts[tc] - (t - tile_start[tc]) * tm, 0, tm), 0).astype(I32)
    e_lo = jnp.asarray(CLASS_EXPERT_LO, I32)[tc]
    e_hi = jnp.asarray(CLASS_EXPERT_HI, I32)[tc]
    return start_row, e_lo, e_hi, nv


def _tok(i, width=ROW_TILE):
    return pl.ds(i * width, width)


def _slot_of(start_ref, cls_ref, rank_ref, r):
    return start_ref[cls_ref[0, r]] + rank_ref[0, r]


def _dispatch_kernel(start_ref, cls_ref, rank_ref, h_ref, g_ref, xs_init, xs_hbm, sbuf, sem, *, tb, n_blk):
    del xs_init
    i = pl.program_id(0)
    slot = i % 2

    def wait(sl):
        pltpu.make_async_copy(sbuf.at[sl], xs_hbm.at[pl.ds(0, tb * ROW_TILE)], sem.at[sl]).wait()

    _rows_to_tiles(sbuf.at[slot], _rms(h_ref[...], g_ref[...]))
    for r in range(tb):
        pltpu.make_async_copy(sbuf.at[slot, _tok(r)], xs_hbm.at[_tok(_slot_of(start_ref, cls_ref, rank_ref, r))],
                              sem.at[slot]).start()

    @pl.when(i > 0)
    def _():
        wait(1 - slot)

    @pl.when(i == n_blk - 1)
    def _():
        wait(slot)


def _dispatch(h, g, start_row, cls_sm, rank_sm, n_slots, tb=TB_MOE):
    n, d = h.shape
    n_blk = cls_sm.shape[0]
    xs0 = jnp.zeros((n_slots * ROW_TILE, LANE), F32)
    idx_spec = pl.BlockSpec((None, 1, tb), lambda i, st: (i, 0, 0), memory_space=pltpu.SMEM)
    return pl.pallas_call(
        functools.partial(_dispatch_kernel, tb=tb, n_blk=n_blk),
        grid_spec=pltpu.PrefetchScalarGridSpec(
            num_scalar_prefetch=1,
            grid=(n_blk,),
            in_specs=[idx_spec, idx_spec,
                      pl.BlockSpec((tb, d), lambda i, st: (i, 0)),
                      pl.BlockSpec((1, d), lambda i, st: (0, 0)),
                      pl.BlockSpec(memory_space=pl.ANY)],
            out_specs=pl.BlockSpec(memory_space=pl.ANY),
            scratch_shapes=[pltpu.VMEM((2, tb * ROW_TILE, LANE), F32), pltpu.SemaphoreType.DMA((2,))],
        ),
        out_shape=jax.ShapeDtypeStruct(xs0.shape, F32),
        input_output_aliases={5: 0},
        compiler_params=_cparams(("arbitrary",)),
    )(start_row, cls_sm, rank_sm, h, g.reshape(1, d), xs0)


def _expert_kernel(elo_ref, ehi_ref, nv_ref, xs_ref, wga, wua, wda, wgb, wub, wdb, ys_ref, *, tm):
    t = pl.program_id(0)

    @pl.when(nv_ref[t] > 0)
    def _():
        x = _tiles_to_rows(xs_ref, tm).astype(BF16)
        g = [jnp.dot(x, w[...], preferred_element_type=F32) for w in (wga, wgb)]
        u = [jnp.dot(x, w[...], preferred_element_type=F32) for w in (wua, wub)]
        act = [((g[k] * jax.nn.sigmoid(g[k])) * u[k]).astype(BF16) for k in range(2)]
        y = [jnp.dot(act[k], w[...], preferred_element_type=F32) for k, w in enumerate((wda, wdb))]
        for half in range(2):
            for c in range(ROW_TILE):
                ys_ref[pl.ds(half * ROW_TILE + c, tm, stride=2 * ROW_TILE), :] = y[half][:, c * LANE:(c + 1) * LANE]

    @pl.when(nv_ref[t] == 0)
    def _():
        ys_ref[...] = jnp.zeros(ys_ref.shape, F32)


def _moe_experts(xs, e_lo, e_hi, nv, layer, wg, wu, wd, tm=TM_MOE):
    d = wg.shape[2]
    assert d == ROW_TILE * LANE
    n_tiles = nv.shape[0]

    def w_spec(rows, cols, which):
        return pl.BlockSpec((None, None, rows, cols), lambda t, lo, hi, nv_: (layer, (lo, hi)[which][t], 0, 0))

    w_specs = [w_spec(d, D_EXPERT, k) for k in (0, 0)] + [w_spec(D_EXPERT, d, 0)]
    w_specs += [w_spec(d, D_EXPERT, k) for k in (1, 1)] + [w_spec(D_EXPERT, d, 1)]
    return pl.pallas_call(
        functools.partial(_expert_kernel, tm=tm),
        grid_spec=pltpu.PrefetchScalarGridSpec(
            num_scalar_prefetch=3,
            grid=(n_tiles,),
            in_specs=[pl.BlockSpec((tm * ROW_TILE, LANE), lambda t, lo, hi, nv_: (t, 0))] + w_specs,
            out_specs=pl.BlockSpec((tm * 2 * ROW_TILE, LANE), lambda t, lo, hi, nv_: (t, 0)),
        ),
        out_shape=jax.ShapeDtypeStruct((n_tiles * tm * 2 * ROW_TILE, LANE), F32),
        compiler_params=_cparams(("arbitrary",)),
    )(e_lo, e_hi, nv, xs, wg, wu, wd, wg, wu, wd)


def _combine_kernel(start_ref, cls_cur, rank_cur, cls_next, rank_next, ys_hbm, h_ref, cw_ref, g_ref, o_ref,
                    ybuf, sem, *, tb, n_blk, final):
    i = pl.program_id(0)
    slot = i % 2
    other = 1 - slot
    pair = 2 * ROW_TILE

    def fetch(cls_ref, rank_ref, r, sl):
        return pltpu.make_async_copy(ys_hbm.at[_tok(_slot_of(start_ref, cls_ref, rank_ref, r), pair)],
                                     ybuf.at[sl, _tok(r, pair)], sem.at[sl])

    def wait(sl):
        pltpu.make_async_copy(ys_hbm.at[pl.ds(0, tb * pair)], ybuf.at[sl], sem.at[sl]).wait()

    @pl.when(i == 0)
    def _():
        def body(r, c):
            fetch(cls_cur, rank_cur, r, 0).start()
            return c
        lax.fori_loop(0, tb, body, 0)

    wait(slot)
    for r in range(tb):
        fetch(cls_next, rank_next, r, other).start()
    yb = ybuf.at[slot]
    y_lo = jnp.concatenate([yb[pl.ds(c, tb, stride=pair), :] for c in range(ROW_TILE)], axis=1)
    y_hi = jnp.concatenate([yb[pl.ds(ROW_TILE + c, tb, stride=pair), :] for c in range(ROW_TILE)], axis=1)
    cw = cw_ref[...]
    out = h_ref[...] + cw[:, 0:1] * y_lo + cw[:, 1:2] * y_hi
    if final:
        out = _rms(out, g_ref[...])
    o_ref[...] = out

    @pl.when(i == n_blk - 1)
    def _():
        wait(other)


def _combine(h, ys, cw, g, start_row, cls_sm, rank_sm, final, tb=TB_MOE):
    n, d = h.shape
    n_blk = n // tb
    cls_x = jnp.concatenate([cls_sm, cls_sm[-1:]], axis=0)
    rank_x = jnp.concatenate([rank_sm, rank_sm[-1:]], axis=0)

    def idx_spec(shift):
        return pl.BlockSpec((None, 1, tb), lambda i, st: (i + shift, 0, 0), memory_space=pltpu.SMEM)

    return pl.pallas_call(
        functools.partial(_combine_kernel, tb=tb, n_blk=n_blk, final=final),
        grid_spec=pltpu.PrefetchScalarGridSpec(
            num_scalar_prefetch=1,
            grid=(n_blk,),
            in_specs=[idx_spec(0), idx_spec(0), idx_spec(1), idx_spec(1),
                      pl.BlockSpec(memory_space=pl.ANY),
                      pl.BlockSpec((tb, d), lambda i, st: (i, 0)),
                      pl.BlockSpec((tb, LANE), lambda i, st: (i, 0)),
                      pl.BlockSpec((1, d), lambda i, st: (0, 0))],
            out_specs=pl.BlockSpec((tb, d), lambda i, st: (i, 0)),
            scratch_shapes=[pltpu.VMEM((2, tb * 2 * ROW_TILE, LANE), F32), pltpu.SemaphoreType.DMA((2,))],
        ),
        out_shape=jax.ShapeDtypeStruct((n, d), F32),
        compiler_params=_cparams(("arbitrary",)),
    )(start_row, cls_x, rank_x, cls_x, rank_x, ys, h, cw, g.reshape(1, d))


def _moe_layer(h, layer, ln_g, w_rg, b_rg, w_re, b_re, w_gate, w_up, w_down, final_g, final):
    n = h.shape[0]
    cls, cw = _router(h, ln_g, w_rg, b_rg, w_re, b_re)
    rank, counts = _class_rank(cls)
    start_row, e_lo, e_hi, nv = _moe_schedule(counts[0], n, TM_MOE)
    cls_sm = cls[:, 0].reshape(n // TB_MOE, 1, TB_MOE)
    rank_sm = rank[:, 0].reshape(n // TB_MOE, 1, TB_MOE)
    xs = _dispatch(h, ln_g, start_row, cls_sm, rank_sm, nv.shape[0] * TM_MOE)
    ys = _moe_experts(xs, e_lo, e_hi, nv, layer, w_gate, w_up, w_down)
    return _combine(h, ys, cw, final_g, start_row, cls_sm, rank_sm, final)


def _ab_weights(w_in):
    scale = HEAD_DIM ** -0.5
    widths = (512, 512, 512, 512, 64, 64, 256, 64, 4)
    offs = [sum(widths[:j]) for j in range(len(widths) + 1)]
    q_a, k_a, v_a, q_b, k_b, v_b, q_i, k_i, w_i = [w_in[:, offs[j]:offs[j + 1]] for j in range(len(widths))]
    pad = jnp.zeros((w_in.shape[0], LANE - IDX_HEADS), w_in.dtype)
    main = jnp.concatenate([q_a * scale, k_a, v_a, q_b * scale, k_b, k_b, v_b, v_b, q_i, k_i, k_i], axis=1)
    return jnp.concatenate([main, w_i, pad], axis=1).astype(BF16)


def kernel(x, t5_bias, ln_mix, ln_ffn, ln_final, ab_w_in, ab_w_out, diff_lambda, diff_subln,
           c_w_in, c_w_out, c_rel_bias, moe_w_rg, moe_b_rg, moe_w_re, moe_b_re,
           moe_w_gate, moe_w_up, moe_w_down):
    b, s, d = x.shape
    n = b * s
    depth = ln_mix.shape[0]
    h = x.reshape(n, d)
    near, far = _t5_tables(t5_bias)
    w_gate, w_up, w_down = (w.astype(BF16) for w in (moe_w_gate, moe_w_up, moe_w_down))
    for l in range(depth):
        if l % 2 == 0:
            e = l // 2
            lam_init = 0.8 - 0.6 * math.exp(-0.3 * l)
            proj, wi = _norm_proj(h, ln_mix[l], _ab_weights(ab_w_in[e]), AB_COLS)
            proj = proj.reshape(b, s, AB_COLS)
            y_a = _attn_a(proj, near[:A_HEADS], far, diff_lambda[e], diff_subln[e], lam_init)
            wi_rows = wi.reshape(b, s, LANE)[:, :, :8].transpose(0, 2, 1)
            y_b = _attn_b(proj, wi_rows, near[A_HEADS:], far)
            h = _out_proj(h, y_a.reshape(n, -1), 0, y_b.reshape(n, -1), 0, ab_w_out[e].astype(BF16))
        else:
            o = l // 2
            scale = HEAD_DIM ** -0.5
            w_in = jnp.concatenate([c_w_in[o][:, :d] * scale, c_w_in[o][:, d:]], axis=1).astype(BF16)
            (proj,) = _norm_proj(h, ln_mix[l], w_in, 3 * d)
            y = _attn_c(proj.reshape(b, s, 3 * d), _c_tables(c_rel_bias[o])).reshape(n, d)
            h = _out_proj(h, y, 0, y, 1, c_w_out[o].astype(BF16))
        h = _moe_layer(h, l, ln_ffn[l], moe_w_rg[l], moe_b_rg[l], moe_w_re[l], moe_b_re[l],
                       w_gate, w_up, w_down, ln_final, l == depth - 1)
    return h.reshape(b, s, d)
```

```python
import functools
import math

import jax
import jax.numpy as jnp
from jax import lax
from jax.experimental import pallas as pl
from jax.experimental.pallas import tpu as pltpu

F32 = jnp.float32
BF16 = jnp.bfloat16
I32 = jnp.int32

D_MODEL = 1024
CHUNK = 64
HEAD_DIM = 64
RMS_EPS = 1e-6
A_HEADS = 4
A_V_DIM = 2 * HEAD_DIM
B_HEADS = 8
IDX_HEADS = 4
TOPK_MAX = 256
C_HEADS = D_MODEL // HEAD_DIM
C_LEFT_CHUNKS = 8
C_REL_CLIP = 128
T5_BUCKETS = 32
T5_MAX_DIST = 1024
N_GROUPS = 4
EXPERTS_PER_GROUP = 8
N_EXPERTS = N_GROUPS * EXPERTS_PER_GROUP
PAIRS_PER_GROUP = EXPERTS_PER_GROUP * (EXPERTS_PER_GROUP - 1) // 2
D_EXPERT = D_MODEL // 2
NEG = -1e30

LANE = 128
VMEM_LIMIT = 56 * 1024 * 1024

TQ = 256
NEAR_TILES = 4
TQ_C = 2 * CHUNK
WIN_C = (C_LEFT_CHUNKS + 2) * CHUNK
TM_MOE = 256

COL_QA, COL_KA, COL_VA, COL_QB, COL_KB, COL_VB, COL_QI, COL_KI, COL_WI = (
    0, 512, 1024, 1536, 2048, 2176, 2304, 2560, 2688)
AB_COLS = 2688


def _cparams(sem, vmem=VMEM_LIMIT):
    return pltpu.CompilerParams(dimension_semantics=sem, vmem_limit_bytes=vmem)


def _dot_nt(a, b):
    return lax.dot_general(a, b, (((1,), (1,)), ((), ())), preferred_element_type=F32)


def _rms(x, g):
    return (x * lax.rsqrt(jnp.mean(x * x, axis=-1, keepdims=True) + RMS_EPS)) * g


def _norm_proj_kernel(x_ref, g_ref, w_ref, o_ref, *tail_ref, n_main):
    xn = _rms(x_ref[...], g_ref[...]).astype(BF16)
    acc = jnp.dot(xn, w_ref[...], preferred_element_type=F32)
    o_ref[...] = acc[:, :n_main].astype(o_ref.dtype)
    if tail_ref:
        tail_ref[0][...] = acc[:, n_main:]


def _norm_proj(x, g, w, n_main, tm=256):
    n, d = x.shape
    n_all = w.shape[1]
    out_shape = [jax.ShapeDtypeStruct((n, n_main), BF16)]
    out_specs = [pl.BlockSpec((tm, n_main), lambda i: (i, 0))]
    if n_all > n_main:
        out_shape.append(jax.ShapeDtypeStruct((n, n_all - n_main), F32))
        out_specs.append(pl.BlockSpec((tm, n_all - n_main), lambda i: (i, 0)))
    return pl.pallas_call(
        functools.partial(_norm_proj_kernel, n_main=n_main),
        grid=(n // tm,),
        in_specs=[pl.BlockSpec((tm, d), lambda i: (i, 0)),
                  pl.BlockSpec((1, d), lambda i: (0, 0)),
                  pl.BlockSpec((d, n_all), lambda i: (0, 0))],
        out_specs=out_specs, out_shape=out_shape,
        compiler_params=_cparams(("parallel",)),
    )(x, g.reshape(1, d), w)


def _out_proj_kernel(h_ref, y0_ref, y1_ref, w0_ref, w1_ref, o_ref):
    acc = jnp.dot(y0_ref[...], w0_ref[...], preferred_element_type=F32)
    acc = acc + jnp.dot(y1_ref[...], w1_ref[...], preferred_element_type=F32)
    o_ref[...] = h_ref[...] + acc


def _out_proj(h, y0, c0, y1, c1, w, tm=512):
    n, d = h.shape
    half = d // 2
    return pl.pallas_call(
        _out_proj_kernel,
        grid=(n // tm,),
        in_specs=[pl.BlockSpec((tm, d), lambda i: (i, 0)),
                  pl.BlockSpec((tm, half), lambda i: (i, c0)),
                  pl.BlockSpec((tm, half), lambda i: (i, c1)),
                  pl.BlockSpec((half, d), lambda i: (0, 0)),
                  pl.BlockSpec((half, d), lambda i: (1, 0))],
        out_specs=pl.BlockSpec((tm, d), lambda i: (i, 0)),
        out_shape=jax.ShapeDtypeStruct((n, d), F32),
        compiler_params=_cparams(("parallel",)),
    )(h, y0, y1, w, w)


def _t5_bucket(rel):
    half = T5_BUCKETS // 2
    max_exact = half // 2
    ret = jnp.where(rel > 0, half, 0)
    n = jnp.abs(rel)
    nf = jnp.maximum(n, 1).astype(F32)
    large = max_exact + (jnp.log(nf / max_exact) / math.log(T5_MAX_DIST / max_exact)
                         * (half - max_exact)).astype(I32)
    large = jnp.minimum(large, half - 1)
    return (ret + jnp.where(n < max_exact, n, large)).astype(I32)


def _toeplitz(g, rows, width):
    length = g.shape[-1]
    assert width < length and rows <= length
    flat = jnp.tile(g, (1,) * (g.ndim - 1) + (rows,))[..., :rows * (length - 1)]
    return flat.reshape(g.shape[:-1] + (rows, length - 1))[..., :width]


def _t5_tables(t5_bias):
    span = NEAR_TILES * TQ
    length = span + TQ
    k = jnp.arange(length, dtype=I32)
    delta = jnp.where(k < span, k, k - length) - (span - TQ)
    g = t5_bias[_t5_bucket(delta)].T
    bias = _toeplitz(g, TQ, span)
    r = jnp.arange(TQ, dtype=I32)[:, None]
    kr = jnp.arange(span, dtype=I32)[None, :] - (span - TQ)
    ok = jnp.floor_divide(kr, CHUNK) <= (r // CHUNK)
    bias = jnp.where(ok[None], bias, NEG)
    near = bias.reshape(-1, TQ, NEAR_TILES, TQ).transpose(0, 2, 1, 3)
    far = t5_bias[_t5_bucket(jnp.asarray(-(span - TQ) - 1, I32))]
    return near, far


N_GRP = 4
STACK = 2 * TQ
TRIP_WIDTHS = (4, 2, 1)


def _tile_loop(lo, hi, fn, widths=TRIP_WIDTHS):
    for width in widths:
        groups = (hi - lo) // width

        def body(j, c, lo=lo, width=width):
            for u in range(width):
                fn(lo + width * j + u)
            return c

        lax.fori_loop(0, groups, body, 0)
        lo = lo + width * groups


def _attend(i, qs_ref, k_fn, v_fn, mask_fn, near_fn, far_cols, mf_ref, mn_ref, l_ref, acc_ref, s_ref=None):
    n_far = jnp.maximum(i - (NEAR_TILES - 1), 0)
    n_far_slots = 0 if s_ref is None else s_ref.shape[0] - NEAR_TILES
    n_far_cached = jnp.minimum(n_far, n_far_slots)

    mf_ref[...] = jnp.full(mf_ref.shape, NEG, F32)
    mn_ref[...] = jnp.full(mn_ref.shape, NEG, F32)

    def rows(g):
        return slice(g * STACK, (g + 1) * STACK)

    def logits(g, kt, jt):
        off = pl.multiple_of(kt * TQ, TQ)
        s = _dot_nt(qs_ref[rows(g), :], k_fn(g, off))
        extra = None if mask_fn is None else mask_fn(kt)[None]
        if jt is not None:
            extra = near_fn(g, jt) if extra is None else extra + near_fn(g, jt)
        if extra is not None:
            s = (s.reshape(2, TQ, TQ) + extra).reshape(STACK, TQ)
        return s

    def loops(make_fn):
        if n_far_slots > 0:
            _tile_loop(0, n_far_cached, make_fn(False, True))
        _tile_loop(n_far_cached, n_far, make_fn(False, False))
        _tile_loop(n_far, i + 1, make_fn(True, s_ref is not None))

    def slot(kt, jt):
        return NEAR_TILES + kt if jt is None else jt

    def pass1(near, cached):
        m_ref = mn_ref if near else mf_ref

        def fn(kt):
            jt = kt - (i - (NEAR_TILES - 1)) if near else None
            s = [logits(g, kt, jt) for g in range(N_GRP)]
            for g in range(N_GRP):
                if cached:
                    s_ref[slot(kt, jt), rows(g), :] = s[g]
                m_ref[rows(g), :] = jnp.maximum(m_ref[rows(g), :], jnp.maximum(s[g][:, :LANE], s[g][:, LANE:]))
        return fn

    loops(pass1)

    for g in range(N_GRP):
        m = jnp.maximum(jnp.max(mf_ref[rows(g), :], axis=-1, keepdims=True) + far_cols[g],
                        jnp.max(mn_ref[rows(g), :], axis=-1, keepdims=True))
        m_near = jnp.broadcast_to(m, (STACK, LANE))
        mn_ref[rows(g), :] = m_near
        mf_ref[rows(g), :] = m_near - far_cols[g]
    l_ref[...] = jnp.zeros(l_ref.shape, F32)
    acc_ref[...] = jnp.zeros(acc_ref.shape, F32)

    def pass2(near, cached):
        shift_ref = mn_ref if near else mf_ref

        def fn(kt):
            jt = kt - (i - (NEAR_TILES - 1)) if near else None
            off = pl.multiple_of(kt * TQ, TQ)
            grp = range(N_GRP)
            s = [s_ref[slot(kt, jt), rows(g), :] if cached else logits(g, kt, jt) for g in grp]
            p0 = [jnp.exp(s[g][:, :LANE] - shift_ref[rows(g), :]) for g in grp]
            p1 = [jnp.exp(s[g][:, LANE:] - shift_ref[rows(g), :]) for g in grp]
            for g in grp:
                l_ref[rows(g), :] += p0[g] + p1[g]
            pv = [jnp.dot(jnp.concatenate([p0[g], p1[g]], axis=1).astype(BF16), v_fn(g, off),
                          preferred_element_type=F32) for g in grp]
            for g in grp:
                acc_ref[rows(g), :] += pv[g]
        return fn

    loops(pass2)
    return [acc_ref[rows(g), :] / jnp.sum(l_ref[rows(g), :], axis=-1, keepdims=True) for g in range(N_GRP)]


def _attend_scratch():
    return [pltpu.VMEM((N_GRP * STACK, LANE), BF16)] + [pltpu.VMEM((N_GRP * STACK, LANE), F32)] * 4


def _split_lanes(x):
    lo = lax.broadcasted_iota(I32, (1, LANE), 1) < HEAD_DIM
    z = jnp.zeros_like(x)
    return jnp.concatenate([jnp.where(lo, x, z), jnp.where(lo, z, x)], axis=0)


def _attn_a_kernel(far_ref, q_ref, k_ref, v_ref, tab_ref, lam_ref, sub_ref, o_ref,
                   qs_ref, mf_ref, mn_ref, l_ref, acc_ref, s_ref, *, lam_init):
    i = pl.program_id(1)
    for g in range(A_HEADS):
        qs_ref[g * STACK:(g + 1) * STACK, :] = _split_lanes(q_ref[:, g * LANE:(g + 1) * LANE])
    outs = _attend(
        i, qs_ref,
        lambda g, off: k_ref[pl.ds(off, TQ), g * LANE:(g + 1) * LANE],
        lambda g, off: v_ref[pl.ds(off, TQ), g * LANE:(g + 1) * LANE],
        None,
        lambda g, jt: tab_ref[g, jt][None],
        [jnp.full((STACK, 1), far_ref[g], F32) for g in range(A_HEADS)],
        mf_ref, mn_ref, l_ref, acc_ref, s_ref)
    lam = lam_ref[...]
    lam_full = (jnp.exp(jnp.sum(lam[0:1] * lam[1:2], axis=-1, keepdims=True))
                - jnp.exp(jnp.sum(lam[2:3] * lam[3:4], axis=-1, keepdims=True)) + lam_init)
    for g, o in enumerate(outs):
        d = o[:TQ] - lam_full * o[TQ:]
        o_ref[:, g * LANE:(g + 1) * LANE] = (_rms(d, sub_ref[...]) * (1.0 - lam_init)).astype(o_ref.dtype)


def _attn_a(proj, near, far, lam, subln, lam_init):
    b, s, _ = proj.shape
    assert A_HEADS == N_GRP
    wide = A_HEADS * A_V_DIM
    return pl.pallas_call(
        functools.partial(_attn_a_kernel, lam_init=lam_init),
        grid_spec=pltpu.PrefetchScalarGridSpec(
            num_scalar_prefetch=1,
            grid=(b, s // TQ),
            in_specs=[
                pl.BlockSpec((None, TQ, wide), lambda b_, i, f: (b_, i, COL_QA // wide)),
                pl.BlockSpec((None, s, wide), lambda b_, i, f: (b_, 0, COL_KA // wide), pipeline_mode=pl.Buffered(1)),
                pl.BlockSpec((None, s, wide), lambda b_, i, f: (b_, 0, COL_VA // wide), pipeline_mode=pl.Buffered(1)),
                pl.BlockSpec((A_HEADS, NEAR_TILES, TQ, TQ), lambda b_, i, f: (0, 0, 0, 0),
                             pipeline_mode=pl.Buffered(1)),
                pl.BlockSpec((4, HEAD_DIM), lambda b_, i, f: (0, 0)),
                pl.BlockSpec((1, A_V_DIM), lambda b_, i, f: (0, 0)),
            ],
            out_specs=pl.BlockSpec((None, TQ, wide), lambda b_, i, f: (b_, i, 0)),
            scratch_shapes=_attend_scratch() + [pltpu.VMEM((s // TQ, N_GRP * STACK, TQ), F32)],
        ),
        out_shape=jax.ShapeDtypeStruct((b, s, wide), BF16),
        compiler_params=_cparams(("parallel", "arbitrary")),
    )(far, proj, proj, proj, near, lam, subln.reshape(1, A_V_DIM))


def _attn_b_kernel(far_ref, qb_ref, kb_ref, vb_ref, qi_ref, ki_ref, wi_ref, tab_ref, o_ref,
                   keyst_ref, selm_ref, jm_ref, qs_ref, mf_ref, mn_ref, l_ref, acc_ref, s_ref,
                   *, top_k, idx_bits):
    i = pl.program_id(1)
    n_kt = i + 1
    lane = lax.broadcasted_iota(I32, (1, LANE), 1)
    lo = lane < HEAD_DIM
    int_min = jnp.int32(-2 ** 31)
    split_heads = _split_lanes

    key_row = lax.broadcasted_iota(I32, (TQ, TQ), 0)
    chunk_gap = (key_row // CHUNK) - (lax.broadcasted_iota(I32, (TQ, TQ), 1) // CHUNK)

    def causal(kt):
        return chunk_gap <= jnp.where(kt < i, jnp.int32(TQ), jnp.int32(0))

    qi = qi_ref[...]
    qis = jnp.concatenate([split_heads(qi[:, :LANE]), split_heads(qi[:, LANE:])], axis=0)
    wi = wi_ref[...] * ((IDX_HEADS ** -0.5) * (HEAD_DIM ** -0.5))

    def score_tile(kt):
        off = pl.multiple_of(kt * TQ, TQ)
        r = jnp.maximum(_dot_nt(ki_ref[pl.ds(off, TQ), :], qis), 0.0)
        isc = r[:, 0:TQ] * wi[0:1, :]
        for hh in range(1, IDX_HEADS):
            isc = isc + r[:, hh * TQ:(hh + 1) * TQ] * wi[hh:hh + 1, :]
        isc = jnp.where(isc == 0.0, 0.0, isc)
        isc = jnp.where(causal(kt), isc, -jnp.inf)
        bits = lax.bitcast_convert_type(isc, I32)
        keyst_ref[kt] = bits ^ ((bits >> 31) & jnp.int32(0x7FFFFFFF))

    _tile_loop(0, n_kt, score_tile, (2, 1))


    def count(pred_fn):
        def body(kt, acc):
            c = jnp.where(pred_fn(kt, keyst_ref[kt]), 1.0, 0.0)
            return acc + jnp.sum(c.reshape(TQ // 32, 32, TQ), axis=0)
        acc = lax.fori_loop(0, n_kt, body, jnp.zeros((32, TQ), F32))
        return jnp.sum(acc, axis=0, keepdims=True)

    def thr_body(bi, state):
        tu, nge = state
        cand_u = tu | (jnp.int32(1) << (31 - bi))
        cand_s = cand_u ^ int_min
        cnt = count(lambda kt, kk: kk >= cand_s)
        keep = cnt >= top_k
        return jnp.where(keep, cand_u, tu), jnp.where(keep, cnt, nge)

    tu, nge = lax.fori_loop(0, 32, thr_body, (jnp.zeros((1, TQ), I32),
                                              jnp.full((1, TQ), (n_kt * TQ).astype(F32), F32)))
    thr = tu ^ int_min

    jm_ref[...] = jnp.full(jm_ref.shape, 2 ** idx_bits - 1, I32)

    @pl.when(jnp.max(nge) > top_k)
    def _():
        cnt_hi = top_k - count(lambda kt, kk: kk > thr)

        def tie_body(bi, jmax):
            trial = jmax | (jnp.int32(1) << (idx_bits - 1 - bi))
            cnt = count(lambda kt, kk: (kk == thr) & ((key_row + kt * TQ) <= trial))
            return jnp.where(cnt <= cnt_hi, trial, jmax)

        jm_ref[...] = jnp.broadcast_to(lax.fori_loop(0, idx_bits, tie_body, jnp.zeros((1, TQ), I32)), jm_ref.shape)

    jmax = jm_ref[0:1, :]

    def mask_tile(kt):
        kk = keyst_ref[kt]
        sel = ((kk > thr) | ((kk == thr) & ((key_row + kt * TQ) <= jmax))) & causal(kt)
        selm_ref[kt] = jnp.where(sel, 0.0, NEG).T

    _tile_loop(0, n_kt, mask_tile, (2, 1))

    for g in range(N_GRP):
        qs_ref[g * STACK:(g + 1) * STACK, :] = split_heads(qb_ref[:, g * LANE:(g + 1) * LANE])
    top_rows = lax.broadcasted_iota(I32, (STACK, 1), 0) < TQ
    outs = _attend(
        i, qs_ref,
        lambda g, off: kb_ref[pl.ds(off, TQ), :],
        lambda g, off: vb_ref[pl.ds(off, TQ), :],
        lambda kt: selm_ref[kt],
        lambda g, jt: tab_ref[2 * g:2 * g + 2, jt],
        [jnp.where(top_rows, far_ref[A_HEADS + 2 * g], far_ref[A_HEADS + 2 * g + 1]) for g in range(N_GRP)],
        mf_ref, mn_ref, l_ref, acc_ref, s_ref)
    for g, o in enumerate(outs):
        o_ref[:, g * LANE:(g + 1) * LANE] = jnp.where(lo, o[:TQ], o[TQ:]).astype(o_ref.dtype)


B_CACHE_TILES = NEAR_TILES + 8


def _attn_b(proj, wi, near_b, far):
    b, s, _ = proj.shape
    top_k = min(TOPK_MAX, s // 4)
    assert top_k <= TQ and s % TQ == 0 and (s & (s - 1)) == 0
    n_kt = s // TQ
    assert B_HEADS == 2 * N_GRP and s < 2 ** 16
    qb_w = B_HEADS * HEAD_DIM
    qi_w = IDX_HEADS * HEAD_DIM
    return pl.pallas_call(
        functools.partial(_attn_b_kernel, top_k=float(top_k), idx_bits=int(math.log2(s))),
        grid_spec=pltpu.PrefetchScalarGridSpec(
            num_scalar_prefetch=1,
            grid=(b, n_kt),
            in_specs=[
                pl.BlockSpec((None, TQ, qb_w), lambda b_, i, f: (b_, i, COL_QB // qb_w)),
                pl.BlockSpec((None, s, LANE), lambda b_, i, f: (b_, 0, COL_KB // LANE)),
                pl.BlockSpec((None, s, LANE), lambda b_, i, f: (b_, 0, COL_VB // LANE)),
                pl.BlockSpec((None, TQ, qi_w), lambda b_, i, f: (b_, i, COL_QI // qi_w)),
                pl.BlockSpec((None, s, LANE), lambda b_, i, f: (b_, 0, COL_KI // LANE)),
                pl.BlockSpec((None, 8, TQ), lambda b_, i, f: (b_, 0, i)),
                pl.BlockSpec((B_HEADS, NEAR_TILES, TQ, TQ), lambda b_, i, f: (0, 0, 0, 0),
                             pipeline_mode=pl.Buffered(1)),
            ],
            out_specs=pl.BlockSpec((None, TQ, qb_w), lambda b_, i, f: (b_, i, 0)),
            scratch_shapes=[pltpu.VMEM((n_kt, TQ, TQ), I32), pltpu.VMEM((n_kt, TQ, TQ), F32),
                            pltpu.VMEM((8, TQ), I32)] + _attend_scratch()
            + [pltpu.VMEM((min(n_kt, B_CACHE_TILES), N_GRP * STACK, TQ), F32)],
        ),
        out_shape=jax.ShapeDtypeStruct((b, s, qb_w), BF16),
        compiler_params=_cparams(("parallel", "arbitrary")),
    )(far, proj, proj, proj, proj, proj, wi, near_b)


PAIRS_C = 4


def _c_tables(rel_table):
    n_var = WIN_C // TQ_C
    length = WIN_C + TQ_C
    k = jnp.arange(length, dtype=I32)
    cr = jnp.where(k < WIN_C, k, k - length)
    off = (jnp.arange(n_var, dtype=I32) * TQ_C)[:, None]
    rel_idx = jnp.clip(off - cr[None, :], -C_REL_CLIP, C_REL_CLIP) + C_REL_CLIP
    g = rel_table[rel_idx].transpose(2, 0, 1)
    bias = _toeplitz(g, TQ_C, WIN_C)
    r = jnp.arange(TQ_C, dtype=I32)[None, :, None]
    c = jnp.arange(WIN_C, dtype=I32)[None, None, :]
    kc = jnp.floor_divide(c - off[:, :, None], CHUNK)
    ok = (kc <= r // CHUNK) & (kc >= r // CHUNK - C_LEFT_CHUNKS)
    return jnp.where(ok[None], bias, NEG)


def _attn_c_kernel(q_ref, k_ref, v_ref, tab_ref, o_ref):
    i = pl.program_id(2)
    n_var = WIN_C // TQ_C
    lane = lax.broadcasted_iota(I32, (1, LANE), 1)
    lo = lane < HEAD_DIM
    var = jnp.minimum(i, n_var - 1)
    ks = pl.multiple_of(jnp.maximum(i - (n_var - 1), 0) * TQ_C, TQ_C)
    pairs = range(PAIRS_C)
    cols = [slice(pp * LANE, (pp + 1) * LANE) for pp in pairs]
    s = [_dot_nt(_split_lanes(q_ref[:, cols[pp]]), k_ref[pl.ds(ks, WIN_C), cols[pp]]) for pp in pairs]
    s = [s[pp] + jnp.concatenate([tab_ref[2 * pp, var], tab_ref[2 * pp + 1, var]], axis=0) for pp in pairs]
    m = [jnp.max(s[pp], axis=-1, keepdims=True) for pp in pairs]
    p = [jnp.exp(s[pp] - m[pp]) for pp in pairs]
    l = [jnp.sum(p[pp], axis=-1, keepdims=True) for pp in pairs]
    pv = [jnp.dot(p[pp].astype(BF16), v_ref[pl.ds(ks, WIN_C), cols[pp]], preferred_element_type=F32) for pp in pairs]
    for pp in pairs:
        o = pv[pp] / l[pp]
        o_ref[:, cols[pp]] = jnp.where(lo, o[:TQ_C], o[TQ_C:]).astype(o_ref.dtype)


def _attn_c(proj, tab):
    b, s, _ = proj.shape
    assert s >= WIN_C and s % TQ_C == 0
    n_grp = C_HEADS // 2 // PAIRS_C
    n_var = WIN_C // TQ_C
    wblk = PAIRS_C * LANE
    tab = tab.reshape(n_grp, 2 * PAIRS_C, n_var, TQ_C, WIN_C)
    return pl.pallas_call(
        _attn_c_kernel,
        grid=(n_grp, b, s // TQ_C),
        in_specs=[
            pl.BlockSpec((None, TQ_C, wblk), lambda h, b_, i: (b_, i, h)),
            pl.BlockSpec((None, s, wblk), lambda h, b_, i: (b_, 0, n_grp + h)),
            pl.BlockSpec((None, s, wblk), lambda h, b_, i: (b_, 0, 2 * n_grp + h)),
            pl.BlockSpec((None, 2 * PAIRS_C, n_var, TQ_C, WIN_C), lambda h, b_, i: (h, 0, 0, 0, 0),
                         pipeline_mode=pl.Buffered(1)),
        ],
        out_specs=pl.BlockSpec((None, TQ_C, wblk), lambda h, b_, i: (b_, i, h)),
        out_shape=jax.ShapeDtypeStruct((b, s, D_MODEL), BF16),
        compiler_params=_cparams(("parallel", "parallel", "arbitrary")),
    )(proj, proj, proj, tab)


ROW_TILE = 8


def _rows_to_tiles(ref, x):
    rows = x.shape[0]
    for c in range(ROW_TILE):
        ref[pl.ds(c, rows, stride=ROW_TILE), :] = x[:, c * LANE:(c + 1) * LANE]


def _tiles_to_rows(ref, rows):
    return jnp.concatenate([ref[pl.ds(c, rows, stride=ROW_TILE), :] for c in range(ROW_TILE)], axis=1)


def _router_kernel(h_ref, g_ref, wr_ref, br_ref, ei_ref, cw_ref):
    xn = _rms(h_ref[...], g_ref[...])
    logits = jnp.dot(xn, wr_ref[...], preferred_element_type=F32,
                     precision=lax.Precision.HIGHEST) + br_ref[...]
    lane = lax.broadcasted_iota(I32, (1, LANE), 1)
    lane_f = lane.astype(F32)
    big = float(LANE)
    is_g = lane < N_GROUPS
    lg = jnp.where(is_g, logits, -jnp.inf)
    gmax = jnp.max(lg, axis=-1, keepdims=True)
    g_sel = jnp.min(jnp.where(lg == gmax, lane_f, big), axis=-1, keepdims=True)
    p_gsel = 1.0 / jnp.sum(jnp.where(is_g, jnp.exp(logits - gmax), 0.0), axis=-1, keepdims=True)
    e_grp = ((lane - N_GROUPS) // EXPERTS_PER_GROUP).astype(F32)
    in_sel = (lane >= N_GROUPS) & (lane < N_GROUPS + N_EXPERTS) & (e_grp == g_sel)
    le = jnp.where(in_sel, logits, -jnp.inf)
    m1 = jnp.max(le, axis=-1, keepdims=True)
    i1 = jnp.min(jnp.where(le == m1, lane_f, big), axis=-1, keepdims=True)
    le2 = jnp.where(lane_f == i1, -jnp.inf, le)
    m2 = jnp.max(le2, axis=-1, keepdims=True)
    i2 = jnp.min(jnp.where(le2 == m2, lane_f, big), axis=-1, keepdims=True)
    t = jnp.exp(m2 - m1)
    c1 = p_gsel / (1.0 + t)
    c2 = p_gsel * t / (1.0 + t)
    a = jnp.minimum(i1, i2) - N_GROUPS - EXPERTS_PER_GROUP * g_sel
    b = jnp.maximum(i1, i2) - N_GROUPS - EXPERTS_PER_GROUP * g_sel
    pair = a * (2 * EXPERTS_PER_GROUP - 1 - a) * 0.5 + (b - a - 1.0)
    cls = g_sel * PAIRS_PER_GROUP + pair
    ei_ref[...] = jnp.broadcast_to(cls, ei_ref.shape).astype(I32)
    first_is_lo = i1 < i2
    cw_ref[...] = jnp.where(lane == 0, jnp.where(first_is_lo, c1, c2), jnp.where(first_is_lo, c2, c1))


def _router(h, g, w_rg, b_rg, w_re, b_re, tm=256):
    n, d = h.shape
    pad = LANE - N_GROUPS - N_EXPERTS
    wr = jnp.concatenate([w_rg, w_re, jnp.zeros((d, pad), F32)], axis=1)
    br = jnp.concatenate([b_rg, b_re, jnp.zeros((pad,), F32)]).reshape(1, LANE)
    return pl.pallas_call(
        _router_kernel,
        grid=(n // tm,),
        in_specs=[pl.BlockSpec((tm, d), lambda i: (i, 0)),
                  pl.BlockSpec((1, d), lambda i: (0, 0)),
                  pl.BlockSpec((d, LANE), lambda i: (0, 0)),
                  pl.BlockSpec((1, LANE), lambda i: (0, 0))],
        out_specs=[pl.BlockSpec((tm, LANE), lambda i: (i, 0)),
                   pl.BlockSpec((tm, LANE), lambda i: (i, 0))],
        out_shape=[jax.ShapeDtypeStruct((n, LANE), I32),
                   jax.ShapeDtypeStruct((n, LANE), F32)],
        compiler_params=_cparams(("parallel",)),
    )(h, g.reshape(1, d), wr, br)


N_CLASSES = N_GROUPS * PAIRS_PER_GROUP
_PAIR_AB = [(a, b) for a in range(EXPERTS_PER_GROUP) for b in range(a + 1, EXPERTS_PER_GROUP)]
CLASS_EXPERT_LO = [g * EXPERTS_PER_GROUP + a for g in range(N_GROUPS) for a, _ in _PAIR_AB]
CLASS_EXPERT_HI = [g * EXPERTS_PER_GROUP + b for g in range(N_GROUPS) for _, b in _PAIR_AB]
TB_MOE = 256


def _rank_kernel(cls_ref, rank_ref, cnt_ref, carry_ref):
    tb = cls_ref.shape[0]

    @pl.when(pl.program_id(0) == 0)
    def _():
        carry_ref[...] = jnp.zeros(carry_ref.shape, F32)

    onehot = cls_ref[...] == lax.broadcasted_iota(I32, (tb, LANE), 1)
    tril = lax.broadcasted_iota(I32, (tb, tb), 0) >= lax.broadcasted_iota(I32, (tb, tb), 1)
    pref = jnp.dot(jnp.where(tril, 1.0, 0.0).astype(BF16), jnp.where(onehot, 1.0, 0.0).astype(BF16),
                   preferred_element_type=F32)
    carry = carry_ref[...]
    rank = jnp.sum(jnp.where(onehot, pref + carry, 0.0), axis=-1, keepdims=True) - 1.0
    rank_ref[...] = jnp.broadcast_to(rank, rank_ref.shape).astype(I32)
    carry = carry + pref[tb - 1:tb, :]
    carry_ref[...] = carry
    cnt_ref[...] = carry


def _class_rank(cls, tb=512):
    n = cls.shape[0]
    return pl.pallas_call(
        _rank_kernel,
        grid=(n // tb,),
        in_specs=[pl.BlockSpec((tb, LANE), lambda i: (i, 0))],
        out_specs=[pl.BlockSpec((tb, LANE), lambda i: (i, 0)), pl.BlockSpec((1, LANE), lambda i: (0, 0))],
        out_shape=[jax.ShapeDtypeStruct((n, LANE), I32), jax.ShapeDtypeStruct((1, LANE), F32)],
        scratch_shapes=[pltpu.VMEM((1, LANE), F32)],
        compiler_params=_cparams(("arbitrary",)),
    )(cls)


def _moe_schedule(counts, n_tok, tm):
    counts = counts[:N_CLASSES].astype(I32)
    tiles_per = (counts + tm - 1) // tm
    tile_end = jnp.cumsum(tiles_per)
    tile_start = tile_end - tiles_per
    start_row = jnp.zeros((LANE,), I32).at[:N_CLASSES].set(tile_start * tm)
    n_tiles = n_tok // tm + N_CLASSES
    t = jnp.arange(n_tiles, dtype=I32)
    tc = jnp.searchsorted(tile_end, t, side="right").astype(I32)
    live = tc < N_CLASSES
    tc = jnp.minimum(tc, N_CLASSES - 1)
    nv = jnp.where(live, jnp.clip(counts[tc] - (t - tile_start[tc]) * tm, 0, tm), 0).astype(I32)
    e_lo = jnp.asarray(CLASS_EXPERT_LO, I32)[tc]
    e_hi = jnp.asarray(CLASS_EXPERT_HI, I32)[tc]
    return start_row, e_lo, e_hi, nv


def _tok(i, width=ROW_TILE):
    return pl.ds(i * width, width)


def _slot_of(start_ref, cls_ref, rank_ref, r):
    return start_ref[cls_ref[0, r]] + rank_ref[0, r]


def _dispatch_kernel(start_ref, cls_ref, rank_ref, h_ref, g_ref, xs_init, xs_hbm, sbuf, sem, *, tb, n_blk):
    del xs_init
    i = pl.program_id(0)
    slot = i % 2

    def wait(sl):
        pltpu.make_async_copy(sbuf.at[sl], xs_hbm.at[pl.ds(0, tb * ROW_TILE)], sem.at[sl]).wait()

    _rows_to_tiles(sbuf.at[slot], _rms(h_ref[...], g_ref[...]))
    for r in range(tb):
        pltpu.make_async_copy(sbuf.at[slot, _tok(r)], xs_hbm.at[_tok(_slot_of(start_ref, cls_ref, rank_ref, r))],
                              sem.at[slot]).start()

    @pl.when(i > 0)
    def _():
        wait(1 - slot)

    @pl.when(i == n_blk - 1)
    def _():
        wait(slot)


def _dispatch(h, g, start_row, cls_sm, rank_sm, n_slots, tb=TB_MOE):
    n, d = h.shape
    n_blk = cls_sm.shape[0]
    xs0 = jnp.zeros((n_slots * ROW_TILE, LANE), F32)
    idx_spec = pl.BlockSpec((None, 1, tb), lambda i, st: (i, 0, 0), memory_space=pltpu.SMEM)
    return pl.pallas_call(
        functools.partial(_dispatch_kernel, tb=tb, n_blk=n_blk),
        grid_spec=pltpu.PrefetchScalarGridSpec(
            num_scalar_prefetch=1,
            grid=(n_blk,),
            in_specs=[idx_spec, idx_spec,
                      pl.BlockSpec((tb, d), lambda i, st: (i, 0)),
                      pl.BlockSpec((1, d), lambda i, st: (0, 0)),
                      pl.BlockSpec(memory_space=pl.ANY)],
            out_specs=pl.BlockSpec(memory_space=pl.ANY),
            scratch_shapes=[pltpu.VMEM((2, tb * ROW_TILE, LANE), F32), pltpu.SemaphoreType.DMA((2,))],
        ),
        out_shape=jax.ShapeDtypeStruct(xs0.shape, F32),
        input_output_aliases={5: 0},
        compiler_params=_cparams(("arbitrary",)),
    )(start_row, cls_sm, rank_sm, h, g.reshape(1, d), xs0)


def _expert_kernel(elo_ref, ehi_ref, nv_ref, xs_ref, wga, wua, wda, wgb, wub, wdb, ys_ref, *, tm):
    t = pl.program_id(0)

    @pl.when(nv_ref[t] > 0)
    def _():
        x = _tiles_to_rows(xs_ref, tm).astype(BF16)
        g = [jnp.dot(x, w[...], preferred_element_type=F32) for w in (wga, wgb)]
        u = [jnp.dot(x, w[...], preferred_element_type=F32) for w in (wua, wub)]
        act = [((g[k] * jax.nn.sigmoid(g[k])) * u[k]).astype(BF16) for k in range(2)]
        y = [jnp.dot(act[k], w[...], preferred_element_type=F32) for k, w in enumerate((wda, wdb))]
        for half in range(2):
            for c in range(ROW_TILE):
                ys_ref[pl.ds(half * ROW_TILE + c, tm, stride=2 * ROW_TILE), :] = y[half][:, c * LANE:(c + 1) * LANE]

    @pl.when(nv_ref[t] == 0)
    def _():
        ys_ref[...] = jnp.zeros(ys_ref.shape, F32)


def _moe_experts(xs, e_lo, e_hi, nv, layer, wg, wu, wd, tm=TM_MOE):
    d = wg.shape[2]
    assert d == ROW_TILE * LANE
    n_tiles = nv.shape[0]

    def w_spec(rows, cols, which):
        return pl.BlockSpec((None, None, rows, cols), lambda t, lo, hi, nv_: (layer, (lo, hi)[which][t], 0, 0))

    w_specs = [w_spec(d, D_EXPERT, k) for k in (0, 0)] + [w_spec(D_EXPERT, d, 0)]
    w_specs += [w_spec(d, D_EXPERT, k) for k in (1, 1)] + [w_spec(D_EXPERT, d, 1)]
    return pl.pallas_call(
        functools.partial(_expert_kernel, tm=tm),
        grid_spec=pltpu.PrefetchScalarGridSpec(
            num_scalar_prefetch=3,
            grid=(n_tiles,),
            in_specs=[pl.BlockSpec((tm * ROW_TILE, LANE), lambda t, lo, hi, nv_: (t, 0))] + w_specs,
            out_specs=pl.BlockSpec((tm * 2 * ROW_TILE, LANE), lambda t, lo, hi, nv_: (t, 0)),
        ),
        out_shape=jax.ShapeDtypeStruct((n_tiles * tm * 2 * ROW_TILE, LANE), F32),
        compiler_params=_cparams(("arbitrary",)),
    )(e_lo, e_hi, nv, xs, wg, wu, wd, wg, wu, wd)


def _combine_kernel(start_ref, cls_cur, rank_cur, cls_next, rank_next, ys_hbm, h_ref, cw_ref, g_ref, o_ref,
                    ybuf, sem, *, tb, n_blk, final):
    i = pl.program_id(0)
    slot = i % 2
    other = 1 - slot
    pair = 2 * ROW_TILE

    def fetch(cls_ref, rank_ref, r, sl):
        return pltpu.make_async_copy(ys_hbm.at[_tok(_slot_of(start_ref, cls_ref, rank_ref, r), pair)],
                                     ybuf.at[sl, _tok(r, pair)], sem.at[sl])

    def wait(sl):
        pltpu.make_async_copy(ys_hbm.at[pl.ds(0, tb * pair)], ybuf.at[sl], sem.at[sl]).wait()

    @pl.when(i == 0)
    def _():
        def body(r, c):
            fetch(cls_cur, rank_cur, r, 0).start()
            return c
        lax.fori_loop(0, tb, body, 0)

    wait(slot)
    for r in range(tb):
        fetch(cls_next, rank_next, r, other).start()
    yb = ybuf.at[slot]
    y_lo = jnp.concatenate([yb[pl.ds(c, tb, stride=pair), :] for c in range(ROW_TILE)], axis=1)
    y_hi = jnp.concatenate([yb[pl.ds(ROW_TILE + c, tb, stride=pair), :] for c in range(ROW_TILE)], axis=1)
    cw = cw_ref[...]
    out = h_ref[...] + cw[:, 0:1] * y_lo + cw[:, 1:2] * y_hi
    if final:
        out = _rms(out, g_ref[...])
    o_ref[...] = out

    @pl.when(i == n_blk - 1)
    def _():
        wait(other)


def _combine(h, ys, cw, g, start_row, cls_sm, rank_sm, final, tb=TB_MOE):
    n, d = h.shape
    n_blk = n // tb
    cls_x = jnp.concatenate([cls_sm, cls_sm[-1:]], axis=0)
    rank_x = jnp.concatenate([rank_sm, rank_sm[-1:]], axis=0)

    def idx_spec(shift):
        return pl.BlockSpec((None, 1, tb), lambda i, st: (i + shift, 0, 0), memory_space=pltpu.SMEM)

    return pl.pallas_call(
        functools.partial(_combine_kernel, tb=tb, n_blk=n_blk, final=final),
        grid_spec=pltpu.PrefetchScalarGridSpec(
            num_scalar_prefetch=1,
            grid=(n_blk,),
            in_specs=[idx_spec(0), idx_spec(0), idx_spec(1), idx_spec(1),
                      pl.BlockSpec(memory_space=pl.ANY),
                      pl.BlockSpec((tb, d), lambda i, st: (i, 0)),
                      pl.BlockSpec((tb, LANE), lambda i, st: (i, 0)),
                      pl.BlockSpec((1, d), lambda i, st: (0, 0))],
            out_specs=pl.BlockSpec((tb, d), lambda i, st: (i, 0)),
            scratch_shapes=[pltpu.VMEM((2, tb * 2 * ROW_TILE, LANE), F32), pltpu.SemaphoreType.DMA((2,))],
        ),
        out_shape=jax.ShapeDtypeStruct((n, d), F32),
        compiler_params=_cparams(("arbitrary",)),
    )(start_row, cls_x, rank_x, cls_x, rank_x, ys, h, cw, g.reshape(1, d))


def _moe_layer(h, layer, ln_g, w_rg, b_rg, w_re, b_re, w_gate, w_up, w_down, final_g, final):
    n = h.shape[0]
    cls, cw = _router(h, ln_g, w_rg, b_rg, w_re, b_re)
    rank, counts = _class_rank(cls)
    start_row, e_lo, e_hi, nv = _moe_schedule(counts[0], n, TM_MOE)
    cls_sm = cls[:, 0].reshape(n // TB_MOE, 1, TB_MOE)
    rank_sm = rank[:, 0].reshape(n // TB_MOE, 1, TB_MOE)
    xs = _dispatch(h, ln_g, start_row, cls_sm, rank_sm, nv.shape[0] * TM_MOE)
    ys = _moe_experts(xs, e_lo, e_hi, nv, layer, w_gate, w_up, w_down)
    return _combine(h, ys, cw, final_g, start_row, cls_sm, rank_sm, final)


def _ab_weights(w_in):
    scale = HEAD_DIM ** -0.5
    widths = (512, 512, 512, 512, 64, 64, 256, 64, 4)
    offs = [sum(widths[:j]) for j in range(len(widths) + 1)]
    q_a, k_a, v_a, q_b, k_b, v_b, q_i, k_i, w_i = [w_in[:, offs[j]:offs[j + 1]] for j in range(len(widths))]
    pad = jnp.zeros((w_in.shape[0], LANE - IDX_HEADS), w_in.dtype)
    main = jnp.concatenate([q_a * scale, k_a, v_a, q_b * scale, k_b, k_b, v_b, v_b, q_i, k_i, k_i], axis=1)
    return jnp.concatenate([main, w_i, pad], axis=1).astype(BF16)


def kernel(x, t5_bias, ln_mix, ln_ffn, ln_final, ab_w_in, ab_w_out, diff_lambda, diff_subln,
           c_w_in, c_w_out, c_rel_bias, moe_w_rg, moe_b_rg, moe_w_re, moe_b_re,
           moe_w_gate, moe_w_up, moe_w_down):
    b, s, d = x.shape
    n = b * s
    depth = ln_mix.shape[0]
    h = x.reshape(n, d)
    near, far = _t5_tables(t5_bias)
    w_gate, w_up, w_down = (w.astype(BF16) for w in (moe_w_gate, moe_w_up, moe_w_down))
    for l in range(depth):
        if l % 2 == 0:
            e = l // 2
            lam_init = 0.8 - 0.6 * math.exp(-0.3 * l)
            proj, wi = _norm_proj(h, ln_mix[l], _ab_weights(ab_w_in[e]), AB_COLS)
            proj = proj.reshape(b, s, AB_COLS)
            y_a = _attn_a(proj, near[:A_HEADS], far, diff_lambda[e], diff_subln[e], lam_init)
            wi_rows = wi.reshape(b, s, LANE)[:, :, :8].transpose(0, 2, 1)
            y_b = _attn_b(proj, wi_rows, near[A_HEADS:], far)
            h = _out_proj(h, y_a.reshape(n, -1), 0, y_b.reshape(n, -1), 0, ab_w_out[e].astype(BF16))
        else:
            o = l // 2
            scale = HEAD_DIM ** -0.5
            w_in = jnp.concatenate([c_w_in[o][:, :d] * scale, c_w_in[o][:, d:]], axis=1).astype(BF16)
            (proj,) = _norm_proj(h, ln_mix[l], w_in, 3 * d)
            y = _attn_c(proj.reshape(b, s, 3 * d), _c_tables(c_rel_bias[o])).reshape(n, d)
            h = _out_proj(h, y, 0, y, 1, c_w_out[o].astype(BF16))
        h = _moe_layer(h, l, ln_ffn[l], moe_w_rg[l], moe_b_rg[l], moe_w_re[l], moe_b_re[l],
                       w_gate, w_up, w_down, ln_final, l == depth - 1)
    return h.reshape(b, s, d)
```

```python
import functools
import math

import jax
import jax.numpy as jnp
from jax import lax
from jax.experimental import pallas as pl
from jax.experimental.pallas import tpu as pltpu

F32 = jnp.float32
BF16 = jnp.bfloat16
I32 = jnp.int32

D_MODEL = 1024
CHUNK = 64
HEAD_DIM = 64
RMS_EPS = 1e-6
A_HEADS = 4
A_V_DIM = 2 * HEAD_DIM
B_HEADS = 8
IDX_HEADS = 4
TOPK_MAX = 256
C_HEADS = D_MODEL // HEAD_DIM
C_LEFT_CHUNKS = 8
C_REL_CLIP = 128
T5_BUCKETS = 32
T5_MAX_DIST = 1024
N_GROUPS = 4
EXPERTS_PER_GROUP = 8
N_EXPERTS = N_GROUPS * EXPERTS_PER_GROUP
PAIRS_PER_GROUP = EXPERTS_PER_GROUP * (EXPERTS_PER_GROUP - 1) // 2
D_EXPERT = D_MODEL // 2
NEG = -1e30

LANE = 128
VMEM_LIMIT = 56 * 1024 * 1024

TQ = 256
NEAR_TILES = 4
TQ_C = 2 * CHUNK
WIN_C = (C_LEFT_CHUNKS + 2) * CHUNK
TM_MOE = 256

COL_QA, COL_KA, COL_VA, COL_QB, COL_KB, COL_VB, COL_QI, COL_KI, COL_WI = (
    0, 512, 1024, 1536, 2048, 2176, 2304, 2560, 2688)
AB_COLS = 2688


def _cparams(sem, vmem=VMEM_LIMIT):
    return pltpu.CompilerParams(dimension_semantics=sem, vmem_limit_bytes=vmem)


def _dot_nt(a, b):
    return lax.dot_general(a, b, (((1,), (1,)), ((), ())), preferred_element_type=F32)


def _rms(x, g):
    return (x * lax.rsqrt(jnp.mean(x * x, axis=-1, keepdims=True) + RMS_EPS)) * g


def _norm_proj_kernel(x_ref, g_ref, w_ref, o_ref, *tail_ref, n_main):
    xn = _rms(x_ref[...], g_ref[...]).astype(BF16)
    acc = jnp.dot(xn, w_ref[...], preferred_element_type=F32)
    o_ref[...] = acc[:, :n_main].astype(o_ref.dtype)
    if tail_ref:
        tail_ref[0][...] = acc[:, n_main:]


def _norm_proj(x, g, w, n_main, tm=256):
    n, d = x.shape
    n_all = w.shape[1]
    out_shape = [jax.ShapeDtypeStruct((n, n_main), BF16)]
    out_specs = [pl.BlockSpec((tm, n_main), lambda i: (i, 0))]
    if n_all > n_main:
        out_shape.append(jax.ShapeDtypeStruct((n, n_all - n_main), F32))
        out_specs.append(pl.BlockSpec((tm, n_all - n_main), lambda i: (i, 0)))
    return pl.pallas_call(
        functools.partial(_norm_proj_kernel, n_main=n_main),
        grid=(n // tm,),
        in_specs=[pl.BlockSpec((tm, d), lambda i: (i, 0)),
                  pl.BlockSpec((1, d), lambda i: (0, 0)),
                  pl.BlockSpec((d, n_all), lambda i: (0, 0))],
        out_specs=out_specs, out_shape=out_shape,
        compiler_params=_cparams(("parallel",)),
    )(x, g.reshape(1, d), w)


def _out_proj_kernel(h_ref, y0_ref, y1_ref, w0_ref, w1_ref, o_ref):
    acc = jnp.dot(y0_ref[...], w0_ref[...], preferred_element_type=F32)
    acc = acc + jnp.dot(y1_ref[...], w1_ref[...], preferred_element_type=F32)
    o_ref[...] = h_ref[...] + acc


def _out_proj(h, y0, c0, y1, c1, w, tm=512):
    n, d = h.shape
    half = d // 2
    return pl.pallas_call(
        _out_proj_kernel,
        grid=(n // tm,),
        in_specs=[pl.BlockSpec((tm, d), lambda i: (i, 0)),
                  pl.BlockSpec((tm, half), lambda i: (i, c0)),
                  pl.BlockSpec((tm, half), lambda i: (i, c1)),
                  pl.BlockSpec((half, d), lambda i: (0, 0)),
                  pl.BlockSpec((half, d), lambda i: (1, 0))],
        out_specs=pl.BlockSpec((tm, d), lambda i: (i, 0)),
        out_shape=jax.ShapeDtypeStruct((n, d), F32),
        compiler_params=_cparams(("parallel",)),
    )(h, y0, y1, w, w)


def _t5_bucket(rel):
    half = T5_BUCKETS // 2
    max_exact = half // 2
    ret = jnp.where(rel > 0, half, 0)
    n = jnp.abs(rel)
    nf = jnp.maximum(n, 1).astype(F32)
    large = max_exact + (jnp.log(nf / max_exact) / math.log(T5_MAX_DIST / max_exact)
                         * (half - max_exact)).astype(I32)
    large = jnp.minimum(large, half - 1)
    return (ret + jnp.where(n < max_exact, n, large)).astype(I32)


def _toeplitz(g, rows, width):
    length = g.shape[-1]
    assert width < length and rows <= length
    flat = jnp.tile(g, (1,) * (g.ndim - 1) + (rows,))[..., :rows * (length - 1)]
    return flat.reshape(g.shape[:-1] + (rows, length - 1))[..., :width]


def _t5_tables(t5_bias):
    span = NEAR_TILES * TQ
    length = span + TQ
    k = jnp.arange(length, dtype=I32)
    delta = jnp.where(k < span, k, k - length) - (span - TQ)
    g = t5_bias[_t5_bucket(delta)].T
    bias = _toeplitz(g, TQ, span)
    r = jnp.arange(TQ, dtype=I32)[:, None]
    kr = jnp.arange(span, dtype=I32)[None, :] - (span - TQ)
    ok = jnp.floor_divide(kr, CHUNK) <= (r // CHUNK)
    bias = jnp.where(ok[None], bias, NEG)
    near = bias.reshape(-1, TQ, NEAR_TILES, TQ).transpose(0, 2, 1, 3)
    far = t5_bias[_t5_bucket(jnp.asarray(-(span - TQ) - 1, I32))]
    return near, far


N_GRP = 4
STACK = 2 * TQ
TRIP_WIDTHS = (4, 2, 1)


def _tile_loop(lo, hi, fn, widths=TRIP_WIDTHS):
    for width in widths:
        groups = (hi - lo) // width

        def body(j, c, lo=lo, width=width):
            for u in range(width):
                fn(lo + width * j + u)
            return c

        lax.fori_loop(0, groups, body, 0)
        lo = lo + width * groups


def _attend(i, qs_ref, k_fn, v_fn, mask_fn, near_fn, far_cols, mf_ref, mn_ref, l_ref, acc_ref, s_ref=None):
    n_far = jnp.maximum(i - (NEAR_TILES - 1), 0)
    n_far_slots = 0 if s_ref is None else s_ref.shape[0] - NEAR_TILES
    n_far_cached = jnp.minimum(n_far, n_far_slots)

    mf_ref[...] = jnp.full(mf_ref.shape, NEG, F32)
    mn_ref[...] = jnp.full(mn_ref.shape, NEG, F32)

    def rows(g):
        return slice(g * STACK, (g + 1) * STACK)

    def logits(g, kt, jt):
        off = pl.multiple_of(kt * TQ, TQ)
        s = _dot_nt(qs_ref[rows(g), :], k_fn(g, off))
        extra = None if mask_fn is None else mask_fn(kt)[None]
        if jt is not None:
            extra = near_fn(g, jt) if extra is None else extra + near_fn(g, jt)
        if extra is not None:
            s = (s.reshape(2, TQ, TQ) + extra).reshape(STACK, TQ)
        return s

    def loops(make_fn):
        if n_far_slots > 0:
            _tile_loop(0, n_far_cached, make_fn(False, True))
        _tile_loop(n_far_cached, n_far, make_fn(False, False))
        _tile_loop(n_far, i + 1, make_fn(True, s_ref is not None))

    def slot(kt, jt):
        return NEAR_TILES + kt if jt is None else jt

    def pass1(near, cached):
        m_ref = mn_ref if near else mf_ref

        def fn(kt):
            jt = kt - (i - (NEAR_TILES - 1)) if near else None
            s = [logits(g, kt, jt) for g in range(N_GRP)]
            for g in range(N_GRP):
                if cached:
                    s_ref[slot(kt, jt), rows(g), :] = s[g]
                m_ref[rows(g), :] = jnp.maximum(m_ref[rows(g), :], jnp.maximum(s[g][:, :LANE], s[g][:, LANE:]))
        return fn

    loops(pass1)

    for g in range(N_GRP):
        m = jnp.maximum(jnp.max(mf_ref[rows(g), :], axis=-1, keepdims=True) + far_cols[g],
                        jnp.max(mn_ref[rows(g), :], axis=-1, keepdims=True))
        m_near = jnp.broadcast_to(m, (STACK, LANE))
        mn_ref[rows(g), :] = m_near
        mf_ref[rows(g), :] = m_near - far_cols[g]
    l_ref[...] = jnp.zeros(l_ref.shape, F32)
    acc_ref[...] = jnp.zeros(acc_ref.shape, F32)

    def pass2(near, cached):
        shift_ref = mn_ref if near else mf_ref

        def fn(kt):
            jt = kt - (i - (NEAR_TILES - 1)) if near else None
            off = pl.multiple_of(kt * TQ, TQ)
            grp = range(N_GRP)
            s = [s_ref[slot(kt, jt), rows(g), :] if cached else logits(g, kt, jt) for g in grp]
            p0 = [jnp.exp(s[g][:, :LANE] - shift_ref[rows(g), :]) for g in grp]
            p1 = [jnp.exp(s[g][:, LANE:] - shift_ref[rows(g), :]) for g in grp]
            for g in grp:
                l_ref[rows(g), :] += p0[g] + p1[g]
            pv = [jnp.dot(jnp.concatenate([p0[g], p1[g]], axis=1).astype(BF16), v_fn(g, off),
                          preferred_element_type=F32) for g in grp]
            for g in grp:
                acc_ref[rows(g), :] += pv[g]
        return fn

    loops(pass2)
    return [acc_ref[rows(g), :] / jnp.sum(l_ref[rows(g), :], axis=-1, keepdims=True) for g in range(N_GRP)]


def _attend_scratch():
    return [pltpu.VMEM((N_GRP * STACK, LANE), BF16)] + [pltpu.VMEM((N_GRP * STACK, LANE), F32)] * 4


def _split_lanes(x):
    lo = lax.broadcasted_iota(I32, (1, LANE), 1) < HEAD_DIM
    z = jnp.zeros_like(x)
    return jnp.concatenate([jnp.where(lo, x, z), jnp.where(lo, z, x)], axis=0)


def _attn_a_kernel(far_ref, q_ref, k_ref, v_ref, tab_ref, lam_ref, sub_ref, o_ref,
                   qs_ref, mf_ref, mn_ref, l_ref, acc_ref, s_ref, *, lam_init):
    i = pl.program_id(1)
    for g in range(A_HEADS):
        qs_ref[g * STACK:(g + 1) * STACK, :] = _split_lanes(q_ref[:, g * LANE:(g + 1) * LANE])
    outs = _attend(
        i, qs_ref,
        lambda g, off: k_ref[pl.ds(off, TQ), g * LANE:(g + 1) * LANE],
        lambda g, off: v_ref[pl.ds(off, TQ), g * LANE:(g + 1) * LANE],
        None,
        lambda g, jt: tab_ref[g, jt][None],
        [jnp.full((STACK, 1), far_ref[g], F32) for g in range(A_HEADS)],
        mf_ref, mn_ref, l_ref, acc_ref, s_ref)
    lam = lam_ref[...]
    lam_full = (jnp.exp(jnp.sum(lam[0:1] * lam[1:2], axis=-1, keepdims=True))
                - jnp.exp(jnp.sum(lam[2:3] * lam[3:4], axis=-1, keepdims=True)) + lam_init)
    for g, o in enumerate(outs):
        d = o[:TQ] - lam_full * o[TQ:]
        o_ref[:, g * LANE:(g + 1) * LANE] = (_rms(d, sub_ref[...]) * (1.0 - lam_init)).astype(o_ref.dtype)


def _attn_a(proj, near, far, lam, subln, lam_init):
    b, s, _ = proj.shape
    assert A_HEADS == N_GRP
    wide = A_HEADS * A_V_DIM
    return pl.pallas_call(
        functools.partial(_attn_a_kernel, lam_init=lam_init),
        grid_spec=pltpu.PrefetchScalarGridSpec(
            num_scalar_prefetch=1,
            grid=(b, s // TQ),
            in_specs=[
                pl.BlockSpec((None, TQ, wide), lambda b_, i, f: (b_, i, COL_QA // wide)),
                pl.BlockSpec((None, s, wide), lambda b_, i, f: (b_, 0, COL_KA // wide), pipeline_mode=pl.Buffered(1)),
                pl.BlockSpec((None, s, wide), lambda b_, i, f: (b_, 0, COL_VA // wide), pipeline_mode=pl.Buffered(1)),
                pl.BlockSpec((A_HEADS, NEAR_TILES, TQ, TQ), lambda b_, i, f: (0, 0, 0, 0),
                             pipeline_mode=pl.Buffered(1)),
                pl.BlockSpec((4, HEAD_DIM), lambda b_, i, f: (0, 0)),
                pl.BlockSpec((1, A_V_DIM), lambda b_, i, f: (0, 0)),
            ],
            out_specs=pl.BlockSpec((None, TQ, wide), lambda b_, i, f: (b_, i, 0)),
            scratch_shapes=_attend_scratch() + [pltpu.VMEM((s // TQ, N_GRP * STACK, TQ), F32)],
        ),
        out_shape=jax.ShapeDtypeStruct((b, s, wide), BF16),
        compiler_params=_cparams(("parallel", "arbitrary")),
    )(far, proj, proj, proj, near, lam, subln.reshape(1, A_V_DIM))


def _attn_b_kernel(far_ref, qb_ref, kb_ref, vb_ref, qi_ref, ki_ref, wi_ref, tab_ref, o_ref,
                   keyst_ref, selm_ref, jm_ref, qs_ref, mf_ref, mn_ref, l_ref, acc_ref, s_ref,
                   *, top_k, idx_bits):
    i = pl.program_id(1)
    n_kt = i + 1
    lane = lax.broadcasted_iota(I32, (1, LANE), 1)
    lo = lane < HEAD_DIM
    int_min = jnp.int32(-2 ** 31)
    split_heads = _split_lanes

    key_row = lax.broadcasted_iota(I32, (TQ, TQ), 0)
    chunk_gap = (key_row // CHUNK) - (lax.broadcasted_iota(I32, (TQ, TQ), 1) // CHUNK)

    def causal(kt):
        return chunk_gap <= jnp.where(kt < i, jnp.int32(TQ), jnp.int32(0))

    qi = qi_ref[...]
    qis = jnp.concatenate([split_heads(qi[:, :LANE]), split_heads(qi[:, LANE:])], axis=0)
    wi = wi_ref[...] * ((IDX_HEADS ** -0.5) * (HEAD_DIM ** -0.5))

    def score_tile(kt):
        off = pl.multiple_of(kt * TQ, TQ)
        r = jnp.maximum(_dot_nt(ki_ref[pl.ds(off, TQ), :], qis), 0.0)
        isc = r[:, 0:TQ] * wi[0:1, :]
        for hh in range(1, IDX_HEADS):
            isc = isc + r[:, hh * TQ:(hh + 1) * TQ] * wi[hh:hh + 1, :]
        isc = jnp.where(isc == 0.0, 0.0, isc)
        isc = jnp.where(causal(kt), isc, -jnp.inf)
        bits = lax.bitcast_convert_type(isc, I32)
        keyst_ref[kt] = bits ^ ((bits >> 31) & jnp.int32(0x7FFFFFFF))

    _tile_loop(0, n_kt, score_tile, (2, 1))


    def count(pred_fn):
        def body(kt, acc):
            c = jnp.where(pred_fn(kt, keyst_ref[kt]), 1.0, 0.0)
            return acc + jnp.sum(c.reshape(TQ // 32, 32, TQ), axis=0)
        acc = lax.fori_loop(0, n_kt, body, jnp.zeros((32, TQ), F32))
        return jnp.sum(acc, axis=0, keepdims=True)

    def thr_body(bi, state):
        tu, nge = state
        cand_u = tu | (jnp.int32(1) << (31 - bi))
        cand_s = cand_u ^ int_min
        cnt = count(lambda kt, kk: kk >= cand_s)
        keep = cnt >= top_k
        return jnp.where(keep, cand_u, tu), jnp.where(keep, cnt, nge)

    tu, nge = lax.fori_loop(0, 32, thr_body, (jnp.zeros((1, TQ), I32),
                                              jnp.full((1, TQ), (n_kt * TQ).astype(F32), F32)))
    thr = tu ^ int_min

    jm_ref[...] = jnp.full(jm_ref.shape, 2 ** idx_bits - 1, I32)

    @pl.when(jnp.max(nge) > top_k)
    def _():
        cnt_hi = top_k - count(lambda kt, kk: kk > thr)

        def tie_body(bi, jmax):
            trial = jmax | (jnp.int32(1) << (idx_bits - 1 - bi))
            cnt = count(lambda kt, kk: (kk == thr) & ((key_row + kt * TQ) <= trial))
            return jnp.where(cnt <= cnt_hi, trial, jmax)

        jm_ref[...] = jnp.broadcast_to(lax.fori_loop(0, idx_bits, tie_body, jnp.zeros((1, TQ), I32)), jm_ref.shape)

    jmax = jm_ref[0:1, :]

    def mask_tile(kt):
        kk = keyst_ref[kt]
        sel = ((kk > thr) | ((kk == thr) & ((key_row + kt * TQ) <= jmax))) & causal(kt)
        selm_ref[kt] = jnp.where(sel, 0.0, NEG).T

    _tile_loop(0, n_kt, mask_tile, (2, 1))

    for g in range(N_GRP):
        qs_ref[g * STACK:(g + 1) * STACK, :] = split_heads(qb_ref[:, g * LANE:(g + 1) * LANE])
    top_rows = lax.broadcasted_iota(I32, (STACK, 1), 0) < TQ
    outs = _attend(
        i, qs_ref,
        lambda g, off: kb_ref[pl.ds(off, TQ), :],
        lambda g, off: vb_ref[pl.ds(off, TQ), :],
        lambda kt: selm_ref[kt],
        lambda g, jt: tab_ref[2 * g:2 * g + 2, jt],
        [jnp.where(top_rows, far_ref[A_HEADS + 2 * g], far_ref[A_HEADS + 2 * g + 1]) for g in range(N_GRP)],
        mf_ref, mn_ref, l_ref, acc_ref, s_ref)
    for g, o in enumerate(outs):
        o_ref[:, g * LANE:(g + 1) * LANE] = jnp.where(lo, o[:TQ], o[TQ:]).astype(o_ref.dtype)


B_CACHE_TILES = NEAR_TILES + 8


def _attn_b(proj, wi, near_b, far):
    b, s, _ = proj.shape
    top_k = min(TOPK_MAX, s // 4)
    assert top_k <= TQ and s % TQ == 0 and (s & (s - 1)) == 0
    n_kt = s // TQ
    assert B_HEADS == 2 * N_GRP and s < 2 ** 16
    qb_w = B_HEADS * HEAD_DIM
    qi_w = IDX_HEADS * HEAD_DIM
    return pl.pallas_call(
        functools.partial(_attn_b_kernel, top_k=float(top_k), idx_bits=int(math.log2(s))),
        grid_spec=pltpu.PrefetchScalarGridSpec(
            num_scalar_prefetch=1,
            grid=(b, n_kt),
            in_specs=[
                pl.BlockSpec((None, TQ, qb_w), lambda b_, i, f: (b_, i, COL_QB // qb_w)),
                pl.BlockSpec((None, s, LANE), lambda b_, i, f: (b_, 0, COL_KB // LANE)),
                pl.BlockSpec((None, s, LANE), lambda b_, i, f: (b_, 0, COL_VB // LANE)),
                pl.BlockSpec((None, TQ, qi_w), lambda b_, i, f: (b_, i, COL_QI // qi_w)),
                pl.BlockSpec((None, s, LANE), lambda b_, i, f: (b_, 0, COL_KI // LANE)),
                pl.BlockSpec((None, 8, TQ), lambda b_, i, f: (b_, 0, i)),
                pl.BlockSpec((B_HEADS, NEAR_TILES, TQ, TQ), lambda b_, i, f: (0, 0, 0, 0),
                             pipeline_mode=pl.Buffered(1)),
            ],
            out_specs=pl.BlockSpec((None, TQ, qb_w), lambda b_, i, f: (b_, i, 0)),
            scratch_shapes=[pltpu.VMEM((n_kt, TQ, TQ), I32), pltpu.VMEM((n_kt, TQ, TQ), F32),
                            pltpu.VMEM((8, TQ), I32)] + _attend_scratch()
            + [pltpu.VMEM((min(n_kt, B_CACHE_TILES), N_GRP * STACK, TQ), F32)],
        ),
        out_shape=jax.ShapeDtypeStruct((b, s, qb_w), BF16),
        compiler_params=_cparams(("parallel", "arbitrary")),
    )(far, proj, proj, proj, proj, proj, wi, near_b)


PAIRS_C = 4


def _c_tables(rel_table):
    n_var = WIN_C // TQ_C
    length = WIN_C + TQ_C
    k = jnp.arange(length, dtype=I32)
    cr = jnp.where(k < WIN_C, k, k - length)
    off = (jnp.arange(n_var, dtype=I32) * TQ_C)[:, None]
    rel_idx = jnp.clip(off - cr[None, :], -C_REL_CLIP, C_REL_CLIP) + C_REL_CLIP
    g = rel_table[rel_idx].transpose(2, 0, 1)
    bias = _toeplitz(g, TQ_C, WIN_C)
    r = jnp.arange(TQ_C, dtype=I32)[None, :, None]
    c = jnp.arange(WIN_C, dtype=I32)[None, None, :]
    kc = jnp.floor_divide(c - off[:, :, None], CHUNK)
    ok = (kc <= r // CHUNK) & (kc >= r // CHUNK - C_LEFT_CHUNKS)
    return jnp.where(ok[None], bias, NEG)


def _attn_c_kernel(q_ref, k_ref, v_ref, tab_ref, o_ref):
    i = pl.program_id(2)
    n_var = WIN_C // TQ_C
    lane = lax.broadcasted_iota(I32, (1, LANE), 1)
    lo = lane < HEAD_DIM
    var = jnp.minimum(i, n_var - 1)
    ks = pl.multiple_of(jnp.maximum(i - (n_var - 1), 0) * TQ_C, TQ_C)
    pairs = range(PAIRS_C)
    cols = [slice(pp * LANE, (pp + 1) * LANE) for pp in pairs]
    s = [_dot_nt(_split_lanes(q_ref[:, cols[pp]]), k_ref[pl.ds(ks, WIN_C), cols[pp]]) for pp in pairs]
    s = [s[pp] + jnp.concatenate([tab_ref[2 * pp, var], tab_ref[2 * pp + 1, var]], axis=0) for pp in pairs]
    m = [jnp.max(s[pp], axis=-1, keepdims=True) for pp in pairs]
    p = [jnp.exp(s[pp] - m[pp]) for pp in pairs]
    l = [jnp.sum(p[pp], axis=-1, keepdims=True) for pp in pairs]
    pv = [jnp.dot(p[pp].astype(BF16), v_ref[pl.ds(ks, WIN_C), cols[pp]], preferred_element_type=F32) for pp in pairs]
    for pp in pairs:
        o = pv[pp] / l[pp]
        o_ref[:, cols[pp]] = jnp.where(lo, o[:TQ_C], o[TQ_C:]).astype(o_ref.dtype)


def _attn_c(proj, tab):
    b, s, _ = proj.shape
    assert s >= WIN_C and s % TQ_C == 0
    n_grp = C_HEADS // 2 // PAIRS_C
    n_var = WIN_C // TQ_C
    wblk = PAIRS_C * LANE
    tab = tab.reshape(n_grp, 2 * PAIRS_C, n_var, TQ_C, WIN_C)
    return pl.pallas_call(
        _attn_c_kernel,
        grid=(n_grp, b, s // TQ_C),
        in_specs=[
            pl.BlockSpec((None, TQ_C, wblk), lambda h, b_, i: (b_, i, h)),
            pl.BlockSpec((None, s, wblk), lambda h, b_, i: (b_, 0, n_grp + h)),
            pl.BlockSpec((None, s, wblk), lambda h, b_, i: (b_, 0, 2 * n_grp + h)),
            pl.BlockSpec((None, 2 * PAIRS_C, n_var, TQ_C, WIN_C), lambda h, b_, i: (h, 0, 0, 0, 0),
                         pipeline_mode=pl.Buffered(1)),
        ],
        out_specs=pl.BlockSpec((None, TQ_C, wblk), lambda h, b_, i: (b_, i, h)),
        out_shape=jax.ShapeDtypeStruct((b, s, D_MODEL), BF16),
        compiler_params=_cparams(("parallel", "parallel", "arbitrary")),
    )(proj, proj, proj, tab)


ROW_TILE = 8


def _rows_to_tiles(ref, x):
    rows = x.shape[0]
    for c in range(ROW_TILE):
        ref[pl.ds(c, rows, stride=ROW_TILE), :] = x[:, c * LANE:(c + 1) * LANE]


def _tiles_to_rows(ref, rows):
    return jnp.concatenate([ref[pl.ds(c, rows, stride=ROW_TILE), :] for c in range(ROW_TILE)], axis=1)


def _router_kernel(h_ref, g_ref, wr_ref, br_ref, ei_ref, cw_ref):
    xn = _rms(h_ref[...], g_ref[...])
    logits = jnp.dot(xn, wr_ref[...], preferred_element_type=F32,
                     precision=lax.Precision.HIGHEST) + br_ref[...]
    lane = lax.broadcasted_iota(I32, (1, LANE), 1)
    lane_f = lane.astype(F32)
    big = float(LANE)
    is_g = lane < N_GROUPS
    lg = jnp.where(is_g, logits, -jnp.inf)
    gmax = jnp.max(lg, axis=-1, keepdims=True)
    g_sel = jnp.min(jnp.where(lg == gmax, lane_f, big), axis=-1, keepdims=True)
    p_gsel = 1.0 / jnp.sum(jnp.where(is_g, jnp.exp(logits - gmax), 0.0), axis=-1, keepdims=True)
    e_grp = ((lane - N_GROUPS) // EXPERTS_PER_GROUP).astype(F32)
    in_sel = (lane >= N_GROUPS) & (lane < N_GROUPS + N_EXPERTS) & (e_grp == g_sel)
    le = jnp.where(in_sel, logits, -jnp.inf)
    m1 = jnp.max(le, axis=-1, keepdims=True)
    i1 = jnp.min(jnp.where(le == m1, lane_f, big), axis=-1, keepdims=True)
    le2 = jnp.where(lane_f == i1, -jnp.inf, le)
    m2 = jnp.max(le2, axis=-1, keepdims=True)
    i2 = jnp.min(jnp.where(le2 == m2, lane_f, big), axis=-1, keepdims=True)
    t = jnp.exp(m2 - m1)
    c1 = p_gsel / (1.0 + t)
    c2 = p_gsel * t / (1.0 + t)
    a = jnp.minimum(i1, i2) - N_GROUPS - EXPERTS_PER_GROUP * g_sel
    b = jnp.maximum(i1, i2) - N_GROUPS - EXPERTS_PER_GROUP * g_sel
    pair = a * (2 * EXPERTS_PER_GROUP - 1 - a) * 0.5 + (b - a - 1.0)
    cls = g_sel * PAIRS_PER_GROUP + pair
    ei_ref[...] = jnp.broadcast_to(cls, ei_ref.shape).astype(I32)
    first_is_lo = i1 < i2
    cw_ref[...] = jnp.where(lane == 0, jnp.where(first_is_lo, c1, c2), jnp.where(first_is_lo, c2, c1))


def _router(h, g, w_rg, b_rg, w_re, b_re, tm=256):
    n, d = h.shape
    pad = LANE - N_GROUPS - N_EXPERTS
    wr = jnp.concatenate([w_rg, w_re, jnp.zeros((d, pad), F32)], axis=1)
    br = jnp.concatenate([b_rg, b_re, jnp.zeros((pad,), F32)]).reshape(1, LANE)
    return pl.pallas_call(
        _router_kernel,
        grid=(n // tm,),
        in_specs=[pl.BlockSpec((tm, d), lambda i: (i, 0)),
                  pl.BlockSpec((1, d), lambda i: (0, 0)),
                  pl.BlockSpec((d, LANE), lambda i: (0, 0)),
                  pl.BlockSpec((1, LANE), lambda i: (0, 0))],
        out_specs=[pl.BlockSpec((tm, LANE), lambda i: (i, 0)),
                   pl.BlockSpec((tm, LANE), lambda i: (i, 0))],
        out_shape=[jax.ShapeDtypeStruct((n, LANE), I32),
                   jax.ShapeDtypeStruct((n, LANE), F32)],
        compiler_params=_cparams(("parallel",)),
    )(h, g.reshape(1, d), wr, br)


N_CLASSES = N_GROUPS * PAIRS_PER_GROUP
_PAIR_AB = [(a, b) for a in range(EXPERTS_PER_GROUP) for b in range(a + 1, EXPERTS_PER_GROUP)]
CLASS_EXPERT_LO = [g * EXPERTS_PER_GROUP + a for g in range(N_GROUPS) for a, _ in _PAIR_AB]
CLASS_EXPERT_HI = [g * EXPERTS_PER_GROUP + b for g in range(N_GROUPS) for _, b in _PAIR_AB]
TB_MOE = 256


def _rank_kernel(cls_ref, rank_ref, cnt_ref, carry_ref):
    tb = cls_ref.shape[0]

    @pl.when(pl.program_id(0) == 0)
    def _():
        carry_ref[...] = jnp.zeros(carry_ref.shape, F32)

    onehot = cls_ref[...] == lax.broadcasted_iota(I32, (tb, LANE), 1)
    tril = lax.broadcasted_iota(I32, (tb, tb), 0) >= lax.broadcasted_iota(I32, (tb, tb), 1)
    pref = jnp.dot(jnp.where(tril, 1.0, 0.0).astype(BF16), jnp.where(onehot, 1.0, 0.0).astype(BF16),
                   preferred_element_type=F32)
    carry = carry_ref[...]
    rank = jnp.sum(jnp.where(onehot, pref + carry, 0.0), axis=-1, keepdims=True) - 1.0
    rank_ref[...] = jnp.broadcast_to(rank, rank_ref.shape).astype(I32)
    carry = carry + pref[tb - 1:tb, :]
    carry_ref[...] = carry
    cnt_ref[...] = carry


def _class_rank(cls, tb=512):
    n = cls.shape[0]
    return pl.pallas_call(
        _rank_kernel,
        grid=(n // tb,),
        in_specs=[pl.BlockSpec((tb, LANE), lambda i: (i, 0))],
        out_specs=[pl.BlockSpec((tb, LANE), lambda i: (i, 0)), pl.BlockSpec((1, LANE), lambda i: (0, 0))],
        out_shape=[jax.ShapeDtypeStruct((n, LANE), I32), jax.ShapeDtypeStruct((1, LANE), F32)],
        scratch_shapes=[pltpu.VMEM((1, LANE), F32)],
        compiler_params=_cparams(("arbitrary",)),
    )(cls)


def _moe_schedule(counts, n_tok, tm):
    counts = counts[:N_CLASSES].astype(I32)
    tiles_per = (counts + tm - 1) // tm
    tile_end = jnp.cumsum(tiles_per)
    tile_start = tile_end - tiles_per
    start_row = jnp.zeros((LANE,), I32).at[:N_CLASSES].set(tile_start * tm)
    n_tiles = n_tok // tm + N_CLASSES
    t = jnp.arange(n_tiles, dtype=I32)
    tc = jnp.searchsorted(tile_end, t, side="right").astype(I32)
    live = tc < N_CLASSES
    tc = jnp.minimum(tc, N_CLASSES - 1)
    nv = jnp.where(live, jnp.clip(counts[tc] - (t - tile_start[tc]) * tm, 0, tm), 0).astype(I32)
    e_lo = jnp.asarray(CLASS_EXPERT_LO, I32)[tc]
    e_hi = jnp.asarray(CLASS_EXPERT_HI, I32)[tc]
    return start_row, e_lo, e_hi, nv


def _tok(i, width=ROW_TILE):
    return pl.ds(i * width, width)


def _slot_of(start_ref, cls_ref, rank_ref, r):
    return start_ref[cls_ref[0, r]] + rank_ref[0, r]


def _dispatch_kernel(start_ref, cls_ref, rank_ref, h_ref, g_ref, xs_init, xs_hbm, sbuf, sem, *, tb, n_blk):
    del xs_init
    i = pl.program_id(0)
    slot = i % 2

    def wait(sl):
        pltpu.make_async_copy(sbuf.at[sl], xs_hbm.at[pl.ds(0, tb * ROW_TILE)], sem.at[sl]).wait()

    _rows_to_tiles(sbuf.at[slot], _rms(h_ref[...], g_ref[...]))
    for r in range(tb):
        pltpu.make_async_copy(sbuf.at[slot, _tok(r)], xs_hbm.at[_tok(_slot_of(start_ref, cls_ref, rank_ref, r))],
                              sem.at[slot]).start()

    @pl.when(i > 0)
    def _():
        wait(1 - slot)

    @pl.when(i == n_blk - 1)
    def _():
        wait(slot)


def _dispatch(h, g, start_row, cls_sm, rank_sm, n_slots, tb=TB_MOE):
    n, d = h.shape
    n_blk = cls_sm.shape[0]
    xs0 = jnp.zeros((n_slots * ROW_TILE, LANE), F32)
    idx_spec = pl.BlockSpec((None, 1, tb), lambda i, st: (i, 0, 0), memory_space=pltpu.SMEM)
    return pl.pallas_call(
        functools.partial(_dispatch_kernel, tb=tb, n_blk=n_blk),
        grid_spec=pltpu.PrefetchScalarGridSpec(
            num_scalar_prefetch=1,
            grid=(n_blk,),
            in_specs=[idx_spec, idx_spec,
                      pl.BlockSpec((tb, d), lambda i, st: (i, 0)),
                      pl.BlockSpec((1, d), lambda i, st: (0, 0)),
                      pl.BlockSpec(memory_space=pl.ANY)],
            out_specs=pl.BlockSpec(memory_space=pl.ANY),
            scratch_shapes=[pltpu.VMEM((2, tb * ROW_TILE, LANE), F32), pltpu.SemaphoreType.DMA((2,))],
        ),
        out_shape=jax.ShapeDtypeStruct(xs0.shape, F32),
        input_output_aliases={5: 0},
        compiler_params=_cparams(("arbitrary",)),
    )(start_row, cls_sm, rank_sm, h, g.reshape(1, d), xs0)


def _expert_kernel(elo_ref, ehi_ref, nv_ref, xs_ref, wga, wua, wda, wgb, wub, wdb, ys_ref, *, tm):
    t = pl.program_id(0)

    @pl.when(nv_ref[t] > 0)
    def _():
        x = _tiles_to_rows(xs_ref, tm).astype(BF16)
        g = [jnp.dot(x, w[...], preferred_element_type=F32) for w in (wga, wgb)]
        u = [jnp.dot(x, w[...], preferred_element_type=F32) for w in (wua, wub)]
        act = [((g[k] * jax.nn.sigmoid(g[k])) * u[k]).astype(BF16) for k in range(2)]
        y = [jnp.dot(act[k], w[...], preferred_element_type=F32) for k, w in enumerate((wda, wdb))]
        for half in range(2):
            for c in range(ROW_TILE):
                ys_ref[pl.ds(half * ROW_TILE + c, tm, stride=2 * ROW_TILE), :] = y[half][:, c * LANE:(c + 1) * LANE]

    @pl.when(nv_ref[t] == 0)
    def _():
        ys_ref[...] = jnp.zeros(ys_ref.shape, F32)


def _moe_experts(xs, e_lo, e_hi, nv, layer, wg, wu, wd, tm=TM_MOE):
    d = wg.shape[2]
    assert d == ROW_TILE * LANE
    n_tiles = nv.shape[0]

    def w_spec(rows, cols, which):
        return pl.BlockSpec((None, None, rows, cols), lambda t, lo, hi, nv_: (layer, (lo, hi)[which][t], 0, 0))

    w_specs = [w_spec(d, D_EXPERT, k) for k in (0, 0)] + [w_spec(D_EXPERT, d, 0)]
    w_specs += [w_spec(d, D_EXPERT, k) for k in (1, 1)] + [w_spec(D_EXPERT, d, 1)]
    return pl.pallas_call(
        functools.partial(_expert_kernel, tm=tm),
        grid_spec=pltpu.PrefetchScalarGridSpec(
            num_scalar_prefetch=3,
            grid=(n_tiles,),
            in_specs=[pl.BlockSpec((tm * ROW_TILE, LANE), lambda t, lo, hi, nv_: (t, 0))] + w_specs,
            out_specs=pl.BlockSpec((tm * 2 * ROW_TILE, LANE), lambda t, lo, hi, nv_: (t, 0)),
        ),
        out_shape=jax.ShapeDtypeStruct((n_tiles * tm * 2 * ROW_TILE, LANE), F32),
        compiler_params=_cparams(("arbitrary",)),
    )(e_lo, e_hi, nv, xs, wg, wu, wd, wg, wu, wd)


COMBINE_PITCH = 3 * ROW_TILE


def _combine_kernel(start_ref, cls_cur, rank_cur, cls_next, rank_next, ys_hbm, h_ref, cw_ref, g_ref, o_ref,
                    ybuf, sem, *, tb, n_blk, final):
    i = pl.program_id(0)
    slot = i % 2
    other = 1 - slot
    pair = 2 * ROW_TILE

    def fetch(cls_ref, rank_ref, r, sl):
        return pltpu.make_async_copy(ys_hbm.at[_tok(_slot_of(start_ref, cls_ref, rank_ref, r), pair)],
                                     ybuf.at[sl, pl.ds(r * COMBINE_PITCH, pair)], sem.at[sl])

    def wait(sl):
        pltpu.make_async_copy(ys_hbm.at[pl.ds(0, tb * pair)], ybuf.at[sl, pl.ds(0, tb * pair)], sem.at[sl]).wait()

    @pl.when(i == 0)
    def _():
        def body(r, c):
            fetch(cls_cur, rank_cur, r, 0).start()
            return c
        lax.fori_loop(0, tb, body, 0)

    wait(slot)
    for r in range(tb):
        fetch(cls_next, rank_next, r, other).start()
    yb = ybuf.at[slot]
    y_lo = jnp.concatenate([yb[pl.ds(c, tb, stride=COMBINE_PITCH), :] for c in range(ROW_TILE)], axis=1)
    y_hi = jnp.concatenate([yb[pl.ds(ROW_TILE + c, tb, stride=COMBINE_PITCH), :] for c in range(ROW_TILE)], axis=1)
    cw = cw_ref[...]
    out = h_ref[...] + cw[:, 0:1] * y_lo + cw[:, 1:2] * y_hi
    if final:
        out = _rms(out, g_ref[...])
    o_ref[...] = out

    @pl.when(i == n_blk - 1)
    def _():
        wait(other)


def _combine(h, ys, cw, g, start_row, cls_sm, rank_sm, final, tb=TB_MOE):
    n, d = h.shape
    n_blk = n // tb
    cls_x = jnp.concatenate([cls_sm, cls_sm[-1:]], axis=0)
    rank_x = jnp.concatenate([rank_sm, rank_sm[-1:]], axis=0)

    def idx_spec(shift):
        return pl.BlockSpec((None, 1, tb), lambda i, st: (i + shift, 0, 0), memory_space=pltpu.SMEM)

    return pl.pallas_call(
        functools.partial(_combine_kernel, tb=tb, n_blk=n_blk, final=final),
        grid_spec=pltpu.PrefetchScalarGridSpec(
            num_scalar_prefetch=1,
            grid=(n_blk,),
            in_specs=[idx_spec(0), idx_spec(0), idx_spec(1), idx_spec(1),
                      pl.BlockSpec(memory_space=pl.ANY),
                      pl.BlockSpec((tb, d), lambda i, st: (i, 0)),
                      pl.BlockSpec((tb, LANE), lambda i, st: (i, 0)),
                      pl.BlockSpec((1, d), lambda i, st: (0, 0))],
            out_specs=pl.BlockSpec((tb, d), lambda i, st: (i, 0)),
            scratch_shapes=[pltpu.VMEM((2, tb * COMBINE_PITCH, LANE), F32), pltpu.SemaphoreType.DMA((2,))],
        ),
        out_shape=jax.ShapeDtypeStruct((n, d), F32),
        compiler_params=_cparams(("arbitrary",)),
    )(start_row, cls_x, rank_x, cls_x, rank_x, ys, h, cw, g.reshape(1, d))


def _moe_layer(h, layer, ln_g, w_rg, b_rg, w_re, b_re, w_gate, w_up, w_down, final_g, final):
    n = h.shape[0]
    cls, cw = _router(h, ln_g, w_rg, b_rg, w_re, b_re)
    rank, counts = _class_rank(cls)
    start_row, e_lo, e_hi, nv = _moe_schedule(counts[0], n, TM_MOE)
    cls_sm = cls[:, 0].reshape(n // TB_MOE, 1, TB_MOE)
    rank_sm = rank[:, 0].reshape(n // TB_MOE, 1, TB_MOE)
    xs = _dispatch(h, ln_g, start_row, cls_sm, rank_sm, nv.shape[0] * TM_MOE)
    ys = _moe_experts(xs, e_lo, e_hi, nv, layer, w_gate, w_up, w_down)
    return _combine(h, ys, cw, final_g, start_row, cls_sm, rank_sm, final)


def _ab_weights(w_in):
    scale = HEAD_DIM ** -0.5
    widths = (512, 512, 512, 512, 64, 64, 256, 64, 4)
    offs = [sum(widths[:j]) for j in range(len(widths) + 1)]
    q_a, k_a, v_a, q_b, k_b, v_b, q_i, k_i, w_i = [w_in[:, offs[j]:offs[j + 1]] for j in range(len(widths))]
    pad = jnp.zeros((w_in.shape[0], LANE - IDX_HEADS), w_in.dtype)
    main = jnp.concatenate([q_a * scale, k_a, v_a, q_b * scale, k_b, k_b, v_b, v_b, q_i, k_i, k_i], axis=1)
    return jnp.concatenate([main, w_i, pad], axis=1).astype(BF16)


def kernel(x, t5_bias, ln_mix, ln_ffn, ln_final, ab_w_in, ab_w_out, diff_lambda, diff_subln,
           c_w_in, c_w_out, c_rel_bias, moe_w_rg, moe_b_rg, moe_w_re, moe_b_re,
           moe_w_gate, moe_w_up, moe_w_down):
    b, s, d = x.shape
    n = b * s
    depth = ln_mix.shape[0]
    h = x.reshape(n, d)
    near, far = _t5_tables(t5_bias)
    w_gate, w_up, w_down = (w.astype(BF16) for w in (moe_w_gate, moe_w_up, moe_w_down))
    for l in range(depth):
        if l % 2 == 0:
            e = l // 2
            lam_init = 0.8 - 0.6 * math.exp(-0.3 * l)
            proj, wi = _norm_proj(h, ln_mix[l], _ab_weights(ab_w_in[e]), AB_COLS)
            proj = proj.reshape(b, s, AB_COLS)
            y_a = _attn_a(proj, near[:A_HEADS], far, diff_lambda[e], diff_subln[e], lam_init)
            wi_rows = wi.reshape(b, s, LANE)[:, :, :8].transpose(0, 2, 1)
            y_b = _attn_b(proj, wi_rows, near[A_HEADS:], far)
            h = _out_proj(h, y_a.reshape(n, -1), 0, y_b.reshape(n, -1), 0, ab_w_out[e].astype(BF16))
        else:
            o = l // 2
            scale = HEAD_DIM ** -0.5
            w_in = jnp.concatenate([c_w_in[o][:, :d] * scale, c_w_in[o][:, d:]], axis=1).astype(BF16)
            (proj,) = _norm_proj(h, ln_mix[l], w_in, 3 * d)
            y = _attn_c(proj.reshape(b, s, 3 * d), _c_tables(c_rel_bias[o])).reshape(n, d)
            h = _out_proj(h, y, 0, y, 1, c_w_out[o].astype(BF16))
        h = _moe_layer(h, l, ln_ffn[l], moe_w_rg[l], moe_b_rg[l], moe_w_re[l], moe_b_re[l],
                       w_gate, w_up, w_down, ln_final, l == depth - 1)
    return h.reshape(b, s, d)
```
